```python
import jax, jax.numpy as jnp
from jax import lax
import numpy as np

D_MODEL = 1024
BATCH = 4
SEQ = 8192
DEPTH = 1

N_HEADS = 8
HEAD_DIM = 64
N_KV = 2
GQA_R = N_HEADS // N_KV
ATTN_WIDTH = N_HEADS * HEAD_DIM
KV_WIDTH = N_KV * HEAD_DIM
CMP_LEN = 32
CMP_STRIDE = 16
CMP_HID = 128
SEL_BLOCK = 64
SEL_TOPN = 16
WINDOW = 512
Q_BLOCK = 128
N_NSA_BRANCH = 3
SGU_WIDTH = 512
N_GROUPS_SGU = 8
SGU_GROUP_DIM = SGU_WIDTH // N_GROUPS_SGU
CHUNK = 128
N_BRANCH = 2
N_EXPERTS = 32
TOP_K = 4
D_FF = D_MODEL
SWIGLU_LIMIT = 7.0
SWIGLU_ALPHA = 1.702
MOE_BLOCK = 512
RMS_EPS = 1e-5
LN_EPS = 1e-5
NEG = -1e30
FORCE = 1e4
D_IN = ATTN_WIDTH + 6 * KV_WIDTH + N_NSA_BRANCH * N_HEADS + 2 * SGU_WIDTH + N_BRANCH * D_MODEL

kernel_name = "hybrid_nsa_sgu_moe_block"


def _split_points():
    p0 = ATTN_WIDTH
    p1 = p0 + 6 * KV_WIDTH
    p2 = p1 + N_NSA_BRANCH * N_HEADS
    p3 = p2 + SGU_WIDTH
    p4 = p3 + SGU_WIDTH
    return [p0, p1, p2, p3, p4]


def rmsnorm(x, g):
    xf = x.astype(jnp.float32)
    y = xf * lax.rsqrt(jnp.mean(xf * xf, axis=-1, keepdims=True) + RMS_EPS)
    return (y * g.astype(jnp.float32)).astype(x.dtype)


def layernorm(x, g, b):
    xf = x.astype(jnp.float32)
    mu = jnp.mean(xf, axis=-1, keepdims=True)
    var = jnp.mean(jnp.square(xf - mu), axis=-1, keepdims=True)
    y = (xf - mu) * lax.rsqrt(var + LN_EPS)
    return (y * g.astype(jnp.float32) + b.astype(jnp.float32)).astype(x.dtype)


def _masked_softmax(s, mask):
    return jax.nn.softmax(jnp.where(mask, s, NEG), axis=-1)


def _alibi_slopes():
    s = 2.0 ** (-8.0 * np.arange(1, N_HEADS + 1) / N_HEADS)
    return jnp.asarray(s, jnp.float32).reshape(N_KV, GQA_R)


def _cmp_to_sel_overlap(n_cmp, n_sel):
    cs = np.arange(n_cmp)[:, None] * CMP_STRIDE
    ss = np.arange(n_sel)[None, :] * SEL_BLOCK
    ov = np.clip(np.minimum(cs + CMP_LEN, ss + SEL_BLOCK) - np.maximum(cs, ss), 0, None)
    return jnp.asarray(ov / CMP_LEN, jnp.float32)


def _compress(k, w1, b1, w2, pe):
    S = k.shape[1]
    n_cmp = (S - CMP_LEN) // CMP_STRIDE + 1
    idx = np.arange(n_cmp)[:, None] * CMP_STRIDE + np.arange(CMP_LEN)[None, :]
    blk = k[:, idx] + pe[None, None, :, None, :]
    hid = jax.nn.gelu(jnp.einsum('bnlgd,ldf->bngf', blk, w1) + b1)
    return jnp.einsum('bngf,fd->bngd', hid, w2)


def nsa_attention(q, kc, vc, ks, vs, kw, vw, gate_logits,
                  w_ck1, b_ck1, w_ck2, pe_k, w_cv1, b_cv1, w_cv2, pe_v):
    B, S = q.shape[0], q.shape[1]
    n_cmp = (S - CMP_LEN) // CMP_STRIDE + 1
    n_sel = S // SEL_BLOCK
    top_n = min(SEL_TOPN, n_sel)
    n_qb = S // Q_BLOCK
    scale = HEAD_DIM ** -0.5
    slopes = _alibi_slopes()[:, :, None, None]

    k_cmp = _compress(kc, w_ck1, b_ck1, w_ck2, pe_k)
    v_cmp = _compress(vc, w_cv1, b_cv1, w_cv2, pe_v)
    c_pos = jnp.arange(n_cmp, dtype=jnp.float32) * CMP_STRIDE + (CMP_LEN - 1) / 2.0
    c_end = jnp.arange(n_cmp) * CMP_STRIDE + CMP_LEN - 1
    overlap = _cmp_to_sel_overlap(n_cmp, n_sel)

    ks_blk = ks.reshape(B, n_sel, SEL_BLOCK, N_KV, HEAD_DIM).transpose(0, 3, 1, 2, 4)
    vs_blk = vs.reshape(B, n_sel, SEL_BLOCK, N_KV, HEAD_DIM).transpose(0, 3, 1, 2, 4)
    kw_pad = jnp.pad(kw, ((0, 0), (WINDOW, 0), (0, 0), (0, 0)))
    vw_pad = jnp.pad(vw, ((0, 0), (WINDOW, 0), (0, 0), (0, 0)))

    qb = q.reshape(B, n_qb, Q_BLOCK, N_KV, GQA_R, HEAD_DIM).transpose(1, 0, 2, 3, 4, 5)
    gb = jax.nn.sigmoid(gate_logits.astype(jnp.float32)).reshape(
        B, n_qb, Q_BLOCK, N_KV, GQA_R, N_NSA_BRANCH).transpose(1, 0, 2, 3, 4, 5)
    b_ix = jnp.arange(B)[:, None, None, None]
    g_ix = jnp.arange(N_KV)[None, :, None, None]
    j_sel = jnp.arange(n_sel)

    def block(args):
        i, qi, gi = args
        t = i * Q_BLOCK + jnp.arange(Q_BLOCK)
        tf = t.astype(jnp.float32)
        s_c = jnp.einsum('bqgrd,bngd->bgrqn', qi, k_cmp).astype(jnp.float32) * scale \
            - slopes * (tf[:, None] - c_pos[None, :])
        p_c = _masked_softmax(s_c, c_end[None, :] <= t[:, None]) \
            * (t >= CMP_LEN - 1).astype(jnp.float32)[:, None]
        o_c = jnp.einsum('bgrqn,bngd->bqgrd', p_c.astype(v_cmp.dtype), v_cmp)
        imp = jnp.einsum('bgrqn,nj->bgqj', p_c, overlap)
        cur = (t // SEL_BLOCK)[:, None]
        forced = (j_sel[None] == 0) | (j_sel[None] == cur) | (j_sel[None] == cur - 1)
        causal_blk = j_sel[None] * SEL_BLOCK <= t[:, None]
        prio = jnp.where(forced, FORCE, jnp.where(causal_blk, imp, -1.0))
        _, sel = lax.top_k(prio, top_n)
        k_sel = ks_blk[b_ix, g_ix, sel].reshape(B, N_KV, Q_BLOCK, top_n * SEL_BLOCK, HEAD_DIM)
        v_sel = vs_blk[b_ix, g_ix, sel].reshape(B, N_KV, Q_BLOCK, top_n * SEL_BLOCK, HEAD_DIM)
        s_pos = (sel[..., None] * SEL_BLOCK + jnp.arange(SEL_BLOCK)).reshape(B, N_KV, Q_BLOCK, top_n * SEL_BLOCK)
        d_sel = (tf[None, None, :, None] - s_pos.astype(jnp.float32))[:, :, None]
        s_s = jnp.einsum('bqgrd,bgqkd->bgrqk', qi, k_sel).astype(jnp.float32) * scale - slopes * d_sel
        p_s = _masked_softmax(s_s, d_sel >= 0)
        o_s = jnp.einsum('bgrqk,bgqkd->bqgrd', p_s.astype(v_sel.dtype), v_sel)
        k_win = lax.dynamic_slice_in_dim(kw_pad, i * Q_BLOCK, WINDOW + Q_BLOCK, axis=1)
        v_win = lax.dynamic_slice_in_dim(vw_pad, i * Q_BLOCK, WINDOW + Q_BLOCK, axis=1)
        w_pos = i * Q_BLOCK - WINDOW + jnp.arange(WINDOW + Q_BLOCK)
        dist = t[:, None] - w_pos[None, :]
        mask_w = (dist >= 0) & (dist < WINDOW) & (w_pos[None, :] >= 0)
        s_w = jnp.einsum('bqgrd,bkgd->bgrqk', qi, k_win).astype(jnp.float32) * scale \
            - slopes * dist.astype(jnp.float32)
        p_w = _masked_softmax(s_w, mask_w)
        o_w = jnp.einsum('bgrqk,bkgd->bqgrd', p_w.astype(v_win.dtype), v_win)
        o = gi[..., 0:1] * o_c + gi[..., 1:2] * o_s + gi[..., 2:3] * o_w
        return o.reshape(B, Q_BLOCK, ATTN_WIDTH).astype(qi.dtype)

    out = lax.map(block, (jnp.arange(n_qb), qb, gb))
    return out.transpose(1, 0, 2, 3).reshape(B, S, ATTN_WIDTH)


def spatial_gating(u, v, ln_g, ln_b, w_s, b_s):
    B, S = u.shape[0], u.shape[1]
    u = jax.nn.gelu(u)
    v = layernorm(jax.nn.gelu(v), ln_g, ln_b)
    vc = v.reshape(B, S // CHUNK, CHUNK, N_GROUPS_SGU, SGU_GROUP_DIM)
    w = w_s * jnp.tril(jnp.ones((CHUNK, CHUNK), w_s.dtype))
    mixed = jnp.einsum('gts,bcsgd->bctgd', w, vc) + b_s.T[:, :, None]
    return u * mixed.reshape(B, S, SGU_WIDTH)


def moe(x, w_router, b_router, w_gu, b_gu, w_down, b_down):
    B, S, D = x.shape
    n_tok = B * S
    xt = x.reshape(n_tok, D)
    logits = (xt @ w_router + b_router).astype(jnp.float32)
    top_val, top_idx = lax.top_k(logits, TOP_K)
    top_w = jax.nn.softmax(top_val, axis=-1)
    n_assign = n_tok * TOP_K
    e_flat = top_idx.reshape(-1)
    tok_flat = jnp.repeat(jnp.arange(n_tok, dtype=jnp.int32), TOP_K)
    w_flat = top_w.reshape(-1)
    order = jnp.argsort(e_flat)
    e_sorted, tok_sorted, w_sorted = e_flat[order], tok_flat[order], w_flat[order]
    counts = jnp.bincount(e_flat, length=N_EXPERTS)
    start = jnp.cumsum(counts) - counts
    padded = (counts + MOE_BLOCK - 1) // MOE_BLOCK * MOE_BLOCK
    pad_end = jnp.cumsum(padded)
    pad_start = pad_end - padded
    dest = pad_start[e_sorted] + jnp.arange(n_assign) - start[e_sorted]
    n_blocks = -(-n_assign // MOE_BLOCK) + N_EXPERTS
    n_slots = n_blocks * MOE_BLOCK
    slot_tok = jnp.zeros((n_slots,), jnp.int32).at[dest].set(tok_sorted)
    slot_w = jnp.zeros((n_slots,), jnp.float32).at[dest].set(w_sorted)
    blk_expert = jnp.minimum(jnp.searchsorted(pad_end, jnp.arange(n_blocks) * MOE_BLOCK, side='right'),
                             N_EXPERTS - 1)

    def expert_block(args):
        e, toks, wts = args
        xb = xt[toks]
        gu = xb @ w_gu[e] + b_gu[e]
        gate = jnp.minimum(gu[:, :D_FF], SWIGLU_LIMIT)
        up = jnp.clip(gu[:, D_FF:], -SWIGLU_LIMIT, SWIGLU_LIMIT)
        act = gate * jax.nn.sigmoid(SWIGLU_ALPHA * gate) * (up + 1.0)
        y = act @ w_down[e] + b_down[e]
        return y * wts[:, None].astype(y.dtype)

    y_slots = lax.map(expert_block, (blk_expert, slot_tok.reshape(n_blocks, MOE_BLOCK),
                                     slot_w.reshape(n_blocks, MOE_BLOCK)))
    out = jnp.zeros_like(xt).at[slot_tok].add(y_slots.reshape(n_slots, D).astype(xt.dtype))
    return out.reshape(B, S, D)


def setup_inputs(seed: int = 0) -> dict:
    key = jax.random.key(seed)
    ks = jax.random.split(key, 32)
    f32 = jnp.float32

    def nrm(k, shape, scale):
        return jax.random.normal(k, shape, f32) * scale

    L = DEPTH
    return {
        "x": nrm(ks[0], (BATCH, SEQ, D_MODEL), 1.0),
        "g_mix": 1.0 + nrm(ks[1], (L, D_MODEL), 0.01),
        "w_in": nrm(ks[2], (L, D_MODEL, D_IN), D_MODEL ** -0.5),
        "w_ck1": nrm(ks[3], (L, CMP_LEN, HEAD_DIM, CMP_HID), (CMP_LEN * HEAD_DIM) ** -0.5),
        "b_ck1": nrm(ks[4], (L, CMP_HID), 0.01),
        "w_ck2": nrm(ks[5], (L, CMP_HID, HEAD_DIM), CMP_HID ** -0.5),
        "pe_k": nrm(ks[6], (L, CMP_LEN, HEAD_DIM), 0.02),
        "w_cv1": nrm(ks[7], (L, CMP_LEN, HEAD_DIM, CMP_HID), (CMP_LEN * HEAD_DIM) ** -0.5),
        "b_cv1": nrm(ks[8], (L, CMP_HID), 0.01),
        "w_cv2": nrm(ks[9], (L, CMP_HID, HEAD_DIM), CMP_HID ** -0.5),
        "pe_v": nrm(ks[10], (L, CMP_LEN, HEAD_DIM), 0.02),
        "sgu_ln_g": 1.0 + nrm(ks[11], (L, SGU_WIDTH), 0.01),
        "sgu_ln_b": nrm(ks[12], (L, SGU_WIDTH), 0.01),
        "w_spatial": nrm(ks[13], (L, N_GROUPS_SGU, CHUNK, CHUNK), CHUNK ** -0.5),
        "b_spatial": 1.0 + nrm(ks[14], (L, N_GROUPS_SGU, CHUNK), 0.1),
        "w_branch_a": nrm(ks[15], (L, ATTN_WIDTH, D_MODEL), ATTN_WIDTH ** -0.5),
        "w_branch_b": nrm(ks[16], (L, SGU_WIDTH, D_MODEL), SGU_WIDTH ** -0.5),
        "w_out": nrm(ks[17], (L, D_MODEL, D_MODEL), D_MODEL ** -0.5),
        "g_moe": 1.0 + nrm(ks[18], (L, D_MODEL), 0.01),
        "w_router": nrm(ks[19], (L, D_MODEL, N_EXPERTS), D_MODEL ** -0.5),
        "b_router": nrm(ks[20], (L, N_EXPERTS), 0.01),
        "w_gate_up": nrm(ks[21], (L, N_EXPERTS, D_MODEL, 2 * D_FF), D_MODEL ** -0.5),
        "b_gate_up": nrm(ks[22], (L, N_EXPERTS, 2 * D_FF), 0.01),
        "w_down": nrm(ks[23], (L, N_EXPERTS, D_FF, D_MODEL), D_FF ** -0.5),
        "b_down": nrm(ks[24], (L, N_EXPERTS, D_MODEL), 0.01),
        "g_final": 1.0 + nrm(ks[25], (D_MODEL,), 0.01),
    }


def reference(x, g_mix, w_in, w_ck1, b_ck1, w_ck2, pe_k, w_cv1, b_cv1, w_cv2, pe_v,
              sgu_ln_g, sgu_ln_b, w_spatial, b_spatial, w_branch_a, w_branch_b, w_out,
              g_moe, w_router, b_router, w_gate_up, b_gate_up, w_down, b_down, g_final):
    B, S = x.shape[0], x.shape[1]
    for l in range(DEPTH):
        h = rmsnorm(x, g_mix[l])
        proj = h @ w_in[l]
        q, kv, ng, u, v, mg = jnp.split(proj, _split_points(), axis=-1)
        q = q.reshape(B, S, N_KV, GQA_R, HEAD_DIM)
        kv = kv.reshape(B, S, 6, N_KV, HEAD_DIM)
        ng = ng.reshape(B, S, N_KV, GQA_R, N_NSA_BRANCH)
        y_a = nsa_attention(q, kv[:, :, 0], kv[:, :, 1], kv[:, :, 2], kv[:, :, 3], kv[:, :, 4], kv[:, :, 5], ng,
                            w_ck1[l], b_ck1[l], w_ck2[l], pe_k[l], w_cv1[l], b_cv1[l], w_cv2[l], pe_v[l])
        y_b = spatial_gating(u, v, sgu_ln_g[l], sgu_ln_b[l], w_spatial[l], b_spatial[l])
        g_a, g_b = jnp.split(jax.nn.sigmoid(mg), 2, axis=-1)
        merged = g_a * (y_a @ w_branch_a[l]) + g_b * (y_b @ w_branch_b[l])
        x = x + merged @ w_out[l]
        x = x + moe(rmsnorm(x, g_moe[l]), w_router[l], b_router[l], w_gate_up[l], b_gate_up[l],
                    w_down[l], b_down[l])
    return rmsnorm(x, g_final)
```

```python
import functools

import numpy as np
import jax
import jax.numpy as jnp
from jax import lax
from jax.experimental import pallas as pl
from jax.experimental.pallas import tpu as pltpu

D_MODEL = 1024
N_HEADS = 8
HEAD_DIM = 64
N_KV = 2
GQA_R = N_HEADS // N_KV
ATTN_WIDTH = N_HEADS * HEAD_DIM
KV_WIDTH = N_KV * HEAD_DIM
CMP_LEN = 32
CMP_STRIDE = 16
CMP_HID = 128
SEL_BLOCK = 64
SEL_TOPN = 16
WINDOW = 512
Q_BLOCK = 128
N_NSA_BRANCH = 3
SGU_WIDTH = 512
N_GROUPS_SGU = 8
SGU_GROUP_DIM = SGU_WIDTH // N_GROUPS_SGU
CHUNK = 128
N_EXPERTS = 32
TOP_K = 4
D_FF = D_MODEL
SWIGLU_LIMIT = 7.0
SWIGLU_ALPHA = 1.702
MOE_BLOCK = 512
RMS_EPS = 1e-5
LN_EPS = 1e-5
NEG = -1e30
FORCE = 1e4

LANES = 128
MASK_BIAS = 1e9
SEL_CHUNK = 256
WIN_KEYS = WINDOW + Q_BLOCK
VMEM_LIMIT = 56 * 1024 * 1024

F32 = jnp.float32
BF16 = jnp.bfloat16
I32 = jnp.int32

_NT = (((1,), (1,)), ((), ()))


def _dot(a, b):
    return jnp.dot(a, b, preferred_element_type=F32)


def _dot_nt(a, b):
    return lax.dot_general(a, b, _NT, preferred_element_type=F32)


def _sigmoid(x):
    return 1.0 / (1.0 + jnp.exp(-x))


def _gelu(x):
    c = np.float32(np.sqrt(2.0 / np.pi))
    return 0.5 * x * (1.0 + jnp.tanh(c * (x + 0.044715 * (x * x * x))))


def _rms(x, g):
    return x * lax.rsqrt(jnp.mean(x * x, axis=-1, keepdims=True) + RMS_EPS) * g


def _proj_kernel(x_ref, gmix_ref, wq_ref, wkv_ref, wng_ref, wu_ref, wv_ref, wmg_ref,
                 lng_ref, lnb_ref, wsp_ref, bsp_ref, wbb_ref,
                 q_ref, kv_ref, gates_ref, ga_ref, gbyb_ref):
    tm = x_ref.shape[0]
    hb = _rms(x_ref[...], gmix_ref[...]).astype(BF16)
    q_ref[...] = _dot(hb, wq_ref[...]).astype(BF16)
    kv_ref[...] = _dot(hb, wkv_ref[...]).astype(BF16)
    gates_ref[...] = _sigmoid(_dot(hb, wng_ref[...]))
    mg = _dot(hb, wmg_ref[...])
    ga_ref[...] = _sigmoid(mg[:, :D_MODEL]).astype(BF16)

    u = _gelu(_dot(hb, wu_ref[...]))
    v = _gelu(_dot(hb, wv_ref[...]))
    mu = jnp.mean(v, axis=-1, keepdims=True)
    vc = v - mu
    var = jnp.mean(vc * vc, axis=-1, keepdims=True)
    vln = (vc * lax.rsqrt(var + LN_EPS) * lng_ref[...] + lnb_ref[...]).astype(BF16)

    row = lax.broadcasted_iota(I32, (CHUNK, CHUNK), 0)
    col = lax.broadcasted_iota(I32, (CHUNK, CHUNK), 1)
    tril = row >= col
    wsp = [jnp.where(tril, wsp_ref[g], 0.0).astype(BF16) for g in range(N_GROUPS_SGU)]
    low_half = col < SGU_GROUP_DIM
    bsp = bsp_ref[...]
    chunks = []
    for c in range(tm // CHUNK):
        rs = slice(c * CHUNK, (c + 1) * CHUNK)
        parts = []
        for p in range(SGU_WIDTH // LANES):
            cs = slice(p * LANES, (p + 1) * LANES)
            vblk = vln[rs, cs]
            mixed = jnp.where(low_half, _dot(wsp[2 * p], vblk), _dot(wsp[2 * p + 1], vblk))
            parts.append(u[rs, cs] * (mixed + bsp[:, cs]))
        chunks.append(jnp.concatenate(parts, axis=1))
    yb = jnp.concatenate(chunks, axis=0).astype(BF16)
    gbyb_ref[...] = (_sigmoid(mg[:, D_MODEL:]) * _dot(yb, wbb_ref[...])).astype(BF16)


def _proj(x2, gmix, wq, wkv, wng, wu, wv, wmg, lng, lnb, wsp, bsp, wbb, tm):
    n = x2.shape[0]
    full = lambda a: pl.BlockSpec(a.shape, lambda i: (0,) * a.ndim)
    rows = lambda w: pl.BlockSpec((tm, w), lambda i: (i, 0))
    ins = (x2, gmix, wq, wkv, wng, wu, wv, wmg, lng, lnb, wsp, bsp, wbb)
    return pl.pallas_call(
        _proj_kernel,
        grid=(n // tm,),
        in_specs=[rows(D_MODEL)] + [full(a) for a in ins[1:]],
        out_specs=[rows(ATTN_WIDTH), rows(6 * KV_WIDTH), rows(LANES), rows(D_MODEL), rows(D_MODEL)],
        out_shape=[jax.ShapeDtypeStruct((n, ATTN_WIDTH), BF16),
                   jax.ShapeDtypeStruct((n, 6 * KV_WIDTH), BF16),
                   jax.ShapeDtypeStruct((n, LANES), F32),
                   jax.ShapeDtypeStruct((n, D_MODEL), BF16),
                   jax.ShapeDtypeStruct((n, D_MODEL), BF16)],
        compiler_params=pltpu.CompilerParams(dimension_semantics=("parallel",),
                                             vmem_limit_bytes=VMEM_LIMIT),
        name="proj",
    )(*ins)


def _compress_kernel(x_ref, pe_ref, w1_ref, b1_ref, w2_ref, o_ref):
    nc = x_ref.shape[3]
    x = x_ref[0, 0, 0].astype(F32)
    xa = (x + pe_ref[0, 0:1, :]).astype(BF16)
    xb = (x + pe_ref[0, 1:2, :]).astype(BF16)
    za = _dot(xa, w1_ref[0, 0])
    zb = _dot(xb, w1_ref[0, 1])
    hid = _gelu(za + pltpu.roll(zb, nc - 1, 0) + b1_ref[0])
    row = lax.broadcasted_iota(I32, hid.shape, 0)
    hid = jnp.where(row < nc - 1, hid, 0.0)
    o_ref[0, 0, 0] = _dot(hid.astype(BF16), w2_ref[0]).astype(BF16)


def _compress(kvc16, pes, w1s, b1s, w2s):
    b, _, g, nc, w = kvc16.shape
    return pl.pallas_call(
        _compress_kernel,
        grid=(b, 2, g),
        in_specs=[pl.BlockSpec((1, 1, 1, nc, w), lambda i, j, k: (i, j, k, 0, 0)),
                  pl.BlockSpec((1, 2, w), lambda i, j, k: (j, 0, 0)),
                  pl.BlockSpec((1, 2, w, CMP_HID), lambda i, j, k: (j, 0, 0, 0)),
                  pl.BlockSpec((1, 1, CMP_HID), lambda i, j, k: (j, 0, 0)),
                  pl.BlockSpec((1, CMP_HID, LANES), lambda i, j, k: (j, 0, 0))],
        out_specs=pl.BlockSpec((1, 1, 1, nc, LANES), lambda i, j, k: (i, j, k, 0, 0)),
        out_shape=jax.ShapeDtypeStruct((b, 2, g, nc, LANES), BF16),
        compiler_params=pltpu.CompilerParams(dimension_semantics=("parallel",) * 3,
                                             vmem_limit_bytes=VMEM_LIMIT),
        name="compress",
    )(kvc16, pes, w1s, b1s, w2s)


def _softmax_rows(s):
    m = jnp.max(s, axis=1, keepdims=True)
    p = jnp.exp(s - m)
    return p, jnp.sum(p, axis=1, keepdims=True)


def _nsa_kernel(qp_ref, gt_ref, kc_ref, vc_ref, ksa_ref, vs_ref, kwa_ref, vw_ref, ovt_ref,
                o_ref, *, n_sel, top_n):
    g = pl.program_id(1)
    t0 = pl.program_id(2) * Q_BLOCK
    rows = GQA_R * Q_BLOCK
    nc = kc_ref.shape[3]
    qp = qp_ref[0, 0].reshape(rows, LANES)
    row = lax.broadcasted_iota(I32, (rows, 1), 0)
    tq = t0 + (row & (Q_BLOCK - 1))
    tf = tq.astype(F32)
    r = row >> 7
    slope = jnp.where(r == 0, 0.5, jnp.where(r == 1, 0.25, jnp.where(r == 2, 0.125, 0.0625)))
    slope = slope * jnp.where(g == 0, 1.0, 0.0625)

    n_io = lax.broadcasted_iota(I32, (1, nc), 1)
    c_pos = n_io.astype(F32) * CMP_STRIDE + (CMP_LEN - 1) / 2.0
    s = _dot_nt(qp, kc_ref[0, 0, 0]) - slope * (tf - c_pos)
    s = jnp.where(n_io * CMP_STRIDE + (CMP_LEN - 1) <= tq, s, NEG)
    p, l = _softmax_rows(s)
    p = p * ((tq >= CMP_LEN - 1).astype(F32) / l)
    o_c = _dot(p.astype(BF16), vc_ref[0, 0, 0])

    ps = p[0:Q_BLOCK]
    for h in range(1, GQA_R):
        ps = ps + p[h * Q_BLOCK:(h + 1) * Q_BLOCK]
    ps_hi = ps.astype(BF16)
    ps_lo = (ps - ps_hi.astype(F32)).astype(BF16)
    ovt = ovt_ref[...]
    imp_t = _dot_nt(ovt, ps_hi) + _dot_nt(ovt, ps_lo)

    j_io = lax.broadcasted_iota(I32, (LANES, Q_BLOCK), 0)
    tl = t0 + lax.broadcasted_iota(I32, (LANES, Q_BLOCK), 1)
    cur = tl >> 6
    forced = (j_io == 0) | (j_io == cur) | (j_io == cur - 1)
    prio = jnp.where(forced, FORCE, jnp.where(j_io * SEL_BLOCK <= tl, imp_t, -1.0))
    prio = jnp.where(j_io < n_sel, prio, NEG)
    sel_t = jnp.zeros((LANES, Q_BLOCK), F32)
    for _ in range(top_n):
        m = jnp.max(prio, axis=0, keepdims=True)
        idx = jnp.min(jnp.where(prio == m, j_io, LANES), axis=0, keepdims=True)
        hit = j_io == idx
        sel_t = jnp.where(hit, 1.0, sel_t)
        prio = jnp.where(hit, NEG, prio)
    bias = ((sel_t.T - 1.0) * MASK_BIAS).astype(BF16)
    qa = jnp.concatenate([qp, jnp.concatenate([bias] * GQA_R, axis=0)], axis=1)

    def sel_step(c, carry, diagonal):
        m, l, acc = carry
        start = pl.multiple_of(c * SEL_CHUNK, SEL_CHUNK)
        s = _dot_nt(qa, ksa_ref[0, 0, pl.ds(start, SEL_CHUNK), :])
        if diagonal:
            kpos = start + lax.broadcasted_iota(I32, (1, SEL_CHUNK), 1)
            s = jnp.where(kpos <= tq, s, -MASK_BIAS)
        m_new = jnp.maximum(m, jnp.max(s, axis=1, keepdims=True))
        alpha = jnp.exp(m - m_new)
        p = jnp.exp(s - m_new)
        l = alpha * l + jnp.sum(p, axis=1, keepdims=True)
        acc = alpha * acc + _dot(p.astype(BF16), vs_ref[0, 0, pl.ds(start, SEL_CHUNK), :])
        return m_new, l, acc

    last = t0 // SEL_CHUNK
    init = (jnp.full((rows, 1), NEG, F32), jnp.zeros((rows, 1), F32), jnp.zeros((rows, LANES), F32))
    carry = lax.fori_loop(0, last, lambda c, cr: sel_step(c, cr, False), init)
    _, l_s, acc_s = sel_step(last, carry, True)
    o_s = acc_s / l_s

    ws = pl.multiple_of(jnp.maximum(t0 - WINDOW, 0), Q_BLOCK)
    s = _dot_nt(qp, kwa_ref[0, 0, pl.ds(ws, WIN_KEYS), :])
    dist = tq - (ws + lax.broadcasted_iota(I32, (1, WIN_KEYS), 1))
    s = jnp.where((dist >= 0) & (dist < WINDOW), s, NEG)
    p, l = _softmax_rows(s)
    o_w = _dot((p / l).astype(BF16), vw_ref[0, 0, pl.ds(ws, WIN_KEYS), :])

    gt = gt_ref[0, 0, 0]
    o = gt[:, 0:1] * o_c + gt[:, 1:2] * o_s + gt[:, 2:3] * o_w
    o_ref[0, 0] = o[:, :HEAD_DIM].reshape(GQA_R, Q_BLOCK, HEAD_DIM).astype(BF16)


def _nsa(qp, gt, kvcmp, ksa, vs, kwa, vw, ovt, n_sel):
    b, g, r, s, _ = qp.shape
    nc = kvcmp.shape[3]
    kvspec = lambda w: pl.BlockSpec((1, 1, s, w), lambda i, j, k: (i, j, 0, 0))
    kern = functools.partial(_nsa_kernel, n_sel=n_sel, top_n=min(SEL_TOPN, n_sel))
    return pl.pallas_call(
        kern,
        grid=(b, g, s // Q_BLOCK),
        in_specs=[pl.BlockSpec((1, 1, r, Q_BLOCK, LANES), lambda i, j, k: (i, j, 0, k, 0)),
                  pl.BlockSpec((1, 1, 1, r * Q_BLOCK, N_NSA_BRANCH), lambda i, j, k: (i, j, k, 0, 0)),
                  pl.BlockSpec((1, 1, 1, nc, LANES), lambda i, j, k: (i, 0, j, 0, 0)),
                  pl.BlockSpec((1, 1, 1, nc, LANES), lambda i, j, k: (i, 1, j, 0, 0)),
                  kvspec(2 * LANES), kvspec(LANES), kvspec(LANES), kvspec(LANES),
                  pl.BlockSpec(ovt.shape, lambda i, j, k: (0, 0))],
        out_specs=pl.BlockSpec((1, 1, r, Q_BLOCK, HEAD_DIM), lambda i, j, k: (i, j, 0, k, 0)),
        out_shape=jax.ShapeDtypeStruct((b, g, r, s, HEAD_DIM), BF16),
        compiler_params=pltpu.CompilerParams(dimension_semantics=("parallel", "parallel", "arbitrary"),
                                             vmem_limit_bytes=VMEM_LIMIT),
        name="nsa",
    )(qp, gt, kvcmp, kvcmp, ksa, vs, kwa, vw, ovt)


def _merge_kernel(x_ref, ya_ref, ga_ref, gbyb_ref, wba_ref, wout_ref, gmoe_ref, wrh_ref, wrl_ref, br_ref,
                  x1_ref, hm_ref, rw_ref, ri_ref, cnt_ref, carry_ref):
    tm = x_ref.shape[0]

    @pl.when(pl.program_id(0) == 0)
    def _():
        carry_ref[...] = jnp.zeros_like(carry_ref)

    merged = ga_ref[...].astype(F32) * _dot(ya_ref[...], wba_ref[...]) + gbyb_ref[...].astype(F32)
    x1 = x_ref[...] + _dot(merged.astype(BF16), wout_ref[...])
    x1_ref[...] = x1
    hm = _rms(x1, gmoe_ref[...])
    hm_ref[...] = hm

    hh = hm.astype(BF16)
    hl = (hm - hh.astype(F32)).astype(BF16)
    wrh = wrh_ref[...]
    logits = _dot(hh, wrh) + _dot(hl, wrh) + _dot(hh, wrl_ref[...]) + br_ref[...]
    lane = lax.broadcasted_iota(I32, (tm, LANES), 1)
    lg = jnp.where(lane < N_EXPERTS, logits, NEG)
    vals, idxs = [], []
    for _ in range(TOP_K):
        m = jnp.max(lg, axis=1, keepdims=True)
        idx = jnp.min(jnp.where(lg == m, lane, LANES), axis=1, keepdims=True)
        vals.append(m)
        idxs.append(idx)
        lg = jnp.where(lane == idx, NEG, lg)
    ex = [jnp.exp(v - vals[0]) for v in vals]
    den = ex[0] + ex[1] + ex[2] + ex[3]

    hits = [lane == idx for idx in idxs]
    multi = jnp.zeros((tm, LANES), F32)
    for h in hits:
        multi = jnp.where(h, 1.0, multi)
    ltri = (lax.broadcasted_iota(I32, (tm, tm), 0) > lax.broadcasted_iota(I32, (tm, tm), 1))
    carry = carry_ref[0:1, :]
    cum = _dot(jnp.where(ltri, 1.0, 0.0).astype(BF16), multi.astype(BF16)) + carry
    rw = jnp.zeros((tm, LANES), F32)
    ri = jnp.zeros((tm, LANES), I32)
    for k in range(TOP_K):
        rank = jnp.sum(jnp.where(hits[k], cum, 0.0), axis=1, keepdims=True).astype(I32)
        rw = jnp.where(lane == k, ex[k] / den, rw)
        ri = jnp.where(lane == k, idxs[k], jnp.where(lane == TOP_K + k, rank, ri))
    rw_ref[...] = rw
    ri_ref[...] = ri
    new_carry = carry + jnp.sum(multi, axis=0, keepdims=True)
    carry_ref[...] = jnp.broadcast_to(new_carry, carry_ref.shape)
    cnt_ref[...] = jnp.broadcast_to(new_carry, cnt_ref.shape)


def _merge(x2, ya, ga, gbyb, wba, wout, gmoe, wrh, wrl, br, tm):
    n = x2.shape[0]
    full = lambda a: pl.BlockSpec(a.shape, lambda i: (0,) * a.ndim)
    rows = lambda w: pl.BlockSpec((tm, w), lambda i: (i, 0))
    return pl.pallas_call(
        _merge_kernel,
        grid=(n // tm,),
        in_specs=[rows(D_MODEL), rows(ATTN_WIDTH), rows(D_MODEL), rows(D_MODEL),
                  full(wba), full(wout), full(gmoe), full(wrh), full(wrl), full(br)],
        out_specs=[rows(D_MODEL), rows(D_MODEL), rows(LANES), rows(LANES),
                   pl.BlockSpec((8, LANES), lambda i: (0, 0))],
        out_shape=[jax.ShapeDtypeStruct((n, D_MODEL), F32),
                   jax.ShapeDtypeStruct((n, D_MODEL), F32),
                   jax.ShapeDtypeStruct((n, LANES), F32),
                   jax.ShapeDtypeStruct((n, LANES), I32),
                   jax.ShapeDtypeStruct((8, LANES), F32)],
        scratch_shapes=[pltpu.VMEM((8, LANES), F32)],
        compiler_params=pltpu.CompilerParams(dimension_semantics=("arbitrary",),
                                             vmem_limit_bytes=VMEM_LIMIT),
        name="merge",
    )(x2, ya, ga, gbyb, wba, wout, gmoe, wrh, wrl, br)


def _row_copy(src, i, dst, d, sem):
    return pltpu.make_async_copy(src.at[pl.ds(i, 1)], dst.at[pl.ds(d, 1)], sem)


def _dispatch_kernel(dest_ref, hm_ref, xs_in_ref, xs_ref, sem):
    del xs_in_ref
    tm = hm_ref.shape[0]

    def issue(i, c):
        for k in range(TOP_K):
            _row_copy(hm_ref, i, xs_ref, dest_ref[i * TOP_K + k], sem).start()
        return c

    lax.fori_loop(0, tm, issue, 0)

    def drain(i, c):
        for k in range(TOP_K):
            _row_copy(hm_ref, 0, xs_ref, 0, sem).wait()
        return c

    lax.fori_loop(0, tm, drain, 0)


def _dispatch(dest_flat, hm, xs_zero, tm):
    n = hm.shape[0]
    return pl.pallas_call(
        _dispatch_kernel,
        grid=(n // tm,),
        in_specs=[pl.BlockSpec((tm * TOP_K,), lambda i: (i,), memory_space=pltpu.SMEM),
                  pl.BlockSpec((tm, D_MODEL), lambda i: (i, 0)),
                  pl.BlockSpec(memory_space=pl.ANY)],
        out_specs=pl.BlockSpec(memory_space=pl.ANY),
        out_shape=jax.ShapeDtypeStruct(xs_zero.shape, xs_zero.dtype),
        scratch_shapes=[pltpu.SemaphoreType.DMA(())],
        input_output_aliases={2: 0},
        compiler_params=pltpu.CompilerParams(dimension_semantics=("arbitrary",),
                                             has_side_effects=True),
        name="dispatch",
    )(dest_flat, hm, xs_zero)


def _expert_kernel(be_ref, nu_ref, xs_ref, wgu_ref, bgu_ref, wd_ref, bd_ref, y_ref):
    @pl.when(pl.program_id(0) >= nu_ref[0])
    def _():
        y_ref[...] = jnp.zeros_like(y_ref)

    @pl.when(pl.program_id(0) < nu_ref[0])
    def _():
        gu = _dot(xs_ref[...].astype(BF16), wgu_ref[0]) + bgu_ref[0]
        gate = jnp.minimum(gu[:, :D_FF], SWIGLU_LIMIT)
        up = jnp.clip(gu[:, D_FF:], -SWIGLU_LIMIT, SWIGLU_LIMIT)
        act = gate * _sigmoid(SWIGLU_ALPHA * gate) * (up + 1.0)
        y_ref[...] = _dot(act.astype(BF16), wd_ref[0]) + bd_ref[0]


def _experts(blk_expert, n_used, xs, wgu, bgu, wd, bd):
    n_blocks = xs.shape[0] // MOE_BLOCK
    blk = lambda i, be, nu: (jnp.minimum(i, nu[0] - 1), 0)
    exp3 = lambda i, be, nu: (be[jnp.minimum(i, nu[0] - 1)], 0, 0)
    return pl.pallas_call(
        _expert_kernel,
        grid_spec=pltpu.PrefetchScalarGridSpec(
            num_scalar_prefetch=2,
            grid=(n_blocks,),
            in_specs=[pl.BlockSpec((MOE_BLOCK, D_MODEL), blk),
                      pl.BlockSpec((1, D_MODEL, 2 * D_FF), exp3),
                      pl.BlockSpec((1, 1, 2 * D_FF), exp3),
                      pl.BlockSpec((1, D_FF, D_MODEL), exp3),
                      pl.BlockSpec((1, 1, D_MODEL), exp3)],
            out_specs=pl.BlockSpec((MOE_BLOCK, D_MODEL), lambda i, be, nu: (i, 0))),
        out_shape=jax.ShapeDtypeStruct(xs.shape, F32),
        compiler_params=pltpu.CompilerParams(dimension_semantics=("arbitrary",),
                                             vmem_limit_bytes=VMEM_LIMIT),
        name="experts",
    )(blk_expert, n_used, xs, wgu, bgu, wd, bd)


def _combine_kernel(dest_ref, rw_ref, x1_ref, gfin_ref, y_ref, o_ref, ybuf, sem):
    tm = x1_ref.shape[0]

    def issue(i, c):
        for k in range(TOP_K):
            _row_copy(y_ref, dest_ref[i * TOP_K + k], ybuf.at[k], i, sem).start()
        return c

    lax.fori_loop(0, tm, issue, 0)

    def drain(i, c):
        for k in range(TOP_K):
            _row_copy(y_ref, 0, ybuf.at[0], 0, sem).wait()
        return c

    lax.fori_loop(0, tm, drain, 0)

    rw = rw_ref[...]
    acc = x1_ref[...]
    for k in range(TOP_K):
        acc = acc + rw[:, k:k + 1] * ybuf[k]
    o_ref[...] = _rms(acc, gfin_ref[...])


def _combine(dest_flat, rw, x1, gfin, y, tm):
    n = x1.shape[0]
    return pl.pallas_call(
        _combine_kernel,
        grid=(n // tm,),
        in_specs=[pl.BlockSpec((tm * TOP_K,), lambda i: (i,), memory_space=pltpu.SMEM),
                  pl.BlockSpec((tm, LANES), lambda i: (i, 0)),
                  pl.BlockSpec((tm, D_MODEL), lambda i: (i, 0)),
                  pl.BlockSpec((1, D_MODEL), lambda i: (0, 0)),
                  pl.BlockSpec(memory_space=pl.ANY)],
        out_specs=pl.BlockSpec((tm, D_MODEL), lambda i: (i, 0)),
        out_shape=jax.ShapeDtypeStruct((n, D_MODEL), F32),
        scratch_shapes=[pltpu.VMEM((TOP_K, tm, D_MODEL), F32), pltpu.SemaphoreType.DMA(())],
        compiler_params=pltpu.CompilerParams(dimension_semantics=("arbitrary",),
                                             vmem_limit_bytes=VMEM_LIMIT),
        name="combine",
    )(dest_flat, rw, x1, gfin, y)


def _overlap_t(nc, n_cmp, n_sel):
    cs = np.arange(n_cmp)[None, :] * CMP_STRIDE
    ss = np.arange(n_sel)[:, None] * SEL_BLOCK
    ov = np.clip(np.minimum(cs + CMP_LEN, ss + SEL_BLOCK) - np.maximum(cs, ss), 0, None) / CMP_LEN
    out = np.zeros((LANES, nc), np.float32)
    out[:n_sel, :n_cmp] = ov
    return jnp.asarray(out, BF16)


def _layer(x2, b, s, g_mix, w_in, w_ck1, b_ck1, w_ck2, pe_k, w_cv1, b_cv1, w_cv2, pe_v,
           sgu_ln_g, sgu_ln_b, w_spatial, b_spatial, w_branch_a, w_branch_b, w_out,
           g_moe, w_router, b_router, w_gate_up, b_gate_up, w_down, b_down):
    n = b * s
    nc = s // CMP_STRIDE
    n_cmp = (s - CMP_LEN) // CMP_STRIDE + 1
    n_sel = s // SEL_BLOCK
    assert nc % LANES == 0 and n_sel <= LANES and s >= WIN_KEYS and n_cmp == nc - 1
    tm = 512 if n % 512 == 0 else 256

    p0 = ATTN_WIDTH
    p1 = p0 + 6 * KV_WIDTH
    p2 = p1 + N_NSA_BRANCH * N_HEADS
    p3 = p2 + SGU_WIDTH
    p4 = p3 + SGU_WIDTH
    wq = (w_in[:, :p0] * HEAD_DIM ** -0.5).astype(BF16)
    wkv = w_in[:, p0:p1].astype(BF16)
    wng = jnp.pad(w_in[:, p1:p2], ((0, 0), (0, LANES - (p2 - p1)))).astype(BF16)
    wu = w_in[:, p2:p3].astype(BF16)
    wv = w_in[:, p3:p4].astype(BF16)
    wmg = w_in[:, p4:].astype(BF16)
    bsp = jnp.repeat(b_spatial.T, SGU_GROUP_DIM, axis=1)

    q, kv, gates, ga, gbyb = _proj(
        x2, g_mix[None], wq, wkv, wng, wu, wv, wmg, sgu_ln_g[None], sgu_ln_b[None],
        w_spatial, bsp, w_branch_b.astype(BF16), tm)

    kv6 = kv.reshape(b, s, 6, N_KV, HEAD_DIM).transpose(2, 0, 3, 1, 4)
    kvc16 = kv6[0:2].transpose(1, 0, 2, 3, 4).reshape(b, 2, N_KV, nc, CMP_STRIDE * HEAD_DIM)
    half = CMP_LEN // 2
    pes = jnp.stack([pe_k, pe_v]).reshape(2, 2, half * HEAD_DIM)
    w1s = jnp.stack([w_ck1, w_cv1]).reshape(2, 2, half * HEAD_DIM, CMP_HID).astype(BF16)
    b1s = jnp.stack([b_ck1, b_cv1])[:, None, :]
    w2s = jnp.pad(jnp.stack([w_ck2, w_cv2]), ((0, 0), (0, 0), (0, LANES - HEAD_DIM))).astype(BF16)
    kvcmp = _compress(kvc16, pes, w1s, b1s, w2s)

    pos = jnp.arange(s)
    zpad = lambda a, w: jnp.pad(a, ((0, 0),) * (a.ndim - 1) + ((0, w - a.shape[-1]),))
    kpos = jnp.broadcast_to(jnp.stack([pos // SEL_BLOCK, pos % SEL_BLOCK], -1).astype(BF16),
                            (b, N_KV, s, 2))
    onehot = jnp.broadcast_to(jax.nn.one_hot(pos // SEL_BLOCK, LANES, dtype=BF16), (b, N_KV, s, LANES))
    ksa = jnp.concatenate([zpad(jnp.concatenate([kv6[2], kpos], -1), LANES), onehot], -1)
    kwa = zpad(jnp.concatenate([kv6[4], kpos], -1), LANES)
    vs = zpad(kv6[3], LANES)
    vw = zpad(kv6[5], LANES)
    slopes = jnp.asarray(2.0 ** (-8.0 * np.arange(1, N_HEADS + 1) / N_HEADS), F32)
    qcols = (slopes[:, None] * jnp.asarray([SEL_BLOCK, 1.0], F32)).astype(BF16).reshape(N_KV, GQA_R, 1, 2)
    qh = q.reshape(b, s, N_KV, GQA_R, HEAD_DIM).transpose(0, 2, 3, 1, 4)
    qp = zpad(jnp.concatenate([qh, jnp.broadcast_to(qcols, (b, N_KV, GQA_R, s, 2))], -1), LANES)
    gt = gates[:, :N_NSA_BRANCH * N_HEADS].reshape(b, s // Q_BLOCK, Q_BLOCK, N_KV, GQA_R, N_NSA_BRANCH)
    gt = gt.transpose(0, 3, 1, 4, 2, 5).reshape(b, N_KV, s // Q_BLOCK, GQA_R * Q_BLOCK, N_NSA_BRANCH)

    ya = _nsa(qp, gt, kvcmp, ksa, vs, kwa, vw, _overlap_t(nc, n_cmp, n_sel), n_sel)
    ya = ya.transpose(0, 3, 1, 2, 4).reshape(n, ATTN_WIDTH)

    wr = jnp.pad(w_router, ((0, 0), (0, LANES - N_EXPERTS)))
    wrh = wr.astype(BF16)
    wrl = (wr - wrh.astype(F32)).astype(BF16)
    br = jnp.pad(b_router, (0, LANES - N_EXPERTS))[None]
    x1, hm, rw, ri, cnt = _merge(x2, ya, ga, gbyb, w_branch_a.astype(BF16), w_out.astype(BF16),
                                 g_moe[None], wrh, wrl, br, tm)

    counts = cnt[0, :N_EXPERTS].astype(I32)
    padded = (counts + MOE_BLOCK - 1) // MOE_BLOCK * MOE_BLOCK
    pad_end = jnp.cumsum(padded)
    pad_start = pad_end - padded
    dest = (pad_start[ri[:, :TOP_K]] + ri[:, TOP_K:2 * TOP_K]).reshape(-1)
    n_blocks = -(-(n * TOP_K) // MOE_BLOCK) + N_EXPERTS
    blk_expert = jnp.minimum(jnp.searchsorted(pad_end, jnp.arange(n_blocks) * MOE_BLOCK, side='right'),
                             N_EXPERTS - 1).astype(I32)
    n_used = (pad_end[-1:] // MOE_BLOCK).astype(I32)

    tg = 256
    xs = _dispatch(dest, hm, jnp.zeros((n_blocks * MOE_BLOCK, D_MODEL), F32), tg)
    y = _experts(blk_expert, n_used, xs, w_gate_up.astype(BF16), b_gate_up[:, None, :],
                 w_down.astype(BF16), b_down[:, None, :])
    return dest, rw, x1, y, tg


def kernel(x, g_mix, w_in, w_ck1, b_ck1, w_ck2, pe_k, w_cv1, b_cv1, w_cv2, pe_v, sgu_ln_g, sgu_ln_b, w_spatial, b_spatial, w_branch_a, w_branch_b, w_out, g_moe, w_router, b_router, w_gate_up, b_gate_up, w_down, b_down, g_final):
    b, s, d = x.shape
    assert g_mix.shape[0] == 1, "the final rmsnorm is fused into the single layer's combine step"
    l = 0
    dest, rw, x1, y, tg = _layer(
        x.reshape(b * s, d), b, s, g_mix[l], w_in[l], w_ck1[l], b_ck1[l], w_ck2[l], pe_k[l], w_cv1[l],
        b_cv1[l], w_cv2[l], pe_v[l], sgu_ln_g[l], sgu_ln_b[l], w_spatial[l], b_spatial[l], w_branch_a[l],
        w_branch_b[l], w_out[l], g_moe[l], w_router[l], b_router[l], w_gate_up[l], b_gate_up[l],
        w_down[l], b_down[l])
    return _combine(dest, rw, x1, g_final[None], y, tg).reshape(b, s, d)
```

```python
import functools

import numpy as np
import jax
import jax.numpy as jnp
from jax import lax
from jax.experimental import pallas as pl
from jax.experimental.pallas import tpu as pltpu

D_MODEL = 1024
N_HEADS = 8
HEAD_DIM = 64
N_KV = 2
GQA_R = N_HEADS // N_KV
ATTN_WIDTH = N_HEADS * HEAD_DIM
KV_WIDTH = N_KV * HEAD_DIM
CMP_LEN = 32
CMP_STRIDE = 16
CMP_HID = 128
SEL_BLOCK = 64
SEL_TOPN = 16
WINDOW = 512
Q_BLOCK = 128
N_NSA_BRANCH = 3
SGU_WIDTH = 512
N_GROUPS_SGU = 8
SGU_GROUP_DIM = SGU_WIDTH // N_GROUPS_SGU
CHUNK = 128
N_EXPERTS = 32
TOP_K = 4
D_FF = D_MODEL
SWIGLU_LIMIT = 7.0
SWIGLU_ALPHA = 1.702
MOE_BLOCK = 512
RMS_EPS = 1e-5
LN_EPS = 1e-5
NEG = -1e30
FORCE = 1e4

LANES = 128
MASK_BIAS = 1e9
KEY_CHUNK = 128
SEL_SLOTS = 4
VMEM_LIMIT = 56 * 1024 * 1024

F32 = jnp.float32
BF16 = jnp.bfloat16
I32 = jnp.int32

_NT = (((1,), (1,)), ((), ()))


def _dot(a, b):
    return jnp.dot(a, b, preferred_element_type=F32)


def _dot_nt(a, b):
    return lax.dot_general(a, b, _NT, preferred_element_type=F32)


def _sigmoid(x):
    return 1.0 / (1.0 + jnp.exp(-x))


def _gelu(x):
    c = np.float32(np.sqrt(2.0 / np.pi))
    return 0.5 * x * (1.0 + jnp.tanh(c * (x + 0.044715 * (x * x * x))))


def _rms(x, g):
    return x * lax.rsqrt(jnp.mean(x * x, axis=-1, keepdims=True) + RMS_EPS) * g


def _proj_kernel(x_ref, gmix_ref, wq_ref, wkv_ref, wng_ref, wu_ref, wv_ref, wmg_ref,
                 lng_ref, lnb_ref, wsp_ref, bsp_ref, wbb_ref,
                 q_ref, kv_ref, gates_ref, ga_ref, gbyb_ref):
    tm = x_ref.shape[0]
    hb = _rms(x_ref[...], gmix_ref[...]).astype(BF16)
    q_ref[...] = _dot(hb, wq_ref[...]).astype(BF16)
    kv_ref[...] = _dot(hb, wkv_ref[...]).astype(BF16)
    gates_ref[...] = _sigmoid(_dot(hb, wng_ref[...]))
    mg = _dot(hb, wmg_ref[...])
    ga_ref[...] = _sigmoid(mg[:, :D_MODEL]).astype(BF16)

    u = _gelu(_dot(hb, wu_ref[...]))
    v = _gelu(_dot(hb, wv_ref[...]))
    mu = jnp.mean(v, axis=-1, keepdims=True)
    vc = v - mu
    var = jnp.mean(vc * vc, axis=-1, keepdims=True)
    vln = (vc * lax.rsqrt(var + LN_EPS) * lng_ref[...] + lnb_ref[...]).astype(BF16)

    row = lax.broadcasted_iota(I32, (CHUNK, CHUNK), 0)
    col = lax.broadcasted_iota(I32, (CHUNK, CHUNK), 1)
    tril = row >= col
    wsp = [jnp.where(tril, wsp_ref[g], 0.0).astype(BF16) for g in range(N_GROUPS_SGU)]
    low_half = col < SGU_GROUP_DIM
    bsp = bsp_ref[...]
    chunks = []
    for c in range(tm // CHUNK):
        rs = slice(c * CHUNK, (c + 1) * CHUNK)
        parts = []
        for p in range(SGU_WIDTH // LANES):
            cs = slice(p * LANES, (p + 1) * LANES)
            vblk = vln[rs, cs]
            mixed = jnp.where(low_half, _dot(wsp[2 * p], vblk), _dot(wsp[2 * p + 1], vblk))
            parts.append(u[rs, cs] * (mixed + bsp[:, cs]))
        chunks.append(jnp.concatenate(parts, axis=1))
    yb = jnp.concatenate(chunks, axis=0).astype(BF16)
    gbyb_ref[...] = (_sigmoid(mg[:, D_MODEL:]) * _dot(yb, wbb_ref[...])).astype(BF16)


def _proj(x2, gmix, wq, wkv, wng, wu, wv, wmg, lng, lnb, wsp, bsp, wbb, tm):
    n = x2.shape[0]
    full = lambda a: pl.BlockSpec(a.shape, lambda i: (0,) * a.ndim)
    rows = lambda w: pl.BlockSpec((tm, w), lambda i: (i, 0))
    ins = (x2, gmix, wq, wkv, wng, wu, wv, wmg, lng, lnb, wsp, bsp, wbb)
    return pl.pallas_call(
        _proj_kernel,
        grid=(n // tm,),
        in_specs=[rows(D_MODEL)] + [full(a) for a in ins[1:]],
        out_specs=[rows(ATTN_WIDTH), rows(6 * KV_WIDTH), rows(LANES), rows(D_MODEL), rows(D_MODEL)],
        out_shape=[jax.ShapeDtypeStruct((n, ATTN_WIDTH), BF16),
                   jax.ShapeDtypeStruct((n, 6 * KV_WIDTH), BF16),
                   jax.ShapeDtypeStruct((n, LANES), F32),
                   jax.ShapeDtypeStruct((n, D_MODEL), BF16),
                   jax.ShapeDtypeStruct((n, D_MODEL), BF16)],
        compiler_params=pltpu.CompilerParams(dimension_semantics=("parallel",),
                                             vmem_limit_bytes=VMEM_LIMIT),
        name="proj",
    )(*ins)


def _compress_kernel(x_ref, pe_ref, w1_ref, b1_ref, w2_ref, o_ref):
    nc = x_ref.shape[3]
    x = x_ref[0, 0, 0].astype(F32)
    xa = (x + pe_ref[0, 0:1, :]).astype(BF16)
    xb = (x + pe_ref[0, 1:2, :]).astype(BF16)
    za = _dot(xa, w1_ref[0, 0])
    zb = _dot(xb, w1_ref[0, 1])
    hid = _gelu(za + pltpu.roll(zb, nc - 1, 0) + b1_ref[0])
    row = lax.broadcasted_iota(I32, hid.shape, 0)
    hid = jnp.where(row < nc - 1, hid, 0.0)
    o_ref[0, 0, 0] = _dot(hid.astype(BF16), w2_ref[0]).astype(BF16)


def _compress(kvc16, pes, w1s, b1s, w2s):
    b, _, g, nc, w = kvc16.shape
    return pl.pallas_call(
        _compress_kernel,
        grid=(b, 2, g),
        in_specs=[pl.BlockSpec((1, 1, 1, nc, w), lambda i, j, k: (i, j, k, 0, 0)),
                  pl.BlockSpec((1, 2, w), lambda i, j, k: (j, 0, 0)),
                  pl.BlockSpec((1, 2, w, CMP_HID), lambda i, j, k: (j, 0, 0, 0)),
                  pl.BlockSpec((1, 1, CMP_HID), lambda i, j, k: (j, 0, 0)),
                  pl.BlockSpec((1, CMP_HID, LANES), lambda i, j, k: (j, 0, 0))],
        out_specs=pl.BlockSpec((1, 1, 1, nc, LANES), lambda i, j, k: (i, j, k, 0, 0)),
        out_shape=jax.ShapeDtypeStruct((b, 2, g, nc, LANES), BF16),
        compiler_params=pltpu.CompilerParams(dimension_semantics=("parallel",) * 3,
                                             vmem_limit_bytes=VMEM_LIMIT),
        name="compress",
    )(kvc16, pes, w1s, b1s, w2s)


def _pairs():
    return [slice(pr * 2 * Q_BLOCK, (pr + 1) * 2 * Q_BLOCK) for pr in range(GQA_R // 2)]


def _attend(ks, vts, masks, q_ref, m_ref, l_ref, acc_ref):
    pairs = _pairs()
    scores = [[_dot(k, q_ref[:, ls]) for k in ks] for ls in pairs]
    for pr, ls in enumerate(pairs):
        sb = [masks[u](scores[pr][u], ls) for u in range(len(ks))]
        m_old = m_ref[0:1, ls]
        m_new = m_old
        for x in sb:
            m_new = jnp.maximum(m_new, jnp.max(x, axis=0, keepdims=True))
        alpha = jnp.exp(m_old - m_new)
        lsum = jnp.zeros_like(m_old)
        pv = jnp.zeros((HEAD_DIM, ls.stop - ls.start), F32)
        for u, x in enumerate(sb):
            p = jnp.exp(x - m_new)
            lsum = lsum + jnp.sum(p, axis=0, keepdims=True)
            pv = pv + _dot(vts[u], p.astype(BF16))
        l_ref[0:1, ls] = alpha * l_ref[0:1, ls] + lsum
        acc_ref[:, ls] = alpha * acc_ref[:, ls] + pv
        m_ref[0:1, ls] = m_new


def _nsa_kernel(qt_ref, gt_ref, kc_ref, vct_ref, ksa_ref, vst_ref, kwa_ref, vwt_ref, ovt_ref, pair_ref,
                o_ref, qa_ref, m_ref, l_ref, acc_ref, act_ref, *, n_sel, top_n):
    g = pl.program_id(1)
    i = pl.program_id(2)
    t0 = i * Q_BLOCK
    width = GQA_R * Q_BLOCK
    n_chunks = ksa_ref.shape[2] // KEY_CHUNK
    n_cchunks = kc_ref.shape[3] // KEY_CHUNK
    lane = lax.broadcasted_iota(I32, (1, width), 1)
    tq = t0 + (lane & (Q_BLOCK - 1))
    tf = tq.astype(F32)
    r = lane >> 7
    slope = jnp.where(r == 0, 0.5, jnp.where(r == 1, 0.25, jnp.where(r == 2, 0.125, 0.0625)))
    slope = slope * jnp.where(g == 0, 1.0, 0.0625)
    sub = lax.broadcasted_iota(I32, (KEY_CHUNK, 1), 0)
    pairs = _pairs()
    qt = qt_ref.at[0, 0, 0]

    def reset():
        m_ref[...] = jnp.full(m_ref.shape, NEG, F32)
        l_ref[...] = jnp.zeros(l_ref.shape, F32)
        acc_ref[...] = jnp.zeros(acc_ref.shape, F32)

    kcs = [kc_ref[0, 0, 0, u * KEY_CHUNK:(u + 1) * KEY_CHUNK, :] for u in range(n_cchunks)]
    sc = [[_dot(k, qt[:, ls]) for k in kcs] for ls in pairs]
    o_c, ps = [], []
    for pr, ls in enumerate(pairs):
        sb = []
        for u in range(n_cchunks):
            n_col = u * KEY_CHUNK + sub
            c_pos = n_col.astype(F32) * CMP_STRIDE + (CMP_LEN - 1) / 2.0
            s = sc[pr][u] - slope[:, ls] * (tf[:, ls] - c_pos)
            sb.append(jnp.where(n_col * CMP_STRIDE + (CMP_LEN - 1) <= tq[:, ls], s, NEG))
        m = sb[0].max(axis=0, keepdims=True)
        for x in sb[1:]:
            m = jnp.maximum(m, jnp.max(x, axis=0, keepdims=True))
        pb = [jnp.exp(x - m) for x in sb]
        l = pb[0].sum(axis=0, keepdims=True)
        for p in pb[1:]:
            l = l + jnp.sum(p, axis=0, keepdims=True)
        w = (tq[:, ls] >= CMP_LEN - 1).astype(F32) / l
        oc = _dot(vct_ref[0, 0, 0], pb[0].astype(BF16))
        for u in range(1, n_cchunks):
            oc = oc + _dot(vct_ref[0, 0, u], pb[u].astype(BF16))
        o_c.append(oc * w)
        ps.append([p[:, :Q_BLOCK] * w[:, :Q_BLOCK] + p[:, Q_BLOCK:] * w[:, Q_BLOCK:] for p in pb])
    imp2 = jnp.zeros((LANES, 2 * Q_BLOCK), F32)
    for u in range(n_cchunks):
        psu = ps[0][u] + ps[1][u]
        hi = psu.astype(BF16)
        lo = (psu - hi.astype(F32)).astype(BF16)
        imp2 = imp2 + _dot(ovt_ref[u], jnp.concatenate([hi, lo], axis=1))
    imp_t = imp2[:, :Q_BLOCK] + imp2[:, Q_BLOCK:]

    j_io = lax.broadcasted_iota(I32, (LANES, Q_BLOCK), 0)
    tl = t0 + lax.broadcasted_iota(I32, (LANES, Q_BLOCK), 1)
    cur = tl >> 6
    forced = (j_io == 0) | (j_io == cur) | (j_io == cur - 1)
    prio = jnp.where(forced, FORCE, jnp.where(j_io * SEL_BLOCK <= tl, imp_t, -1.0))
    prio = jnp.where(j_io < n_sel, prio, NEG)
    sel_t = jnp.zeros((LANES, Q_BLOCK), F32)
    for _ in range(top_n):
        m = jnp.max(prio, axis=0, keepdims=True)
        idx = jnp.min(jnp.where(prio == m, j_io, LANES), axis=0, keepdims=True)
        hit = j_io == idx
        sel_t = jnp.where(hit, 1.0, sel_t)
        prio = jnp.where(hit, NEG, prio)

    qa_ref[0:LANES, :] = qt[...]
    bias_t = ((sel_t - 1.0) * MASK_BIAS).astype(BF16)
    for h in range(GQA_R):
        qa_ref[LANES:2 * LANES, h * Q_BLOCK:(h + 1) * Q_BLOCK] = bias_t

    per_block = _dot_nt(jnp.ones((8, Q_BLOCK), BF16), sel_t.astype(BF16))
    per_chunk = _dot(per_block.astype(BF16), pair_ref[...])
    n_act = jnp.int32(0)
    for c in range(n_chunks):
        act_ref[n_act] = jnp.int32(c)
        n_act = n_act + jnp.where((per_chunk[0, c] > 0.0) & (c < i), 1, 0)
    act_ref[n_act] = i
    for u in range(1, SEL_SLOTS):
        act_ref[n_act + u] = jnp.int32(-1)

    reset()
    ks, vts, masks = [], [], []
    for u in range(WINDOW // KEY_CHUNK + 1):
        c = i - WINDOW // KEY_CHUNK + u
        cl = jnp.maximum(c, 0)
        ks.append(kwa_ref[0, 0, pl.ds(pl.multiple_of(cl * KEY_CHUNK, KEY_CHUNK), KEY_CHUNK), :])
        vts.append(vwt_ref[0, 0, cl])
        kpos = jnp.where(c < 0, -2 * WINDOW, c * KEY_CHUNK) + sub

        def mask(s, ls, kpos=kpos):
            dist = tq[:, ls] - kpos
            return jnp.where((dist >= 0) & (dist < WINDOW), s, NEG)

        masks.append(mask)
    _attend(ks, vts, masks, qt, m_ref, l_ref, acc_ref)
    o_w = acc_ref[...] / l_ref[0:1, :]

    reset()

    def sel_body(k, carry):
        ks, vts, masks = [], [], []
        for u in range(SEL_SLOTS):
            c = act_ref[k * SEL_SLOTS + u]
            cl = jnp.maximum(c, 0)
            ks.append(ksa_ref[0, 0, pl.ds(pl.multiple_of(cl * KEY_CHUNK, KEY_CHUNK), KEY_CHUNK), :])
            vts.append(vst_ref[0, 0, cl])
            kpos = jnp.where(c < 0, n_chunks * KEY_CHUNK, c * KEY_CHUNK) + sub
            masks.append(lambda s, ls, kpos=kpos: jnp.where(kpos <= tq[:, ls], s, -MASK_BIAS))
        _attend(ks, vts, masks, qa_ref, m_ref, l_ref, acc_ref)
        return carry

    lax.fori_loop(0, (n_act + SEL_SLOTS) // SEL_SLOTS, sel_body, 0)
    o_s = acc_ref[...] / l_ref[0:1, :]

    gt = gt_ref[0, 0, 0]
    o = gt[0:1, :] * jnp.concatenate(o_c, axis=1) + gt[1:2, :] * o_s + gt[2:3, :] * o_w
    o_ref[0, 0, 0] = o.astype(BF16)


def _nsa(qt, gt, kvcmp, vct, ksa, vst, kwa, vwt, ovt, pair, n_sel):
    b, g, nt, _, width = qt.shape
    s = ksa.shape[2]
    nc = kvcmp.shape[3]
    tile = lambda rows: pl.BlockSpec((1, 1, 1, rows, width), lambda i, j, k: (i, j, k, 0, 0))
    keys = lambda w: pl.BlockSpec((1, 1, s, w), lambda i, j, k: (i, j, 0, 0))
    vals = lambda a: pl.BlockSpec((1, 1) + a.shape[2:], lambda i, j, k: (i, j, 0, 0, 0))
    const = lambda a: pl.BlockSpec(a.shape, lambda i, j, k: (0,) * a.ndim)
    kern = functools.partial(_nsa_kernel, n_sel=n_sel, top_n=min(SEL_TOPN, n_sel))
    return pl.pallas_call(
        kern,
        grid=(b, g, nt),
        in_specs=[tile(LANES), tile(8),
                  pl.BlockSpec((1, 1, 1, nc, LANES), lambda i, j, k: (i, 0, j, 0, 0)),
                  vals(vct), keys(2 * LANES), vals(vst), keys(LANES), vals(vwt), const(ovt), const(pair)],
        out_specs=tile(HEAD_DIM),
        out_shape=jax.ShapeDtypeStruct((b, g, nt, HEAD_DIM, width), BF16),
        scratch_shapes=[pltpu.VMEM((2 * LANES, width), BF16),
                        pltpu.VMEM((8, width), F32),
                        pltpu.VMEM((8, width), F32),
                        pltpu.VMEM((HEAD_DIM, width), F32),
                        pltpu.SMEM((s // KEY_CHUNK + SEL_SLOTS,), I32)],
        compiler_params=pltpu.CompilerParams(dimension_semantics=("parallel", "parallel", "arbitrary"),
                                             vmem_limit_bytes=VMEM_LIMIT),
        name="nsa",
    )(qt, gt, kvcmp, vct, ksa, vst, kwa, vwt, ovt, pair)


def _merge_kernel(x_ref, ya_ref, ga_ref, gbyb_ref, wba_ref, wout_ref, gmoe_ref, wrh_ref, wrl_ref, br_ref,
                  x1_ref, hm_ref, rw_ref, ri_ref, cnt_ref, carry_ref):
    tm = x_ref.shape[0]

    @pl.when(pl.program_id(0) == 0)
    def _():
        carry_ref[...] = jnp.zeros_like(carry_ref)

    merged = ga_ref[...].astype(F32) * _dot(ya_ref[...], wba_ref[...]) + gbyb_ref[...].astype(F32)
    x1 = x_ref[...] + _dot(merged.astype(BF16), wout_ref[...])
    x1_ref[...] = x1
    hm = _rms(x1, gmoe_ref[...])
    hm_ref[...] = hm

    hh = hm.astype(BF16)
    hl = (hm - hh.astype(F32)).astype(BF16)
    wrh = wrh_ref[...]
    logits = _dot(hh, wrh) + _dot(hl, wrh) + _dot(hh, wrl_ref[...]) + br_ref[...]
    lane = lax.broadcasted_iota(I32, (tm, LANES), 1)
    lg = jnp.where(lane < N_EXPERTS, logits, NEG)
    vals, idxs = [], []
    for _ in range(TOP_K):
        m = jnp.max(lg, axis=1, keepdims=True)
        idx = jnp.min(jnp.where(lg == m, lane, LANES), axis=1, keepdims=True)
        vals.append(m)
        idxs.append(idx)
        lg = jnp.where(lane == idx, NEG, lg)
    ex = [jnp.exp(v - vals[0]) for v in vals]
    den = ex[0] + ex[1] + ex[2] + ex[3]

    hits = [lane == idx for idx in idxs]
    multi = jnp.zeros((tm, LANES), F32)
    for h in hits:
        multi = jnp.where(h, 1.0, multi)
    ltri = (lax.broadcasted_iota(I32, (tm, tm), 0) > lax.broadcasted_iota(I32, (tm, tm), 1))
    carry = carry_ref[0:1, :]
    cum = _dot(jnp.where(ltri, 1.0, 0.0).astype(BF16), multi.astype(BF16)) + carry
    rw = jnp.zeros((tm, LANES), F32)
    ri = jnp.zeros((tm, LANES), I32)
    for k in range(TOP_K):
        rank = jnp.sum(jnp.where(hits[k], cum, 0.0), axis=1, keepdims=True).astype(I32)
        rw = jnp.where(lane == k, ex[k] / den, rw)
        ri = jnp.where(lane == k, idxs[k], jnp.where(lane == TOP_K + k, rank, ri))
    rw_ref[...] = rw
    ri_ref[...] = ri
    new_carry = carry + jnp.sum(multi, axis=0, keepdims=True)
    carry_ref[...] = jnp.broadcast_to(new_carry, carry_ref.shape)
    cnt_ref[...] = jnp.broadcast_to(new_carry, cnt_ref.shape)


def _merge(x2, ya, ga, gbyb, wba, wout, gmoe, wrh, wrl, br, tm):
    n = x2.shape[0]
    full = lambda a: pl.BlockSpec(a.shape, lambda i: (0,) * a.ndim)
    rows = lambda w: pl.BlockSpec((tm, w), lambda i: (i, 0))
    return pl.pallas_call(
        _merge_kernel,
        grid=(n // tm,),
        in_specs=[rows(D_MODEL), rows(ATTN_WIDTH), rows(D_MODEL), rows(D_MODEL),
                  full(wba), full(wout), full(gmoe), full(wrh), full(wrl), full(br)],
        out_specs=[rows(D_MODEL), rows(D_MODEL), rows(LANES), rows(LANES),
                   pl.BlockSpec((8, LANES), lambda i: (0, 0))],
        out_shape=[jax.ShapeDtypeStruct((n, D_MODEL), F32),
                   jax.ShapeDtypeStruct((n, D_MODEL), F32),
                   jax.ShapeDtypeStruct((n, LANES), F32),
                   jax.ShapeDtypeStruct((n, LANES), I32),
                   jax.ShapeDtypeStruct((8, LANES), F32)],
        scratch_shapes=[pltpu.VMEM((8, LANES), F32)],
        compiler_params=pltpu.CompilerParams(dimension_semantics=("arbitrary",),
                                             vmem_limit_bytes=VMEM_LIMIT),
        name="merge",
    )(x2, ya, ga, gbyb, wba, wout, gmoe, wrh, wrl, br)


def _row_copy(src, i, dst, d, sem):
    return pltpu.make_async_copy(src.at[pl.ds(i, 1)], dst.at[pl.ds(d, 1)], sem)


def _dispatch_kernel(dest_ref, hm_ref, xs_in_ref, xs_ref, sem):
    del xs_in_ref
    tm = hm_ref.shape[0]

    def issue(i, c):
        for k in range(TOP_K):
            _row_copy(hm_ref, i, xs_ref, dest_ref[i * TOP_K + k], sem).start()
        return c

    lax.fori_loop(0, tm, issue, 0)

    def drain(i, c):
        for k in range(TOP_K):
            _row_copy(hm_ref, 0, xs_ref, 0, sem).wait()
        return c

    lax.fori_loop(0, tm, drain, 0)


def _dispatch(dest_flat, hm, xs_zero, tm):
    n = hm.shape[0]
    return pl.pallas_call(
        _dispatch_kernel,
        grid=(n // tm,),
        in_specs=[pl.BlockSpec((tm * TOP_K,), lambda i: (i,), memory_space=pltpu.SMEM),
                  pl.BlockSpec((tm, D_MODEL), lambda i: (i, 0)),
                  pl.BlockSpec(memory_space=pl.ANY)],
        out_specs=pl.BlockSpec(memory_space=pl.ANY),
        out_shape=jax.ShapeDtypeStruct(xs_zero.shape, xs_zero.dtype),
        scratch_shapes=[pltpu.SemaphoreType.DMA(())],
        input_output_aliases={2: 0},
        compiler_params=pltpu.CompilerParams(dimension_semantics=("arbitrary",),
                                             has_side_effects=True),
        name="dispatch",
    )(dest_flat, hm, xs_zero)


def _expert_kernel(be_ref, nu_ref, xs_ref, wgu_ref, bgu_ref, wd_ref, bd_ref, y_ref):
    @pl.when(pl.program_id(0) >= nu_ref[0])
    def _():
        y_ref[...] = jnp.zeros_like(y_ref)

    @pl.when(pl.program_id(0) < nu_ref[0])
    def _():
        gu = _dot(xs_ref[...].astype(BF16), wgu_ref[0]) + bgu_ref[0]
        gate = jnp.minimum(gu[:, :D_FF], SWIGLU_LIMIT)
        up = jnp.clip(gu[:, D_FF:], -SWIGLU_LIMIT, SWIGLU_LIMIT)
        act = gate * _sigmoid(SWIGLU_ALPHA * gate) * (up + 1.0)
        y_ref[...] = _dot(act.astype(BF16), wd_ref[0]) + bd_ref[0]


def _experts(blk_expert, n_used, xs, wgu, bgu, wd, bd):
    n_blocks = xs.shape[0] // MOE_BLOCK
    blk = lambda i, be, nu: (jnp.minimum(i, nu[0] - 1), 0)
    exp3 = lambda i, be, nu: (be[jnp.minimum(i, nu[0] - 1)], 0, 0)
    return pl.pallas_call(
        _expert_kernel,
        grid_spec=pltpu.PrefetchScalarGridSpec(
            num_scalar_prefetch=2,
            grid=(n_blocks,),
            in_specs=[pl.BlockSpec((MOE_BLOCK, D_MODEL), blk),
                      pl.BlockSpec((1, D_MODEL, 2 * D_FF), exp3),
                      pl.BlockSpec((1, 1, 2 * D_FF), exp3),
                      pl.BlockSpec((1, D_FF, D_MODEL), exp3),
                      pl.BlockSpec((1, 1, D_MODEL), exp3)],
            out_specs=pl.BlockSpec((MOE_BLOCK, D_MODEL), lambda i, be, nu: (i, 0))),
        out_shape=jax.ShapeDtypeStruct(xs.shape, F32),
        compiler_params=pltpu.CompilerParams(dimension_semantics=("arbitrary",),
                                             vmem_limit_bytes=VMEM_LIMIT),
        name="experts",
    )(blk_expert, n_used, xs, wgu, bgu, wd, bd)


def _combine_kernel(dest_ref, rw_ref, x1_ref, gfin_ref, y_ref, o_ref, ybuf, sem):
    tm = x1_ref.shape[0]

    def issue(i, c):
        for k in range(TOP_K):
            _row_copy(y_ref, dest_ref[i * TOP_K + k], ybuf.at[k], i, sem).start()
        return c

    lax.fori_loop(0, tm, issue, 0)

    def drain(i, c):
        for k in range(TOP_K):
            _row_copy(y_ref, 0, ybuf.at[0], 0, sem).wait()
        return c

    lax.fori_loop(0, tm, drain, 0)

    rw = rw_ref[...]
    acc = x1_ref[...]
    for k in range(TOP_K):
        acc = acc + rw[:, k:k + 1] * ybuf[k]
    o_ref[...] = _rms(acc, gfin_ref[...])


def _combine(dest_flat, rw, x1, gfin, y, tm):
    n = x1.shape[0]
    return pl.pallas_call(
        _combine_kernel,
        grid=(n // tm,),
        in_specs=[pl.BlockSpec((tm * TOP_K,), lambda i: (i,), memory_space=pltpu.SMEM),
                  pl.BlockSpec((tm, LANES), lambda i: (i, 0)),
                  pl.BlockSpec((tm, D_MODEL), lambda i: (i, 0)),
                  pl.BlockSpec((1, D_MODEL), lambda i: (0, 0)),
                  pl.BlockSpec(memory_space=pl.ANY)],
        out_specs=pl.BlockSpec((tm, D_MODEL), lambda i: (i, 0)),
        out_shape=jax.ShapeDtypeStruct((n, D_MODEL), F32),
        scratch_shapes=[pltpu.VMEM((TOP_K, tm, D_MODEL), F32), pltpu.SemaphoreType.DMA(())],
        compiler_params=pltpu.CompilerParams(dimension_semantics=("arbitrary",),
                                             vmem_limit_bytes=VMEM_LIMIT),
        name="combine",
    )(dest_flat, rw, x1, gfin, y)


def _overlap_t(nc, n_cmp, n_sel):
    cs = np.arange(n_cmp)[None, :] * CMP_STRIDE
    ss = np.arange(n_sel)[:, None] * SEL_BLOCK
    ov = np.clip(np.minimum(cs + CMP_LEN, ss + SEL_BLOCK) - np.maximum(cs, ss), 0, None) / CMP_LEN
    out = np.zeros((LANES, nc), np.float32)
    out[:n_sel, :n_cmp] = ov
    return jnp.asarray(out, BF16)


def _layer(x2, b, s, g_mix, w_in, w_ck1, b_ck1, w_ck2, pe_k, w_cv1, b_cv1, w_cv2, pe_v,
           sgu_ln_g, sgu_ln_b, w_spatial, b_spatial, w_branch_a, w_branch_b, w_out,
           g_moe, w_router, b_router, w_gate_up, b_gate_up, w_down, b_down):
    n = b * s
    nc = s // CMP_STRIDE
    n_cmp = (s - CMP_LEN) // CMP_STRIDE + 1
    n_sel = s // SEL_BLOCK
    assert nc % KEY_CHUNK == 0 and n_sel <= LANES and n_cmp == nc - 1
    tm = 512 if n % 512 == 0 else 256

    p0 = ATTN_WIDTH
    p1 = p0 + 6 * KV_WIDTH
    p2 = p1 + N_NSA_BRANCH * N_HEADS
    p3 = p2 + SGU_WIDTH
    p4 = p3 + SGU_WIDTH
    wq = (w_in[:, :p0] * HEAD_DIM ** -0.5).astype(BF16)
    wkv = w_in[:, p0:p1].astype(BF16)
    wng = jnp.pad(w_in[:, p1:p2], ((0, 0), (0, LANES - (p2 - p1)))).astype(BF16)
    wu = w_in[:, p2:p3].astype(BF16)
    wv = w_in[:, p3:p4].astype(BF16)
    wmg = w_in[:, p4:].astype(BF16)
    bsp = jnp.repeat(b_spatial.T, SGU_GROUP_DIM, axis=1)

    q, kv, gates, ga, gbyb = _proj(
        x2, g_mix[None], wq, wkv, wng, wu, wv, wmg, sgu_ln_g[None], sgu_ln_b[None],
        w_spatial, bsp, w_branch_b.astype(BF16), tm)

    kv6 = kv.reshape(b, s, 6, N_KV, HEAD_DIM).transpose(2, 0, 3, 1, 4)
    kvc16 = kv6[0:2].transpose(1, 0, 2, 3, 4).reshape(b, 2, N_KV, nc, CMP_STRIDE * HEAD_DIM)
    half = CMP_LEN // 2
    pes = jnp.stack([pe_k, pe_v]).reshape(2, 2, half * HEAD_DIM)
    w1s = jnp.stack([w_ck1, w_cv1]).reshape(2, 2, half * HEAD_DIM, CMP_HID).astype(BF16)
    b1s = jnp.stack([b_ck1, b_cv1])[:, None, :]
    w2s = jnp.pad(jnp.stack([w_ck2, w_cv2]), ((0, 0), (0, 0), (0, LANES - HEAD_DIM))).astype(BF16)
    kvcmp = _compress(kvc16, pes, w1s, b1s, w2s)

    nt = s // Q_BLOCK
    pos = jnp.arange(s)
    zpad = lambda a, w: jnp.pad(a, ((0, 0),) * (a.ndim - 1) + ((0, w - a.shape[-1]),))
    kpos = jnp.broadcast_to(jnp.stack([pos // SEL_BLOCK, pos % SEL_BLOCK], -1).astype(BF16),
                            (b, N_KV, s, 2))
    onehot = jnp.broadcast_to(jax.nn.one_hot(pos // SEL_BLOCK, LANES, dtype=BF16), (b, N_KV, s, LANES))
    ksa = jnp.concatenate([zpad(jnp.concatenate([kv6[2], kpos], -1), LANES), onehot], -1)
    kwa = zpad(jnp.concatenate([kv6[4], kpos], -1), LANES)
    chunked_t = lambda a: a.reshape(b, N_KV, -1, KEY_CHUNK, HEAD_DIM).transpose(0, 1, 2, 4, 3)
    vst = chunked_t(kv6[3])
    vwt = chunked_t(kv6[5])
    vct = chunked_t(kvcmp[:, 1, :, :, :HEAD_DIM])
    slopes = jnp.asarray(2.0 ** (-8.0 * np.arange(1, N_HEADS + 1) / N_HEADS), F32)
    qcols = (slopes[:, None] * jnp.asarray([SEL_BLOCK, 1.0], F32)).astype(BF16).reshape(N_KV, GQA_R, 1, 2)
    qh = q.reshape(b, s, N_KV, GQA_R, HEAD_DIM).transpose(0, 2, 3, 1, 4)
    qp = zpad(jnp.concatenate([qh, jnp.broadcast_to(qcols, (b, N_KV, GQA_R, s, 2))], -1), LANES)
    qt = qp.reshape(b, N_KV, GQA_R, nt, Q_BLOCK, LANES).transpose(0, 1, 3, 5, 2, 4)
    qt = qt.reshape(b, N_KV, nt, LANES, GQA_R * Q_BLOCK)
    gt = gates[:, :N_NSA_BRANCH * N_HEADS].reshape(b, nt, Q_BLOCK, N_KV, GQA_R, N_NSA_BRANCH)
    gt = gt.transpose(0, 3, 1, 5, 4, 2).reshape(b, N_KV, nt, N_NSA_BRANCH, GQA_R * Q_BLOCK)
    gt = jnp.pad(gt, ((0, 0), (0, 0), (0, 0), (0, 8 - N_NSA_BRANCH), (0, 0)))
    ovt = _overlap_t(nc, n_cmp, n_sel).reshape(LANES, nc // KEY_CHUNK, KEY_CHUNK).transpose(1, 0, 2)
    blk = np.arange(LANES)
    pair = jnp.asarray(blk[:, None] // (KEY_CHUNK // SEL_BLOCK) == blk[None, :], BF16)

    ya = _nsa(qt, gt, kvcmp, vct, ksa, vst, kwa, vwt, ovt, pair, n_sel)
    ya = ya.reshape(b, N_KV, nt, HEAD_DIM, GQA_R, Q_BLOCK).transpose(0, 2, 5, 1, 4, 3).reshape(n, ATTN_WIDTH)

    wr = jnp.pad(w_router, ((0, 0), (0, LANES - N_EXPERTS)))
    wrh = wr.astype(BF16)
    wrl = (wr - wrh.astype(F32)).astype(BF16)
    br = jnp.pad(b_router, (0, LANES - N_EXPERTS))[None]
    x1, hm, rw, ri, cnt = _merge(x2, ya, ga, gbyb, w_branch_a.astype(BF16), w_out.astype(BF16),
                                 g_moe[None], wrh, wrl, br, tm)

    counts = cnt[0, :N_EXPERTS].astype(I32)
    padded = (counts + MOE_BLOCK - 1) // MOE_BLOCK * MOE_BLOCK
    pad_end = jnp.cumsum(padded)
    pad_start = pad_end - padded
    dest = (pad_start[ri[:, :TOP_K]] + ri[:, TOP_K:2 * TOP_K]).reshape(-1)
    n_blocks = -(-(n * TOP_K) // MOE_BLOCK) + N_EXPERTS
    blk_start = jnp.arange(n_blocks, dtype=I32) * MOE_BLOCK
    blk_expert = jnp.minimum(jnp.sum((pad_end[None, :] <= blk_start[:, None]).astype(I32), axis=1),
                             N_EXPERTS - 1)
    n_used = (pad_end[-1:] // MOE_BLOCK).astype(I32)

    tg = 256
    xs = _dispatch(dest, hm, jnp.zeros((n_blocks * MOE_BLOCK, D_MODEL), F32), tg)
    y = _experts(blk_expert, n_used, xs, w_gate_up.astype(BF16), b_gate_up[:, None, :],
                 w_down.astype(BF16), b_down[:, None, :])
    return dest, rw, x1, y, tg


def kernel(x, g_mix, w_in, w_ck1, b_ck1, w_ck2, pe_k, w_cv1, b_cv1, w_cv2, pe_v, sgu_ln_g, sgu_ln_b, w_spatial, b_spatial, w_branch_a, w_branch_b, w_out, g_moe, w_router, b_router, w_gate_up, b_gate_up, w_down, b_down, g_final):
    b, s, d = x.shape
    assert g_mix.shape[0] == 1, "the final rmsnorm is fused into the single layer's combine step"
    l = 0
    dest, rw, x1, y, tg = _layer(
        x.reshape(b * s, d), b, s, g_mix[l], w_in[l], w_ck1[l], b_ck1[l], w_ck2[l], pe_k[l], w_cv1[l],
        b_cv1[l], w_cv2[l], pe_v[l], sgu_ln_g[l], sgu_ln_b[l], w_spatial[l], b_spatial[l], w_branch_a[l],
        w_branch_b[l], w_out[l], g_moe[l], w_router[l], b_router[l], w_gate_up[l], b_gate_up[l],
        w_down[l], b_down[l])
    return _combine(dest, rw, x1, g_final[None], y, tg).reshape(b, s, d)
```

```python
import functools

import numpy as np
import jax
import jax.numpy as jnp
from jax import lax
from jax.experimental import pallas as pl
from jax.experimental.pallas import tpu as pltpu

D_MODEL = 1024
N_HEADS = 8
HEAD_DIM = 64
N_KV = 2
GQA_R = N_HEADS // N_KV
ATTN_WIDTH = N_HEADS * HEAD_DIM
KV_WIDTH = N_KV * HEAD_DIM
CMP_LEN = 32
CMP_STRIDE = 16
CMP_HID = 128
SEL_BLOCK = 64
SEL_TOPN = 16
WINDOW = 512
Q_BLOCK = 128
N_NSA_BRANCH = 3
SGU_WIDTH = 512
N_GROUPS_SGU = 8
SGU_GROUP_DIM = SGU_WIDTH // N_GROUPS_SGU
CHUNK = 128
N_EXPERTS = 32
TOP_K = 4
D_FF = D_MODEL
SWIGLU_LIMIT = 7.0
SWIGLU_ALPHA = 1.702
MOE_BLOCK = 512
RMS_EPS = 1e-5
LN_EPS = 1e-5
NEG = -1e30
FORCE = 1e4

LANES = 128
ROW_TILE = D_MODEL // LANES
MASK_BIAS = 1e9
KEY_CHUNK = 128
SEL_SLOTS = 4
VMEM_LIMIT = 56 * 1024 * 1024

F32 = jnp.float32
BF16 = jnp.bfloat16
I32 = jnp.int32

_NT = (((1,), (1,)), ((), ()))


def _dot(a, b):
    return jnp.dot(a, b, preferred_element_type=F32)


def _dot_nt(a, b):
    return lax.dot_general(a, b, _NT, preferred_element_type=F32)


def _sigmoid(x):
    return 1.0 / (1.0 + jnp.exp(-x))


def _gelu(x):
    c = np.float32(np.sqrt(2.0 / np.pi))
    return 0.5 * x * (1.0 + jnp.tanh(c * (x + 0.044715 * (x * x * x))))


def _rms(x, g):
    return x * lax.rsqrt(jnp.mean(x * x, axis=-1, keepdims=True) + RMS_EPS) * g


def _proj_kernel(x_ref, gmix_ref, wq_ref, wkv_ref, wng_ref, wu_ref, wv_ref, wmg_ref,
                 lng_ref, lnb_ref, wsp_ref, bsp_ref, wbb_ref,
                 q_ref, kv_ref, gates_ref, ga_ref, gbyb_ref):
    tm = x_ref.shape[0]
    hb = _rms(x_ref[...], gmix_ref[...]).astype(BF16)
    q_ref[...] = _dot(hb, wq_ref[...]).astype(BF16)
    kv_ref[...] = _dot(hb, wkv_ref[...]).astype(BF16)
    gates_ref[...] = _sigmoid(_dot(hb, wng_ref[...]))
    mg = _dot(hb, wmg_ref[...])
    ga_ref[...] = _sigmoid(mg[:, :D_MODEL]).astype(BF16)

    u = _gelu(_dot(hb, wu_ref[...]))
    v = _gelu(_dot(hb, wv_ref[...]))
    mu = jnp.mean(v, axis=-1, keepdims=True)
    vc = v - mu
    var = jnp.mean(vc * vc, axis=-1, keepdims=True)
    vln = (vc * lax.rsqrt(var + LN_EPS) * lng_ref[...] + lnb_ref[...]).astype(BF16)

    row = lax.broadcasted_iota(I32, (CHUNK, CHUNK), 0)
    col = lax.broadcasted_iota(I32, (CHUNK, CHUNK), 1)
    tril = row >= col
    wsp = [jnp.where(tril, wsp_ref[g], 0.0).astype(BF16) for g in range(N_GROUPS_SGU)]
    low_half = col < SGU_GROUP_DIM
    bsp = bsp_ref[...]
    chunks = []
    for c in range(tm // CHUNK):
        rs = slice(c * CHUNK, (c + 1) * CHUNK)
        parts = []
        for p in range(SGU_WIDTH // LANES):
            cs = slice(p * LANES, (p + 1) * LANES)
            vblk = vln[rs, cs]
            mixed = jnp.where(low_half, _dot(wsp[2 * p], vblk), _dot(wsp[2 * p + 1], vblk))
            parts.append(u[rs, cs] * (mixed + bsp[:, cs]))
        chunks.append(jnp.concatenate(parts, axis=1))
    yb = jnp.concatenate(chunks, axis=0).astype(BF16)
    gbyb_ref[...] = (_sigmoid(mg[:, D_MODEL:]) * _dot(yb, wbb_ref[...])).astype(BF16)


def _proj(x2, gmix, wq, wkv, wng, wu, wv, wmg, lng, lnb, wsp, bsp, wbb, tm):
    n = x2.shape[0]
    full = lambda a: pl.BlockSpec(a.shape, lambda i: (0,) * a.ndim)
    rows = lambda w: pl.BlockSpec((tm, w), lambda i: (i, 0))
    ins = (x2, gmix, wq, wkv, wng, wu, wv, wmg, lng, lnb, wsp, bsp, wbb)
    return pl.pallas_call(
        _proj_kernel,
        grid=(n // tm,),
        in_specs=[rows(D_MODEL)] + [full(a) for a in ins[1:]],
        out_specs=[rows(ATTN_WIDTH), rows(6 * KV_WIDTH), rows(LANES), rows(D_MODEL), rows(D_MODEL)],
        out_shape=[jax.ShapeDtypeStruct((n, ATTN_WIDTH), BF16),
                   jax.ShapeDtypeStruct((n, 6 * KV_WIDTH), BF16),
                   jax.ShapeDtypeStruct((n, LANES), F32),
                   jax.ShapeDtypeStruct((n, D_MODEL), BF16),
                   jax.ShapeDtypeStruct((n, D_MODEL), BF16)],
        compiler_params=pltpu.CompilerParams(dimension_semantics=("parallel",),
                                             vmem_limit_bytes=VMEM_LIMIT),
        name="proj",
    )(*ins)


def _compress_kernel(x_ref, pe_ref, w1_ref, b1_ref, w2_ref, o_ref):
    nc = x_ref.shape[3]
    x = x_ref[0, 0, 0].astype(F32)
    xa = (x + pe_ref[0, 0:1, :]).astype(BF16)
    xb = (x + pe_ref[0, 1:2, :]).astype(BF16)
    za = _dot(xa, w1_ref[0, 0])
    zb = _dot(xb, w1_ref[0, 1])
    hid = _gelu(za + pltpu.roll(zb, nc - 1, 0) + b1_ref[0])
    row = lax.broadcasted_iota(I32, hid.shape, 0)
    hid = jnp.where(row < nc - 1, hid, 0.0)
    o_ref[0, 0, 0] = _dot(hid.astype(BF16), w2_ref[0]).astype(BF16)


def _compress(kvc16, pes, w1s, b1s, w2s):
    b, _, g, nc, w = kvc16.shape
    return pl.pallas_call(
        _compress_kernel,
        grid=(b, 2, g),
        in_specs=[pl.BlockSpec((1, 1, 1, nc, w), lambda i, j, k: (i, j, k, 0, 0)),
                  pl.BlockSpec((1, 2, w), lambda i, j, k: (j, 0, 0)),
                  pl.BlockSpec((1, 2, w, CMP_HID), lambda i, j, k: (j, 0, 0, 0)),
                  pl.BlockSpec((1, 1, CMP_HID), lambda i, j, k: (j, 0, 0)),
                  pl.BlockSpec((1, CMP_HID, LANES), lambda i, j, k: (j, 0, 0))],
        out_specs=pl.BlockSpec((1, 1, 1, nc, LANES), lambda i, j, k: (i, j, k, 0, 0)),
        out_shape=jax.ShapeDtypeStruct((b, 2, g, nc, LANES), BF16),
        compiler_params=pltpu.CompilerParams(dimension_semantics=("parallel",) * 3,
                                             vmem_limit_bytes=VMEM_LIMIT),
        name="compress",
    )(kvc16, pes, w1s, b1s, w2s)


def _pairs():
    return [slice(pr * 2 * Q_BLOCK, (pr + 1) * 2 * Q_BLOCK) for pr in range(GQA_R // 2)]


def _attend(ks, vts, masks, q_ref, m_ref, l_ref, acc_ref):
    pairs = _pairs()
    scores = [[_dot(k, q_ref[:, ls]) for k in ks] for ls in pairs]
    for pr, ls in enumerate(pairs):
        sb = [masks[u](scores[pr][u], ls) for u in range(len(ks))]
        m_old = m_ref[0:1, ls]
        m_new = m_old
        for x in sb:
            m_new = jnp.maximum(m_new, jnp.max(x, axis=0, keepdims=True))
        alpha = jnp.exp(m_old - m_new)
        lsum = jnp.zeros_like(m_old)
        pv = jnp.zeros((HEAD_DIM, ls.stop - ls.start), F32)
        for u, x in enumerate(sb):
            p = jnp.exp(x - m_new)
            lsum = lsum + jnp.sum(p, axis=0, keepdims=True)
            pv = pv + _dot(vts[u], p.astype(BF16))
        l_ref[0:1, ls] = alpha * l_ref[0:1, ls] + lsum
        acc_ref[:, ls] = alpha * acc_ref[:, ls] + pv
        m_ref[0:1, ls] = m_new


def _nsa_kernel(qt_ref, gt_ref, kc_ref, vct_ref, ksa_ref, vst_ref, kwa_ref, vwt_ref, ovt_ref, pair_ref,
                o_ref, qa_ref, m_ref, l_ref, acc_ref, act_ref, *, n_sel, top_n):
    g = pl.program_id(1)
    i = pl.program_id(2)
    t0 = i * Q_BLOCK
    width = GQA_R * Q_BLOCK
    n_chunks = ksa_ref.shape[2] // KEY_CHUNK
    n_cchunks = kc_ref.shape[3] // KEY_CHUNK
    lane = lax.broadcasted_iota(I32, (1, width), 1)
    tq = t0 + (lane & (Q_BLOCK - 1))
    tf = tq.astype(F32)
    r = lane >> 7
    slope = jnp.where(r == 0, 0.5, jnp.where(r == 1, 0.25, jnp.where(r == 2, 0.125, 0.0625)))
    slope = slope * jnp.where(g == 0, 1.0, 0.0625)
    sub = lax.broadcasted_iota(I32, (KEY_CHUNK, 1), 0)
    pairs = _pairs()
    qt = qt_ref.at[0, 0, 0]

    def reset():
        m_ref[...] = jnp.full(m_ref.shape, NEG, F32)
        l_ref[...] = jnp.zeros(l_ref.shape, F32)
        acc_ref[...] = jnp.zeros(acc_ref.shape, F32)

    kcs = [kc_ref[0, 0, 0, u * KEY_CHUNK:(u + 1) * KEY_CHUNK, :] for u in range(n_cchunks)]
    sc = [[_dot(k, qt[:, ls]) for k in kcs] for ls in pairs]
    o_c, ps = [], []
    for pr, ls in enumerate(pairs):
        sb = []
        for u in range(n_cchunks):
            n_col = u * KEY_CHUNK + sub
            c_pos = n_col.astype(F32) * CMP_STRIDE + (CMP_LEN - 1) / 2.0
            s = sc[pr][u] - slope[:, ls] * (tf[:, ls] - c_pos)
            sb.append(jnp.where(n_col * CMP_STRIDE + (CMP_LEN - 1) <= tq[:, ls], s, NEG))
        m = sb[0].max(axis=0, keepdims=True)
        for x in sb[1:]:
            m = jnp.maximum(m, jnp.max(x, axis=0, keepdims=True))
        pb = [jnp.exp(x - m) for x in sb]
        l = pb[0].sum(axis=0, keepdims=True)
        for p in pb[1:]:
            l = l + jnp.sum(p, axis=0, keepdims=True)
        w = (tq[:, ls] >= CMP_LEN - 1).astype(F32) / l
        oc = _dot(vct_ref[0, 0, 0], pb[0].astype(BF16))
        for u in range(1, n_cchunks):
            oc = oc + _dot(vct_ref[0, 0, u], pb[u].astype(BF16))
        o_c.append(oc * w)
        ps.append([p[:, :Q_BLOCK] * w[:, :Q_BLOCK] + p[:, Q_BLOCK:] * w[:, Q_BLOCK:] for p in pb])
    imp2 = jnp.zeros((LANES, 2 * Q_BLOCK), F32)
    for u in range(n_cchunks):
        psu = ps[0][u] + ps[1][u]
        hi = psu.astype(BF16)
        lo = (psu - hi.astype(F32)).astype(BF16)
        imp2 = imp2 + _dot(ovt_ref[u], jnp.concatenate([hi, lo], axis=1))
    imp_t = imp2[:, :Q_BLOCK] + imp2[:, Q_BLOCK:]

    j_io = lax.broadcasted_iota(I32, (LANES, Q_BLOCK), 0)
    tl = t0 + lax.broadcasted_iota(I32, (LANES, Q_BLOCK), 1)
    cur = tl >> 6
    forced = (j_io == 0) | (j_io == cur) | (j_io == cur - 1)
    prio = jnp.where(forced, FORCE, jnp.where(j_io * SEL_BLOCK <= tl, imp_t, -1.0))
    prio = jnp.where(j_io < n_sel, prio, NEG)
    sel_t = jnp.zeros((LANES, Q_BLOCK), F32)
    for _ in range(top_n):
        m = jnp.max(prio, axis=0, keepdims=True)
        idx = jnp.min(jnp.where(prio == m, j_io, LANES), axis=0, keepdims=True)
        hit = j_io == idx
        sel_t = jnp.where(hit, 1.0, sel_t)
        prio = jnp.where(hit, NEG, prio)

    qa_ref[0:LANES, :] = qt[...]
    bias_t = ((sel_t - 1.0) * MASK_BIAS).astype(BF16)
    for h in range(GQA_R):
        qa_ref[LANES:2 * LANES, h * Q_BLOCK:(h + 1) * Q_BLOCK] = bias_t

    per_block = _dot_nt(jnp.ones((8, Q_BLOCK), BF16), sel_t.astype(BF16))
    per_chunk = _dot(per_block.astype(BF16), pair_ref[...])
    n_act = jnp.int32(0)
    for c in range(n_chunks):
        act_ref[n_act] = jnp.int32(c)
        n_act = n_act + jnp.where((per_chunk[0, c] > 0.0) & (c < i), 1, 0)
    act_ref[n_act] = i
    for u in range(1, SEL_SLOTS):
        act_ref[n_act + u] = jnp.int32(-1)

    reset()
    ks, vts, masks = [], [], []
    for u in range(WINDOW // KEY_CHUNK + 1):
        c = i - WINDOW // KEY_CHUNK + u
        cl = jnp.maximum(c, 0)
        ks.append(kwa_ref[0, 0, pl.ds(pl.multiple_of(cl * KEY_CHUNK, KEY_CHUNK), KEY_CHUNK), :])
        vts.append(vwt_ref[0, 0, cl])
        kpos = jnp.where(c < 0, -2 * WINDOW, c * KEY_CHUNK) + sub

        def mask(s, ls, kpos=kpos):
            dist = tq[:, ls] - kpos
            return jnp.where((dist >= 0) & (dist < WINDOW), s, NEG)

        masks.append(mask)
    _attend(ks, vts, masks, qt, m_ref, l_ref, acc_ref)
    o_w = acc_ref[...] / l_ref[0:1, :]

    reset()

    def sel_body(k, carry):
        ks, vts, masks = [], [], []
        for u in range(SEL_SLOTS):
            c = act_ref[k * SEL_SLOTS + u]
            cl = jnp.maximum(c, 0)
            ks.append(ksa_ref[0, 0, pl.ds(pl.multiple_of(cl * KEY_CHUNK, KEY_CHUNK), KEY_CHUNK), :])
            vts.append(vst_ref[0, 0, cl])
            kpos = jnp.where(c < 0, n_chunks * KEY_CHUNK, c * KEY_CHUNK) + sub
            masks.append(lambda s, ls, kpos=kpos: jnp.where(kpos <= tq[:, ls], s, -MASK_BIAS))
        _attend(ks, vts, masks, qa_ref, m_ref, l_ref, acc_ref)
        return carry

    lax.fori_loop(0, (n_act + SEL_SLOTS) // SEL_SLOTS, sel_body, 0)
    o_s = acc_ref[...] / l_ref[0:1, :]

    gt = gt_ref[0, 0, 0]
    o = gt[0:1, :] * jnp.concatenate(o_c, axis=1) + gt[1:2, :] * o_s + gt[2:3, :] * o_w
    o_ref[0, 0, 0] = o.astype(BF16)


def _nsa(qt, gt, kvcmp, vct, ksa, vst, kwa, vwt, ovt, pair, n_sel):
    b, g, nt, _, width = qt.shape
    s = ksa.shape[2]
    nc = kvcmp.shape[3]
    tile = lambda rows: pl.BlockSpec((1, 1, 1, rows, width), lambda i, j, k: (i, j, k, 0, 0))
    keys = lambda w: pl.BlockSpec((1, 1, s, w), lambda i, j, k: (i, j, 0, 0))
    vals = lambda a: pl.BlockSpec((1, 1) + a.shape[2:], lambda i, j, k: (i, j, 0, 0, 0))
    const = lambda a: pl.BlockSpec(a.shape, lambda i, j, k: (0,) * a.ndim)
    kern = functools.partial(_nsa_kernel, n_sel=n_sel, top_n=min(SEL_TOPN, n_sel))
    return pl.pallas_call(
        kern,
        grid=(b, g, nt),
        in_specs=[tile(LANES), tile(8),
                  pl.BlockSpec((1, 1, 1, nc, LANES), lambda i, j, k: (i, 0, j, 0, 0)),
                  vals(vct), keys(2 * LANES), vals(vst), keys(LANES), vals(vwt), const(ovt), const(pair)],
        out_specs=tile(HEAD_DIM),
        out_shape=jax.ShapeDtypeStruct((b, g, nt, HEAD_DIM, width), BF16),
        scratch_shapes=[pltpu.VMEM((2 * LANES, width), BF16),
                        pltpu.VMEM((8, width), F32),
                        pltpu.VMEM((8, width), F32),
                        pltpu.VMEM((HEAD_DIM, width), F32),
                        pltpu.SMEM((s // KEY_CHUNK + SEL_SLOTS,), I32)],
        compiler_params=pltpu.CompilerParams(dimension_semantics=("parallel", "parallel", "arbitrary"),
                                             vmem_limit_bytes=VMEM_LIMIT),
        name="nsa",
    )(qt, gt, kvcmp, vct, ksa, vst, kwa, vwt, ovt, pair)


def _merge_kernel(x_ref, ya_ref, ga_ref, gbyb_ref, wba_ref, wout_ref, gmoe_ref, wrh_ref, wrl_ref, br_ref,
                  x1_ref, hm_ref, rw_ref, ri_ref, cnt_ref, carry_ref):
    tm = x_ref.shape[0]

    @pl.when(pl.program_id(0) == 0)
    def _():
        carry_ref[...] = jnp.zeros_like(carry_ref)

    merged = ga_ref[...].astype(F32) * _dot(ya_ref[...], wba_ref[...]) + gbyb_ref[...].astype(F32)
    x1 = x_ref[...] + _dot(merged.astype(BF16), wout_ref[...])
    x1_ref[...] = x1
    hm = _rms(x1, gmoe_ref[...])
    for a in range(ROW_TILE):
        hm_ref[pl.ds(a, tm, stride=ROW_TILE), :] = hm[:, a * LANES:(a + 1) * LANES]

    hh = hm.astype(BF16)
    hl = (hm - hh.astype(F32)).astype(BF16)
    wrh = wrh_ref[...]
    logits = _dot(hh, wrh) + _dot(hl, wrh) + _dot(hh, wrl_ref[...]) + br_ref[...]
    lane = lax.broadcasted_iota(I32, (tm, LANES), 1)
    lg = jnp.where(lane < N_EXPERTS, logits, NEG)
    vals, idxs = [], []
    for _ in range(TOP_K):
        m = jnp.max(lg, axis=1, keepdims=True)
        idx = jnp.min(jnp.where(lg == m, lane, LANES), axis=1, keepdims=True)
        vals.append(m)
        idxs.append(idx)
        lg = jnp.where(lane == idx, NEG, lg)
    ex = [jnp.exp(v - vals[0]) for v in vals]
    den = ex[0] + ex[1] + ex[2] + ex[3]

    hits = [lane == idx for idx in idxs]
    multi = jnp.zeros((tm, LANES), F32)
    for h in hits:
        multi = jnp.where(h, 1.0, multi)
    ltri = (lax.broadcasted_iota(I32, (tm, tm), 0) > lax.broadcasted_iota(I32, (tm, tm), 1))
    carry = carry_ref[0:1, :]
    cum = _dot(jnp.where(ltri, 1.0, 0.0).astype(BF16), multi.astype(BF16)) + carry
    rw = jnp.zeros((tm, LANES), F32)
    ri = jnp.zeros((tm, LANES), I32)
    for k in range(TOP_K):
        rank = jnp.sum(jnp.where(hits[k], cum, 0.0), axis=1, keepdims=True).astype(I32)
        rw = jnp.where(lane == k, ex[k] / den, rw)
        ri = jnp.where(lane == k, idxs[k], jnp.where(lane == TOP_K + k, rank, ri))
    rw_ref[...] = rw
    ri_ref[...] = ri
    new_carry = carry + jnp.sum(multi, axis=0, keepdims=True)
    carry_ref[...] = jnp.broadcast_to(new_carry, carry_ref.shape)
    cnt_ref[...] = jnp.broadcast_to(new_carry, cnt_ref.shape)


def _merge(x2, ya, ga, gbyb, wba, wout, gmoe, wrh, wrl, br, tm):
    n = x2.shape[0]
    full = lambda a: pl.BlockSpec(a.shape, lambda i: (0,) * a.ndim)
    rows = lambda w: pl.BlockSpec((tm, w), lambda i: (i, 0))
    return pl.pallas_call(
        _merge_kernel,
        grid=(n // tm,),
        in_specs=[rows(D_MODEL), rows(ATTN_WIDTH), rows(D_MODEL), rows(D_MODEL),
                  full(wba), full(wout), full(gmoe), full(wrh), full(wrl), full(br)],
        out_specs=[rows(D_MODEL), pl.BlockSpec((tm * ROW_TILE, LANES), lambda i: (i, 0)), rows(LANES), rows(LANES),
                   pl.BlockSpec((8, LANES), lambda i: (0, 0))],
        out_shape=[jax.ShapeDtypeStruct((n, D_MODEL), F32),
                   jax.ShapeDtypeStruct((n * ROW_TILE, LANES), F32),
                   jax.ShapeDtypeStruct((n, LANES), F32),
                   jax.ShapeDtypeStruct((n, LANES), I32),
                   jax.ShapeDtypeStruct((8, LANES), F32)],
        scratch_shapes=[pltpu.VMEM((8, LANES), F32)],
        compiler_params=pltpu.CompilerParams(dimension_semantics=("arbitrary",),
                                             vmem_limit_bytes=VMEM_LIMIT),
        name="merge",
    )(x2, ya, ga, gbyb, wba, wout, gmoe, wrh, wrl, br)


def _tile_copy(src, i, dst, d, sem):
    return pltpu.make_async_copy(src.at[pl.ds(pl.multiple_of(i * ROW_TILE, ROW_TILE), ROW_TILE)],
                                 dst.at[pl.ds(pl.multiple_of(d * ROW_TILE, ROW_TILE), ROW_TILE)], sem)


def _dispatch_kernel(seg_ref, dest_ref, hm_ref, xs_ref, zero_ref, sem, zsem, *, n_pad):
    tm = hm_ref.shape[0] // ROW_TILE

    @pl.when(pl.program_id(0) == 0)
    def _():
        zero_ref[...] = jnp.zeros_like(zero_ref)

        def seg(e, c):
            def fill(r, c2):
                _tile_copy(zero_ref, 0, xs_ref, r, zsem).start()
                return c2
            return lax.fori_loop(seg_ref[0, e], seg_ref[1, e], fill, c)

        lax.fori_loop(0, N_EXPERTS + 1, seg, 0)
        pad_rows = xs_ref.at[pl.ds(0, n_pad * ROW_TILE)]
        pltpu.make_async_copy(pad_rows, pad_rows, zsem).wait()

    def issue(i, c):
        for k in range(TOP_K):
            _tile_copy(hm_ref, i, xs_ref, dest_ref[i * TOP_K + k], sem).start(priority=k % 2)
        return c

    lax.fori_loop(0, tm, issue, 0)
    for k in range(TOP_K):
        pltpu.make_async_copy(hm_ref, xs_ref.at[pl.ds(0, tm * ROW_TILE)], sem).wait()


def _dispatch(seg, dest_flat, hm, n_slots, tm):
    n = hm.shape[0] // ROW_TILE
    kern = functools.partial(_dispatch_kernel, n_pad=n_slots - n * TOP_K)
    return pl.pallas_call(
        kern,
        grid_spec=pltpu.PrefetchScalarGridSpec(
            num_scalar_prefetch=1,
            grid=(n // tm,),
            in_specs=[pl.BlockSpec((tm * TOP_K,), lambda i, sg: (i,), memory_space=pltpu.SMEM),
                      pl.BlockSpec((tm * ROW_TILE, LANES), lambda i, sg: (i, 0))],
            out_specs=pl.BlockSpec(memory_space=pl.ANY),
            scratch_shapes=[pltpu.VMEM((ROW_TILE, LANES), F32),
                            pltpu.SemaphoreType.DMA(()), pltpu.SemaphoreType.DMA(())]),
        out_shape=jax.ShapeDtypeStruct((n_slots * ROW_TILE, LANES), F32),
        compiler_params=pltpu.CompilerParams(dimension_semantics=("arbitrary",),
                                             has_side_effects=True),
        name="dispatch",
    )(seg, dest_flat, hm)


def _expert_kernel(be_ref, nu_ref, xs_ref, wgu_ref, bgu_ref, wd_ref, bd_ref, y_ref, wgu_bf, wd_bf):
    i = pl.program_id(0)

    @pl.when(i >= nu_ref[0])
    def _():
        y_ref[...] = jnp.zeros_like(y_ref)

    @pl.when((i == 0) | (be_ref[i] != be_ref[jnp.maximum(i - 1, 0)]))
    def _():
        wgu_bf[...] = wgu_ref[0].astype(BF16)
        wd_bf[...] = wd_ref[0].astype(BF16)

    @pl.when(i < nu_ref[0])
    def _():
        x = jnp.concatenate([xs_ref[pl.ds(a, MOE_BLOCK, stride=ROW_TILE), :] for a in range(ROW_TILE)], axis=1)
        gu = _dot(x.astype(BF16), wgu_bf[...]) + bgu_ref[0]
        gate = jnp.minimum(gu[:, :D_FF], SWIGLU_LIMIT)
        up = jnp.clip(gu[:, D_FF:], -SWIGLU_LIMIT, SWIGLU_LIMIT)
        act = gate * _sigmoid(SWIGLU_ALPHA * gate) * (up + 1.0)
        y = _dot(act.astype(BF16), wd_bf[...]) + bd_ref[0]
        for a in range(ROW_TILE):
            y_ref[pl.ds(a, MOE_BLOCK, stride=ROW_TILE), :] = y[:, a * LANES:(a + 1) * LANES]


def _experts(blk_expert, n_used, xs, wgu, bgu, wd, bd):
    n_blocks = xs.shape[0] // (MOE_BLOCK * ROW_TILE)
    blk = lambda i, be, nu: (jnp.minimum(i, nu[0] - 1), 0)
    exp3 = lambda i, be, nu: (be[jnp.minimum(i, nu[0] - 1)], 0, 0)
    return pl.pallas_call(
        _expert_kernel,
        grid_spec=pltpu.PrefetchScalarGridSpec(
            num_scalar_prefetch=2,
            grid=(n_blocks,),
            in_specs=[pl.BlockSpec((MOE_BLOCK * ROW_TILE, LANES), blk),
                      pl.BlockSpec((1, D_MODEL, 2 * D_FF), exp3),
                      pl.BlockSpec((1, 1, 2 * D_FF), exp3),
                      pl.BlockSpec((1, D_FF, D_MODEL), exp3),
                      pl.BlockSpec((1, 1, D_MODEL), exp3)],
            out_specs=pl.BlockSpec((MOE_BLOCK * ROW_TILE, LANES), lambda i, be, nu: (i, 0)),
            scratch_shapes=[pltpu.VMEM((D_MODEL, 2 * D_FF), BF16), pltpu.VMEM((D_FF, D_MODEL), BF16)]),
        out_shape=jax.ShapeDtypeStruct(xs.shape, F32),
        compiler_params=pltpu.CompilerParams(dimension_semantics=("arbitrary",),
                                             vmem_limit_bytes=VMEM_LIMIT),
        name="experts",
    )(blk_expert, n_used, xs, wgu, bgu, wd, bd)


def _combine_kernel(dest_ref, rw_ref, x1_ref, gfin_ref, y_ref, o_ref, ybuf, sem):
    tm = x1_ref.shape[0]

    def issue(i, c):
        for k in range(TOP_K):
            _tile_copy(y_ref, dest_ref[i * TOP_K + k], ybuf.at[k], i, sem).start(priority=k % 2)
        return c

    lax.fori_loop(0, tm, issue, 0)
    for k in range(TOP_K):
        pltpu.make_async_copy(y_ref.at[pl.ds(0, tm * ROW_TILE)], ybuf.at[k], sem).wait()

    rw = rw_ref[...]
    x1 = x1_ref[...]
    cols = []
    for a in range(ROW_TILE):
        acc = x1[:, a * LANES:(a + 1) * LANES]
        for k in range(TOP_K):
            acc = acc + rw[:, k:k + 1] * ybuf[k, pl.ds(a, tm, stride=ROW_TILE), :]
        cols.append(acc)
    o_ref[...] = _rms(jnp.concatenate(cols, axis=1), gfin_ref[...])


def _combine(dest_flat, rw, x1, gfin, y, tm):
    n = x1.shape[0]
    return pl.pallas_call(
        _combine_kernel,
        grid=(n // tm,),
        in_specs=[pl.BlockSpec((tm * TOP_K,), lambda i: (i,), memory_space=pltpu.SMEM),
                  pl.BlockSpec((tm, LANES), lambda i: (i, 0)),
                  pl.BlockSpec((tm, D_MODEL), lambda i: (i, 0)),
                  pl.BlockSpec((1, D_MODEL), lambda i: (0, 0)),
                  pl.BlockSpec(memory_space=pl.ANY)],
        out_specs=pl.BlockSpec((tm, D_MODEL), lambda i: (i, 0)),
        out_shape=jax.ShapeDtypeStruct((n, D_MODEL), F32),
        scratch_shapes=[pltpu.VMEM((TOP_K, tm * ROW_TILE, LANES), F32), pltpu.SemaphoreType.DMA(())],
        compiler_params=pltpu.CompilerParams(dimension_semantics=("arbitrary",),
                                             vmem_limit_bytes=VMEM_LIMIT),
        name="combine",
    )(dest_flat, rw, x1, gfin, y)


def _overlap_t(nc, n_cmp, n_sel):
    cs = np.arange(n_cmp)[None, :] * CMP_STRIDE
    ss = np.arange(n_sel)[:, None] * SEL_BLOCK
    ov = np.clip(np.minimum(cs + CMP_LEN, ss + SEL_BLOCK) - np.maximum(cs, ss), 0, None) / CMP_LEN
    out = np.zeros((LANES, nc), np.float32)
    out[:n_sel, :n_cmp] = ov
    return jnp.asarray(out, BF16)


def _layer(x2, b, s, g_mix, w_in, w_ck1, b_ck1, w_ck2, pe_k, w_cv1, b_cv1, w_cv2, pe_v,
           sgu_ln_g, sgu_ln_b, w_spatial, b_spatial, w_branch_a, w_branch_b, w_out,
           g_moe, w_router, b_router, w_gate_up, b_gate_up, w_down, b_down):
    n = b * s
    nc = s // CMP_STRIDE
    n_cmp = (s - CMP_LEN) // CMP_STRIDE + 1
    n_sel = s // SEL_BLOCK
    assert nc % KEY_CHUNK == 0 and n_sel <= LANES and n_cmp == nc - 1
    tm = 512 if n % 512 == 0 else 256

    p0 = ATTN_WIDTH
    p1 = p0 + 6 * KV_WIDTH
    p2 = p1 + N_NSA_BRANCH * N_HEADS
    p3 = p2 + SGU_WIDTH
    p4 = p3 + SGU_WIDTH
    wq = (w_in[:, :p0] * HEAD_DIM ** -0.5).astype(BF16)
    wkv = w_in[:, p0:p1].astype(BF16)
    wng = jnp.pad(w_in[:, p1:p2], ((0, 0), (0, LANES - (p2 - p1)))).astype(BF16)
    wu = w_in[:, p2:p3].astype(BF16)
    wv = w_in[:, p3:p4].astype(BF16)
    wmg = w_in[:, p4:].astype(BF16)
    bsp = jnp.repeat(b_spatial.T, SGU_GROUP_DIM, axis=1)

    q, kv, gates, ga, gbyb = _proj(
        x2, g_mix[None], wq, wkv, wng, wu, wv, wmg, sgu_ln_g[None], sgu_ln_b[None],
        w_spatial, bsp, w_branch_b.astype(BF16), tm)

    kv6 = kv.reshape(b, s, 6, N_KV, HEAD_DIM).transpose(2, 0, 3, 1, 4)
    kvc16 = kv6[0:2].transpose(1, 0, 2, 3, 4).reshape(b, 2, N_KV, nc, CMP_STRIDE * HEAD_DIM)
    half = CMP_LEN // 2
    pes = jnp.stack([pe_k, pe_v]).reshape(2, 2, half * HEAD_DIM)
    w1s = jnp.stack([w_ck1, w_cv1]).reshape(2, 2, half * HEAD_DIM, CMP_HID).astype(BF16)
    b1s = jnp.stack([b_ck1, b_cv1])[:, None, :]
    w2s = jnp.pad(jnp.stack([w_ck2, w_cv2]), ((0, 0), (0, 0), (0, LANES - HEAD_DIM))).astype(BF16)
    kvcmp = _compress(kvc16, pes, w1s, b1s, w2s)

    nt = s // Q_BLOCK
    pos = jnp.arange(s)
    zpad = lambda a, w: jnp.pad(a, ((0, 0),) * (a.ndim - 1) + ((0, w - a.shape[-1]),))
    kpos = jnp.broadcast_to(jnp.stack([pos // SEL_BLOCK, pos % SEL_BLOCK], -1).astype(BF16),
                            (b, N_KV, s, 2))
    onehot = jnp.broadcast_to(jax.nn.one_hot(pos // SEL_BLOCK, LANES, dtype=BF16), (b, N_KV, s, LANES))
    ksa = jnp.concatenate([zpad(jnp.concatenate([kv6[2], kpos], -1), LANES), onehot], -1)
    kwa = zpad(jnp.concatenate([kv6[4], kpos], -1), LANES)
    chunked_t = lambda a: a.reshape(b, N_KV, -1, KEY_CHUNK, HEAD_DIM).transpose(0, 1, 2, 4, 3)
    vst = chunked_t(kv6[3])
    vwt = chunked_t(kv6[5])
    vct = chunked_t(kvcmp[:, 1, :, :, :HEAD_DIM])
    slopes = jnp.asarray(2.0 ** (-8.0 * np.arange(1, N_HEADS + 1) / N_HEADS), F32)
    qcols = (slopes[:, None] * jnp.asarray([SEL_BLOCK, 1.0], F32)).astype(BF16).reshape(N_KV, GQA_R, 1, 2)
    qh = q.reshape(b, s, N_KV, GQA_R, HEAD_DIM).transpose(0, 2, 3, 1, 4)
    qp = zpad(jnp.concatenate([qh, jnp.broadcast_to(qcols, (b, N_KV, GQA_R, s, 2))], -1), LANES)
    qt = qp.reshape(b, N_KV, GQA_R, nt, Q_BLOCK, LANES).transpose(0, 1, 3, 5, 2, 4)
    qt = qt.reshape(b, N_KV, nt, LANES, GQA_R * Q_BLOCK)
    gt = gates[:, :N_NSA_BRANCH * N_HEADS].reshape(b, nt, Q_BLOCK, N_KV, GQA_R, N_NSA_BRANCH)
    gt = gt.transpose(0, 3, 1, 5, 4, 2).reshape(b, N_KV, nt, N_NSA_BRANCH, GQA_R * Q_BLOCK)
    gt = jnp.pad(gt, ((0, 0), (0, 0), (0, 0), (0, 8 - N_NSA_BRANCH), (0, 0)))
    ovt = _overlap_t(nc, n_cmp, n_sel).reshape(LANES, nc // KEY_CHUNK, KEY_CHUNK).transpose(1, 0, 2)
    blk = np.arange(LANES)
    pair = jnp.asarray(blk[:, None] // (KEY_CHUNK // SEL_BLOCK) == blk[None, :], BF16)

    ya = _nsa(qt, gt, kvcmp, vct, ksa, vst, kwa, vwt, ovt, pair, n_sel)
    ya = ya.reshape(b, N_KV, nt, HEAD_DIM, GQA_R, Q_BLOCK).transpose(0, 2, 5, 1, 4, 3).reshape(n, ATTN_WIDTH)

    wr = jnp.pad(w_router, ((0, 0), (0, LANES - N_EXPERTS)))
    wrh = wr.astype(BF16)
    wrl = (wr - wrh.astype(F32)).astype(BF16)
    br = jnp.pad(b_router, (0, LANES - N_EXPERTS))[None]
    x1, hm, rw, ri, cnt = _merge(x2, ya, ga, gbyb, w_branch_a.astype(BF16), w_out.astype(BF16),
                                 g_moe[None], wrh, wrl, br, tm)

    counts = cnt[0, :N_EXPERTS].astype(I32)
    padded = (counts + MOE_BLOCK - 1) // MOE_BLOCK * MOE_BLOCK
    pad_end = jnp.cumsum(padded)
    pad_start = pad_end - padded
    dest = (pad_start[ri[:, :TOP_K]] + ri[:, TOP_K:2 * TOP_K]).reshape(-1)
    n_blocks = -(-(n * TOP_K) // MOE_BLOCK) + N_EXPERTS
    blk_start = jnp.arange(n_blocks, dtype=I32) * MOE_BLOCK
    blk_expert = jnp.minimum(jnp.sum((pad_end[None, :] <= blk_start[:, None]).astype(I32), axis=1),
                             N_EXPERTS - 1)
    n_used = (pad_end[-1:] // MOE_BLOCK).astype(I32)
    n_slots = n_blocks * MOE_BLOCK
    seg = jnp.stack([jnp.concatenate([pad_start + counts, pad_end[-1:]]),
                     jnp.concatenate([pad_end, jnp.full((1,), n_slots, I32)])]).astype(I32)

    tg = 256
    xs = _dispatch(seg, dest, hm, n_slots, tg)
    y = _experts(blk_expert, n_used, xs, w_gate_up, b_gate_up[:, None, :], w_down, b_down[:, None, :])
    return dest, rw, x1, y, tg


def kernel(x, g_mix, w_in, w_ck1, b_ck1, w_ck2, pe_k, w_cv1, b_cv1, w_cv2, pe_v, sgu_ln_g, sgu_ln_b, w_spatial, b_spatial, w_branch_a, w_branch_b, w_out, g_moe, w_router, b_router, w_gate_up, b_gate_up, w_down, b_down, g_final):
    b, s, d = x.shape
    assert g_mix.shape[0] == 1, "the final rmsnorm is fused into the single layer's combine step"
    l = 0
    dest, rw, x1, y, tg = _layer(
        x.reshape(b * s, d), b, s, g_mix[l], w_in[l], w_ck1[l], b_ck1[l], w_ck2[l], pe_k[l], w_cv1[l],
        b_cv1[l], w_cv2[l], pe_v[l], sgu_ln_g[l], sgu_ln_b[l], w_spatial[l], b_spatial[l], w_branch_a[l],
        w_branch_b[l], w_out[l], g_moe[l], w_router[l], b_router[l], w_gate_up[l], b_gate_up[l],
        w_down[l], b_down[l])
    return _combine(dest, rw, x1, g_final[None], y, tg).reshape(b, s, d)
```

```python
import functools

import numpy as np
import jax
import jax.numpy as jnp
from jax import lax
from jax.experimental import pallas as pl
from jax.experimental.pallas import tpu as pltpu

D_MODEL = 1024
N_HEADS = 8
HEAD_DIM = 64
N_KV = 2
GQA_R = N_HEADS // N_KV
ATTN_WIDTH = N_HEADS * HEAD_DIM
KV_WIDTH = N_KV * HEAD_DIM
CMP_LEN = 32
CMP_STRIDE = 16
CMP_HID = 128
SEL_BLOCK = 64
SEL_TOPN = 16
WINDOW = 512
Q_BLOCK = 128
N_NSA_BRANCH = 3
SGU_WIDTH = 512
N_GROUPS_SGU = 8
SGU_GROUP_DIM = SGU_WIDTH // N_GROUPS_SGU
CHUNK = 128
N_EXPERTS = 32
TOP_K = 4
D_FF = D_MODEL
SWIGLU_LIMIT = 7.0
SWIGLU_ALPHA = 1.702
MOE_BLOCK = 512
RMS_EPS = 1e-5
LN_EPS = 1e-5
NEG = -1e30
FORCE = 1e4

LANES = 128
ROW_TILE = D_MODEL // LANES
MASK_BIAS = 1e9
KEY_CHUNK = 128
SEL_SLOTS = 4
GATE_ROWS = 16
VMEM_LIMIT = 56 * 1024 * 1024

F32 = jnp.float32
BF16 = jnp.bfloat16
I32 = jnp.int32

_NT = (((1,), (1,)), ((), ()))


def _dot(a, b):
    return jnp.dot(a, b, preferred_element_type=F32)


def _dot_nt(a, b):
    return lax.dot_general(a, b, _NT, preferred_element_type=F32)


def _sigmoid(x):
    return 1.0 / (1.0 + jnp.exp(-x))


def _gelu(x):
    c = np.float32(np.sqrt(2.0 / np.pi))
    return 0.5 * x * (1.0 + jnp.tanh(c * (x + 0.044715 * (x * x * x))))


def _rms(x, g):
    return x * lax.rsqrt(jnp.mean(x * x, axis=-1, keepdims=True) + RMS_EPS) * g


T_Q = 0
T_VS = T_Q + N_HEADS * LANES
T_VW = T_VS + N_KV * HEAD_DIM
T_GATE = T_VW + N_KV * HEAD_DIM
T_ROWS = T_GATE + N_KV * GATE_ROWS
K_SEL = 0
K_WIN = K_SEL + N_KV * 2 * LANES
K_CMP = K_WIN + N_KV * LANES
K_COLS = K_CMP + 4 * HEAD_DIM


def _proj_kernel(x_ref, gmix_ref, wt_ref, tconst_ref, wk_ref, kconst_ref, wu_ref, wv_ref, wmg_ref,
                 lng_ref, lnb_ref, wsp_ref, bsp_ref, wbb_ref,
                 qt_ref, gt_ref, vst_ref, vwt_ref, ksa_ref, kwa_ref, kcx_ref, vcx_ref, ga_ref, gbyb_ref):
    tm = x_ref.shape[0]
    hb = _rms(x_ref[...], gmix_ref[...]).astype(BF16)

    for h in range(tm // (2 * Q_BLOCK)):
        t = _dot_nt(wt_ref[...], hb[h * 2 * Q_BLOCK:(h + 1) * 2 * Q_BLOCK]) + tconst_ref[...]
        for jj in range(2):
            j = 2 * h + jj
            ls = slice(jj * Q_BLOCK, (jj + 1) * Q_BLOCK)
            for g in range(N_KV):
                for r in range(GQA_R):
                    r0 = T_Q + (g * GQA_R + r) * LANES
                    qt_ref[0, g, j, :, r * Q_BLOCK:(r + 1) * Q_BLOCK] = t[r0:r0 + LANES, ls].astype(BF16)
                vst_ref[0, g, j] = t[T_VS + g * HEAD_DIM:T_VS + (g + 1) * HEAD_DIM, ls].astype(BF16)
                vwt_ref[0, g, j] = t[T_VW + g * HEAD_DIM:T_VW + (g + 1) * HEAD_DIM, ls].astype(BF16)
                gt_ref[0, g, j] = _sigmoid(t[T_GATE + g * GATE_ROWS:T_GATE + (g + 1) * GATE_ROWS, ls])

    kk = _dot(hb, wk_ref[...])
    kconst = kconst_ref[...].astype(F32)
    for g in range(N_KV):
        c0 = K_SEL + g * 2 * LANES
        ksa_ref[0, g] = (kk[:, c0:c0 + 2 * LANES] + kconst).astype(BF16)
        c0 = K_WIN + g * LANES
        kwa_ref[0, g] = (kk[:, c0:c0 + LANES] + kconst[:, :LANES]).astype(BF16)
    kcx_ref[0] = kk[:, K_CMP:K_CMP + LANES]
    vcx_ref[0] = kk[:, K_CMP + LANES:K_COLS]

    mg = _dot(hb, wmg_ref[...])
    ga_ref[...] = _sigmoid(mg[:, :D_MODEL]).astype(BF16)

    u = _gelu(_dot(hb, wu_ref[...]))
    v = _gelu(_dot(hb, wv_ref[...]))
    mu = jnp.mean(v, axis=-1, keepdims=True)
    vc = v - mu
    var = jnp.mean(vc * vc, axis=-1, keepdims=True)
    vln = (vc * lax.rsqrt(var + LN_EPS) * lng_ref[...] + lnb_ref[...]).astype(BF16)

    row = lax.broadcasted_iota(I32, (CHUNK, CHUNK), 0)
    col = lax.broadcasted_iota(I32, (CHUNK, CHUNK), 1)
    tril = row >= col
    wsp = [jnp.where(tril, wsp_ref[g], 0.0).astype(BF16) for g in range(N_GROUPS_SGU)]
    low_half = col < SGU_GROUP_DIM
    bsp = bsp_ref[...]
    chunks = []
    for c in range(tm // CHUNK):
        rs = slice(c * CHUNK, (c + 1) * CHUNK)
        parts = []
        for p in range(SGU_WIDTH // LANES):
            cs = slice(p * LANES, (p + 1) * LANES)
            vblk = vln[rs, cs]
            mixed = jnp.where(low_half, _dot(wsp[2 * p], vblk), _dot(wsp[2 * p + 1], vblk))
            parts.append(u[rs, cs] * (mixed + bsp[:, cs]))
        chunks.append(jnp.concatenate(parts, axis=1))
    yb = jnp.concatenate(chunks, axis=0).astype(BF16)
    gbyb_ref[...] = (_sigmoid(mg[:, D_MODEL:]) * _dot(yb, wbb_ref[...])).astype(BF16)


def _proj(x2, b, s, gmix, wt, tconst, wk, kconst, wu, wv, wmg, lng, lnb, wsp, bsp, wbb, tm):
    n = x2.shape[0]
    tiles = s // tm
    qtiles = tm // Q_BLOCK
    full = lambda a: pl.BlockSpec(a.shape, lambda i, j: (0,) * a.ndim)
    rows = lambda w: pl.BlockSpec((tm, w), lambda i, j: (i * tiles + j, 0))
    per_g = lambda *blk: pl.BlockSpec((1, N_KV) + blk, lambda i, j: (i, 0, j) + (0,) * (len(blk) - 1))
    ins = (x2, gmix, wt, tconst, wk, kconst, wu, wv, wmg, lng, lnb, wsp, bsp, wbb)
    in_specs = [rows(D_MODEL)] + [full(a) for a in ins[1:]]
    in_specs[5] = pl.BlockSpec((tm, 2 * LANES), lambda i, j: (j, 0))
    nt = s // Q_BLOCK
    return pl.pallas_call(
        _proj_kernel,
        grid=(b, tiles),
        in_specs=in_specs,
        out_specs=[per_g(qtiles, LANES, GQA_R * Q_BLOCK), per_g(qtiles, GATE_ROWS, Q_BLOCK),
                   per_g(qtiles, HEAD_DIM, Q_BLOCK), per_g(qtiles, HEAD_DIM, Q_BLOCK),
                   per_g(tm, 2 * LANES), per_g(tm, LANES),
                   pl.BlockSpec((1, tm, LANES), lambda i, j: (i, j, 0)),
                   pl.BlockSpec((1, tm, LANES), lambda i, j: (i, j, 0)),
                   rows(D_MODEL), rows(D_MODEL)],
        out_shape=[jax.ShapeDtypeStruct((b, N_KV, nt, LANES, GQA_R * Q_BLOCK), BF16),
                   jax.ShapeDtypeStruct((b, N_KV, nt, GATE_ROWS, Q_BLOCK), F32),
                   jax.ShapeDtypeStruct((b, N_KV, nt, HEAD_DIM, Q_BLOCK), BF16),
                   jax.ShapeDtypeStruct((b, N_KV, nt, HEAD_DIM, Q_BLOCK), BF16),
                   jax.ShapeDtypeStruct((b, N_KV, s, 2 * LANES), BF16),
                   jax.ShapeDtypeStruct((b, N_KV, s, LANES), BF16),
                   jax.ShapeDtypeStruct((b, s, LANES), F32),
                   jax.ShapeDtypeStruct((b, s, LANES), F32),
                   jax.ShapeDtypeStruct((n, D_MODEL), BF16),
                   jax.ShapeDtypeStruct((n, D_MODEL), BF16)],
        compiler_params=pltpu.CompilerParams(dimension_semantics=("parallel", "parallel"),
                                             vmem_limit_bytes=VMEM_LIMIT),
        name="proj",
    )(*ins)


def _compress_kernel(xk_ref, xv_ref, pe_ref, w1_ref, b1_ref, w2k_ref, w2vt_ref, kc_ref, vct_ref):
    nc = kc_ref.shape[2]
    half = CMP_LEN // 2
    hids = []
    for which, x_ref in enumerate((xk_ref, xv_ref)):
        za = jnp.zeros((nc, N_KV * CMP_HID), F32)
        zb = jnp.zeros((nc, N_KV * CMP_HID), F32)
        for l in range(half):
            xl = x_ref[0, pl.ds(l, nc, stride=CMP_STRIDE), :]
            za = za + _dot((xl + pe_ref[which, 0, l:l + 1, :]).astype(BF16), w1_ref[which, 0, l])
            zb = zb + _dot((xl + pe_ref[which, 1, l:l + 1, :]).astype(BF16), w1_ref[which, 1, l])
        hid = _gelu(za + pltpu.roll(zb, nc - 1, 0) + b1_ref[which])
        row = lax.broadcasted_iota(I32, hid.shape, 0)
        hids.append(jnp.where(row < nc - 1, hid, 0.0).astype(BF16))
    for g in range(N_KV):
        cs = slice(g * CMP_HID, (g + 1) * CMP_HID)
        kc_ref[0, g] = _dot(hids[0][:, cs], w2k_ref[...]).astype(BF16)
        vt = _dot_nt(w2vt_ref[...], hids[1][:, cs])
        for c in range(nc // KEY_CHUNK):
            vct_ref[0, g, c] = vt[:, c * KEY_CHUNK:(c + 1) * KEY_CHUNK].astype(BF16)


def _compress(xk, xv, pes, w1, b1, w2k, w2vt):
    b, s, w = xk.shape
    nc = s // CMP_STRIDE
    full = lambda a: pl.BlockSpec(a.shape, lambda i: (0,) * a.ndim)
    seq = pl.BlockSpec((1, s, w), lambda i: (i, 0, 0))
    return pl.pallas_call(
        _compress_kernel,
        grid=(b,),
        in_specs=[seq, seq, full(pes), full(w1), full(b1), full(w2k), full(w2vt)],
        out_specs=[pl.BlockSpec((1, N_KV, nc, LANES), lambda i: (i, 0, 0, 0)),
                   pl.BlockSpec((1, N_KV, nc // KEY_CHUNK, HEAD_DIM, KEY_CHUNK), lambda i: (i, 0, 0, 0, 0))],
        out_shape=[jax.ShapeDtypeStruct((b, N_KV, nc, LANES), BF16),
                   jax.ShapeDtypeStruct((b, N_KV, nc // KEY_CHUNK, HEAD_DIM, KEY_CHUNK), BF16)],
        compiler_params=pltpu.CompilerParams(dimension_semantics=("parallel",),
                                             vmem_limit_bytes=VMEM_LIMIT),
        name="compress",
    )(xk, xv, pes, w1, b1, w2k, w2vt)


def _pairs():
    return [slice(pr * 2 * Q_BLOCK, (pr + 1) * 2 * Q_BLOCK) for pr in range(GQA_R // 2)]


def _attend(ks, vts, masks, q_ref, m_ref, l_ref, acc_ref):
    pairs = _pairs()
    scores = [[_dot(k, q_ref[:, ls]) for k in ks] for ls in pairs]
    for pr, ls in enumerate(pairs):
        sb = [masks[u](scores[pr][u], ls) for u in range(len(ks))]
        m_old = m_ref[0:1, ls]
        m_new = m_old
        for x in sb:
            m_new = jnp.maximum(m_new, jnp.max(x, axis=0, keepdims=True))
        alpha = jnp.exp(m_old - m_new)
        lsum = jnp.zeros_like(m_old)
        pv = jnp.zeros((HEAD_DIM, ls.stop - ls.start), F32)
        for u, x in enumerate(sb):
            p = jnp.exp(x - m_new)
            lsum = lsum + jnp.sum(p, axis=0, keepdims=True)
            pv = pv + _dot(vts[u], p.astype(BF16))
        l_ref[0:1, ls] = alpha * l_ref[0:1, ls] + lsum
        acc_ref[:, ls] = alpha * acc_ref[:, ls] + pv
        m_ref[0:1, ls] = m_new


def _nsa_kernel(qt_ref, gt_ref, kc_ref, vct_ref, ksa_ref, vst_ref, kwa_ref, vwt_ref, ovt_ref, pair_ref,
                o_ref, qa_ref, m_ref, l_ref, acc_ref, act_ref, *, n_sel, top_n):
    g = pl.program_id(1)
    i = pl.program_id(2)
    t0 = i * Q_BLOCK
    width = GQA_R * Q_BLOCK
    n_chunks = ksa_ref.shape[2] // KEY_CHUNK
    n_cchunks = kc_ref.shape[2] // KEY_CHUNK
    lane = lax.broadcasted_iota(I32, (1, width), 1)
    tq = t0 + (lane & (Q_BLOCK - 1))
    tf = tq.astype(F32)
    r = lane >> 7
    slope = jnp.where(r == 0, 0.5, jnp.where(r == 1, 0.25, jnp.where(r == 2, 0.125, 0.0625)))
    slope = slope * jnp.where(g == 0, 1.0, 0.0625)
    sub = lax.broadcasted_iota(I32, (KEY_CHUNK, 1), 0)
    pairs = _pairs()
    qt = qt_ref.at[0, 0, 0]

    def reset():
        m_ref[...] = jnp.full(m_ref.shape, NEG, F32)
        l_ref[...] = jnp.zeros(l_ref.shape, F32)
        acc_ref[...] = jnp.zeros(acc_ref.shape, F32)

    kcs = [kc_ref[0, 0, u * KEY_CHUNK:(u + 1) * KEY_CHUNK, :] for u in range(n_cchunks)]
    sc = [[_dot(k, qt[:, ls]) for k in kcs] for ls in pairs]
    o_c, ps = [], []
    for pr, ls in enumerate(pairs):
        sb = []
        for u in range(n_cchunks):
            n_col = u * KEY_CHUNK + sub
            c_pos = n_col.astype(F32) * CMP_STRIDE + (CMP_LEN - 1) / 2.0
            s = sc[pr][u] - slope[:, ls] * (tf[:, ls] - c_pos)
            sb.append(jnp.where(n_col * CMP_STRIDE + (CMP_LEN - 1) <= tq[:, ls], s, NEG))
        m = sb[0].max(axis=0, keepdims=True)
        for x in sb[1:]:
            m = jnp.maximum(m, jnp.max(x, axis=0, keepdims=True))
        pb = [jnp.exp(x - m) for x in sb]
        l = pb[0].sum(axis=0, keepdims=True)
        for p in pb[1:]:
            l = l + jnp.sum(p, axis=0, keepdims=True)
        w = (tq[:, ls] >= CMP_LEN - 1).astype(F32) / l
        oc = _dot(vct_ref[0, 0, 0], pb[0].astype(BF16))
        for u in range(1, n_cchunks):
            oc = oc + _dot(vct_ref[0, 0, u], pb[u].astype(BF16))
        o_c.append(oc * w)
        ps.append([p[:, :Q_BLOCK] * w[:, :Q_BLOCK] + p[:, Q_BLOCK:] * w[:, Q_BLOCK:] for p in pb])
    imp2 = jnp.zeros((LANES, 2 * Q_BLOCK), F32)
    for u in range(n_cchunks):
        psu = ps[0][u] + ps[1][u]
        hi = psu.astype(BF16)
        lo = (psu - hi.astype(F32)).astype(BF16)
        imp2 = imp2 + _dot(ovt_ref[u], jnp.concatenate([hi, lo], axis=1))
    imp_t = imp2[:, :Q_BLOCK] + imp2[:, Q_BLOCK:]

    j_io = lax.broadcasted_iota(I32, (LANES, Q_BLOCK), 0)
    tl = t0 + lax.broadcasted_iota(I32, (LANES, Q_BLOCK), 1)
    cur = tl >> 6
    forced = (j_io == 0) | (j_io == cur) | (j_io == cur - 1)
    prio = jnp.where(forced, FORCE, jnp.where(j_io * SEL_BLOCK <= tl, imp_t, -1.0))
    prio = jnp.where(j_io < n_sel, prio, NEG)
    sel_t = jnp.zeros((LANES, Q_BLOCK), F32)
    for _ in range(top_n):
        m = jnp.max(prio, axis=0, keepdims=True)
        idx = jnp.min(jnp.where(prio == m, j_io, LANES), axis=0, keepdims=True)
        hit = j_io == idx
        sel_t = jnp.where(hit, 1.0, sel_t)
        prio = jnp.where(hit, NEG, prio)

    qa_ref[0:LANES, :] = qt[...]
    bias_t = ((sel_t - 1.0) * MASK_BIAS).astype(BF16)
    for h in range(GQA_R):
        qa_ref[LANES:2 * LANES, h * Q_BLOCK:(h + 1) * Q_BLOCK] = bias_t

    per_block = _dot_nt(jnp.ones((8, Q_BLOCK), BF16), sel_t.astype(BF16))
    per_chunk = _dot(per_block.astype(BF16), pair_ref[...])
    n_act = jnp.int32(0)
    for c in range(n_chunks):
        act_ref[n_act] = jnp.int32(c)
        n_act = n_act + jnp.where((per_chunk[0, c] > 0.0) & (c < i), 1, 0)
    act_ref[n_act] = i
    for u in range(1, SEL_SLOTS):
        act_ref[n_act + u] = jnp.int32(-1)

    reset()
    ks, vts, masks = [], [], []
    for u in range(WINDOW // KEY_CHUNK + 1):
        c = i - WINDOW // KEY_CHUNK + u
        cl = jnp.maximum(c, 0)
        ks.append(kwa_ref[0, 0, pl.ds(pl.multiple_of(cl * KEY_CHUNK, KEY_CHUNK), KEY_CHUNK), :])
        vts.append(vwt_ref[0, 0, cl])
        kpos = jnp.where(c < 0, -2 * WINDOW, c * KEY_CHUNK) + sub

        def mask(s, ls, kpos=kpos):
            dist = tq[:, ls] - kpos
            return jnp.where((dist >= 0) & (dist < WINDOW), s, NEG)

        masks.append(mask)
    _attend(ks, vts, masks, qt, m_ref, l_ref, acc_ref)
    o_w = acc_ref[...] / l_ref[0:1, :]

    reset()

    def sel_body(k, carry):
        ks, vts, masks = [], [], []
        for u in range(SEL_SLOTS):
            c = act_ref[k * SEL_SLOTS + u]
            cl = jnp.maximum(c, 0)
            ks.append(ksa_ref[0, 0, pl.ds(pl.multiple_of(cl * KEY_CHUNK, KEY_CHUNK), KEY_CHUNK), :])
            vts.append(vst_ref[0, 0, cl])
            kpos = jnp.where(c < 0, n_chunks * KEY_CHUNK, c * KEY_CHUNK) + sub
            masks.append(lambda s, ls, kpos=kpos: jnp.where(kpos <= tq[:, ls], s, -MASK_BIAS))
        _attend(ks, vts, masks, qa_ref, m_ref, l_ref, acc_ref)
        return carry

    lax.fori_loop(0, (n_act + SEL_SLOTS) // SEL_SLOTS, sel_body, 0)
    o_s = acc_ref[...] / l_ref[0:1, :]

    gt = gt_ref[0, 0, 0]

    def gate(br):
        return jnp.concatenate([gt[br * GQA_R + h:br * GQA_R + h + 1, :] for h in range(GQA_R)], axis=1)

    o = gate(0) * jnp.concatenate(o_c, axis=1) + gate(1) * o_s + gate(2) * o_w
    o_ref[0] = jnp.concatenate([o[:, h * Q_BLOCK:(h + 1) * Q_BLOCK].T for h in range(GQA_R)],
                               axis=1).astype(BF16)


def _nsa(qt, gt, kc, vct, ksa, vst, kwa, vwt, ovt, pair, n_sel):
    b, g, nt, _, width = qt.shape
    s = ksa.shape[2]
    tile = lambda a: pl.BlockSpec((1, 1, 1) + a.shape[3:], lambda i, j, k: (i, j, k, 0, 0))
    whole = lambda a: pl.BlockSpec((1, 1) + a.shape[2:], lambda i, j, k: (i, j) + (0,) * (a.ndim - 2))
    const = lambda a: pl.BlockSpec(a.shape, lambda i, j, k: (0,) * a.ndim)
    kern = functools.partial(_nsa_kernel, n_sel=n_sel, top_n=min(SEL_TOPN, n_sel))
    return pl.pallas_call(
        kern,
        grid=(b, g, nt),
        in_specs=[tile(qt), tile(gt), whole(kc), whole(vct), whole(ksa), whole(vst), whole(kwa), whole(vwt),
                  const(ovt), const(pair)],
        out_specs=pl.BlockSpec((1, Q_BLOCK, GQA_R * HEAD_DIM), lambda i, j, k: (i, k, j)),
        out_shape=jax.ShapeDtypeStruct((b, s, ATTN_WIDTH), BF16),
        scratch_shapes=[pltpu.VMEM((2 * LANES, width), BF16),
                        pltpu.VMEM((8, width), F32),
                        pltpu.VMEM((8, width), F32),
                        pltpu.VMEM((HEAD_DIM, width), F32),
                        pltpu.SMEM((s // KEY_CHUNK + SEL_SLOTS,), I32)],
        compiler_params=pltpu.CompilerParams(dimension_semantics=("parallel", "parallel", "arbitrary"),
                                             vmem_limit_bytes=VMEM_LIMIT),
        name="nsa",
    )(qt, gt, kc, vct, ksa, vst, kwa, vwt, ovt, pair)


def _merge_kernel(x_ref, ya_ref, ga_ref, gbyb_ref, wba_ref, wout_ref, gmoe_ref, wrh_ref, wrl_ref, br_ref,
                  x1_ref, hm_ref, rw_ref, ri_ref, cnt_ref, carry_ref):
    tm = x_ref.shape[0]

    @pl.when(pl.program_id(0) == 0)
    def _():
        carry_ref[...] = jnp.zeros_like(carry_ref)

    merged = ga_ref[...].astype(F32) * _dot(ya_ref[...], wba_ref[...]) + gbyb_ref[...].astype(F32)
    x1 = x_ref[...] + _dot(merged.astype(BF16), wout_ref[...])
    x1_ref[...] = x1
    hm = _rms(x1, gmoe_ref[...])
    for a in range(ROW_TILE):
        hm_ref[pl.ds(a, tm, stride=ROW_TILE), :] = hm[:, a * LANES:(a + 1) * LANES]

    hh = hm.astype(BF16)
    hl = (hm - hh.astype(F32)).astype(BF16)
    wrh = wrh_ref[...]
    logits = _dot(hh, wrh) + _dot(hl, wrh) + _dot(hh, wrl_ref[...]) + br_ref[...]
    lane = lax.broadcasted_iota(I32, (tm, LANES), 1)
    lg = jnp.where(lane < N_EXPERTS, logits, NEG)
    vals, idxs = [], []
    for _ in range(TOP_K):
        m = jnp.max(lg, axis=1, keepdims=True)
        idx = jnp.min(jnp.where(lg == m, lane, LANES), axis=1, keepdims=True)
        vals.append(m)
        idxs.append(idx)
        lg = jnp.where(lane == idx, NEG, lg)
    ex = [jnp.exp(v - vals[0]) for v in vals]
    den = ex[0] + ex[1] + ex[2] + ex[3]

    hits = [lane == idx for idx in idxs]
    multi = jnp.zeros((tm, LANES), F32)
    for h in hits:
        multi = jnp.where(h, 1.0, multi)
    ltri = (lax.broadcasted_iota(I32, (tm, tm), 0) > lax.broadcasted_iota(I32, (tm, tm), 1))
    carry = carry_ref[0:1, :]
    cum = _dot(jnp.where(ltri, 1.0, 0.0).astype(BF16), multi.astype(BF16)) + carry
    rw = jnp.zeros((tm, LANES), F32)
    ri = jnp.zeros((tm, LANES), I32)
    for k in range(TOP_K):
        rank = jnp.sum(jnp.where(hits[k], cum, 0.0), axis=1, keepdims=True).astype(I32)
        rw = jnp.where(lane == k, ex[k] / den, rw)
        ri = jnp.where(lane == k, idxs[k], jnp.where(lane == TOP_K + k, rank, ri))
    rw_ref[...] = rw
    ri_ref[...] = ri
    new_carry = carry + jnp.sum(multi, axis=0, keepdims=True)
    carry_ref[...] = jnp.broadcast_to(new_carry, carry_ref.shape)
    cnt_ref[...] = jnp.broadcast_to(new_carry, cnt_ref.shape)


def _merge(x2, ya, ga, gbyb, wba, wout, gmoe, wrh, wrl, br, tm):
    n = x2.shape[0]
    full = lambda a: pl.BlockSpec(a.shape, lambda i: (0,) * a.ndim)
    rows = lambda w: pl.BlockSpec((tm, w), lambda i: (i, 0))
    return pl.pallas_call(
        _merge_kernel,
        grid=(n // tm,),
        in_specs=[rows(D_MODEL), rows(ATTN_WIDTH), rows(D_MODEL), rows(D_MODEL),
                  full(wba), full(wout), full(gmoe), full(wrh), full(wrl), full(br)],
        out_specs=[rows(D_MODEL), pl.BlockSpec((tm * ROW_TILE, LANES), lambda i: (i, 0)), rows(LANES), rows(LANES),
                   pl.BlockSpec((8, LANES), lambda i: (0, 0))],
        out_shape=[jax.ShapeDtypeStruct((n, D_MODEL), F32),
                   jax.ShapeDtypeStruct((n * ROW_TILE, LANES), F32),
                   jax.ShapeDtypeStruct((n, LANES), F32),
                   jax.ShapeDtypeStruct((n, LANES), I32),
                   jax.ShapeDtypeStruct((8, LANES), F32)],
        scratch_shapes=[pltpu.VMEM((8, LANES), F32)],
        compiler_params=pltpu.CompilerParams(dimension_semantics=("arbitrary",),
                                             vmem_limit_bytes=VMEM_LIMIT),
        name="merge",
    )(x2, ya, ga, gbyb, wba, wout, gmoe, wrh, wrl, br)


def _tile_copy(src, i, dst, d, sem):
    return pltpu.make_async_copy(src.at[pl.ds(pl.multiple_of(i * ROW_TILE, ROW_TILE), ROW_TILE)],
                                 dst.at[pl.ds(pl.multiple_of(d * ROW_TILE, ROW_TILE), ROW_TILE)], sem)


def _dispatch_kernel(seg_ref, dest_ref, hm_ref, xs_ref, zero_ref, sem, zsem, *, n_pad):
    tm = hm_ref.shape[0] // ROW_TILE

    @pl.when(pl.program_id(0) == 0)
    def _():
        zero_ref[...] = jnp.zeros_like(zero_ref)

        def seg(e, c):
            def fill(r, c2):
                _tile_copy(zero_ref, 0, xs_ref, r, zsem).start()
                return c2
            return lax.fori_loop(seg_ref[0, e], seg_ref[1, e], fill, c)

        lax.fori_loop(0, N_EXPERTS + 1, seg, 0)
        pad_rows = xs_ref.at[pl.ds(0, n_pad * ROW_TILE)]
        pltpu.make_async_copy(pad_rows, pad_rows, zsem).wait()

    def issue(i, c):
        for k in range(TOP_K):
            _tile_copy(hm_ref, i, xs_ref, dest_ref[i * TOP_K + k], sem).start(priority=k % 2)
        return c

    lax.fori_loop(0, tm, issue, 0)
    for k in range(TOP_K):
        pltpu.make_async_copy(hm_ref, xs_ref.at[pl.ds(0, tm * ROW_TILE)], sem).wait()


def _dispatch(seg, dest_flat, hm, n_slots, tm):
    n = hm.shape[0] // ROW_TILE
    kern = functools.partial(_dispatch_kernel, n_pad=n_slots - n * TOP_K)
    return pl.pallas_call(
        kern,
        grid_spec=pltpu.PrefetchScalarGridSpec(
            num_scalar_prefetch=1,
            grid=(n // tm,),
            in_specs=[pl.BlockSpec((tm * TOP_K,), lambda i, sg: (i,), memory_space=pltpu.SMEM),
                      pl.BlockSpec((tm * ROW_TILE, LANES), lambda i, sg: (i, 0))],
            out_specs=pl.BlockSpec(memory_space=pl.ANY),
            scratch_shapes=[pltpu.VMEM((ROW_TILE, LANES), F32),
                            pltpu.SemaphoreType.DMA(()), pltpu.SemaphoreType.DMA(())]),
        out_shape=jax.ShapeDtypeStruct((n_slots * ROW_TILE, LANES), F32),
        compiler_params=pltpu.CompilerParams(dimension_semantics=("arbitrary",),
                                             has_side_effects=True),
        name="dispatch",
    )(seg, dest_flat, hm)


def _expert_kernel(be_ref, nu_ref, xs_ref, wgu_ref, bgu_ref, wd_ref, bd_ref, y_ref, wgu_bf, wd_bf):
    i = pl.program_id(0)

    @pl.when(i >= nu_ref[0])
    def _():
        y_ref[...] = jnp.zeros_like(y_ref)

    @pl.when((i == 0) | (be_ref[i] != be_ref[jnp.maximum(i - 1, 0)]))
    def _():
        wgu_bf[...] = wgu_ref[0].astype(BF16)
        wd_bf[...] = wd_ref[0].astype(BF16)

    @pl.when(i < nu_ref[0])
    def _():
        x = jnp.concatenate([xs_ref[pl.ds(a, MOE_BLOCK, stride=ROW_TILE), :] for a in range(ROW_TILE)], axis=1)
        gu = _dot(x.astype(BF16), wgu_bf[...]) + bgu_ref[0]
        gate = jnp.minimum(gu[:, :D_FF], SWIGLU_LIMIT)
        up = jnp.clip(gu[:, D_FF:], -SWIGLU_LIMIT, SWIGLU_LIMIT)
        act = gate * _sigmoid(SWIGLU_ALPHA * gate) * (up + 1.0)
        y = _dot(act.astype(BF16), wd_bf[...]) + bd_ref[0]
        for a in range(ROW_TILE):
            y_ref[pl.ds(a, MOE_BLOCK, stride=ROW_TILE), :] = y[:, a * LANES:(a + 1) * LANES]


def _experts(blk_expert, n_used, xs, wgu, bgu, wd, bd):
    n_blocks = xs.shape[0] // (MOE_BLOCK * ROW_TILE)
    blk = lambda i, be, nu: (jnp.minimum(i, nu[0] - 1), 0)
    exp3 = lambda i, be, nu: (be[jnp.minimum(i, nu[0] - 1)], 0, 0)
    return pl.pallas_call(
        _expert_kernel,
        grid_spec=pltpu.PrefetchScalarGridSpec(
            num_scalar_prefetch=2,
            grid=(n_blocks,),
            in_specs=[pl.BlockSpec((MOE_BLOCK * ROW_TILE, LANES), blk),
                      pl.BlockSpec((1, D_MODEL, 2 * D_FF), exp3),
                      pl.BlockSpec((1, 1, 2 * D_FF), exp3),
                      pl.BlockSpec((1, D_FF, D_MODEL), exp3),
                      pl.BlockSpec((1, 1, D_MODEL), exp3)],
            out_specs=pl.BlockSpec((MOE_BLOCK * ROW_TILE, LANES), lambda i, be, nu: (i, 0)),
            scratch_shapes=[pltpu.VMEM((D_MODEL, 2 * D_FF), BF16), pltpu.VMEM((D_FF, D_MODEL), BF16)]),
        out_shape=jax.ShapeDtypeStruct(xs.shape, F32),
        compiler_params=pltpu.CompilerParams(dimension_semantics=("arbitrary",),
                                             vmem_limit_bytes=VMEM_LIMIT),
        name="experts",
    )(blk_expert, n_used, xs, wgu, bgu, wd, bd)


def _combine_kernel(dest_ref, rw_ref, x1_ref, gfin_ref, y_ref, o_ref, ybuf, sem):
    tm = x1_ref.shape[0]

    def issue(i, c):
        for k in range(TOP_K):
            _tile_copy(y_ref, dest_ref[i * TOP_K + k], ybuf.at[k], i, sem).start(priority=k % 2)
        return c

    lax.fori_loop(0, tm, issue, 0)
    for k in range(TOP_K):
        pltpu.make_async_copy(y_ref.at[pl.ds(0, tm * ROW_TILE)], ybuf.at[k], sem).wait()

    rw = rw_ref[...]
    x1 = x1_ref[...]
    cols = []
    for a in range(ROW_TILE):
        acc = x1[:, a * LANES:(a + 1) * LANES]
        for k in range(TOP_K):
            acc = acc + rw[:, k:k + 1] * ybuf[k, pl.ds(a, tm, stride=ROW_TILE), :]
        cols.append(acc)
    o_ref[...] = _rms(jnp.concatenate(cols, axis=1), gfin_ref[...])


def _combine(dest_flat, rw, x1, gfin, y, tm):
    n = x1.shape[0]
    return pl.pallas_call(
        _combine_kernel,
        grid=(n // tm,),
        in_specs=[pl.BlockSpec((tm * TOP_K,), lambda i: (i,), memory_space=pltpu.SMEM),
                  pl.BlockSpec((tm, LANES), lambda i: (i, 0)),
                  pl.BlockSpec((tm, D_MODEL), lambda i: (i, 0)),
                  pl.BlockSpec((1, D_MODEL), lambda i: (0, 0)),
                  pl.BlockSpec(memory_space=pl.ANY)],
        out_specs=pl.BlockSpec((tm, D_MODEL), lambda i: (i, 0)),
        out_shape=jax.ShapeDtypeStruct((n, D_MODEL), F32),
        scratch_shapes=[pltpu.VMEM((TOP_K, tm * ROW_TILE, LANES), F32), pltpu.SemaphoreType.DMA(())],
        compiler_params=pltpu.CompilerParams(dimension_semantics=("arbitrary",),
                                             vmem_limit_bytes=VMEM_LIMIT),
        name="combine",
    )(dest_flat, rw, x1, gfin, y)


def _overlap_t(nc, n_cmp, n_sel):
    cs = np.arange(n_cmp)[None, :] * CMP_STRIDE
    ss = np.arange(n_sel)[:, None] * SEL_BLOCK
    ov = np.clip(np.minimum(cs + CMP_LEN, ss + SEL_BLOCK) - np.maximum(cs, ss), 0, None) / CMP_LEN
    out = np.zeros((LANES, nc), np.float32)
    out[:n_sel, :n_cmp] = ov
    return jnp.asarray(out, BF16)


def _layer(x2, b, s, g_mix, w_in, w_ck1, b_ck1, w_ck2, pe_k, w_cv1, b_cv1, w_cv2, pe_v,
           sgu_ln_g, sgu_ln_b, w_spatial, b_spatial, w_branch_a, w_branch_b, w_out,
           g_moe, w_router, b_router, w_gate_up, b_gate_up, w_down, b_down):
    n = b * s
    nc = s // CMP_STRIDE
    n_cmp = (s - CMP_LEN) // CMP_STRIDE + 1
    n_sel = s // SEL_BLOCK
    assert nc % KEY_CHUNK == 0 and n_sel <= LANES and n_cmp == nc - 1
    tm = 512
    assert s % tm == 0

    p0 = ATTN_WIDTH
    p1 = p0 + 6 * KV_WIDTH
    p2 = p1 + N_NSA_BRANCH * N_HEADS
    p3 = p2 + SGU_WIDTH
    p4 = p3 + SGU_WIDTH
    zpad = lambda a, w: jnp.pad(a, ((0, 0),) * (a.ndim - 1) + ((0, w - a.shape[-1]),))
    wq = zpad((w_in[:, :p0] * HEAD_DIM ** -0.5).reshape(D_MODEL, N_HEADS, HEAD_DIM), LANES)
    wkv = w_in[:, p0:p1].reshape(D_MODEL, 6, N_KV, HEAD_DIM)
    wng = w_in[:, p1:p2].reshape(D_MODEL, N_KV, GQA_R, N_NSA_BRANCH).transpose(0, 1, 3, 2)
    wng = zpad(wng.reshape(D_MODEL, N_KV, N_NSA_BRANCH * GQA_R), GATE_ROWS)
    wt = jnp.concatenate([wq.reshape(D_MODEL, -1), wkv[:, 3].reshape(D_MODEL, -1),
                          wkv[:, 5].reshape(D_MODEL, -1), wng.reshape(D_MODEL, -1)], axis=1).T.astype(BF16)
    slopes = 2.0 ** (-8.0 * np.arange(1, N_HEADS + 1) / N_HEADS)
    tcol = np.zeros((T_ROWS, 1), np.float32)
    tcol[T_Q + np.arange(N_HEADS) * LANES + HEAD_DIM, 0] = slopes * SEL_BLOCK
    tcol[T_Q + np.arange(N_HEADS) * LANES + HEAD_DIM + 1, 0] = slopes
    tconst = jnp.asarray(np.broadcast_to(tcol, (T_ROWS, 2 * Q_BLOCK)))
    wk = jnp.concatenate([zpad(wkv[:, 2, g], 2 * LANES) for g in range(N_KV)]
                         + [zpad(wkv[:, 4, g], LANES) for g in range(N_KV)]
                         + [wkv[:, 0].reshape(D_MODEL, -1), wkv[:, 1].reshape(D_MODEL, -1)], axis=1).astype(BF16)
    pos = np.arange(s)
    kc_np = np.zeros((s, 2 * LANES), np.float32)
    kc_np[:, HEAD_DIM] = pos // SEL_BLOCK
    kc_np[:, HEAD_DIM + 1] = pos % SEL_BLOCK
    kc_np[pos, LANES + pos // SEL_BLOCK] = 1.0
    kconst = jnp.asarray(kc_np, BF16)
    wu = w_in[:, p2:p3].astype(BF16)
    wv = w_in[:, p3:p4].astype(BF16)
    wmg = w_in[:, p4:].astype(BF16)
    bsp = jnp.repeat(b_spatial.T, SGU_GROUP_DIM, axis=1)

    qt, gt, vst, vwt, ksa, kwa, kcx, vcx, ga, gbyb = _proj(
        x2, b, s, g_mix[None], wt, tconst, wk, kconst, wu, wv, wmg, sgu_ln_g[None], sgu_ln_b[None],
        w_spatial, bsp, w_branch_b.astype(BF16), tm)

    half = CMP_LEN // 2
    eye = jnp.eye(N_KV, dtype=F32)[None, None, :, None, :, None]
    bdiag = lambda w: (w.reshape(2, half, 1, HEAD_DIM, 1, CMP_HID) * eye).reshape(
        2, half, N_KV * HEAD_DIM, N_KV * CMP_HID)
    w1 = jnp.stack([bdiag(w_ck1), bdiag(w_cv1)]).astype(BF16)
    pes = jnp.stack([jnp.tile(pe_k.reshape(2, half, HEAD_DIM), (1, 1, N_KV)),
                     jnp.tile(pe_v.reshape(2, half, HEAD_DIM), (1, 1, N_KV))])
    b1 = jnp.stack([jnp.tile(b_ck1, N_KV), jnp.tile(b_cv1, N_KV)])[:, None, :]
    kc, vct = _compress(kcx, vcx, pes, w1, b1, zpad(w_ck2, LANES).astype(BF16), w_cv2.T.astype(BF16))

    ovt = _overlap_t(nc, n_cmp, n_sel).reshape(LANES, nc // KEY_CHUNK, KEY_CHUNK).transpose(1, 0, 2)
    blk = np.arange(LANES)
    pair = jnp.asarray(blk[:, None] // (KEY_CHUNK // SEL_BLOCK) == blk[None, :], BF16)
    ya = _nsa(qt, gt, kc, vct, ksa, vst, kwa, vwt, ovt, pair, n_sel).reshape(n, ATTN_WIDTH)

    wr = jnp.pad(w_router, ((0, 0), (0, LANES - N_EXPERTS)))
    wrh = wr.astype(BF16)
    wrl = (wr - wrh.astype(F32)).astype(BF16)
    br = jnp.pad(b_router, (0, LANES - N_EXPERTS))[None]
    x1, hm, rw, ri, cnt = _merge(x2, ya, ga, gbyb, w_branch_a.astype(BF16), w_out.astype(BF16),
                                 g_moe[None], wrh, wrl, br, tm)

    counts = cnt[0, :N_EXPERTS].astype(I32)
    padded = (counts + MOE_BLOCK - 1) // MOE_BLOCK * MOE_BLOCK
    pad_end = jnp.cumsum(padded)
    pad_start = pad_end - padded
    dest = (pad_start[ri[:, :TOP_K]] + ri[:, TOP_K:2 * TOP_K]).reshape(-1)
    n_blocks = -(-(n * TOP_K) // MOE_BLOCK) + N_EXPERTS
    blk_start = jnp.arange(n_blocks, dtype=I32) * MOE_BLOCK
    blk_expert = jnp.minimum(jnp.sum((pad_end[None, :] <= blk_start[:, None]).astype(I32), axis=1),
                             N_EXPERTS - 1)
    n_used = (pad_end[-1:] // MOE_BLOCK).astype(I32)
    n_slots = n_blocks * MOE_BLOCK
    seg = jnp.stack([jnp.concatenate([pad_start + counts, pad_end[-1:]]),
                     jnp.concatenate([pad_end, jnp.full((1,), n_slots, I32)])]).astype(I32)

    tg = 256
    xs = _dispatch(seg, dest, hm, n_slots, tg)
    y = _experts(blk_expert, n_used, xs, w_gate_up, b_gate_up[:, None, :], w_down, b_down[:, None, :])
    return dest, rw, x1, y, tg


def kernel(x, g_mix, w_in, w_ck1, b_ck1, w_ck2, pe_k, w_cv1, b_cv1, w_cv2, pe_v, sgu_ln_g, sgu_ln_b, w_spatial, b_spatial, w_branch_a, w_branch_b, w_out, g_moe, w_router, b_router, w_gate_up, b_gate_up, w_down, b_down, g_final):
    b, s, d = x.shape
    assert g_mix.shape[0] == 1, "the final rmsnorm is fused into the single layer's combine step"
    l = 0
    dest, rw, x1, y, tg = _layer(
        x.reshape(b * s, d), b, s, g_mix[l], w_in[l], w_ck1[l], b_ck1[l], w_ck2[l], pe_k[l], w_cv1[l],
        b_cv1[l], w_cv2[l], pe_v[l], sgu_ln_g[l], sgu_ln_b[l], w_spatial[l], b_spatial[l], w_branch_a[l],
        w_branch_b[l], w_out[l], g_moe[l], w_router[l], b_router[l], w_gate_up[l], b_gate_up[l],
        w_down[l], b_down[l])
    return _combine(dest, rw, x1, g_final[None], y, tg).reshape(b, s, d)
```

```python
import functools

import numpy as np
import jax
import jax.numpy as jnp
from jax import lax
from jax.experimental import pallas as pl
from jax.experimental.pallas import tpu as pltpu

D_MODEL = 1024
N_HEADS = 8
HEAD_DIM = 64
N_KV = 2
GQA_R = N_HEADS // N_KV
ATTN_WIDTH = N_HEADS * HEAD_DIM
KV_WIDTH = N_KV * HEAD_DIM
CMP_LEN = 32
CMP_STRIDE = 16
CMP_HID = 128
SEL_BLOCK = 64
SEL_TOPN = 16
WINDOW = 512
Q_BLOCK = 128
N_NSA_BRANCH = 3
SGU_WIDTH = 512
N_GROUPS_SGU = 8
SGU_GROUP_DIM = SGU_WIDTH // N_GROUPS_SGU
CHUNK = 128
N_EXPERTS = 32
TOP_K = 4
D_FF = D_MODEL
SWIGLU_LIMIT = 7.0
SWIGLU_ALPHA = 1.702
MOE_BLOCK = 512
RMS_EPS = 1e-5
LN_EPS = 1e-5
NEG = -1e30
FORCE = 1e4

LANES = 128
ROW_TILE = D_MODEL // LANES
MASK_BIAS = 1e9
LOG2E = float(np.log2(np.e))
KEY_CHUNK = 128
SEL_SLOTS = 4
GATE_ROWS = 16
VMEM_LIMIT = 56 * 1024 * 1024

F32 = jnp.float32
BF16 = jnp.bfloat16
I32 = jnp.int32

_NT = (((1,), (1,)), ((), ()))


def _dot(a, b):
    return jnp.dot(a, b, preferred_element_type=F32)


def _dot_nt(a, b):
    return lax.dot_general(a, b, _NT, preferred_element_type=F32)


def _sigmoid(x):
    return 1.0 / (1.0 + jnp.exp(-x))


def _gelu(x):
    c = np.float32(np.sqrt(2.0 / np.pi))
    return 0.5 * x * (1.0 + jnp.tanh(c * (x + 0.044715 * (x * x * x))))


def _rms(x, g):
    return x * lax.rsqrt(jnp.mean(x * x, axis=-1, keepdims=True) + RMS_EPS) * g


T_Q = 0
T_VS = T_Q + N_HEADS * LANES
T_VW = T_VS + N_KV * HEAD_DIM
T_GATE = T_VW + N_KV * HEAD_DIM
T_ROWS = T_GATE + N_KV * GATE_ROWS
K_SEL = 0
K_WIN = K_SEL + N_KV * 2 * LANES
K_CMP = K_WIN + N_KV * LANES
K_COLS = K_CMP + 4 * HEAD_DIM


def _proj_kernel(x_ref, gmix_ref, wt_ref, tconst_ref, wk_ref, kconst_ref, wu_ref, wv_ref, wmg_ref,
                 lng_ref, lnb_ref, wsp_ref, bsp_ref, wbb_ref,
                 qt_ref, gt_ref, vst_ref, vwt_ref, ksa_ref, kwa_ref, kcx_ref, vcx_ref, ga_ref, gbyb_ref):
    tm = x_ref.shape[0]
    hb = _rms(x_ref[...], gmix_ref[...]).astype(BF16)

    for h in range(tm // (2 * Q_BLOCK)):
        t = _dot_nt(wt_ref[...], hb[h * 2 * Q_BLOCK:(h + 1) * 2 * Q_BLOCK]) + tconst_ref[...]
        for jj in range(2):
            j = 2 * h + jj
            ls = slice(jj * Q_BLOCK, (jj + 1) * Q_BLOCK)
            for g in range(N_KV):
                for r in range(GQA_R):
                    r0 = T_Q + (g * GQA_R + r) * LANES
                    qt_ref[0, g, j, :, r * Q_BLOCK:(r + 1) * Q_BLOCK] = t[r0:r0 + LANES, ls].astype(BF16)
                vst_ref[0, g, j] = t[T_VS + g * HEAD_DIM:T_VS + (g + 1) * HEAD_DIM, ls].astype(BF16)
                vwt_ref[0, g, j] = t[T_VW + g * HEAD_DIM:T_VW + (g + 1) * HEAD_DIM, ls].astype(BF16)
                gt_ref[0, g, j] = _sigmoid(t[T_GATE + g * GATE_ROWS:T_GATE + (g + 1) * GATE_ROWS, ls])

    kk = _dot(hb, wk_ref[...])
    kconst = kconst_ref[...].astype(F32)
    for g in range(N_KV):
        c0 = K_SEL + g * 2 * LANES
        ksa_ref[0, g] = (kk[:, c0:c0 + 2 * LANES] + kconst).astype(BF16)
        c0 = K_WIN + g * LANES
        kwa_ref[0, g] = (kk[:, c0:c0 + LANES] + kconst[:, :LANES]).astype(BF16)
    kcx_ref[0] = kk[:, K_CMP:K_CMP + LANES]
    vcx_ref[0] = kk[:, K_CMP + LANES:K_COLS]

    mg = _dot(hb, wmg_ref[...])
    ga_ref[...] = _sigmoid(mg[:, :D_MODEL]).astype(BF16)

    u = _gelu(_dot(hb, wu_ref[...]))
    v = _gelu(_dot(hb, wv_ref[...]))
    mu = jnp.mean(v, axis=-1, keepdims=True)
    vc = v - mu
    var = jnp.mean(vc * vc, axis=-1, keepdims=True)
    vln = (vc * lax.rsqrt(var + LN_EPS) * lng_ref[...] + lnb_ref[...]).astype(BF16)

    row = lax.broadcasted_iota(I32, (CHUNK, CHUNK), 0)
    col = lax.broadcasted_iota(I32, (CHUNK, CHUNK), 1)
    tril = row >= col
    wsp = [jnp.where(tril, wsp_ref[g], 0.0).astype(BF16) for g in range(N_GROUPS_SGU)]
    low_half = col < SGU_GROUP_DIM
    bsp = bsp_ref[...]
    chunks = []
    for c in range(tm // CHUNK):
        rs = slice(c * CHUNK, (c + 1) * CHUNK)
        parts = []
        for p in range(SGU_WIDTH // LANES):
            cs = slice(p * LANES, (p + 1) * LANES)
            vblk = vln[rs, cs]
            mixed = jnp.where(low_half, _dot(wsp[2 * p], vblk), _dot(wsp[2 * p + 1], vblk))
            parts.append(u[rs, cs] * (mixed + bsp[:, cs]))
        chunks.append(jnp.concatenate(parts, axis=1))
    yb = jnp.concatenate(chunks, axis=0).astype(BF16)
    gbyb_ref[...] = (_sigmoid(mg[:, D_MODEL:]) * _dot(yb, wbb_ref[...])).astype(BF16)


def _proj(x2, b, s, gmix, wt, tconst, wk, kconst, wu, wv, wmg, lng, lnb, wsp, bsp, wbb, tm):
    n = x2.shape[0]
    tiles = s // tm
    qtiles = tm // Q_BLOCK
    full = lambda a: pl.BlockSpec(a.shape, lambda i, j: (0,) * a.ndim)
    rows = lambda w: pl.BlockSpec((tm, w), lambda i, j: (i * tiles + j, 0))
    per_g = lambda *blk: pl.BlockSpec((1, N_KV) + blk, lambda i, j: (i, 0, j) + (0,) * (len(blk) - 1))
    ins = (x2, gmix, wt, tconst, wk, kconst, wu, wv, wmg, lng, lnb, wsp, bsp, wbb)
    in_specs = [rows(D_MODEL)] + [full(a) for a in ins[1:]]
    in_specs[5] = pl.BlockSpec((tm, 2 * LANES), lambda i, j: (j, 0))
    nt = s // Q_BLOCK
    return pl.pallas_call(
        _proj_kernel,
        grid=(b, tiles),
        in_specs=in_specs,
        out_specs=[per_g(qtiles, LANES, GQA_R * Q_BLOCK), per_g(qtiles, GATE_ROWS, Q_BLOCK),
                   per_g(qtiles, HEAD_DIM, Q_BLOCK), per_g(qtiles, HEAD_DIM, Q_BLOCK),
                   per_g(tm, 2 * LANES), per_g(tm, LANES),
                   pl.BlockSpec((1, tm, LANES), lambda i, j: (i, j, 0)),
                   pl.BlockSpec((1, tm, LANES), lambda i, j: (i, j, 0)),
                   rows(D_MODEL), rows(D_MODEL)],
        out_shape=[jax.ShapeDtypeStruct((b, N_KV, nt, LANES, GQA_R * Q_BLOCK), BF16),
                   jax.ShapeDtypeStruct((b, N_KV, nt, GATE_ROWS, Q_BLOCK), F32),
                   jax.ShapeDtypeStruct((b, N_KV, nt, HEAD_DIM, Q_BLOCK), BF16),
                   jax.ShapeDtypeStruct((b, N_KV, nt, HEAD_DIM, Q_BLOCK), BF16),
                   jax.ShapeDtypeStruct((b, N_KV, s, 2 * LANES), BF16),
                   jax.ShapeDtypeStruct((b, N_KV, s, LANES), BF16),
                   jax.ShapeDtypeStruct((b, s, LANES), F32),
                   jax.ShapeDtypeStruct((b, s, LANES), F32),
                   jax.ShapeDtypeStruct((n, D_MODEL), BF16),
                   jax.ShapeDtypeStruct((n, D_MODEL), BF16)],
        compiler_params=pltpu.CompilerParams(dimension_semantics=("parallel", "parallel"),
                                             vmem_limit_bytes=VMEM_LIMIT),
        name="proj",
    )(*ins)


def _compress_kernel(xk_ref, xv_ref, pe_ref, w1_ref, b1_ref, w2k_ref, w2vt_ref, kconst_ref, kc_ref, vct_ref):
    nc = kc_ref.shape[2]
    half = CMP_LEN // 2
    hids = []
    for which, x_ref in enumerate((xk_ref, xv_ref)):
        za = jnp.zeros((nc, N_KV * CMP_HID), F32)
        zb = jnp.zeros((nc, N_KV * CMP_HID), F32)
        for l in range(half):
            xl = x_ref[0, pl.ds(l, nc, stride=CMP_STRIDE), :]
            za = za + _dot((xl + pe_ref[which, 0, l:l + 1, :]).astype(BF16), w1_ref[which, 0, l])
            zb = zb + _dot((xl + pe_ref[which, 1, l:l + 1, :]).astype(BF16), w1_ref[which, 1, l])
        hid = _gelu(za + pltpu.roll(zb, nc - 1, 0) + b1_ref[which])
        row = lax.broadcasted_iota(I32, hid.shape, 0)
        hids.append(jnp.where(row < nc - 1, hid, 0.0).astype(BF16))
    for g in range(N_KV):
        cs = slice(g * CMP_HID, (g + 1) * CMP_HID)
        kc_ref[0, g] = (_dot(hids[0][:, cs], w2k_ref[...]) + kconst_ref[...]).astype(BF16)
        vt = _dot_nt(w2vt_ref[...], hids[1][:, cs])
        for c in range(nc // KEY_CHUNK):
            vct_ref[0, g, c] = vt[:, c * KEY_CHUNK:(c + 1) * KEY_CHUNK].astype(BF16)


def _compress(xk, xv, pes, w1, b1, w2k, w2vt, kconst):
    b, s, w = xk.shape
    nc = s // CMP_STRIDE
    full = lambda a: pl.BlockSpec(a.shape, lambda i: (0,) * a.ndim)
    seq = pl.BlockSpec((1, s, w), lambda i: (i, 0, 0))
    return pl.pallas_call(
        _compress_kernel,
        grid=(b,),
        in_specs=[seq, seq, full(pes), full(w1), full(b1), full(w2k), full(w2vt), full(kconst)],
        out_specs=[pl.BlockSpec((1, N_KV, nc, LANES), lambda i: (i, 0, 0, 0)),
                   pl.BlockSpec((1, N_KV, nc // KEY_CHUNK, HEAD_DIM, KEY_CHUNK), lambda i: (i, 0, 0, 0, 0))],
        out_shape=[jax.ShapeDtypeStruct((b, N_KV, nc, LANES), BF16),
                   jax.ShapeDtypeStruct((b, N_KV, nc // KEY_CHUNK, HEAD_DIM, KEY_CHUNK), BF16)],
        compiler_params=pltpu.CompilerParams(dimension_semantics=("parallel",),
                                             vmem_limit_bytes=VMEM_LIMIT),
        name="compress",
    )(xk, xv, pes, w1, b1, w2k, w2vt, kconst)


def _pairs():
    return [slice(pr * 2 * Q_BLOCK, (pr + 1) * 2 * Q_BLOCK) for pr in range(GQA_R // 2)]


def _colmax(blocks):
    part = None
    for x in blocks:
        y = jnp.max(x.reshape(-1, 8, x.shape[-1]), axis=0)
        part = y if part is None else jnp.maximum(part, y)
    return jnp.max(part, axis=0, keepdims=True)


def _colsum(blocks):
    part = None
    for x in blocks:
        y = jnp.sum(x.reshape(-1, 8, x.shape[-1]), axis=0)
        part = y if part is None else part + y
    return jnp.sum(part, axis=0, keepdims=True)


def _softmax_pv(scores, vts, masks, m_ref, l_ref, acc_ref):
    for pr, ls in enumerate(_pairs()):
        sb = [x if masks[u] is None else masks[u](x, ls) for u, x in enumerate(scores[pr])]
        m_old = m_ref[0:1, ls]
        m_new = jnp.maximum(m_old, _colmax(sb))
        alpha = jnp.exp2(m_old - m_new)
        ps = [jnp.exp2(x - m_new) for x in sb]
        pv = _dot(vts[0], ps[0].astype(BF16))
        for u in range(1, len(ps)):
            pv = pv + _dot(vts[u], ps[u].astype(BF16))
        l_ref[0:1, ls] = alpha * l_ref[0:1, ls] + _colsum(ps)
        acc_ref[:, ls] = alpha * acc_ref[:, ls] + pv
        m_ref[0:1, ls] = m_new


def _nsa_kernel(qt_ref, gt_ref, kc_ref, vct_ref, ksa_ref, vst_ref, kwa_ref, vwt_ref, ovt_ref, pair_ref, tri_ref,
                o_ref, qa_ref, s0_ref, s1_ref, m_ref, l_ref, acc_ref, act_ref, *, n_sel, top_n):
    i = pl.program_id(2)
    t0 = i * Q_BLOCK
    width = GQA_R * Q_BLOCK
    n_chunks = ksa_ref.shape[2] // KEY_CHUNK
    n_cchunks = kc_ref.shape[2] // KEY_CHUNK
    lane = lax.broadcasted_iota(I32, (1, width), 1)
    tq = t0 + (lane & (Q_BLOCK - 1))
    sub = lax.broadcasted_iota(I32, (KEY_CHUNK, 1), 0)
    pairs = _pairs()
    qt = qt_ref.at[0, 0, 0]

    def reset():
        m_ref[...] = jnp.full(m_ref.shape, NEG, F32)
        l_ref[...] = jnp.zeros(l_ref.shape, F32)
        acc_ref[...] = jnp.zeros(acc_ref.shape, F32)

    kcs = [kc_ref[0, 0, u * KEY_CHUNK:(u + 1) * KEY_CHUNK, :] for u in range(n_cchunks)]
    sc = [[_dot(k, qt[:, ls]) for k in kcs] for ls in pairs]
    o_c, ps = [], []
    for pr, ls in enumerate(pairs):
        sb = []
        for u in range(n_cchunks):
            c_end = (u * KEY_CHUNK + sub) * CMP_STRIDE + (CMP_LEN - 1)
            sb.append(jnp.where(c_end <= tq[:, ls], sc[pr][u], NEG))
        m = _colmax(sb)
        pb = [jnp.exp2(x - m) for x in sb]
        l = _colsum(pb)
        w = (tq[:, ls] >= CMP_LEN - 1).astype(F32) / l
        oc = _dot(vct_ref[0, 0, 0], pb[0].astype(BF16))
        for u in range(1, n_cchunks):
            oc = oc + _dot(vct_ref[0, 0, u], pb[u].astype(BF16))
        o_c.append(oc * w)
        ps.append([p[:, :Q_BLOCK] * w[:, :Q_BLOCK] + p[:, Q_BLOCK:] * w[:, Q_BLOCK:] for p in pb])
    imp2 = jnp.zeros((LANES, 2 * Q_BLOCK), F32)
    for u in range(n_cchunks):
        psu = ps[0][u] + ps[1][u]
        hi = psu.astype(BF16)
        lo = (psu - hi.astype(F32)).astype(BF16)
        imp2 = imp2 + _dot(ovt_ref[u], jnp.concatenate([hi, lo], axis=1))
    imp_t = imp2[:, :Q_BLOCK] + imp2[:, Q_BLOCK:]

    j_io = lax.broadcasted_iota(I32, (LANES, Q_BLOCK), 0)
    tl = t0 + lax.broadcasted_iota(I32, (LANES, Q_BLOCK), 1)
    cur = tl >> 6
    forced = (j_io == 0) | (j_io == cur) | (j_io == cur - 1)
    prio = jnp.where(forced, FORCE, jnp.where(j_io * SEL_BLOCK <= tl, imp_t, -1.0))
    prio = jnp.where(j_io < n_sel, prio, NEG)
    sel_t = jnp.zeros((LANES, Q_BLOCK), F32)
    for _ in range(top_n):
        m = jnp.max(prio, axis=0, keepdims=True)
        idx = jnp.min(jnp.where(prio == m, j_io, LANES), axis=0, keepdims=True)
        hit = j_io == idx
        sel_t = jnp.where(hit, 1.0, sel_t)
        prio = jnp.where(hit, NEG, prio)

    qa_ref[0:LANES, :] = qt[...]
    bias_t = ((sel_t - 1.0) * MASK_BIAS).astype(BF16)
    for h in range(GQA_R):
        qa_ref[LANES:2 * LANES, h * Q_BLOCK:(h + 1) * Q_BLOCK] = bias_t

    per_block = _dot_nt(jnp.ones((8, Q_BLOCK), BF16), sel_t.astype(BF16))
    per_chunk = _dot(per_block.astype(BF16), pair_ref[...])
    n_act = jnp.int32(0)
    for c in range(n_chunks):
        act_ref[n_act] = jnp.int32(c)
        n_act = n_act + jnp.where((per_chunk[0, c] > 0.0) & (c < i), 1, 0)
    act_ref[n_act] = i
    for u in range(1, 2 * SEL_SLOTS):
        act_ref[n_act + u] = jnp.int32(-1)

    reset()
    n_win = WINDOW // KEY_CHUNK
    ks, vts, masks = [], [], []
    for u in range(n_win + 1):
        c = i - n_win + u
        cl = jnp.maximum(c, 0)
        ks.append(kwa_ref[0, 0, pl.ds(pl.multiple_of(cl * KEY_CHUNK, KEY_CHUNK), KEY_CHUNK), :])
        vts.append(vwt_ref[0, 0, cl])
        off = jnp.where(c < 0, NEG, 0.0)
        if u == 0:
            edge = tri_ref[0] + off
            masks.append(lambda x, ls, edge=edge: x + edge)
        elif u == n_win:
            masks.append(lambda x, ls: x + tri_ref[1])
        else:
            masks.append(lambda x, ls, off=off: x + off)
    _softmax_pv([[_dot(k, qt[:, ls]) for k in ks] for ls in pairs], vts, masks, m_ref, l_ref, acc_ref)
    o_w = acc_ref[...] / l_ref[0:1, :]

    reset()

    def sel_scores(k, dst):
        for u in range(SEL_SLOTS):
            cl = jnp.maximum(act_ref[k * SEL_SLOTS + u], 0)
            keys = ksa_ref[0, 0, pl.ds(pl.multiple_of(cl * KEY_CHUNK, KEY_CHUNK), KEY_CHUNK), :]
            for ls in pairs:
                dst[u, :, ls] = _dot(keys, qa_ref[:, ls])

    def sel_step(k, cur, nxt, last):
        sel_scores(k + 1, nxt)
        vts, masks = [], []
        for u in range(SEL_SLOTS):
            c = act_ref[k * SEL_SLOTS + u]
            vts.append(vst_ref[0, 0, jnp.maximum(c, 0)])
            if last:
                kpos = jnp.where(c < 0, n_chunks * KEY_CHUNK, c * KEY_CHUNK) + sub
                masks.append(lambda x, ls, kpos=kpos: jnp.where(kpos <= tq[:, ls], x, -MASK_BIAS))
            else:
                masks.append(None)
        scores = [[cur[u, :, ls] for u in range(SEL_SLOTS)] for ls in pairs]
        _softmax_pv(scores, vts, masks, m_ref, l_ref, acc_ref)

    n_steps = (n_act + SEL_SLOTS) // SEL_SLOTS

    def sel_body(k, carry):
        for odd, (cur, nxt) in enumerate(((s0_ref, s1_ref), (s1_ref, s0_ref))):
            for last in (False, True):
                @pl.when((k % 2 == odd) & ((k == n_steps - 1) == last))
                def _():
                    sel_step(k, cur, nxt, last)

        return carry

    sel_scores(0, s0_ref)
    lax.fori_loop(0, n_steps, sel_body, 0)
    o_s = acc_ref[...] / l_ref[0:1, :]

    gt = gt_ref[0, 0, 0]

    def gate(br):
        return jnp.concatenate([gt[br * GQA_R + h:br * GQA_R + h + 1, :] for h in range(GQA_R)], axis=1)

    o = gate(0) * jnp.concatenate(o_c, axis=1) + gate(1) * o_s + gate(2) * o_w
    o_ref[0] = jnp.concatenate([o[:, h * Q_BLOCK:(h + 1) * Q_BLOCK].T for h in range(GQA_R)],
                               axis=1).astype(BF16)


def _nsa(qt, gt, kc, vct, ksa, vst, kwa, vwt, ovt, pair, tri, n_sel):
    b, g, nt, _, width = qt.shape
    s = ksa.shape[2]
    tile = lambda a: pl.BlockSpec((1, 1, 1) + a.shape[3:], lambda i, j, k: (i, j, k, 0, 0))
    whole = lambda a: pl.BlockSpec((1, 1) + a.shape[2:], lambda i, j, k: (i, j) + (0,) * (a.ndim - 2))
    const = lambda a: pl.BlockSpec(a.shape, lambda i, j, k: (0,) * a.ndim)
    kern = functools.partial(_nsa_kernel, n_sel=n_sel, top_n=min(SEL_TOPN, n_sel))
    return pl.pallas_call(
        kern,
        grid=(b, g, nt),
        in_specs=[tile(qt), tile(gt), whole(kc), whole(vct), whole(ksa), whole(vst), whole(kwa), whole(vwt),
                  const(ovt), const(pair), const(tri)],
        out_specs=pl.BlockSpec((1, Q_BLOCK, GQA_R * HEAD_DIM), lambda i, j, k: (i, k, j)),
        out_shape=jax.ShapeDtypeStruct((b, s, ATTN_WIDTH), BF16),
        scratch_shapes=[pltpu.VMEM((2 * LANES, width), BF16),
                        pltpu.VMEM((SEL_SLOTS, KEY_CHUNK, width), F32),
                        pltpu.VMEM((SEL_SLOTS, KEY_CHUNK, width), F32),
                        pltpu.VMEM((8, width), F32),
                        pltpu.VMEM((8, width), F32),
                        pltpu.VMEM((HEAD_DIM, width), F32),
                        pltpu.SMEM((s // KEY_CHUNK + 2 * SEL_SLOTS,), I32)],
        compiler_params=pltpu.CompilerParams(dimension_semantics=("parallel", "parallel", "arbitrary"),
                                             vmem_limit_bytes=VMEM_LIMIT),
        name="nsa",
    )(qt, gt, kc, vct, ksa, vst, kwa, vwt, ovt, pair, tri)


def _merge_kernel(x_ref, ya_ref, ga_ref, gbyb_ref, wba_ref, wout_ref, gmoe_ref, wrh_ref, wrl_ref, br_ref,
                  x1_ref, hm_ref, rw_ref, ri_ref, cnt_ref, carry_ref):
    tm = x_ref.shape[0]

    @pl.when(pl.program_id(0) == 0)
    def _():
        carry_ref[...] = jnp.zeros_like(carry_ref)

    merged = ga_ref[...].astype(F32) * _dot(ya_ref[...], wba_ref[...]) + gbyb_ref[...].astype(F32)
    x1 = x_ref[...] + _dot(merged.astype(BF16), wout_ref[...])
    x1_ref[...] = x1
    hm = _rms(x1, gmoe_ref[...])
    for a in range(ROW_TILE):
        hm_ref[pl.ds(a, tm, stride=ROW_TILE), :] = hm[:, a * LANES:(a + 1) * LANES]

    hh = hm.astype(BF16)
    hl = (hm - hh.astype(F32)).astype(BF16)
    wrh = wrh_ref[...]
    logits = _dot(hh, wrh) + _dot(hl, wrh) + _dot(hh, wrl_ref[...]) + br_ref[...]
    lane = lax.broadcasted_iota(I32, (tm, LANES), 1)
    lg = jnp.where(lane < N_EXPERTS, logits, NEG)
    vals, idxs = [], []
    for _ in range(TOP_K):
        m = jnp.max(lg, axis=1, keepdims=True)
        idx = jnp.min(jnp.where(lg == m, lane, LANES), axis=1, keepdims=True)
        vals.append(m)
        idxs.append(idx)
        lg = jnp.where(lane == idx, NEG, lg)
    ex = [jnp.exp(v - vals[0]) for v in vals]
    den = ex[0] + ex[1] + ex[2] + ex[3]

    hits = [lane == idx for idx in idxs]
    multi = jnp.zeros((tm, LANES), F32)
    for h in hits:
        multi = jnp.where(h, 1.0, multi)
    ltri = (lax.broadcasted_iota(I32, (tm, tm), 0) > lax.broadcasted_iota(I32, (tm, tm), 1))
    carry = carry_ref[0:1, :]
    cum = _dot(jnp.where(ltri, 1.0, 0.0).astype(BF16), multi.astype(BF16)) + carry
    rw = jnp.zeros((tm, LANES), F32)
    ri = jnp.zeros((tm, LANES), I32)
    for k in range(TOP_K):
        rank = jnp.sum(jnp.where(hits[k], cum, 0.0), axis=1, keepdims=True).astype(I32)
        rw = jnp.where(lane == k, ex[k] / den, rw)
        ri = jnp.where(lane == k, idxs[k], jnp.where(lane == TOP_K + k, rank, ri))
    rw_ref[...] = rw
    ri_ref[...] = ri
    new_carry = carry + jnp.sum(multi, axis=0, keepdims=True)
    carry_ref[...] = jnp.broadcast_to(new_carry, carry_ref.shape)
    cnt_ref[...] = jnp.broadcast_to(new_carry, cnt_ref.shape)


def _merge(x2, ya, ga, gbyb, wba, wout, gmoe, wrh, wrl, br, tm):
    n = x2.shape[0]
    full = lambda a: pl.BlockSpec(a.shape, lambda i: (0,) * a.ndim)
    rows = lambda w: pl.BlockSpec((tm, w), lambda i: (i, 0))
    return pl.pallas_call(
        _merge_kernel,
        grid=(n // tm,),
        in_specs=[rows(D_MODEL), rows(ATTN_WIDTH), rows(D_MODEL), rows(D_MODEL),
                  full(wba), full(wout), full(gmoe), full(wrh), full(wrl), full(br)],
        out_specs=[rows(D_MODEL), pl.BlockSpec((tm * ROW_TILE, LANES), lambda i: (i, 0)), rows(LANES), rows(LANES),
                   pl.BlockSpec((8, LANES), lambda i: (0, 0))],
        out_shape=[jax.ShapeDtypeStruct((n, D_MODEL), F32),
                   jax.ShapeDtypeStruct((n * ROW_TILE, LANES), F32),
                   jax.ShapeDtypeStruct((n, LANES), F32),
                   jax.ShapeDtypeStruct((n, LANES), I32),
                   jax.ShapeDtypeStruct((8, LANES), F32)],
        scratch_shapes=[pltpu.VMEM((8, LANES), F32)],
        compiler_params=pltpu.CompilerParams(dimension_semantics=("arbitrary",),
                                             vmem_limit_bytes=VMEM_LIMIT),
        name="merge",
    )(x2, ya, ga, gbyb, wba, wout, gmoe, wrh, wrl, br)


def _tile_copy(src, i, dst, d, sem):
    return pltpu.make_async_copy(src.at[pl.ds(pl.multiple_of(i * ROW_TILE, ROW_TILE), ROW_TILE)],
                                 dst.at[pl.ds(pl.multiple_of(d * ROW_TILE, ROW_TILE), ROW_TILE)], sem)


def _dispatch_kernel(seg_ref, dest_ref, hm_ref, xs_ref, zero_ref, sem, zsem, *, n_pad):
    tm = hm_ref.shape[0] // ROW_TILE

    @pl.when(pl.program_id(0) == 0)
    def _():
        zero_ref[...] = jnp.zeros_like(zero_ref)

        def seg(e, c):
            def fill(r, c2):
                _tile_copy(zero_ref, 0, xs_ref, r, zsem).start()
                return c2
            return lax.fori_loop(seg_ref[0, e], seg_ref[1, e], fill, c)

        lax.fori_loop(0, N_EXPERTS + 1, seg, 0)
        pad_rows = xs_ref.at[pl.ds(0, n_pad * ROW_TILE)]
        pltpu.make_async_copy(pad_rows, pad_rows, zsem).wait()

    def issue(i, c):
        for k in range(TOP_K):
            _tile_copy(hm_ref, i, xs_ref, dest_ref[i * TOP_K + k], sem).start(priority=k % 2)
        return c

    lax.fori_loop(0, tm, issue, 0)
    for k in range(TOP_K):
        pltpu.make_async_copy(hm_ref, xs_ref.at[pl.ds(0, tm * ROW_TILE)], sem).wait()


def _dispatch(seg, dest_flat, hm, n_slots, tm):
    n = hm.shape[0] // ROW_TILE
    kern = functools.partial(_dispatch_kernel, n_pad=n_slots - n * TOP_K)
    return pl.pallas_call(
        kern,
        grid_spec=pltpu.PrefetchScalarGridSpec(
            num_scalar_prefetch=1,
            grid=(n // tm,),
            in_specs=[pl.BlockSpec((tm * TOP_K,), lambda i, sg: (i,), memory_space=pltpu.SMEM),
                      pl.BlockSpec((tm * ROW_TILE, LANES), lambda i, sg: (i, 0))],
            out_specs=pl.BlockSpec(memory_space=pl.ANY),
            scratch_shapes=[pltpu.VMEM((ROW_TILE, LANES), F32),
                            pltpu.SemaphoreType.DMA(()), pltpu.SemaphoreType.DMA(())]),
        out_shape=jax.ShapeDtypeStruct((n_slots * ROW_TILE, LANES), F32),
        compiler_params=pltpu.CompilerParams(dimension_semantics=("arbitrary",),
                                             has_side_effects=True),
        name="dispatch",
    )(seg, dest_flat, hm)


def _expert_kernel(be_ref, nu_ref, xs_ref, wgu_ref, bgu_ref, wd_ref, bd_ref, y_ref, wgu_bf, wd_bf):
    i = pl.program_id(0)

    @pl.when(i >= nu_ref[0])
    def _():
        y_ref[...] = jnp.zeros_like(y_ref)

    @pl.when((i == 0) | (be_ref[i] != be_ref[jnp.maximum(i - 1, 0)]))
    def _():
        wgu_bf[...] = wgu_ref[0].astype(BF16)
        wd_bf[...] = wd_ref[0].astype(BF16)

    @pl.when(i < nu_ref[0])
    def _():
        x = jnp.concatenate([xs_ref[pl.ds(a, MOE_BLOCK, stride=ROW_TILE), :] for a in range(ROW_TILE)], axis=1)
        gu = _dot(x.astype(BF16), wgu_bf[...]) + bgu_ref[0]
        gate = jnp.minimum(gu[:, :D_FF], SWIGLU_LIMIT)
        up = jnp.clip(gu[:, D_FF:], -SWIGLU_LIMIT, SWIGLU_LIMIT)
        act = gate * _sigmoid(SWIGLU_ALPHA * gate) * (up + 1.0)
        y = _dot(act.astype(BF16), wd_bf[...]) + bd_ref[0]
        for a in range(ROW_TILE):
            y_ref[pl.ds(a, MOE_BLOCK, stride=ROW_TILE), :] = y[:, a * LANES:(a + 1) * LANES]


def _experts(blk_expert, n_used, xs, wgu, bgu, wd, bd):
    n_blocks = xs.shape[0] // (MOE_BLOCK * ROW_TILE)
    blk = lambda i, be, nu: (jnp.minimum(i, nu[0] - 1), 0)
    exp3 = lambda i, be, nu: (be[jnp.minimum(i, nu[0] - 1)], 0, 0)
    return pl.pallas_call(
        _expert_kernel,
        grid_spec=pltpu.PrefetchScalarGridSpec(
            num_scalar_prefetch=2,
            grid=(n_blocks,),
            in_specs=[pl.BlockSpec((MOE_BLOCK * ROW_TILE, LANES), blk),
                      pl.BlockSpec((1, D_MODEL, 2 * D_FF), exp3),
                      pl.BlockSpec((1, 1, 2 * D_FF), exp3),
                      pl.BlockSpec((1, D_FF, D_MODEL), exp3),
                      pl.BlockSpec((1, 1, D_MODEL), exp3)],
            out_specs=pl.BlockSpec((MOE_BLOCK * ROW_TILE, LANES), lambda i, be, nu: (i, 0)),
            scratch_shapes=[pltpu.VMEM((D_MODEL, 2 * D_FF), BF16), pltpu.VMEM((D_FF, D_MODEL), BF16)]),
        out_shape=jax.ShapeDtypeStruct(xs.shape, F32),
        compiler_params=pltpu.CompilerParams(dimension_semantics=("arbitrary",),
                                             vmem_limit_bytes=VMEM_LIMIT),
        name="experts",
    )(blk_expert, n_used, xs, wgu, bgu, wd, bd)


def _combine_kernel(dest_ref, rw_ref, x1_ref, gfin_ref, y_ref, o_ref, ybuf, sem):
    tm = x1_ref.shape[0]

    def issue(i, c):
        for k in range(TOP_K):
            _tile_copy(y_ref, dest_ref[i * TOP_K + k], ybuf.at[k], i, sem).start(priority=k % 2)
        return c

    lax.fori_loop(0, tm, issue, 0)
    for k in range(TOP_K):
        pltpu.make_async_copy(y_ref.at[pl.ds(0, tm * ROW_TILE)], ybuf.at[k], sem).wait()

    rw = rw_ref[...]
    x1 = x1_ref[...]
    cols = []
    for a in range(ROW_TILE):
        acc = x1[:, a * LANES:(a + 1) * LANES]
        for k in range(TOP_K):
            acc = acc + rw[:, k:k + 1] * ybuf[k, pl.ds(a, tm, stride=ROW_TILE), :]
        cols.append(acc)
    o_ref[...] = _rms(jnp.concatenate(cols, axis=1), gfin_ref[...])


def _combine(dest_flat, rw, x1, gfin, y, tm):
    n = x1.shape[0]
    return pl.pallas_call(
        _combine_kernel,
        grid=(n // tm,),
        in_specs=[pl.BlockSpec((tm * TOP_K,), lambda i: (i,), memory_space=pltpu.SMEM),
                  pl.BlockSpec((tm, LANES), lambda i: (i, 0)),
                  pl.BlockSpec((tm, D_MODEL), lambda i: (i, 0)),
                  pl.BlockSpec((1, D_MODEL), lambda i: (0, 0)),
                  pl.BlockSpec(memory_space=pl.ANY)],
        out_specs=pl.BlockSpec((tm, D_MODEL), lambda i: (i, 0)),
        out_shape=jax.ShapeDtypeStruct((n, D_MODEL), F32),
        scratch_shapes=[pltpu.VMEM((TOP_K, tm * ROW_TILE, LANES), F32), pltpu.SemaphoreType.DMA(())],
        compiler_params=pltpu.CompilerParams(dimension_semantics=("arbitrary",),
                                             vmem_limit_bytes=VMEM_LIMIT),
        name="combine",
    )(dest_flat, rw, x1, gfin, y)


def _overlap_t(nc, n_cmp, n_sel):
    cs = np.arange(n_cmp)[None, :] * CMP_STRIDE
    ss = np.arange(n_sel)[:, None] * SEL_BLOCK
    ov = np.clip(np.minimum(cs + CMP_LEN, ss + SEL_BLOCK) - np.maximum(cs, ss), 0, None) / CMP_LEN
    out = np.zeros((LANES, nc), np.float32)
    out[:n_sel, :n_cmp] = ov
    return jnp.asarray(out, BF16)


def _layer(x2, b, s, g_mix, w_in, w_ck1, b_ck1, w_ck2, pe_k, w_cv1, b_cv1, w_cv2, pe_v,
           sgu_ln_g, sgu_ln_b, w_spatial, b_spatial, w_branch_a, w_branch_b, w_out,
           g_moe, w_router, b_router, w_gate_up, b_gate_up, w_down, b_down):
    n = b * s
    nc = s // CMP_STRIDE
    n_cmp = (s - CMP_LEN) // CMP_STRIDE + 1
    n_sel = s // SEL_BLOCK
    assert nc % KEY_CHUNK == 0 and n_sel <= LANES and n_cmp == nc - 1
    tm = 512
    assert s % tm == 0

    p0 = ATTN_WIDTH
    p1 = p0 + 6 * KV_WIDTH
    p2 = p1 + N_NSA_BRANCH * N_HEADS
    p3 = p2 + SGU_WIDTH
    p4 = p3 + SGU_WIDTH
    zpad = lambda a, w: jnp.pad(a, ((0, 0),) * (a.ndim - 1) + ((0, w - a.shape[-1]),))
    wq = zpad((w_in[:, :p0] * (HEAD_DIM ** -0.5 * LOG2E)).reshape(D_MODEL, N_HEADS, HEAD_DIM), LANES)
    wkv = w_in[:, p0:p1].reshape(D_MODEL, 6, N_KV, HEAD_DIM)
    wng = w_in[:, p1:p2].reshape(D_MODEL, N_KV, GQA_R, N_NSA_BRANCH).transpose(0, 1, 3, 2)
    wng = zpad(wng.reshape(D_MODEL, N_KV, N_NSA_BRANCH * GQA_R), GATE_ROWS)
    wt = jnp.concatenate([wq.reshape(D_MODEL, -1), wkv[:, 3].reshape(D_MODEL, -1),
                          wkv[:, 5].reshape(D_MODEL, -1), wng.reshape(D_MODEL, -1)], axis=1).T.astype(BF16)
    slopes = 2.0 ** (-8.0 * np.arange(1, N_HEADS + 1) / N_HEADS)
    tcol = np.zeros((T_ROWS, 1), np.float32)
    head_rows = T_Q + np.arange(N_HEADS) * LANES + HEAD_DIM
    bf16_round = lambda a: a.astype(BF16).astype(np.float32)
    for k, coef in enumerate((slopes * SEL_BLOCK * LOG2E, slopes * LOG2E)):
        hi = bf16_round(coef.astype(np.float32))
        tcol[head_rows + 2 * k, 0] = hi
        tcol[head_rows + 2 * k + 1, 0] = bf16_round(coef.astype(np.float32) - hi)
    tconst = jnp.asarray(np.broadcast_to(tcol, (T_ROWS, 2 * Q_BLOCK)))
    wk = jnp.concatenate([zpad(wkv[:, 2, g], 2 * LANES) for g in range(N_KV)]
                         + [zpad(wkv[:, 4, g], LANES) for g in range(N_KV)]
                         + [wkv[:, 0].reshape(D_MODEL, -1), wkv[:, 1].reshape(D_MODEL, -1)], axis=1).astype(BF16)
    pos = np.arange(s)
    kc_np = np.zeros((s, 2 * LANES), np.float32)
    kc_np[:, HEAD_DIM:HEAD_DIM + 2] = (pos // SEL_BLOCK)[:, None]
    kc_np[:, HEAD_DIM + 2:HEAD_DIM + 4] = (pos % SEL_BLOCK)[:, None]
    kc_np[pos, LANES + pos // SEL_BLOCK] = 1.0
    kconst = jnp.asarray(kc_np, BF16)
    wu = w_in[:, p2:p3].astype(BF16)
    wv = w_in[:, p3:p4].astype(BF16)
    wmg = w_in[:, p4:].astype(BF16)
    bsp = jnp.repeat(b_spatial.T, SGU_GROUP_DIM, axis=1)

    qt, gt, vst, vwt, ksa, kwa, kcx, vcx, ga, gbyb = _proj(
        x2, b, s, g_mix[None], wt, tconst, wk, kconst, wu, wv, wmg, sgu_ln_g[None], sgu_ln_b[None],
        w_spatial, bsp, w_branch_b.astype(BF16), tm)

    half = CMP_LEN // 2
    eye = jnp.eye(N_KV, dtype=F32)[None, None, :, None, :, None]
    bdiag = lambda w: (w.reshape(2, half, 1, HEAD_DIM, 1, CMP_HID) * eye).reshape(
        2, half, N_KV * HEAD_DIM, N_KV * CMP_HID)
    w1 = jnp.stack([bdiag(w_ck1), bdiag(w_cv1)]).astype(BF16)
    pes = jnp.stack([jnp.tile(pe_k.reshape(2, half, HEAD_DIM), (1, 1, N_KV)),
                     jnp.tile(pe_v.reshape(2, half, HEAD_DIM), (1, 1, N_KV))])
    b1 = jnp.stack([jnp.tile(b_ck1, N_KV), jnp.tile(b_cv1, N_KV)])[:, None, :]
    blk_n = np.arange(nc)
    cc_np = np.zeros((nc, LANES), np.float32)
    cc_np[:, HEAD_DIM:HEAD_DIM + 2] = (blk_n // (SEL_BLOCK // CMP_STRIDE))[:, None]
    cc_np[:, HEAD_DIM + 2:HEAD_DIM + 4] = (blk_n % (SEL_BLOCK // CMP_STRIDE) * CMP_STRIDE)[:, None]
    kc, vct = _compress(kcx, vcx, pes, w1, b1, zpad(w_ck2, LANES).astype(BF16), w_cv2.T.astype(BF16),
                        jnp.asarray(cc_np))

    ovt = _overlap_t(nc, n_cmp, n_sel).reshape(LANES, nc // KEY_CHUNK, KEY_CHUNK).transpose(1, 0, 2)
    blk = np.arange(LANES)
    pair = jnp.asarray(blk[:, None] // (KEY_CHUNK // SEL_BLOCK) == blk[None, :], BF16)
    a_io, q_io = np.meshgrid(np.arange(KEY_CHUNK), np.arange(Q_BLOCK), indexing="ij")
    tri = np.stack([np.where(a_io > q_io, 0.0, NEG), np.where(a_io <= q_io, 0.0, NEG)]).astype(np.float32)
    tri = jnp.asarray(np.tile(tri, (1, 1, 2)))
    ya = _nsa(qt, gt, kc, vct, ksa, vst, kwa, vwt, ovt, pair, tri, n_sel).reshape(n, ATTN_WIDTH)

    wr = jnp.pad(w_router, ((0, 0), (0, LANES - N_EXPERTS)))
    wrh = wr.astype(BF16)
    wrl = (wr - wrh.astype(F32)).astype(BF16)
    br = jnp.pad(b_router, (0, LANES - N_EXPERTS))[None]
    x1, hm, rw, ri, cnt = _merge(x2, ya, ga, gbyb, w_branch_a.astype(BF16), w_out.astype(BF16),
                                 g_moe[None], wrh, wrl, br, tm)

    counts = cnt[0, :N_EXPERTS].astype(I32)
    padded = (counts + MOE_BLOCK - 1) // MOE_BLOCK * MOE_BLOCK
    pad_end = jnp.cumsum(padded)
    pad_start = pad_end - padded
    dest = (pad_start[ri[:, :TOP_K]] + ri[:, TOP_K:2 * TOP_K]).reshape(-1)
    n_blocks = -(-(n * TOP_K) // MOE_BLOCK) + N_EXPERTS
    blk_start = jnp.arange(n_blocks, dtype=I32) * MOE_BLOCK
    blk_expert = jnp.minimum(jnp.sum((pad_end[None, :] <= blk_start[:, None]).astype(I32), axis=1),
                             N_EXPERTS - 1)
    n_used = (pad_end[-1:] // MOE_BLOCK).astype(I32)
    n_slots = n_blocks * MOE_BLOCK
    seg = jnp.stack([jnp.concatenate([pad_start + counts, pad_end[-1:]]),
                     jnp.concatenate([pad_end, jnp.full((1,), n_slots, I32)])]).astype(I32)

    tg = 256
    xs = _dispatch(seg, dest, hm, n_slots, tg)
    y = _experts(blk_expert, n_used, xs, w_gate_up, b_gate_up[:, None, :], w_down, b_down[:, None, :])
    return dest, rw, x1, y, tg


def kernel(x, g_mix, w_in, w_ck1, b_ck1, w_ck2, pe_k, w_cv1, b_cv1, w_cv2, pe_v, sgu_ln_g, sgu_ln_b, w_spatial, b_spatial, w_branch_a, w_branch_b, w_out, g_moe, w_router, b_router, w_gate_up, b_gate_up, w_down, b_down, g_final):
    b, s, d = x.shape
    assert g_mix.shape[0] == 1, "the final rmsnorm is fused into the single layer's combine step"
    l = 0
    dest, rw, x1, y, tg = _layer(
        x.reshape(b * s, d), b, s, g_mix[l], w_in[l], w_ck1[l], b_ck1[l], w_ck2[l], pe_k[l], w_cv1[l],
        b_cv1[l], w_cv2[l], pe_v[l], sgu_ln_g[l], sgu_ln_b[l], w_spatial[l], b_spatial[l], w_branch_a[l],
        w_branch_b[l], w_out[l], g_moe[l], w_router[l], b_router[l], w_gate_up[l], b_gate_up[l],
        w_down[l], b_down[l])
    return _combine(dest, rw, x1, g_final[None], y, tg).reshape(b, s, d)
```

```python
import functools

import numpy as np
import jax
import jax.numpy as jnp
from jax import lax
from jax.experimental import pallas as pl
from jax.experimental.pallas import tpu as pltpu

D_MODEL = 1024
N_HEADS = 8
HEAD_DIM = 64
N_KV = 2
GQA_R = N_HEADS // N_KV
ATTN_WIDTH = N_HEADS * HEAD_DIM
KV_WIDTH = N_KV * HEAD_DIM
CMP_LEN = 32
CMP_STRIDE = 16
CMP_HID = 128
SEL_BLOCK = 64
SEL_TOPN = 16
WINDOW = 512
Q_BLOCK = 128
N_NSA_BRANCH = 3
SGU_WIDTH = 512
N_GROUPS_SGU = 8
SGU_GROUP_DIM = SGU_WIDTH // N_GROUPS_SGU
CHUNK = 128
N_EXPERTS = 32
TOP_K = 4
D_FF = D_MODEL
SWIGLU_LIMIT = 7.0
SWIGLU_ALPHA = 1.702
MOE_BLOCK = 512
RMS_EPS = 1e-5
LN_EPS = 1e-5
NEG = -1e30
FORCE = 1e4

LANES = 128
ROW_TILE = D_MODEL // LANES
MASK_BIAS = 1e9
LOG2E = float(np.log2(np.e))
KEY_CHUNK = 128
SEL_SLOTS = 4
GATE_ROWS = 16
VMEM_LIMIT = 56 * 1024 * 1024

F32 = jnp.float32
BF16 = jnp.bfloat16
I32 = jnp.int32

_NT = (((1,), (1,)), ((), ()))


def _dot(a, b):
    return jnp.dot(a, b, preferred_element_type=F32)


def _dot_nt(a, b):
    return lax.dot_general(a, b, _NT, preferred_element_type=F32)


def _sigmoid(x):
    return 1.0 / (1.0 + jnp.exp(-x))


def _gelu(x):
    c = np.float32(np.sqrt(2.0 / np.pi))
    return 0.5 * x * (1.0 + jnp.tanh(c * (x + 0.044715 * (x * x * x))))


def _rms(x, g):
    return x * lax.rsqrt(jnp.mean(x * x, axis=-1, keepdims=True) + RMS_EPS) * g


T_Q = 0
T_VS = T_Q + N_HEADS * LANES
T_VW = T_VS + N_KV * HEAD_DIM
T_GATE = T_VW + N_KV * HEAD_DIM
T_ROWS = T_GATE + N_KV * GATE_ROWS
K_SEL = 0
K_WIN = K_SEL + N_KV * 2 * LANES
K_CMP = K_WIN + N_KV * LANES
K_COLS = K_CMP + 4 * HEAD_DIM


def _proj_kernel(x_ref, gmix_ref, wt_ref, tconst_ref, wk_ref, kconst_ref, wu_ref, wv_ref, wmg_ref,
                 lng_ref, lnb_ref, wsp_ref, bsp_ref, wbb_ref,
                 qt_ref, gt_ref, vst_ref, vwt_ref, ksa_ref, kwa_ref, kcx_ref, vcx_ref, ga_ref, gbyb_ref):
    tm = x_ref.shape[0]
    hb = _rms(x_ref[...], gmix_ref[...]).astype(BF16)

    for h in range(tm // (2 * Q_BLOCK)):
        t = _dot_nt(wt_ref[...], hb[h * 2 * Q_BLOCK:(h + 1) * 2 * Q_BLOCK]) + tconst_ref[...]
        for jj in range(2):
            j = 2 * h + jj
            ls = slice(jj * Q_BLOCK, (jj + 1) * Q_BLOCK)
            for g in range(N_KV):
                for r in range(GQA_R):
                    r0 = T_Q + (g * GQA_R + r) * LANES
                    qt_ref[0, g, j, :, r * Q_BLOCK:(r + 1) * Q_BLOCK] = t[r0:r0 + LANES, ls].astype(BF16)
                vst_ref[0, g, j] = t[T_VS + g * HEAD_DIM:T_VS + (g + 1) * HEAD_DIM, ls].astype(BF16)
                vwt_ref[0, g, j] = t[T_VW + g * HEAD_DIM:T_VW + (g + 1) * HEAD_DIM, ls].astype(BF16)
                gt_ref[0, g, j] = _sigmoid(t[T_GATE + g * GATE_ROWS:T_GATE + (g + 1) * GATE_ROWS, ls])

    kk = _dot(hb, wk_ref[...])
    kconst = kconst_ref[...].astype(F32)
    for g in range(N_KV):
        c0 = K_SEL + g * 2 * LANES
        ksa_ref[0, g] = (kk[:, c0:c0 + 2 * LANES] + kconst).astype(BF16)
        c0 = K_WIN + g * LANES
        kwa_ref[0, g] = (kk[:, c0:c0 + LANES] + kconst[:, :LANES]).astype(BF16)
    kcx_ref[0] = kk[:, K_CMP:K_CMP + LANES]
    vcx_ref[0] = kk[:, K_CMP + LANES:K_COLS]

    mg = _dot(hb, wmg_ref[...])
    ga_ref[...] = _sigmoid(mg[:, :D_MODEL]).astype(BF16)

    u = _gelu(_dot(hb, wu_ref[...]))
    v = _gelu(_dot(hb, wv_ref[...]))
    mu = jnp.mean(v, axis=-1, keepdims=True)
    vc = v - mu
    var = jnp.mean(vc * vc, axis=-1, keepdims=True)
    vln = (vc * lax.rsqrt(var + LN_EPS) * lng_ref[...] + lnb_ref[...]).astype(BF16)

    row = lax.broadcasted_iota(I32, (CHUNK, CHUNK), 0)
    col = lax.broadcasted_iota(I32, (CHUNK, CHUNK), 1)
    tril = row >= col
    wsp = [jnp.where(tril, wsp_ref[g], 0.0).astype(BF16) for g in range(N_GROUPS_SGU)]
    low_half = col < SGU_GROUP_DIM
    bsp = bsp_ref[...]
    chunks = []
    for c in range(tm // CHUNK):
        rs = slice(c * CHUNK, (c + 1) * CHUNK)
        parts = []
        for p in range(SGU_WIDTH // LANES):
            cs = slice(p * LANES, (p + 1) * LANES)
            vblk = vln[rs, cs]
            mixed = jnp.where(low_half, _dot(wsp[2 * p], vblk), _dot(wsp[2 * p + 1], vblk))
            parts.append(u[rs, cs] * (mixed + bsp[:, cs]))
        chunks.append(jnp.concatenate(parts, axis=1))
    yb = jnp.concatenate(chunks, axis=0).astype(BF16)
    gbyb_ref[...] = (_sigmoid(mg[:, D_MODEL:]) * _dot(yb, wbb_ref[...])).astype(BF16)


def _proj(x2, b, s, gmix, wt, tconst, wk, kconst, wu, wv, wmg, lng, lnb, wsp, bsp, wbb, tm):
    n = x2.shape[0]
    tiles = s // tm
    qtiles = tm // Q_BLOCK
    full = lambda a: pl.BlockSpec(a.shape, lambda i, j: (0,) * a.ndim)
    rows = lambda w: pl.BlockSpec((tm, w), lambda i, j: (i * tiles + j, 0))
    per_g = lambda *blk: pl.BlockSpec((1, N_KV) + blk, lambda i, j: (i, 0, j) + (0,) * (len(blk) - 1))
    ins = (x2, gmix, wt, tconst, wk, kconst, wu, wv, wmg, lng, lnb, wsp, bsp, wbb)
    in_specs = [rows(D_MODEL)] + [full(a) for a in ins[1:]]
    in_specs[5] = pl.BlockSpec((tm, 2 * LANES), lambda i, j: (j, 0))
    nt = s // Q_BLOCK
    return pl.pallas_call(
        _proj_kernel,
        grid=(b, tiles),
        in_specs=in_specs,
        out_specs=[per_g(qtiles, LANES, GQA_R * Q_BLOCK), per_g(qtiles, GATE_ROWS, Q_BLOCK),
                   per_g(qtiles, HEAD_DIM, Q_BLOCK), per_g(qtiles, HEAD_DIM, Q_BLOCK),
                   per_g(tm, 2 * LANES), per_g(tm, LANES),
                   pl.BlockSpec((1, tm, LANES), lambda i, j: (i, j, 0)),
                   pl.BlockSpec((1, tm, LANES), lambda i, j: (i, j, 0)),
                   rows(D_MODEL), rows(D_MODEL)],
        out_shape=[jax.ShapeDtypeStruct((b, N_KV, nt, LANES, GQA_R * Q_BLOCK), BF16),
                   jax.ShapeDtypeStruct((b, N_KV, nt, GATE_ROWS, Q_BLOCK), F32),
                   jax.ShapeDtypeStruct((b, N_KV, nt, HEAD_DIM, Q_BLOCK), BF16),
                   jax.ShapeDtypeStruct((b, N_KV, nt, HEAD_DIM, Q_BLOCK), BF16),
                   jax.ShapeDtypeStruct((b, N_KV, s, 2 * LANES), BF16),
                   jax.ShapeDtypeStruct((b, N_KV, s, LANES), BF16),
                   jax.ShapeDtypeStruct((b, s, LANES), F32),
                   jax.ShapeDtypeStruct((b, s, LANES), F32),
                   jax.ShapeDtypeStruct((n, D_MODEL), BF16),
                   jax.ShapeDtypeStruct((n, D_MODEL), BF16)],
        compiler_params=pltpu.CompilerParams(dimension_semantics=("parallel", "parallel"),
                                             vmem_limit_bytes=VMEM_LIMIT),
        name="proj",
    )(*ins)


def _compress_kernel(xk_ref, xv_ref, pe_ref, w1_ref, b1_ref, w2k_ref, w2vt_ref, kconst_ref, kc_ref, vct_ref):
    nc = kc_ref.shape[2]
    half = CMP_LEN // 2
    hids = []
    for which, x_ref in enumerate((xk_ref, xv_ref)):
        za = jnp.zeros((nc, N_KV * CMP_HID), F32)
        zb = jnp.zeros((nc, N_KV * CMP_HID), F32)
        for l in range(half):
            xl = x_ref[0, pl.ds(l, nc, stride=CMP_STRIDE), :]
            za = za + _dot((xl + pe_ref[which, 0, l:l + 1, :]).astype(BF16), w1_ref[which, 0, l])
            zb = zb + _dot((xl + pe_ref[which, 1, l:l + 1, :]).astype(BF16), w1_ref[which, 1, l])
        hid = _gelu(za + pltpu.roll(zb, nc - 1, 0) + b1_ref[which])
        row = lax.broadcasted_iota(I32, hid.shape, 0)
        hids.append(jnp.where(row < nc - 1, hid, 0.0).astype(BF16))
    for g in range(N_KV):
        cs = slice(g * CMP_HID, (g + 1) * CMP_HID)
        kc_ref[0, g] = (_dot(hids[0][:, cs], w2k_ref[...]) + kconst_ref[...]).astype(BF16)
        vt = _dot_nt(w2vt_ref[...], hids[1][:, cs])
        for c in range(nc // KEY_CHUNK):
            vct_ref[0, g, c] = vt[:, c * KEY_CHUNK:(c + 1) * KEY_CHUNK].astype(BF16)


def _compress(xk, xv, pes, w1, b1, w2k, w2vt, kconst):
    b, s, w = xk.shape
    nc = s // CMP_STRIDE
    full = lambda a: pl.BlockSpec(a.shape, lambda i: (0,) * a.ndim)
    seq = pl.BlockSpec((1, s, w), lambda i: (i, 0, 0))
    return pl.pallas_call(
        _compress_kernel,
        grid=(b,),
        in_specs=[seq, seq, full(pes), full(w1), full(b1), full(w2k), full(w2vt), full(kconst)],
        out_specs=[pl.BlockSpec((1, N_KV, nc, LANES), lambda i: (i, 0, 0, 0)),
                   pl.BlockSpec((1, N_KV, nc // KEY_CHUNK, HEAD_DIM, KEY_CHUNK), lambda i: (i, 0, 0, 0, 0))],
        out_shape=[jax.ShapeDtypeStruct((b, N_KV, nc, LANES), BF16),
                   jax.ShapeDtypeStruct((b, N_KV, nc // KEY_CHUNK, HEAD_DIM, KEY_CHUNK), BF16)],
        compiler_params=pltpu.CompilerParams(dimension_semantics=("parallel",),
                                             vmem_limit_bytes=VMEM_LIMIT),
        name="compress",
    )(xk, xv, pes, w1, b1, w2k, w2vt, kconst)


def _pairs():
    return [slice(pr * 2 * Q_BLOCK, (pr + 1) * 2 * Q_BLOCK) for pr in range(GQA_R // 2)]


def _colmax(blocks):
    part = None
    for x in blocks:
        y = jnp.max(x.reshape(-1, 8, x.shape[-1]), axis=0)
        part = y if part is None else jnp.maximum(part, y)
    return jnp.max(part, axis=0, keepdims=True)


def _colsum(blocks):
    part = None
    for x in blocks:
        y = jnp.sum(x.reshape(-1, 8, x.shape[-1]), axis=0)
        part = y if part is None else part + y
    return jnp.sum(part, axis=0, keepdims=True)


def _softmax_pv(scores, vts, masks, m_ref, l_ref, acc_ref):
    for pr, ls in enumerate(_pairs()):
        sb = [x if masks[u] is None else masks[u](x, ls) for u, x in enumerate(scores[pr])]
        m_old = m_ref[0:1, ls]
        m_new = jnp.maximum(m_old, _colmax(sb))
        alpha = jnp.exp2(m_old - m_new)
        ps = [jnp.exp2(x - m_new) for x in sb]
        pv = _dot(vts[0], ps[0].astype(BF16))
        for u in range(1, len(ps)):
            pv = pv + _dot(vts[u], ps[u].astype(BF16))
        l_ref[0:1, ls] = alpha * l_ref[0:1, ls] + _colsum(ps)
        acc_ref[:, ls] = alpha * acc_ref[:, ls] + pv
        m_ref[0:1, ls] = m_new


def _nsa_kernel(qt_ref, gt_ref, kc_ref, vct_ref, ksa_ref, vst_ref, kwa_ref, vwt_ref, ovt_ref, pair_ref, tri_ref,
                o_ref, qa_ref, s0_ref, s1_ref, m_ref, l_ref, acc_ref, act_ref, *, n_sel, top_n):
    i = pl.program_id(1)
    t0 = i * Q_BLOCK
    width = GQA_R * Q_BLOCK
    n_chunks = ksa_ref.shape[2] // KEY_CHUNK
    n_cchunks = kc_ref.shape[2] // KEY_CHUNK
    n_slots = act_ref.shape[0] // N_KV
    lane = lax.broadcasted_iota(I32, (1, width), 1)
    tq = t0 + (lane & (Q_BLOCK - 1))
    sub = lax.broadcasted_iota(I32, (KEY_CHUNK, 1), 0)
    pairs = _pairs()
    heads = range(N_KV)

    def reset(g):
        m_ref[g] = jnp.full(m_ref.shape[1:], NEG, F32)
        l_ref[g] = jnp.zeros(l_ref.shape[1:], F32)
        acc_ref[g] = jnp.zeros(acc_ref.shape[1:], F32)

    sc = [[[_dot(kc_ref[0, g, u * KEY_CHUNK:(u + 1) * KEY_CHUNK, :], qt_ref[0, g, 0, :, ls])
            for u in range(n_cchunks)] for ls in pairs] for g in heads]
    o_c, imp_t = [], []
    for g in heads:
        ocg, ps = [], []
        for pr, ls in enumerate(pairs):
            sb = []
            for u in range(n_cchunks):
                c_end = (u * KEY_CHUNK + sub) * CMP_STRIDE + (CMP_LEN - 1)
                sb.append(jnp.where(c_end <= tq[:, ls], sc[g][pr][u], NEG))
            m = _colmax(sb)
            pb = [jnp.exp2(x - m) for x in sb]
            l = _colsum(pb)
            w = (tq[:, ls] >= CMP_LEN - 1).astype(F32) / l
            oc = _dot(vct_ref[0, g, 0], pb[0].astype(BF16))
            for u in range(1, n_cchunks):
                oc = oc + _dot(vct_ref[0, g, u], pb[u].astype(BF16))
            ocg.append(oc * w)
            ps.append([p[:, :Q_BLOCK] * w[:, :Q_BLOCK] + p[:, Q_BLOCK:] * w[:, Q_BLOCK:] for p in pb])
        imp2 = jnp.zeros((LANES, 2 * Q_BLOCK), F32)
        for u in range(n_cchunks):
            psu = ps[0][u] + ps[1][u]
            hi = psu.astype(BF16)
            lo = (psu - hi.astype(F32)).astype(BF16)
            imp2 = imp2 + _dot(ovt_ref[u], jnp.concatenate([hi, lo], axis=1))
        o_c.append(jnp.concatenate(ocg, axis=1))
        imp_t.append(imp2[:, :Q_BLOCK] + imp2[:, Q_BLOCK:])

    j_io = lax.broadcasted_iota(I32, (LANES, Q_BLOCK), 0)
    tl = t0 + lax.broadcasted_iota(I32, (LANES, Q_BLOCK), 1)
    cur = tl >> 6
    forced = (j_io == 0) | (j_io == cur) | (j_io == cur - 1)
    prio = [jnp.where(j_io < n_sel, jnp.where(forced, FORCE, jnp.where(j_io * SEL_BLOCK <= tl, imp_t[g], -1.0)), NEG)
            for g in heads]
    sel_t = [jnp.zeros((LANES, Q_BLOCK), F32) for g in heads]
    for _ in range(top_n):
        for g in heads:
            m = jnp.max(prio[g], axis=0, keepdims=True)
            idx = jnp.min(jnp.where(prio[g] == m, j_io, LANES), axis=0, keepdims=True)
            hit = j_io == idx
            sel_t[g] = jnp.where(hit, 1.0, sel_t[g])
            prio[g] = jnp.where(hit, NEG, prio[g])

    n_act = []
    for g in heads:
        qa_ref[g, 0:LANES, :] = qt_ref[0, g, 0]
        bias_t = ((sel_t[g] - 1.0) * MASK_BIAS).astype(BF16)
        for h in range(GQA_R):
            qa_ref[g, LANES:2 * LANES, h * Q_BLOCK:(h + 1) * Q_BLOCK] = bias_t
        per_block = _dot_nt(jnp.ones((8, Q_BLOCK), BF16), sel_t[g].astype(BF16))
        per_chunk = _dot(per_block.astype(BF16), pair_ref[...])
        na = jnp.int32(0)
        for c in range(n_chunks):
            act_ref[g * n_slots + na] = jnp.int32(c)
            na = na + jnp.where((per_chunk[0, c] > 0.0) & (c < i), 1, 0)
        act_ref[g * n_slots + na] = i
        for u in range(1, 2 * SEL_SLOTS):
            act_ref[g * n_slots + na + u] = jnp.int32(-1)
        n_act.append(na)

    n_win = WINDOW // KEY_CHUNK
    cls, masks = [], []
    for u in range(n_win + 1):
        c = i - n_win + u
        cls.append(jnp.maximum(c, 0))
        off = jnp.where(c < 0, NEG, 0.0)
        if u == 0:
            edge = tri_ref[0] + off
            masks.append(lambda x, ls, edge=edge: x + edge)
        elif u == n_win:
            masks.append(lambda x, ls: x + tri_ref[1])
        else:
            masks.append(lambda x, ls, off=off: x + off)
    wsc = [[[_dot(kwa_ref[0, g, pl.ds(pl.multiple_of(cl * KEY_CHUNK, KEY_CHUNK), KEY_CHUNK), :],
                  qt_ref[0, g, 0, :, ls]) for cl in cls] for ls in pairs] for g in heads]
    o_w = []
    for g in heads:
        reset(g)
        _softmax_pv(wsc[g], [vwt_ref[0, g, cl] for cl in cls], masks, m_ref.at[g], l_ref.at[g], acc_ref.at[g])
        o_w.append(acc_ref[g] / l_ref[g, 0:1, :])

    def sel_scores(g, k, dst):
        for u in range(SEL_SLOTS):
            cl = jnp.maximum(act_ref[g * n_slots + k * SEL_SLOTS + u], 0)
            keys = ksa_ref[0, g, pl.ds(pl.multiple_of(cl * KEY_CHUNK, KEY_CHUNK), KEY_CHUNK), :]
            for ls in pairs:
                dst[g, u, :, ls] = _dot(keys, qa_ref[g, :, ls])

    def sel_step(g, k, cur_s, nxt_s, last):
        sel_scores(g, k + 1, nxt_s)
        vts, msk = [], []
        for u in range(SEL_SLOTS):
            c = act_ref[g * n_slots + k * SEL_SLOTS + u]
            vts.append(vst_ref[0, g, jnp.maximum(c, 0)])
            if last:
                kpos = jnp.where(c < 0, n_chunks * KEY_CHUNK, c * KEY_CHUNK) + sub
                msk.append(lambda x, ls, kpos=kpos: jnp.where(kpos <= tq[:, ls], x, -MASK_BIAS))
            else:
                msk.append(None)
        scores = [[cur_s[g, u, :, ls] for u in range(SEL_SLOTS)] for ls in pairs]
        _softmax_pv(scores, vts, msk, m_ref.at[g], l_ref.at[g], acc_ref.at[g])

    o_s = []
    for g in heads:
        reset(g)
        sel_scores(g, 0, s0_ref)
    for g in heads:
        n_steps = (n_act[g] + SEL_SLOTS) // SEL_SLOTS

        def sel_body(k, carry, g=g, n_steps=n_steps):
            for odd, (cur_s, nxt_s) in enumerate(((s0_ref, s1_ref), (s1_ref, s0_ref))):
                for last in (False, True):
                    @pl.when((k % 2 == odd) & ((k == n_steps - 1) == last))
                    def _():
                        sel_step(g, k, cur_s, nxt_s, last)

            return carry

        lax.fori_loop(0, n_steps, sel_body, 0)
        o_s.append(acc_ref[g] / l_ref[g, 0:1, :])

    for g in heads:
        gt = gt_ref[0, g, 0]

        def gate(br):
            return jnp.concatenate([gt[br * GQA_R + h:br * GQA_R + h + 1, :] for h in range(GQA_R)], axis=1)

        o = gate(0) * o_c[g] + gate(1) * o_s[g] + gate(2) * o_w[g]
        o_ref[0, :, g * GQA_R * HEAD_DIM:(g + 1) * GQA_R * HEAD_DIM] = jnp.concatenate(
            [o[:, h * Q_BLOCK:(h + 1) * Q_BLOCK].T for h in range(GQA_R)], axis=1).astype(BF16)


def _nsa(qt, gt, kc, vct, ksa, vst, kwa, vwt, ovt, pair, tri, n_sel):
    b, g, nt, _, width = qt.shape
    s = ksa.shape[2]
    tile = lambda a: pl.BlockSpec((1, g, 1) + a.shape[3:], lambda i, k: (i, 0, k, 0, 0))
    whole = lambda a: pl.BlockSpec((1,) + a.shape[1:], lambda i, k: (i,) + (0,) * (a.ndim - 1))
    const = lambda a: pl.BlockSpec(a.shape, lambda i, k: (0,) * a.ndim)
    kern = functools.partial(_nsa_kernel, n_sel=n_sel, top_n=min(SEL_TOPN, n_sel))
    return pl.pallas_call(
        kern,
        grid=(b, nt),
        in_specs=[tile(qt), tile(gt), whole(kc), whole(vct), whole(ksa), whole(vst), whole(kwa), whole(vwt),
                  const(ovt), const(pair), const(tri)],
        out_specs=pl.BlockSpec((1, Q_BLOCK, ATTN_WIDTH), lambda i, k: (i, k, 0)),
        out_shape=jax.ShapeDtypeStruct((b, s, ATTN_WIDTH), BF16),
        scratch_shapes=[pltpu.VMEM((g, 2 * LANES, width), BF16),
                        pltpu.VMEM((g, SEL_SLOTS, KEY_CHUNK, width), F32),
                        pltpu.VMEM((g, SEL_SLOTS, KEY_CHUNK, width), F32),
                        pltpu.VMEM((g, 8, width), F32),
                        pltpu.VMEM((g, 8, width), F32),
                        pltpu.VMEM((g, HEAD_DIM, width), F32),
                        pltpu.SMEM((g * (s // KEY_CHUNK + 2 * SEL_SLOTS),), I32)],
        compiler_params=pltpu.CompilerParams(dimension_semantics=("parallel", "arbitrary"),
                                             vmem_limit_bytes=VMEM_LIMIT),
        name="nsa",
    )(qt, gt, kc, vct, ksa, vst, kwa, vwt, ovt, pair, tri)


def _merge_kernel(x_ref, ya_ref, ga_ref, gbyb_ref, wba_ref, wout_ref, gmoe_ref, wrh_ref, wrl_ref, br_ref,
                  x1_ref, hm_ref, rw_ref, ri_ref, cnt_ref, carry_ref):
    tm = x_ref.shape[0]

    @pl.when(pl.program_id(0) == 0)
    def _():
        carry_ref[...] = jnp.zeros_like(carry_ref)

    merged = ga_ref[...].astype(F32) * _dot(ya_ref[...], wba_ref[...]) + gbyb_ref[...].astype(F32)
    x1 = x_ref[...] + _dot(merged.astype(BF16), wout_ref[...])
    x1_ref[...] = x1
    hm = _rms(x1, gmoe_ref[...])
    for a in range(ROW_TILE):
        hm_ref[pl.ds(a, tm, stride=ROW_TILE), :] = hm[:, a * LANES:(a + 1) * LANES]

    hh = hm.astype(BF16)
    hl = (hm - hh.astype(F32)).astype(BF16)
    wrh = wrh_ref[...]
    logits = _dot(hh, wrh) + _dot(hl, wrh) + _dot(hh, wrl_ref[...]) + br_ref[...]
    lane = lax.broadcasted_iota(I32, (tm, LANES), 1)
    lg = jnp.where(lane < N_EXPERTS, logits, NEG)
    vals, idxs = [], []
    for _ in range(TOP_K):
        m = jnp.max(lg, axis=1, keepdims=True)
        idx = jnp.min(jnp.where(lg == m, lane, LANES), axis=1, keepdims=True)
        vals.append(m)
        idxs.append(idx)
        lg = jnp.where(lane == idx, NEG, lg)
    ex = [jnp.exp(v - vals[0]) for v in vals]
    den = ex[0] + ex[1] + ex[2] + ex[3]

    hits = [lane == idx for idx in idxs]
    multi = jnp.zeros((tm, LANES), F32)
    for h in hits:
        multi = jnp.where(h, 1.0, multi)
    ltri = (lax.broadcasted_iota(I32, (tm, tm), 0) > lax.broadcasted_iota(I32, (tm, tm), 1))
    carry = carry_ref[0:1, :]
    cum = _dot(jnp.where(ltri, 1.0, 0.0).astype(BF16), multi.astype(BF16)) + carry
    rw = jnp.zeros((tm, LANES), F32)
    ri = jnp.zeros((tm, LANES), I32)
    for k in range(TOP_K):
        rank = jnp.sum(jnp.where(hits[k], cum, 0.0), axis=1, keepdims=True).astype(I32)
        rw = jnp.where(lane == k, ex[k] / den, rw)
        ri = jnp.where(lane == k, idxs[k], jnp.where(lane == TOP_K + k, rank, ri))
    rw_ref[...] = rw
    ri_ref[...] = ri
    new_carry = carry + jnp.sum(multi, axis=0, keepdims=True)
    carry_ref[...] = jnp.broadcast_to(new_carry, carry_ref.shape)
    cnt_ref[...] = jnp.broadcast_to(new_carry, cnt_ref.shape)


def _merge(x2, ya, ga, gbyb, wba, wout, gmoe, wrh, wrl, br, tm):
    n = x2.shape[0]
    full = lambda a: pl.BlockSpec(a.shape, lambda i: (0,) * a.ndim)
    rows = lambda w: pl.BlockSpec((tm, w), lambda i: (i, 0))
    return pl.pallas_call(
        _merge_kernel,
        grid=(n // tm,),
        in_specs=[rows(D_MODEL), rows(ATTN_WIDTH), rows(D_MODEL), rows(D_MODEL),
                  full(wba), full(wout), full(gmoe), full(wrh), full(wrl), full(br)],
        out_specs=[rows(D_MODEL), pl.BlockSpec((tm * ROW_TILE, LANES), lambda i: (i, 0)), rows(LANES), rows(LANES),
                   pl.BlockSpec((8, LANES), lambda i: (0, 0))],
        out_shape=[jax.ShapeDtypeStruct((n, D_MODEL), F32),
                   jax.ShapeDtypeStruct((n * ROW_TILE, LANES), F32),
                   jax.ShapeDtypeStruct((n, LANES), F32),
                   jax.ShapeDtypeStruct((n, LANES), I32),
                   jax.ShapeDtypeStruct((8, LANES), F32)],
        scratch_shapes=[pltpu.VMEM((8, LANES), F32)],
        compiler_params=pltpu.CompilerParams(dimension_semantics=("arbitrary",),
                                             vmem_limit_bytes=VMEM_LIMIT),
        name="merge",
    )(x2, ya, ga, gbyb, wba, wout, gmoe, wrh, wrl, br)


def _tile_copy(src, i, dst, d, sem):
    return pltpu.make_async_copy(src.at[pl.ds(pl.multiple_of(i * ROW_TILE, ROW_TILE), ROW_TILE)],
                                 dst.at[pl.ds(pl.multiple_of(d * ROW_TILE, ROW_TILE), ROW_TILE)], sem)


def _dispatch_kernel(seg_ref, dest_ref, hm_ref, xs_ref, zero_ref, sem, zsem, *, n_pad):
    tm = hm_ref.shape[0] // ROW_TILE

    @pl.when(pl.program_id(0) == 0)
    def _():
        zero_ref[...] = jnp.zeros_like(zero_ref)

        def seg(e, c):
            def fill(r, c2):
                _tile_copy(zero_ref, 0, xs_ref, r, zsem).start()
                return c2
            return lax.fori_loop(seg_ref[0, e], seg_ref[1, e], fill, c)

        lax.fori_loop(0, N_EXPERTS + 1, seg, 0)
        pad_rows = xs_ref.at[pl.ds(0, n_pad * ROW_TILE)]
        pltpu.make_async_copy(pad_rows, pad_rows, zsem).wait()

    def issue(i, c):
        for k in range(TOP_K):
            _tile_copy(hm_ref, i, xs_ref, dest_ref[i * TOP_K + k], sem).start(priority=k % 2)
        return c

    lax.fori_loop(0, tm, issue, 0)
    for k in range(TOP_K):
        pltpu.make_async_copy(hm_ref, xs_ref.at[pl.ds(0, tm * ROW_TILE)], sem).wait()


def _dispatch(seg, dest_flat, hm, n_slots, tm):
    n = hm.shape[0] // ROW_TILE
    kern = functools.partial(_dispatch_kernel, n_pad=n_slots - n * TOP_K)
    return pl.pallas_call(
        kern,
        grid_spec=pltpu.PrefetchScalarGridSpec(
            num_scalar_prefetch=1,
            grid=(n // tm,),
            in_specs=[pl.BlockSpec((tm * TOP_K,), lambda i, sg: (i,), memory_space=pltpu.SMEM),
                      pl.BlockSpec((tm * ROW_TILE, LANES), lambda i, sg: (i, 0))],
            out_specs=pl.BlockSpec(memory_space=pl.ANY),
            scratch_shapes=[pltpu.VMEM((ROW_TILE, LANES), F32),
                            pltpu.SemaphoreType.DMA(()), pltpu.SemaphoreType.DMA(())]),
        out_shape=jax.ShapeDtypeStruct((n_slots * ROW_TILE, LANES), F32),
        compiler_params=pltpu.CompilerParams(dimension_semantics=("arbitrary",),
                                             has_side_effects=True),
        name="dispatch",
    )(seg, dest_flat, hm)


def _expert_kernel(be_ref, nu_ref, xs_ref, wgu_ref, bgu_ref, wd_ref, bd_ref, y_ref, wgu_bf, wd_bf):
    i = pl.program_id(0)

    @pl.when(i >= nu_ref[0])
    def _():
        y_ref[...] = jnp.zeros_like(y_ref)

    @pl.when((i == 0) | (be_ref[i] != be_ref[jnp.maximum(i - 1, 0)]))
    def _():
        wgu_bf[...] = wgu_ref[0].astype(BF16)
        wd_bf[...] = wd_ref[0].astype(BF16)

    @pl.when(i < nu_ref[0])
    def _():
        x = jnp.concatenate([xs_ref[pl.ds(a, MOE_BLOCK, stride=ROW_TILE), :] for a in range(ROW_TILE)], axis=1)
        gu = _dot(x.astype(BF16), wgu_bf[...]) + bgu_ref[0]
        gate = jnp.minimum(gu[:, :D_FF], SWIGLU_LIMIT)
        up = jnp.clip(gu[:, D_FF:], -SWIGLU_LIMIT, SWIGLU_LIMIT)
        act = gate * _sigmoid(SWIGLU_ALPHA * gate) * (up + 1.0)
        y = _dot(act.astype(BF16), wd_bf[...]) + bd_ref[0]
        for a in range(ROW_TILE):
            y_ref[pl.ds(a, MOE_BLOCK, stride=ROW_TILE), :] = y[:, a * LANES:(a + 1) * LANES]


def _experts(blk_expert, n_used, xs, wgu, bgu, wd, bd):
    n_blocks = xs.shape[0] // (MOE_BLOCK * ROW_TILE)
    blk = lambda i, be, nu: (jnp.minimum(i, nu[0] - 1), 0)
    exp3 = lambda i, be, nu: (be[jnp.minimum(i, nu[0] - 1)], 0, 0)
    return pl.pallas_call(
        _expert_kernel,
        grid_spec=pltpu.PrefetchScalarGridSpec(
            num_scalar_prefetch=2,
            grid=(n_blocks,),
            in_specs=[pl.BlockSpec((MOE_BLOCK * ROW_TILE, LANES), blk),
                      pl.BlockSpec((1, D_MODEL, 2 * D_FF), exp3),
                      pl.BlockSpec((1, 1, 2 * D_FF), exp3),
                      pl.BlockSpec((1, D_FF, D_MODEL), exp3),
                      pl.BlockSpec((1, 1, D_MODEL), exp3)],
            out_specs=pl.BlockSpec((MOE_BLOCK * ROW_TILE, LANES), lambda i, be, nu: (i, 0)),
            scratch_shapes=[pltpu.VMEM((D_MODEL, 2 * D_FF), BF16), pltpu.VMEM((D_FF, D_MODEL), BF16)]),
        out_shape=jax.ShapeDtypeStruct(xs.shape, F32),
        compiler_params=pltpu.CompilerParams(dimension_semantics=("arbitrary",),
                                             vmem_limit_bytes=VMEM_LIMIT),
        name="experts",
    )(blk_expert, n_used, xs, wgu, bgu, wd, bd)


def _combine_kernel(dest_ref, rw_ref, x1_ref, gfin_ref, y_ref, o_ref, ybuf, sem):
    tm = x1_ref.shape[0]

    def issue(i, c):
        for k in range(TOP_K):
            _tile_copy(y_ref, dest_ref[i * TOP_K + k], ybuf.at[k], i, sem).start(priority=k % 2)
        return c

    lax.fori_loop(0, tm, issue, 0)
    for k in range(TOP_K):
        pltpu.make_async_copy(y_ref.at[pl.ds(0, tm * ROW_TILE)], ybuf.at[k], sem).wait()

    rw = rw_ref[...]
    x1 = x1_ref[...]
    cols = []
    for a in range(ROW_TILE):
        acc = x1[:, a * LANES:(a + 1) * LANES]
        for k in range(TOP_K):
            acc = acc + rw[:, k:k + 1] * ybuf[k, pl.ds(a, tm, stride=ROW_TILE), :]
        cols.append(acc)
    o_ref[...] = _rms(jnp.concatenate(cols, axis=1), gfin_ref[...])


def _combine(dest_flat, rw, x1, gfin, y, tm):
    n = x1.shape[0]
    return pl.pallas_call(
        _combine_kernel,
        grid=(n // tm,),
        in_specs=[pl.BlockSpec((tm * TOP_K,), lambda i: (i,), memory_space=pltpu.SMEM),
                  pl.BlockSpec((tm, LANES), lambda i: (i, 0)),
                  pl.BlockSpec((tm, D_MODEL), lambda i: (i, 0)),
                  pl.BlockSpec((1, D_MODEL), lambda i: (0, 0)),
                  pl.BlockSpec(memory_space=pl.ANY)],
        out_specs=pl.BlockSpec((tm, D_MODEL), lambda i: (i, 0)),
        out_shape=jax.ShapeDtypeStruct((n, D_MODEL), F32),
        scratch_shapes=[pltpu.VMEM((TOP_K, tm * ROW_TILE, LANES), F32), pltpu.SemaphoreType.DMA(())],
        compiler_params=pltpu.CompilerParams(dimension_semantics=("arbitrary",),
                                             vmem_limit_bytes=VMEM_LIMIT),
        name="combine",
    )(dest_flat, rw, x1, gfin, y)


def _overlap_t(nc, n_cmp, n_sel):
    cs = np.arange(n_cmp)[None, :] * CMP_STRIDE
    ss = np.arange(n_sel)[:, None] * SEL_BLOCK
    ov = np.clip(np.minimum(cs + CMP_LEN, ss + SEL_BLOCK) - np.maximum(cs, ss), 0, None) / CMP_LEN
    out = np.zeros((LANES, nc), np.float32)
    out[:n_sel, :n_cmp] = ov
    return jnp.asarray(out, BF16)


def _layer(x2, b, s, g_mix, w_in, w_ck1, b_ck1, w_ck2, pe_k, w_cv1, b_cv1, w_cv2, pe_v,
           sgu_ln_g, sgu_ln_b, w_spatial, b_spatial, w_branch_a, w_branch_b, w_out,
           g_moe, w_router, b_router, w_gate_up, b_gate_up, w_down, b_down):
    n = b * s
    nc = s // CMP_STRIDE
    n_cmp = (s - CMP_LEN) // CMP_STRIDE + 1
    n_sel = s // SEL_BLOCK
    assert nc % KEY_CHUNK == 0 and n_sel <= LANES and n_cmp == nc - 1
    tm = 512
    assert s % tm == 0

    p0 = ATTN_WIDTH
    p1 = p0 + 6 * KV_WIDTH
    p2 = p1 + N_NSA_BRANCH * N_HEADS
    p3 = p2 + SGU_WIDTH
    p4 = p3 + SGU_WIDTH
    zpad = lambda a, w: jnp.pad(a, ((0, 0),) * (a.ndim - 1) + ((0, w - a.shape[-1]),))
    wq = zpad((w_in[:, :p0] * (HEAD_DIM ** -0.5 * LOG2E)).reshape(D_MODEL, N_HEADS, HEAD_DIM), LANES)
    wkv = w_in[:, p0:p1].reshape(D_MODEL, 6, N_KV, HEAD_DIM)
    wng = w_in[:, p1:p2].reshape(D_MODEL, N_KV, GQA_R, N_NSA_BRANCH).transpose(0, 1, 3, 2)
    wng = zpad(wng.reshape(D_MODEL, N_KV, N_NSA_BRANCH * GQA_R), GATE_ROWS)
    wt = jnp.concatenate([wq.reshape(D_MODEL, -1), wkv[:, 3].reshape(D_MODEL, -1),
                          wkv[:, 5].reshape(D_MODEL, -1), wng.reshape(D_MODEL, -1)], axis=1).T.astype(BF16)
    slopes = 2.0 ** (-8.0 * np.arange(1, N_HEADS + 1) / N_HEADS)
    tcol = np.zeros((T_ROWS, 1), np.float32)
    head_rows = T_Q + np.arange(N_HEADS) * LANES + HEAD_DIM
    bf16_round = lambda a: a.astype(BF16).astype(np.float32)
    for k, coef in enumerate((slopes * SEL_BLOCK * LOG2E, slopes * LOG2E)):
        hi = bf16_round(coef.astype(np.float32))
        tcol[head_rows + 2 * k, 0] = hi
        tcol[head_rows + 2 * k + 1, 0] = bf16_round(coef.astype(np.float32) - hi)
    tconst = jnp.asarray(np.broadcast_to(tcol, (T_ROWS, 2 * Q_BLOCK)))
    wk = jnp.concatenate([zpad(wkv[:, 2, g], 2 * LANES) for g in range(N_KV)]
                         + [zpad(wkv[:, 4, g], LANES) for g in range(N_KV)]
                         + [wkv[:, 0].reshape(D_MODEL, -1), wkv[:, 1].reshape(D_MODEL, -1)], axis=1).astype(BF16)
    pos = np.arange(s)
    kc_np = np.zeros((s, 2 * LANES), np.float32)
    kc_np[:, HEAD_DIM:HEAD_DIM + 2] = (pos // SEL_BLOCK)[:, None]
    kc_np[:, HEAD_DIM + 2:HEAD_DIM + 4] = (pos % SEL_BLOCK)[:, None]
    kc_np[pos, LANES + pos // SEL_BLOCK] = 1.0
    kconst = jnp.asarray(kc_np, BF16)
    wu = w_in[:, p2:p3].astype(BF16)
    wv = w_in[:, p3:p4].astype(BF16)
    wmg = w_in[:, p4:].astype(BF16)
    bsp = jnp.repeat(b_spatial.T, SGU_GROUP_DIM, axis=1)

    qt, gt, vst, vwt, ksa, kwa, kcx, vcx, ga, gbyb = _proj(
        x2, b, s, g_mix[None], wt, tconst, wk, kconst, wu, wv, wmg, sgu_ln_g[None], sgu_ln_b[None],
        w_spatial, bsp, w_branch_b.astype(BF16), tm)

    half = CMP_LEN // 2
    eye = jnp.eye(N_KV, dtype=F32)[None, None, :, None, :, None]
    bdiag = lambda w: (w.reshape(2, half, 1, HEAD_DIM, 1, CMP_HID) * eye).reshape(
        2, half, N_KV * HEAD_DIM, N_KV * CMP_HID)
    w1 = jnp.stack([bdiag(w_ck1), bdiag(w_cv1)]).astype(BF16)
    pes = jnp.stack([jnp.tile(pe_k.reshape(2, half, HEAD_DIM), (1, 1, N_KV)),
                     jnp.tile(pe_v.reshape(2, half, HEAD_DIM), (1, 1, N_KV))])
    b1 = jnp.stack([jnp.tile(b_ck1, N_KV), jnp.tile(b_cv1, N_KV)])[:, None, :]
    blk_n = np.arange(nc)
    cc_np = np.zeros((nc, LANES), np.float32)
    cc_np[:, HEAD_DIM:HEAD_DIM + 2] = (blk_n // (SEL_BLOCK // CMP_STRIDE))[:, None]
    cc_np[:, HEAD_DIM + 2:HEAD_DIM + 4] = (blk_n % (SEL_BLOCK // CMP_STRIDE) * CMP_STRIDE)[:, None]
    kc, vct = _compress(kcx, vcx, pes, w1, b1, zpad(w_ck2, LANES).astype(BF16), w_cv2.T.astype(BF16),
                        jnp.asarray(cc_np))

    ovt = _overlap_t(nc, n_cmp, n_sel).reshape(LANES, nc // KEY_CHUNK, KEY_CHUNK).transpose(1, 0, 2)
    blk = np.arange(LANES)
    pair = jnp.asarray(blk[:, None] // (KEY_CHUNK // SEL_BLOCK) == blk[None, :], BF16)
    a_io, q_io = np.meshgrid(np.arange(KEY_CHUNK), np.arange(Q_BLOCK), indexing="ij")
    tri = np.stack([np.where(a_io > q_io, 0.0, NEG), np.where(a_io <= q_io, 0.0, NEG)]).astype(np.float32)
    tri = jnp.asarray(np.tile(tri, (1, 1, 2)))
    ya = _nsa(qt, gt, kc, vct, ksa, vst, kwa, vwt, ovt, pair, tri, n_sel).reshape(n, ATTN_WIDTH)

    wr = jnp.pad(w_router, ((0, 0), (0, LANES - N_EXPERTS)))
    wrh = wr.astype(BF16)
    wrl = (wr - wrh.astype(F32)).astype(BF16)
    br = jnp.pad(b_router, (0, LANES - N_EXPERTS))[None]
    x1, hm, rw, ri, cnt = _merge(x2, ya, ga, gbyb, w_branch_a.astype(BF16), w_out.astype(BF16),
                                 g_moe[None], wrh, wrl, br, tm)

    counts = cnt[0, :N_EXPERTS].astype(I32)
    padded = (counts + MOE_BLOCK - 1) // MOE_BLOCK * MOE_BLOCK
    pad_end = jnp.cumsum(padded)
    pad_start = pad_end - padded
    dest = (pad_start[ri[:, :TOP_K]] + ri[:, TOP_K:2 * TOP_K]).reshape(-1)
    n_blocks = -(-(n * TOP_K) // MOE_BLOCK) + N_EXPERTS
    blk_start = jnp.arange(n_blocks, dtype=I32) * MOE_BLOCK
    blk_expert = jnp.minimum(jnp.sum((pad_end[None, :] <= blk_start[:, None]).astype(I32), axis=1),
                             N_EXPERTS - 1)
    n_used = (pad_end[-1:] // MOE_BLOCK).astype(I32)
    n_slots = n_blocks * MOE_BLOCK
    seg = jnp.stack([jnp.concatenate([pad_start + counts, pad_end[-1:]]),
                     jnp.concatenate([pad_end, jnp.full((1,), n_slots, I32)])]).astype(I32)

    tg = 256
    xs = _dispatch(seg, dest, hm, n_slots, tg)
    y = _experts(blk_expert, n_used, xs, w_gate_up, b_gate_up[:, None, :], w_down, b_down[:, None, :])
    return dest, rw, x1, y, tg


def kernel(x, g_mix, w_in, w_ck1, b_ck1, w_ck2, pe_k, w_cv1, b_cv1, w_cv2, pe_v, sgu_ln_g, sgu_ln_b, w_spatial, b_spatial, w_branch_a, w_branch_b, w_out, g_moe, w_router, b_router, w_gate_up, b_gate_up, w_down, b_down, g_final):
    b, s, d = x.shape
    assert g_mix.shape[0] == 1, "the final rmsnorm is fused into the single layer's combine step"
    l = 0
    dest, rw, x1, y, tg = _layer(
        x.reshape(b * s, d), b, s, g_mix[l], w_in[l], w_ck1[l], b_ck1[l], w_ck2[l], pe_k[l], w_cv1[l],
        b_cv1[l], w_cv2[l], pe_v[l], sgu_ln_g[l], sgu_ln_b[l], w_spatial[l], b_spatial[l], w_branch_a[l],
        w_branch_b[l], w_out[l], g_moe[l], w_router[l], b_router[l], w_gate_up[l], b_gate_up[l],
        w_down[l], b_down[l])
    return _combine(dest, rw, x1, g_final[None], y, tg).reshape(b, s, d)
```

```python
import functools

import numpy as np
import jax
import jax.numpy as jnp
from jax import lax
from jax.experimental import pallas as pl
from jax.experimental.pallas import tpu as pltpu

D_MODEL = 1024
N_HEADS = 8
HEAD_DIM = 64
N_KV = 2
GQA_R = N_HEADS // N_KV
ATTN_WIDTH = N_HEADS * HEAD_DIM
KV_WIDTH = N_KV * HEAD_DIM
CMP_LEN = 32
CMP_STRIDE = 16
CMP_HID = 128
SEL_BLOCK = 64
SEL_TOPN = 16
WINDOW = 512
Q_BLOCK = 128
N_NSA_BRANCH = 3
SGU_WIDTH = 512
N_GROUPS_SGU = 8
SGU_GROUP_DIM = SGU_WIDTH // N_GROUPS_SGU
CHUNK = 128
N_EXPERTS = 32
TOP_K = 4
D_FF = D_MODEL
SWIGLU_LIMIT = 7.0
SWIGLU_ALPHA = 1.702
MOE_BLOCK = 512
RMS_EPS = 1e-5
LN_EPS = 1e-5
NEG = -1e30
FORCE = 1e4

LANES = 128
ROW_TILE = D_MODEL // LANES
MASK_BIAS = 1e9
LOG2E = float(np.log2(np.e))
KEY_CHUNK = 128
SEL_SLOTS = 4
GATE_ROWS = 16
VMEM_LIMIT = 56 * 1024 * 1024

F32 = jnp.float32
BF16 = jnp.bfloat16
I32 = jnp.int32

_NT = (((1,), (1,)), ((), ()))


def _dot(a, b):
    return jnp.dot(a, b, preferred_element_type=F32)


def _dot_nt(a, b):
    return lax.dot_general(a, b, _NT, preferred_element_type=F32)


def _sigmoid(x):
    return 1.0 / (1.0 + jnp.exp(-x))


def _gelu(x):
    c = np.float32(np.sqrt(2.0 / np.pi))
    return 0.5 * x * (1.0 + jnp.tanh(c * (x + 0.044715 * (x * x * x))))


def _rms(x, g):
    return x * lax.rsqrt(jnp.mean(x * x, axis=-1, keepdims=True) + RMS_EPS) * g


T_Q = 0
T_VS = T_Q + N_HEADS * LANES
T_VW = T_VS + N_KV * HEAD_DIM
T_GATE = T_VW + N_KV * HEAD_DIM
T_ROWS = T_GATE + N_KV * GATE_ROWS
K_SEL = 0
K_WIN = K_SEL + N_KV * 2 * LANES
K_CMP = K_WIN + N_KV * LANES
K_COLS = K_CMP + 4 * HEAD_DIM


def _proj_kernel(x_ref, gmix_ref, wt_ref, tconst_ref, wk_ref, kconst_ref, wu_ref, wv_ref, wmg_ref,
                 lng_ref, lnb_ref, wsp_ref, bsp_ref, wbb_ref,
                 qt_ref, gt_ref, vst_ref, vwt_ref, ksa_ref, kwa_ref, kcx_ref, vcx_ref, ga_ref, gbyb_ref):
    tm = x_ref.shape[0]
    hb = _rms(x_ref[...], gmix_ref[...]).astype(BF16)

    for h in range(tm // (2 * Q_BLOCK)):
        t = _dot_nt(wt_ref[...], hb[h * 2 * Q_BLOCK:(h + 1) * 2 * Q_BLOCK]) + tconst_ref[...]
        for jj in range(2):
            j = 2 * h + jj
            ls = slice(jj * Q_BLOCK, (jj + 1) * Q_BLOCK)
            for g in range(N_KV):
                for r in range(GQA_R):
                    r0 = T_Q + (g * GQA_R + r) * LANES
                    qt_ref[0, g, j, :, r * Q_BLOCK:(r + 1) * Q_BLOCK] = t[r0:r0 + LANES, ls].astype(BF16)
                vst_ref[0, g, j] = t[T_VS + g * HEAD_DIM:T_VS + (g + 1) * HEAD_DIM, ls].astype(BF16)
                vwt_ref[0, g, j] = t[T_VW + g * HEAD_DIM:T_VW + (g + 1) * HEAD_DIM, ls].astype(BF16)
                gt_ref[0, g, j] = _sigmoid(t[T_GATE + g * GATE_ROWS:T_GATE + (g + 1) * GATE_ROWS, ls])

    kk = _dot(hb, wk_ref[...])
    kconst = kconst_ref[...].astype(F32)
    for g in range(N_KV):
        c0 = K_SEL + g * 2 * LANES
        ksa_ref[0, g] = (kk[:, c0:c0 + 2 * LANES] + kconst).astype(BF16)
        c0 = K_WIN + g * LANES
        kwa_ref[0, g] = (kk[:, c0:c0 + LANES] + kconst[:, :LANES]).astype(BF16)
    kcx_ref[0] = kk[:, K_CMP:K_CMP + LANES]
    vcx_ref[0] = kk[:, K_CMP + LANES:K_COLS]

    mg = _dot(hb, wmg_ref[...])
    ga_ref[...] = _sigmoid(mg[:, :D_MODEL]).astype(BF16)

    u = _gelu(_dot(hb, wu_ref[...]))
    v = _gelu(_dot(hb, wv_ref[...]))
    mu = jnp.mean(v, axis=-1, keepdims=True)
    vc = v - mu
    var = jnp.mean(vc * vc, axis=-1, keepdims=True)
    vln = (vc * lax.rsqrt(var + LN_EPS) * lng_ref[...] + lnb_ref[...]).astype(BF16)

    row = lax.broadcasted_iota(I32, (CHUNK, CHUNK), 0)
    col = lax.broadcasted_iota(I32, (CHUNK, CHUNK), 1)
    tril = row >= col
    wsp = [jnp.where(tril, wsp_ref[g], 0.0).astype(BF16) for g in range(N_GROUPS_SGU)]
    low_half = col < SGU_GROUP_DIM
    bsp = bsp_ref[...]
    chunks = []
    for c in range(tm // CHUNK):
        rs = slice(c * CHUNK, (c + 1) * CHUNK)
        parts = []
        for p in range(SGU_WIDTH // LANES):
            cs = slice(p * LANES, (p + 1) * LANES)
            vblk = vln[rs, cs]
            mixed = jnp.where(low_half, _dot(wsp[2 * p], vblk), _dot(wsp[2 * p + 1], vblk))
            parts.append(u[rs, cs] * (mixed + bsp[:, cs]))
        chunks.append(jnp.concatenate(parts, axis=1))
    yb = jnp.concatenate(chunks, axis=0).astype(BF16)
    gbyb_ref[...] = (_sigmoid(mg[:, D_MODEL:]) * _dot(yb, wbb_ref[...])).astype(BF16)


def _proj(x2, b, s, gmix, wt, tconst, wk, kconst, wu, wv, wmg, lng, lnb, wsp, bsp, wbb, tm):
    n = x2.shape[0]
    tiles = s // tm
    qtiles = tm // Q_BLOCK
    full = lambda a: pl.BlockSpec(a.shape, lambda i, j: (0,) * a.ndim)
    rows = lambda w: pl.BlockSpec((tm, w), lambda i, j: (i * tiles + j, 0))
    per_g = lambda *blk: pl.BlockSpec((1, N_KV) + blk, lambda i, j: (i, 0, j) + (0,) * (len(blk) - 1))
    ins = (x2, gmix, wt, tconst, wk, kconst, wu, wv, wmg, lng, lnb, wsp, bsp, wbb)
    in_specs = [rows(D_MODEL)] + [full(a) for a in ins[1:]]
    in_specs[5] = pl.BlockSpec((tm, 2 * LANES), lambda i, j: (j, 0))
    nt = s // Q_BLOCK
    return pl.pallas_call(
        _proj_kernel,
        grid=(b, tiles),
        in_specs=in_specs,
        out_specs=[per_g(qtiles, LANES, GQA_R * Q_BLOCK), per_g(qtiles, GATE_ROWS, Q_BLOCK),
                   per_g(qtiles, HEAD_DIM, Q_BLOCK), per_g(qtiles, HEAD_DIM, Q_BLOCK),
                   per_g(tm, 2 * LANES), per_g(tm, LANES),
                   pl.BlockSpec((1, tm, LANES), lambda i, j: (i, j, 0)),
                   pl.BlockSpec((1, tm, LANES), lambda i, j: (i, j, 0)),
                   rows(D_MODEL), rows(D_MODEL)],
        out_shape=[jax.ShapeDtypeStruct((b, N_KV, nt, LANES, GQA_R * Q_BLOCK), BF16),
                   jax.ShapeDtypeStruct((b, N_KV, nt, GATE_ROWS, Q_BLOCK), F32),
                   jax.ShapeDtypeStruct((b, N_KV, nt, HEAD_DIM, Q_BLOCK), BF16),
                   jax.ShapeDtypeStruct((b, N_KV, nt, HEAD_DIM, Q_BLOCK), BF16),
                   jax.ShapeDtypeStruct((b, N_KV, s, 2 * LANES), BF16),
                   jax.ShapeDtypeStruct((b, N_KV, s, LANES), BF16),
                   jax.ShapeDtypeStruct((b, s, LANES), F32),
                   jax.ShapeDtypeStruct((b, s, LANES), F32),
                   jax.ShapeDtypeStruct((n, D_MODEL), BF16),
                   jax.ShapeDtypeStruct((n, D_MODEL), BF16)],
        compiler_params=pltpu.CompilerParams(dimension_semantics=("parallel", "parallel"),
                                             vmem_limit_bytes=VMEM_LIMIT),
        name="proj",
    )(*ins)


def _compress_kernel(xk_ref, xv_ref, pe_ref, w1_ref, b1_ref, w2k_ref, w2vt_ref, kconst_ref, kc_ref, vct_ref):
    nc = kc_ref.shape[2]
    half = CMP_LEN // 2
    hids = []
    for which, x_ref in enumerate((xk_ref, xv_ref)):
        za = jnp.zeros((nc, N_KV * CMP_HID), F32)
        zb = jnp.zeros((nc, N_KV * CMP_HID), F32)
        for l in range(half):
            xl = x_ref[0, pl.ds(l, nc, stride=CMP_STRIDE), :]
            za = za + _dot((xl + pe_ref[which, 0, l:l + 1, :]).astype(BF16), w1_ref[which, 0, l])
            zb = zb + _dot((xl + pe_ref[which, 1, l:l + 1, :]).astype(BF16), w1_ref[which, 1, l])
        hid = _gelu(za + pltpu.roll(zb, nc - 1, 0) + b1_ref[which])
        row = lax.broadcasted_iota(I32, hid.shape, 0)
        hids.append(jnp.where(row < nc - 1, hid, 0.0).astype(BF16))
    for g in range(N_KV):
        cs = slice(g * CMP_HID, (g + 1) * CMP_HID)
        kc_ref[0, g] = (_dot(hids[0][:, cs], w2k_ref[...]) + kconst_ref[...]).astype(BF16)
        vt = _dot_nt(w2vt_ref[...], hids[1][:, cs])
        for c in range(nc // KEY_CHUNK):
            vct_ref[0, g, c] = vt[:, c * KEY_CHUNK:(c + 1) * KEY_CHUNK].astype(BF16)


def _compress(xk, xv, pes, w1, b1, w2k, w2vt, kconst):
    b, s, w = xk.shape
    nc = s // CMP_STRIDE
    full = lambda a: pl.BlockSpec(a.shape, lambda i: (0,) * a.ndim)
    seq = pl.BlockSpec((1, s, w), lambda i: (i, 0, 0))
    return pl.pallas_call(
        _compress_kernel,
        grid=(b,),
        in_specs=[seq, seq, full(pes), full(w1), full(b1), full(w2k), full(w2vt), full(kconst)],
        out_specs=[pl.BlockSpec((1, N_KV, nc, LANES), lambda i: (i, 0, 0, 0)),
                   pl.BlockSpec((1, N_KV, nc // KEY_CHUNK, HEAD_DIM, KEY_CHUNK), lambda i: (i, 0, 0, 0, 0))],
        out_shape=[jax.ShapeDtypeStruct((b, N_KV, nc, LANES), BF16),
                   jax.ShapeDtypeStruct((b, N_KV, nc // KEY_CHUNK, HEAD_DIM, KEY_CHUNK), BF16)],
        compiler_params=pltpu.CompilerParams(dimension_semantics=("parallel",),
                                             vmem_limit_bytes=VMEM_LIMIT),
        name="compress",
    )(xk, xv, pes, w1, b1, w2k, w2vt, kconst)


def _pairs():
    return [slice(pr * 2 * Q_BLOCK, (pr + 1) * 2 * Q_BLOCK) for pr in range(GQA_R // 2)]


def _colmax(blocks):
    part = None
    for x in blocks:
        y = jnp.max(x.reshape(-1, 8, x.shape[-1]), axis=0)
        part = y if part is None else jnp.maximum(part, y)
    return jnp.max(part, axis=0, keepdims=True)


def _colsum(blocks):
    part = None
    for x in blocks:
        y = jnp.sum(x.reshape(-1, 8, x.shape[-1]), axis=0)
        part = y if part is None else part + y
    return jnp.sum(part, axis=0, keepdims=True)


def _softmax_pv(scores, vts, masks, m_ref, l_ref, acc_ref):
    for pr, ls in enumerate(_pairs()):
        sb = [x if masks[u] is None else masks[u](x, ls) for u, x in enumerate(scores[pr])]
        m_old = m_ref[0:1, ls]
        m_new = jnp.maximum(m_old, _colmax(sb))
        alpha = jnp.exp2(m_old - m_new)
        ps = [jnp.exp2(x - m_new) for x in sb]
        pv = _dot(vts[0], ps[0].astype(BF16))
        for u in range(1, len(ps)):
            pv = pv + _dot(vts[u], ps[u].astype(BF16))
        l_ref[0:1, ls] = alpha * l_ref[0:1, ls] + _colsum(ps)
        acc_ref[:, ls] = alpha * acc_ref[:, ls] + pv
        m_ref[0:1, ls] = m_new


def _nsa_kernel(qt_ref, gt_ref, kc_ref, vct_ref, ksa_ref, vst_ref, kwa_ref, vwt_ref, ovt_ref, pair_ref, tri_ref,
                o_ref, qa_ref, s0_ref, s1_ref, m_ref, l_ref, acc_ref, act_ref, *, n_sel, top_n):
    i = pl.program_id(1)
    t0 = i * Q_BLOCK
    width = GQA_R * Q_BLOCK
    n_chunks = ksa_ref.shape[2] // KEY_CHUNK
    n_cchunks = kc_ref.shape[2] // KEY_CHUNK
    n_slots = act_ref.shape[0] // N_KV
    lane = lax.broadcasted_iota(I32, (1, width), 1)
    tq = t0 + (lane & (Q_BLOCK - 1))
    sub = lax.broadcasted_iota(I32, (KEY_CHUNK, 1), 0)
    pairs = _pairs()
    heads = range(N_KV)

    def reset(g):
        m_ref[g] = jnp.full(m_ref.shape[1:], NEG, F32)
        l_ref[g] = jnp.zeros(l_ref.shape[1:], F32)
        acc_ref[g] = jnp.zeros(acc_ref.shape[1:], F32)

    sc = [[[_dot(kc_ref[0, g, u * KEY_CHUNK:(u + 1) * KEY_CHUNK, :], qt_ref[0, g, 0, :, ls])
            for u in range(n_cchunks)] for ls in pairs] for g in heads]
    o_c, imp_t = [], []
    for g in heads:
        ocg, ps = [], []
        for pr, ls in enumerate(pairs):
            sb = []
            for u in range(n_cchunks):
                c_end = (u * KEY_CHUNK + sub) * CMP_STRIDE + (CMP_LEN - 1)
                sb.append(jnp.where(c_end <= tq[:, ls], sc[g][pr][u], NEG))
            m = _colmax(sb)
            pb = [jnp.exp2(x - m) for x in sb]
            l = _colsum(pb)
            w = (tq[:, ls] >= CMP_LEN - 1).astype(F32) / l
            oc = _dot(vct_ref[0, g, 0], pb[0].astype(BF16))
            for u in range(1, n_cchunks):
                oc = oc + _dot(vct_ref[0, g, u], pb[u].astype(BF16))
            ocg.append(oc * w)
            ps.append([p[:, :Q_BLOCK] * w[:, :Q_BLOCK] + p[:, Q_BLOCK:] * w[:, Q_BLOCK:] for p in pb])
        imp2 = jnp.zeros((LANES, 2 * Q_BLOCK), F32)
        for u in range(n_cchunks):
            psu = ps[0][u] + ps[1][u]
            hi = psu.astype(BF16)
            lo = (psu - hi.astype(F32)).astype(BF16)
            imp2 = imp2 + _dot(ovt_ref[u], jnp.concatenate([hi, lo], axis=1))
        o_c.append(jnp.concatenate(ocg, axis=1))
        imp_t.append(imp2[:, :Q_BLOCK] + imp2[:, Q_BLOCK:])

    n_win = WINDOW // KEY_CHUNK
    cls, masks = [], []
    for u in range(n_win + 1):
        c = i - n_win + u
        cls.append(jnp.maximum(c, 0))
        off = jnp.where(c < 0, NEG, 0.0)
        if u == 0:
            edge = tri_ref[0] + off
            masks.append(lambda x, ls, edge=edge: x + edge)
        elif u == n_win:
            masks.append(lambda x, ls: x + tri_ref[1])
        else:
            masks.append(lambda x, ls, off=off: x + off)
    wsc = [[[_dot(kwa_ref[0, g, pl.ds(pl.multiple_of(cl * KEY_CHUNK, KEY_CHUNK), KEY_CHUNK), :],
                  qt_ref[0, g, 0, :, ls]) for cl in cls] for ls in pairs] for g in heads]
    o_w = []
    for g in heads:
        reset(g)
        _softmax_pv(wsc[g], [vwt_ref[0, g, cl] for cl in cls], masks, m_ref.at[g], l_ref.at[g], acc_ref.at[g])
        o_w.append(acc_ref[g] / l_ref[g, 0:1, :])

    j_io = lax.broadcasted_iota(I32, (LANES, Q_BLOCK), 0)
    tl = t0 + lax.broadcasted_iota(I32, (LANES, Q_BLOCK), 1)
    cur = tl >> 6
    forced = (j_io == 0) | (j_io == cur) | (j_io == cur - 1)
    prio = [jnp.where(j_io < n_sel, jnp.where(forced, FORCE, jnp.where(j_io * SEL_BLOCK <= tl, imp_t[g], -1.0)), NEG)
            for g in heads]
    for _ in range(top_n):
        for g in heads:
            m = jnp.max(prio[g], axis=0, keepdims=True)
            idx = jnp.min(jnp.where(prio[g] == m, j_io, LANES), axis=0, keepdims=True)
            prio[g] = jnp.where(j_io == idx, NEG, prio[g])
    sel_t = [jnp.where((prio[g] == NEG) & (j_io < n_sel), 1.0, 0.0) for g in heads]

    n_act = []
    for g in heads:
        qa_ref[g, 0:LANES, :] = qt_ref[0, g, 0]
        bias_t = ((sel_t[g] - 1.0) * MASK_BIAS).astype(BF16)
        for h in range(GQA_R):
            qa_ref[g, LANES:2 * LANES, h * Q_BLOCK:(h + 1) * Q_BLOCK] = bias_t
        per_block = _dot_nt(jnp.ones((8, Q_BLOCK), BF16), sel_t[g].astype(BF16))
        per_chunk = _dot(per_block.astype(BF16), pair_ref[...])
        na = jnp.int32(0)
        for c in range(n_chunks):
            act_ref[g * n_slots + na] = jnp.int32(c)
            na = na + jnp.where((per_chunk[0, c] > 0.0) & (c < i), 1, 0)
        act_ref[g * n_slots + na] = i
        for u in range(1, 2 * SEL_SLOTS):
            act_ref[g * n_slots + na + u] = jnp.int32(-1)
        n_act.append(na)

    def sel_scores(g, k, dst):
        for u in range(SEL_SLOTS):
            cl = jnp.maximum(act_ref[g * n_slots + k * SEL_SLOTS + u], 0)
            keys = ksa_ref[0, g, pl.ds(pl.multiple_of(cl * KEY_CHUNK, KEY_CHUNK), KEY_CHUNK), :]
            for ls in pairs:
                dst[g, u, :, ls] = _dot(keys, qa_ref[g, :, ls])

    def sel_step(g, k, cur_s, nxt_s, last):
        sel_scores(g, k + 1, nxt_s)
        vts, msk = [], []
        for u in range(SEL_SLOTS):
            c = act_ref[g * n_slots + k * SEL_SLOTS + u]
            vts.append(vst_ref[0, g, jnp.maximum(c, 0)])
            if last:
                kpos = jnp.where(c < 0, n_chunks * KEY_CHUNK, c * KEY_CHUNK) + sub
                msk.append(lambda x, ls, kpos=kpos: jnp.where(kpos <= tq[:, ls], x, -MASK_BIAS))
            else:
                msk.append(None)
        scores = [[cur_s[g, u, :, ls] for u in range(SEL_SLOTS)] for ls in pairs]
        _softmax_pv(scores, vts, msk, m_ref.at[g], l_ref.at[g], acc_ref.at[g])

    o_s = []
    for g in heads:
        reset(g)
        sel_scores(g, 0, s0_ref)
    for g in heads:
        n_steps = (n_act[g] + SEL_SLOTS) // SEL_SLOTS

        def sel_body(k, carry, g=g, n_steps=n_steps):
            for odd, (cur_s, nxt_s) in enumerate(((s0_ref, s1_ref), (s1_ref, s0_ref))):
                for last in (False, True):
                    @pl.when((k % 2 == odd) & ((k == n_steps - 1) == last))
                    def _():
                        sel_step(g, k, cur_s, nxt_s, last)

            return carry

        lax.fori_loop(0, n_steps, sel_body, 0)
        o_s.append(acc_ref[g] / l_ref[g, 0:1, :])

    for g in heads:
        gt = gt_ref[0, g, 0]

        def gate(br):
            return jnp.concatenate([gt[br * GQA_R + h:br * GQA_R + h + 1, :] for h in range(GQA_R)], axis=1)

        o = gate(0) * o_c[g] + gate(1) * o_s[g] + gate(2) * o_w[g]
        o_ref[0, :, g * GQA_R * HEAD_DIM:(g + 1) * GQA_R * HEAD_DIM] = jnp.concatenate(
            [o[:, h * Q_BLOCK:(h + 1) * Q_BLOCK].T for h in range(GQA_R)], axis=1).astype(BF16)


def _nsa(qt, gt, kc, vct, ksa, vst, kwa, vwt, ovt, pair, tri, n_sel):
    b, g, nt, _, width = qt.shape
    s = ksa.shape[2]
    tile = lambda a: pl.BlockSpec((1, g, 1) + a.shape[3:], lambda i, k: (i, 0, k, 0, 0))
    whole = lambda a: pl.BlockSpec((1,) + a.shape[1:], lambda i, k: (i,) + (0,) * (a.ndim - 1))
    const = lambda a: pl.BlockSpec(a.shape, lambda i, k: (0,) * a.ndim)
    kern = functools.partial(_nsa_kernel, n_sel=n_sel, top_n=min(SEL_TOPN, n_sel))
    return pl.pallas_call(
        kern,
        grid=(b, nt),
        in_specs=[tile(qt), tile(gt), whole(kc), whole(vct), whole(ksa), whole(vst), whole(kwa), whole(vwt),
                  const(ovt), const(pair), const(tri)],
        out_specs=pl.BlockSpec((1, Q_BLOCK, ATTN_WIDTH), lambda i, k: (i, k, 0)),
        out_shape=jax.ShapeDtypeStruct((b, s, ATTN_WIDTH), BF16),
        scratch_shapes=[pltpu.VMEM((g, 2 * LANES, width), BF16),
                        pltpu.VMEM((g, SEL_SLOTS, KEY_CHUNK, width), F32),
                        pltpu.VMEM((g, SEL_SLOTS, KEY_CHUNK, width), F32),
                        pltpu.VMEM((g, 8, width), F32),
                        pltpu.VMEM((g, 8, width), F32),
                        pltpu.VMEM((g, HEAD_DIM, width), F32),
                        pltpu.SMEM((g * (s // KEY_CHUNK + 2 * SEL_SLOTS),), I32)],
        compiler_params=pltpu.CompilerParams(dimension_semantics=("parallel", "arbitrary"),
                                             vmem_limit_bytes=VMEM_LIMIT),
        name="nsa",
    )(qt, gt, kc, vct, ksa, vst, kwa, vwt, ovt, pair, tri)


def _merge_kernel(x_ref, ya_ref, ga_ref, gbyb_ref, wba_ref, wout_ref, gmoe_ref, wr_ref, br_ref, ltri_ref,
                  x1_ref, hm_ref, rw_ref, ri_ref, cnt_ref, carry_ref):
    tm = x_ref.shape[0]

    @pl.when(pl.program_id(0) == 0)
    def _():
        carry_ref[...] = jnp.zeros_like(carry_ref)

    merged = ga_ref[...].astype(F32) * _dot(ya_ref[...], wba_ref[...]) + gbyb_ref[...].astype(F32)
    x1 = x_ref[...] + _dot(merged.astype(BF16), wout_ref[...])
    x1_ref[...] = x1
    hm = _rms(x1, gmoe_ref[...])
    for a in range(ROW_TILE):
        hm_ref[pl.ds(a, tm, stride=ROW_TILE), :] = hm[:, a * LANES:(a + 1) * LANES]

    hh = hm.astype(BF16)
    hl = (hm - hh.astype(F32)).astype(BF16)
    both = _dot(hh, wr_ref[...])
    logits = both[:, :LANES] + both[:, LANES:] + _dot(hl, wr_ref[:, :LANES]) + br_ref[...]
    lane = lax.broadcasted_iota(I32, (tm, LANES), 1)
    lg = jnp.where(lane < N_EXPERTS, logits, NEG)
    vals, idxs = [], []
    for _ in range(TOP_K):
        m = jnp.max(lg, axis=1, keepdims=True)
        idx = jnp.min(jnp.where(lg == m, lane, LANES), axis=1, keepdims=True)
        vals.append(m)
        idxs.append(idx)
        lg = jnp.where(lane == idx, NEG, lg)
    ex = [jnp.exp(v - vals[0]) for v in vals]
    den = ex[0] + ex[1] + ex[2] + ex[3]

    hits = [lane == idx for idx in idxs]
    multi = jnp.zeros((tm, LANES), F32)
    for h in hits:
        multi = jnp.where(h, 1.0, multi)
    carry = carry_ref[0:1, :]
    cum = _dot(ltri_ref[...], multi.astype(BF16)) + carry
    rw = jnp.zeros((tm, LANES), F32)
    ri = jnp.zeros((tm, LANES), I32)
    for k in range(TOP_K):
        rank = jnp.sum(jnp.where(hits[k], cum, 0.0), axis=1, keepdims=True).astype(I32)
        rw = jnp.where(lane == k, ex[k] / den, rw)
        ri = jnp.where(lane == k, idxs[k], jnp.where(lane == TOP_K + k, rank, ri))
    rw_ref[...] = rw
    ri_ref[...] = ri
    new_carry = carry + jnp.sum(multi, axis=0, keepdims=True)
    carry_ref[...] = jnp.broadcast_to(new_carry, carry_ref.shape)
    cnt_ref[...] = jnp.broadcast_to(new_carry, cnt_ref.shape)


def _merge(x2, ya, ga, gbyb, wba, wout, gmoe, wr, br, tm):
    n = x2.shape[0]
    full = lambda a: pl.BlockSpec(a.shape, lambda i: (0,) * a.ndim)
    rows = lambda w: pl.BlockSpec((tm, w), lambda i: (i, 0))
    ltri = jnp.asarray(np.tril(np.ones((tm, tm), np.float32), -1), BF16)
    return pl.pallas_call(
        _merge_kernel,
        grid=(n // tm,),
        in_specs=[rows(D_MODEL), rows(ATTN_WIDTH), rows(D_MODEL), rows(D_MODEL),
                  full(wba), full(wout), full(gmoe), full(wr), full(br), full(ltri)],
        out_specs=[rows(D_MODEL), pl.BlockSpec((tm * ROW_TILE, LANES), lambda i: (i, 0)), rows(LANES), rows(LANES),
                   pl.BlockSpec((8, LANES), lambda i: (0, 0))],
        out_shape=[jax.ShapeDtypeStruct((n, D_MODEL), F32),
                   jax.ShapeDtypeStruct((n * ROW_TILE, LANES), F32),
                   jax.ShapeDtypeStruct((n, LANES), F32),
                   jax.ShapeDtypeStruct((n, LANES), I32),
                   jax.ShapeDtypeStruct((8, LANES), F32)],
        scratch_shapes=[pltpu.VMEM((8, LANES), F32)],
        compiler_params=pltpu.CompilerParams(dimension_semantics=("arbitrary",),
                                             vmem_limit_bytes=VMEM_LIMIT),
        name="merge",
    )(x2, ya, ga, gbyb, wba, wout, gmoe, wr, br, ltri)


def _tile_copy(src, i, dst, d, sem):
    return pltpu.make_async_copy(src.at[pl.ds(pl.multiple_of(i * ROW_TILE, ROW_TILE), ROW_TILE)],
                                 dst.at[pl.ds(pl.multiple_of(d * ROW_TILE, ROW_TILE), ROW_TILE)], sem)


def _dispatch_kernel(seg_ref, dest_ref, hm_ref, xs_ref, zero_ref, sem, zsem, *, n_pad, tm):
    step = pl.program_id(0)

    @pl.when(step == 0)
    def _():
        zero_ref[...] = jnp.zeros_like(zero_ref)

        def seg(e, c):
            def fill(r, c2):
                _tile_copy(zero_ref, 0, xs_ref, r, zsem).start()
                return c2
            return lax.fori_loop(seg_ref[0, e], seg_ref[1, e], fill, c)

        lax.fori_loop(0, N_EXPERTS + 1, seg, 0)
        pad_rows = xs_ref.at[pl.ds(0, n_pad * ROW_TILE)]
        pltpu.make_async_copy(pad_rows, pad_rows, zsem).wait()

    def issue(r, c):
        row = step * tm + r
        slots = [dest_ref[r * TOP_K + k] for k in range(TOP_K)]
        for k in range(TOP_K):
            _tile_copy(hm_ref, row, xs_ref, slots[k], sem).start(priority=k % 2)
        return c

    lax.fori_loop(0, tm, issue, 0)

    def drain_one_step():
        for k in range(TOP_K):
            pltpu.make_async_copy(hm_ref.at[pl.ds(0, tm * ROW_TILE)], xs_ref.at[pl.ds(0, tm * ROW_TILE)], sem).wait()

    @pl.when(step > 0)
    def _():
        drain_one_step()

    @pl.when(step == pl.num_programs(0) - 1)
    def _():
        drain_one_step()


def _dispatch(seg, dest_flat, hm, n_slots, tm):
    n = hm.shape[0] // ROW_TILE
    kern = functools.partial(_dispatch_kernel, n_pad=n_slots - n * TOP_K, tm=tm)
    return pl.pallas_call(
        kern,
        grid_spec=pltpu.PrefetchScalarGridSpec(
            num_scalar_prefetch=1,
            grid=(n // tm,),
            in_specs=[pl.BlockSpec((tm * TOP_K,), lambda i, sg: (i,), memory_space=pltpu.SMEM),
                      pl.BlockSpec(memory_space=pl.ANY)],
            out_specs=pl.BlockSpec(memory_space=pl.ANY),
            scratch_shapes=[pltpu.VMEM((ROW_TILE, LANES), F32),
                            pltpu.SemaphoreType.DMA(()), pltpu.SemaphoreType.DMA(())]),
        out_shape=jax.ShapeDtypeStruct((n_slots * ROW_TILE, LANES), F32),
        compiler_params=pltpu.CompilerParams(dimension_semantics=("arbitrary",),
                                             has_side_effects=True),
        name="dispatch",
    )(seg, dest_flat, hm)


def _expert_kernel(be_ref, nu_ref, xs_ref, wgu_ref, bgu_ref, wd_ref, bd_ref, y_ref, wgu_bf, wd_bf):
    i = pl.program_id(0)

    @pl.when(i >= nu_ref[0])
    def _():
        y_ref[...] = jnp.zeros_like(y_ref)

    @pl.when((i == 0) | (be_ref[i] != be_ref[jnp.maximum(i - 1, 0)]))
    def _():
        wgu_bf[...] = wgu_ref[0].astype(BF16)
        wd_bf[...] = wd_ref[0].astype(BF16)

    @pl.when(i < nu_ref[0])
    def _():
        x = jnp.concatenate([xs_ref[pl.ds(a, MOE_BLOCK, stride=ROW_TILE), :] for a in range(ROW_TILE)], axis=1)
        gu = _dot(x.astype(BF16), wgu_bf[...]) + bgu_ref[0]
        gate = jnp.minimum(gu[:, :D_FF], SWIGLU_LIMIT)
        up = jnp.clip(gu[:, D_FF:], -SWIGLU_LIMIT, SWIGLU_LIMIT)
        act = gate * _sigmoid(SWIGLU_ALPHA * gate) * (up + 1.0)
        y = _dot(act.astype(BF16), wd_bf[...]) + bd_ref[0]
        for a in range(ROW_TILE):
            y_ref[pl.ds(a, MOE_BLOCK, stride=ROW_TILE), :] = y[:, a * LANES:(a + 1) * LANES]


def _experts(blk_expert, n_used, xs, wgu, bgu, wd, bd):
    n_blocks = xs.shape[0] // (MOE_BLOCK * ROW_TILE)
    blk = lambda i, be, nu: (jnp.minimum(i, nu[0] - 1), 0)
    exp3 = lambda i, be, nu: (be[jnp.minimum(i, nu[0] - 1)], 0, 0)
    return pl.pallas_call(
        _expert_kernel,
        grid_spec=pltpu.PrefetchScalarGridSpec(
            num_scalar_prefetch=2,
            grid=(n_blocks,),
            in_specs=[pl.BlockSpec((MOE_BLOCK * ROW_TILE, LANES), blk),
                      pl.BlockSpec((1, D_MODEL, 2 * D_FF), exp3),
                      pl.BlockSpec((1, 1, 2 * D_FF), exp3),
                      pl.BlockSpec((1, D_FF, D_MODEL), exp3),
                      pl.BlockSpec((1, 1, D_MODEL), exp3)],
            out_specs=pl.BlockSpec((MOE_BLOCK * ROW_TILE, LANES), lambda i, be, nu: (i, 0)),
            scratch_shapes=[pltpu.VMEM((D_MODEL, 2 * D_FF), BF16), pltpu.VMEM((D_FF, D_MODEL), BF16)]),
        out_shape=jax.ShapeDtypeStruct(xs.shape, F32),
        compiler_params=pltpu.CompilerParams(dimension_semantics=("arbitrary",),
                                             vmem_limit_bytes=VMEM_LIMIT),
        name="experts",
    )(blk_expert, n_used, xs, wgu, bgu, wd, bd)


def _combine_kernel(dest_ref, dest_next_ref, rw_ref, x1_ref, gfin_ref, y_ref, o_ref, ybuf0, ybuf1, sem0, sem1):
    tm = x1_ref.shape[0]
    i = pl.program_id(0)

    def gather(dst_ref, ybuf, sem):
        def issue(r, c):
            slots = [dst_ref[r * TOP_K + k] for k in range(TOP_K)]
            for k in range(TOP_K):
                _tile_copy(y_ref, slots[k], ybuf.at[k], r, sem).start(priority=k % 2)
            return c

        lax.fori_loop(0, tm, issue, 0)

    def finish(ybuf, sem):
        for k in range(TOP_K):
            pltpu.make_async_copy(y_ref.at[pl.ds(0, tm * ROW_TILE)], ybuf.at[k], sem).wait()
        rw = rw_ref[...]
        x1 = x1_ref[...]
        cols = []
        for a in range(ROW_TILE):
            acc = x1[:, a * LANES:(a + 1) * LANES]
            for k in range(TOP_K):
                acc = acc + rw[:, k:k + 1] * ybuf[k, pl.ds(a, tm, stride=ROW_TILE), :]
            cols.append(acc)
        o_ref[...] = _rms(jnp.concatenate(cols, axis=1), gfin_ref[...])

    @pl.when(i == 0)
    def _():
        gather(dest_ref, ybuf0, sem0)

    for parity, (cur, nxt) in enumerate((((ybuf0, sem0), (ybuf1, sem1)), ((ybuf1, sem1), (ybuf0, sem0)))):
        @pl.when(i % 2 == parity)
        def _():
            @pl.when(i + 1 < pl.num_programs(0))
            def _():
                gather(dest_next_ref, *nxt)

            finish(*cur)


def _combine(dest_flat, rw, x1, gfin, y, tm):
    n = x1.shape[0]
    steps = n // tm
    ybuf = pltpu.VMEM((TOP_K, tm * ROW_TILE, LANES), F32)
    return pl.pallas_call(
        _combine_kernel,
        grid=(steps,),
        in_specs=[pl.BlockSpec((tm * TOP_K,), lambda i: (i,), memory_space=pltpu.SMEM),
                  pl.BlockSpec((tm * TOP_K,), lambda i: (jnp.minimum(i + 1, steps - 1),), memory_space=pltpu.SMEM),
                  pl.BlockSpec((tm, LANES), lambda i: (i, 0)),
                  pl.BlockSpec((tm, D_MODEL), lambda i: (i, 0)),
                  pl.BlockSpec((1, D_MODEL), lambda i: (0, 0)),
                  pl.BlockSpec(memory_space=pl.ANY)],
        out_specs=pl.BlockSpec((tm, D_MODEL), lambda i: (i, 0)),
        out_shape=jax.ShapeDtypeStruct((n, D_MODEL), F32),
        scratch_shapes=[ybuf, ybuf, pltpu.SemaphoreType.DMA(()), pltpu.SemaphoreType.DMA(())],
        compiler_params=pltpu.CompilerParams(dimension_semantics=("arbitrary",),
                                             vmem_limit_bytes=VMEM_LIMIT),
        name="combine",
    )(dest_flat, dest_flat, rw, x1, gfin, y)


def _overlap_t(nc, n_cmp, n_sel):
    cs = np.arange(n_cmp)[None, :] * CMP_STRIDE
    ss = np.arange(n_sel)[:, None] * SEL_BLOCK
    ov = np.clip(np.minimum(cs + CMP_LEN, ss + SEL_BLOCK) - np.maximum(cs, ss), 0, None) / CMP_LEN
    out = np.zeros((LANES, nc), np.float32)
    out[:n_sel, :n_cmp] = ov
    return jnp.asarray(out, BF16)


def _layer(x2, b, s, g_mix, w_in, w_ck1, b_ck1, w_ck2, pe_k, w_cv1, b_cv1, w_cv2, pe_v,
           sgu_ln_g, sgu_ln_b, w_spatial, b_spatial, w_branch_a, w_branch_b, w_out,
           g_moe, w_router, b_router, w_gate_up, b_gate_up, w_down, b_down):
    n = b * s
    nc = s // CMP_STRIDE
    n_cmp = (s - CMP_LEN) // CMP_STRIDE + 1
    n_sel = s // SEL_BLOCK
    assert nc % KEY_CHUNK == 0 and n_sel <= LANES and n_cmp == nc - 1
    tm = 512
    assert s % tm == 0

    p0 = ATTN_WIDTH
    p1 = p0 + 6 * KV_WIDTH
    p2 = p1 + N_NSA_BRANCH * N_HEADS
    p3 = p2 + SGU_WIDTH
    p4 = p3 + SGU_WIDTH
    zpad = lambda a, w: jnp.pad(a, ((0, 0),) * (a.ndim - 1) + ((0, w - a.shape[-1]),))
    wq = zpad((w_in[:, :p0] * (HEAD_DIM ** -0.5 * LOG2E)).reshape(D_MODEL, N_HEADS, HEAD_DIM), LANES)
    wkv = w_in[:, p0:p1].reshape(D_MODEL, 6, N_KV, HEAD_DIM)
    wng = w_in[:, p1:p2].reshape(D_MODEL, N_KV, GQA_R, N_NSA_BRANCH).transpose(0, 1, 3, 2)
    wng = zpad(wng.reshape(D_MODEL, N_KV, N_NSA_BRANCH * GQA_R), GATE_ROWS)
    wt = jnp.concatenate([wq.reshape(D_MODEL, -1), wkv[:, 3].reshape(D_MODEL, -1),
                          wkv[:, 5].reshape(D_MODEL, -1), wng.reshape(D_MODEL, -1)], axis=1).T.astype(BF16)
    slopes = 2.0 ** (-8.0 * np.arange(1, N_HEADS + 1) / N_HEADS)
    tcol = np.zeros((T_ROWS, 1), np.float32)
    head_rows = T_Q + np.arange(N_HEADS) * LANES + HEAD_DIM
    bf16_round = lambda a: a.astype(BF16).astype(np.float32)
    for k, coef in enumerate((slopes * SEL_BLOCK * LOG2E, slopes * LOG2E)):
        hi = bf16_round(coef.astype(np.float32))
        tcol[head_rows + 2 * k, 0] = hi
        tcol[head_rows + 2 * k + 1, 0] = bf16_round(coef.astype(np.float32) - hi)
    tconst = jnp.asarray(np.broadcast_to(tcol, (T_ROWS, 2 * Q_BLOCK)))
    wk = jnp.concatenate([zpad(wkv[:, 2, g], 2 * LANES) for g in range(N_KV)]
                         + [zpad(wkv[:, 4, g], LANES) for g in range(N_KV)]
                         + [wkv[:, 0].reshape(D_MODEL, -1), wkv[:, 1].reshape(D_MODEL, -1)], axis=1).astype(BF16)
    pos = np.arange(s)
    kc_np = np.zeros((s, 2 * LANES), np.float32)
    kc_np[:, HEAD_DIM:HEAD_DIM + 2] = (pos // SEL_BLOCK)[:, None]
    kc_np[:, HEAD_DIM + 2:HEAD_DIM + 4] = (pos % SEL_BLOCK)[:, None]
    kc_np[pos, LANES + pos // SEL_BLOCK] = 1.0
    kconst = jnp.asarray(kc_np, BF16)
    wu = w_in[:, p2:p3].astype(BF16)
    wv = w_in[:, p3:p4].astype(BF16)
    wmg = w_in[:, p4:].astype(BF16)
    bsp = jnp.repeat(b_spatial.T, SGU_GROUP_DIM, axis=1)

    qt, gt, vst, vwt, ksa, kwa, kcx, vcx, ga, gbyb = _proj(
        x2, b, s, g_mix[None], wt, tconst, wk, kconst, wu, wv, wmg, sgu_ln_g[None], sgu_ln_b[None],
        w_spatial, bsp, w_branch_b.astype(BF16), tm)

    half = CMP_LEN // 2
    eye = jnp.eye(N_KV, dtype=F32)[None, None, :, None, :, None]
    bdiag = lambda w: (w.reshape(2, half, 1, HEAD_DIM, 1, CMP_HID) * eye).reshape(
        2, half, N_KV * HEAD_DIM, N_KV * CMP_HID)
    w1 = jnp.stack([bdiag(w_ck1), bdiag(w_cv1)]).astype(BF16)
    pes = jnp.stack([jnp.tile(pe_k.reshape(2, half, HEAD_DIM), (1, 1, N_KV)),
                     jnp.tile(pe_v.reshape(2, half, HEAD_DIM), (1, 1, N_KV))])
    b1 = jnp.stack([jnp.tile(b_ck1, N_KV), jnp.tile(b_cv1, N_KV)])[:, None, :]
    blk_n = np.arange(nc)
    cc_np = np.zeros((nc, LANES), np.float32)
    cc_np[:, HEAD_DIM:HEAD_DIM + 2] = (blk_n // (SEL_BLOCK // CMP_STRIDE))[:, None]
    cc_np[:, HEAD_DIM + 2:HEAD_DIM + 4] = (blk_n % (SEL_BLOCK // CMP_STRIDE) * CMP_STRIDE)[:, None]
    kc, vct = _compress(kcx, vcx, pes, w1, b1, zpad(w_ck2, LANES).astype(BF16), w_cv2.T.astype(BF16),
                        jnp.asarray(cc_np))

    ovt = _overlap_t(nc, n_cmp, n_sel).reshape(LANES, nc // KEY_CHUNK, KEY_CHUNK).transpose(1, 0, 2)
    blk = np.arange(LANES)
    pair = jnp.asarray(blk[:, None] // (KEY_CHUNK // SEL_BLOCK) == blk[None, :], BF16)
    a_io, q_io = np.meshgrid(np.arange(KEY_CHUNK), np.arange(Q_BLOCK), indexing="ij")
    tri = np.stack([np.where(a_io > q_io, 0.0, NEG), np.where(a_io <= q_io, 0.0, NEG)]).astype(np.float32)
    tri = jnp.asarray(np.tile(tri, (1, 1, 2)))
    ya = _nsa(qt, gt, kc, vct, ksa, vst, kwa, vwt, ovt, pair, tri, n_sel).reshape(n, ATTN_WIDTH)

    wr = jnp.pad(w_router, ((0, 0), (0, LANES - N_EXPERTS)))
    wrh = wr.astype(BF16)
    wr2 = jnp.concatenate([wrh, (wr - wrh.astype(F32)).astype(BF16)], axis=1)
    br = jnp.pad(b_router, (0, LANES - N_EXPERTS))[None]
    x1, hm, rw, ri, cnt = _merge(x2, ya, ga, gbyb, w_branch_a.astype(BF16), w_out.astype(BF16),
                                 g_moe[None], wr2, br, tm)

    counts = cnt[0, :N_EXPERTS].astype(I32)
    padded = (counts + MOE_BLOCK - 1) // MOE_BLOCK * MOE_BLOCK
    pad_end = jnp.cumsum(padded)
    pad_start = pad_end - padded
    dest = (pad_start[ri[:, :TOP_K]] + ri[:, TOP_K:2 * TOP_K]).reshape(-1)
    n_blocks = -(-(n * TOP_K) // MOE_BLOCK) + N_EXPERTS
    blk_start = jnp.arange(n_blocks, dtype=I32) * MOE_BLOCK
    blk_expert = jnp.minimum(jnp.sum((pad_end[None, :] <= blk_start[:, None]).astype(I32), axis=1),
                             N_EXPERTS - 1)
    n_used = (pad_end[-1:] // MOE_BLOCK).astype(I32)
    n_slots = n_blocks * MOE_BLOCK
    seg = jnp.stack([jnp.concatenate([pad_start + counts, pad_end[-1:]]),
                     jnp.concatenate([pad_end, jnp.full((1,), n_slots, I32)])]).astype(I32)

    tg = 256
    xs = _dispatch(seg, dest, hm, n_slots, tg)
    y = _experts(blk_expert, n_used, xs, w_gate_up, b_gate_up[:, None, :], w_down, b_down[:, None, :])
    return dest, rw, x1, y, tg


def kernel(x, g_mix, w_in, w_ck1, b_ck1, w_ck2, pe_k, w_cv1, b_cv1, w_cv2, pe_v, sgu_ln_g, sgu_ln_b, w_spatial, b_spatial, w_branch_a, w_branch_b, w_out, g_moe, w_router, b_router, w_gate_up, b_gate_up, w_down, b_down, g_final):
    b, s, d = x.shape
    assert g_mix.shape[0] == 1, "the final rmsnorm is fused into the single layer's combine step"
    l = 0
    dest, rw, x1, y, tg = _layer(
        x.reshape(b * s, d), b, s, g_mix[l], w_in[l], w_ck1[l], b_ck1[l], w_ck2[l], pe_k[l], w_cv1[l],
        b_cv1[l], w_cv2[l], pe_v[l], sgu_ln_g[l], sgu_ln_b[l], w_spatial[l], b_spatial[l], w_branch_a[l],
        w_branch_b[l], w_out[l], g_moe[l], w_router[l], b_router[l], w_gate_up[l], b_gate_up[l],
        w_down[l], b_down[l])
    return _combine(dest, rw, x1, g_final[None], y, tg).reshape(b, s, d)
```

```python
import functools

import numpy as np
import jax
import jax.numpy as jnp
from jax import lax
from jax.experimental import pallas as pl
from jax.experimental.pallas import tpu as pltpu

D_MODEL = 1024
N_HEADS = 8
HEAD_DIM = 64
N_KV = 2
GQA_R = N_HEADS // N_KV
ATTN_WIDTH = N_HEADS * HEAD_DIM
KV_WIDTH = N_KV * HEAD_DIM
CMP_LEN = 32
CMP_STRIDE = 16
CMP_HID = 128
SEL_BLOCK = 64
SEL_TOPN = 16
WINDOW = 512
Q_BLOCK = 128
N_NSA_BRANCH = 3
SGU_WIDTH = 512
N_GROUPS_SGU = 8
SGU_GROUP_DIM = SGU_WIDTH // N_GROUPS_SGU
CHUNK = 128
N_EXPERTS = 32
TOP_K = 4
D_FF = D_MODEL
SWIGLU_LIMIT = 7.0
SWIGLU_ALPHA = 1.702
MOE_BLOCK = 512
RMS_EPS = 1e-5
LN_EPS = 1e-5
NEG = -1e30
FORCE = 1e4

LANES = 128
ROW_TILE = D_MODEL // LANES
MASK_BIAS = 1e9
LOG2E = float(np.log2(np.e))
KEY_CHUNK = 128
SEL_SLOTS = 4
GATE_ROWS = 16
VMEM_LIMIT = 56 * 1024 * 1024

F32 = jnp.float32
BF16 = jnp.bfloat16
I32 = jnp.int32

_NT = (((1,), (1,)), ((), ()))


def _dot(a, b):
    return jnp.dot(a, b, preferred_element_type=F32)


def _dot_nt(a, b):
    return lax.dot_general(a, b, _NT, preferred_element_type=F32)


def _sigmoid(x):
    return 1.0 / (1.0 + jnp.exp(-x))


def _gelu(x):
    c = np.float32(np.sqrt(2.0 / np.pi))
    return 0.5 * x * (1.0 + jnp.tanh(c * (x + 0.044715 * (x * x * x))))


def _rms(x, g):
    return x * lax.rsqrt(jnp.mean(x * x, axis=-1, keepdims=True) + RMS_EPS) * g


T_Q = 0
T_VS = T_Q + N_HEADS * LANES
T_VW = T_VS + N_KV * HEAD_DIM
T_GATE = T_VW + N_KV * HEAD_DIM
T_ROWS = T_GATE + N_KV * GATE_ROWS
K_SEL = 0
K_WIN = K_SEL + N_KV * 2 * LANES
K_CMP = K_WIN + N_KV * LANES
K_COLS = K_CMP + 4 * HEAD_DIM


def _proj_kernel(x_ref, gmix_ref, wt_ref, tconst_ref, wk_ref, kconst_ref, wu_ref, wv_ref, wmg_ref,
                 lng_ref, lnb_ref, wsp_ref, bsp_ref, wbb_ref,
                 qt_ref, gt_ref, vst_ref, vwt_ref, ksa_ref, kwa_ref, kcx_ref, vcx_ref, ga_ref, gbyb_ref):
    tm = x_ref.shape[0]
    hb = _rms(x_ref[...], gmix_ref[...]).astype(BF16)

    for h in range(tm // (2 * Q_BLOCK)):
        t = _dot_nt(wt_ref[...], hb[h * 2 * Q_BLOCK:(h + 1) * 2 * Q_BLOCK]) + tconst_ref[...]
        for jj in range(2):
            j = 2 * h + jj
            ls = slice(jj * Q_BLOCK, (jj + 1) * Q_BLOCK)
            for g in range(N_KV):
                for r in range(GQA_R):
                    r0 = T_Q + (g * GQA_R + r) * LANES
                    qt_ref[0, g, j, :, r * Q_BLOCK:(r + 1) * Q_BLOCK] = t[r0:r0 + LANES, ls].astype(BF16)
                vst_ref[0, g, j] = t[T_VS + g * HEAD_DIM:T_VS + (g + 1) * HEAD_DIM, ls].astype(BF16)
                vwt_ref[0, g, j] = t[T_VW + g * HEAD_DIM:T_VW + (g + 1) * HEAD_DIM, ls].astype(BF16)
                gt_ref[0, g, j] = _sigmoid(t[T_GATE + g * GATE_ROWS:T_GATE + (g + 1) * GATE_ROWS, ls])

    kk = _dot(hb, wk_ref[...])
    kconst = kconst_ref[...].astype(F32)
    for g in range(N_KV):
        c0 = K_SEL + g * 2 * LANES
        ksa_ref[0, g] = (kk[:, c0:c0 + 2 * LANES] + kconst).astype(BF16)
        c0 = K_WIN + g * LANES
        kwa_ref[0, g] = (kk[:, c0:c0 + LANES] + kconst[:, :LANES]).astype(BF16)
    kcx_ref[0] = kk[:, K_CMP:K_CMP + LANES]
    vcx_ref[0] = kk[:, K_CMP + LANES:K_COLS]

    mg = _dot(hb, wmg_ref[...])
    ga_ref[...] = _sigmoid(mg[:, :D_MODEL]).astype(BF16)

    u = _gelu(_dot(hb, wu_ref[...]))
    v = _gelu(_dot(hb, wv_ref[...]))
    mu = jnp.mean(v, axis=-1, keepdims=True)
    vc = v - mu
    var = jnp.mean(vc * vc, axis=-1, keepdims=True)
    vln = (vc * lax.rsqrt(var + LN_EPS) * lng_ref[...] + lnb_ref[...]).astype(BF16)

    row = lax.broadcasted_iota(I32, (CHUNK, CHUNK), 0)
    col = lax.broadcasted_iota(I32, (CHUNK, CHUNK), 1)
    tril = row >= col
    wsp = [jnp.where(tril, wsp_ref[g], 0.0).astype(BF16) for g in range(N_GROUPS_SGU)]
    low_half = col < SGU_GROUP_DIM
    bsp = bsp_ref[...]
    chunks = []
    for c in range(tm // CHUNK):
        rs = slice(c * CHUNK, (c + 1) * CHUNK)
        parts = []
        for p in range(SGU_WIDTH // LANES):
            cs = slice(p * LANES, (p + 1) * LANES)
            vblk = vln[rs, cs]
            mixed = jnp.where(low_half, _dot(wsp[2 * p], vblk), _dot(wsp[2 * p + 1], vblk))
            parts.append(u[rs, cs] * (mixed + bsp[:, cs]))
        chunks.append(jnp.concatenate(parts, axis=1))
    yb = jnp.concatenate(chunks, axis=0).astype(BF16)
    gbyb_ref[...] = (_sigmoid(mg[:, D_MODEL:]) * _dot(yb, wbb_ref[...])).astype(BF16)


def _proj(x2, b, s, gmix, wt, tconst, wk, kconst, wu, wv, wmg, lng, lnb, wsp, bsp, wbb, tm):
    n = x2.shape[0]
    tiles = s // tm
    qtiles = tm // Q_BLOCK
    full = lambda a: pl.BlockSpec(a.shape, lambda i, j: (0,) * a.ndim)
    rows = lambda w: pl.BlockSpec((tm, w), lambda i, j: (i * tiles + j, 0))
    per_g = lambda *blk: pl.BlockSpec((1, N_KV) + blk, lambda i, j: (i, 0, j) + (0,) * (len(blk) - 1))
    ins = (x2, gmix, wt, tconst, wk, kconst, wu, wv, wmg, lng, lnb, wsp, bsp, wbb)
    in_specs = [rows(D_MODEL)] + [full(a) for a in ins[1:]]
    in_specs[5] = pl.BlockSpec((tm, 2 * LANES), lambda i, j: (j, 0))
    nt = s // Q_BLOCK
    return pl.pallas_call(
        _proj_kernel,
        grid=(b, tiles),
        in_specs=in_specs,
        out_specs=[per_g(qtiles, LANES, GQA_R * Q_BLOCK), per_g(qtiles, GATE_ROWS, Q_BLOCK),
                   per_g(qtiles, HEAD_DIM, Q_BLOCK), per_g(qtiles, HEAD_DIM, Q_BLOCK),
                   per_g(tm, 2 * LANES), per_g(tm, LANES),
                   pl.BlockSpec((1, tm, LANES), lambda i, j: (i, j, 0)),
                   pl.BlockSpec((1, tm, LANES), lambda i, j: (i, j, 0)),
                   rows(D_MODEL), rows(D_MODEL)],
        out_shape=[jax.ShapeDtypeStruct((b, N_KV, nt, LANES, GQA_R * Q_BLOCK), BF16),
                   jax.ShapeDtypeStruct((b, N_KV, nt, GATE_ROWS, Q_BLOCK), F32),
                   jax.ShapeDtypeStruct((b, N_KV, nt, HEAD_DIM, Q_BLOCK), BF16),
                   jax.ShapeDtypeStruct((b, N_KV, nt, HEAD_DIM, Q_BLOCK), BF16),
                   jax.ShapeDtypeStruct((b, N_KV, s, 2 * LANES), BF16),
                   jax.ShapeDtypeStruct((b, N_KV, s, LANES), BF16),
                   jax.ShapeDtypeStruct((b, s, LANES), F32),
                   jax.ShapeDtypeStruct((b, s, LANES), F32),
                   jax.ShapeDtypeStruct((n, D_MODEL), BF16),
                   jax.ShapeDtypeStruct((n, D_MODEL), BF16)],
        compiler_params=pltpu.CompilerParams(dimension_semantics=("parallel", "parallel"),
                                             vmem_limit_bytes=VMEM_LIMIT),
        name="proj",
    )(*ins)


def _compress_kernel(xk_ref, xv_ref, pe_ref, w1_ref, b1_ref, w2k_ref, w2vt_ref, kconst_ref, kc_ref, vct_ref):
    nc = kc_ref.shape[2]
    half = CMP_LEN // 2
    hids = []
    for which, x_ref in enumerate((xk_ref, xv_ref)):
        za = jnp.zeros((nc, N_KV * CMP_HID), F32)
        zb = jnp.zeros((nc, N_KV * CMP_HID), F32)
        for l in range(half):
            xl = x_ref[0, pl.ds(l, nc, stride=CMP_STRIDE), :]
            za = za + _dot((xl + pe_ref[which, 0, l:l + 1, :]).astype(BF16), w1_ref[which, 0, l])
            zb = zb + _dot((xl + pe_ref[which, 1, l:l + 1, :]).astype(BF16), w1_ref[which, 1, l])
        hid = _gelu(za + pltpu.roll(zb, nc - 1, 0) + b1_ref[which])
        row = lax.broadcasted_iota(I32, hid.shape, 0)
        hids.append(jnp.where(row < nc - 1, hid, 0.0).astype(BF16))
    for g in range(N_KV):
        cs = slice(g * CMP_HID, (g + 1) * CMP_HID)
        kc_ref[0, g] = (_dot(hids[0][:, cs], w2k_ref[...]) + kconst_ref[...]).astype(BF16)
        vt = _dot_nt(w2vt_ref[...], hids[1][:, cs])
        for c in range(nc // KEY_CHUNK):
            vct_ref[0, g, c] = vt[:, c * KEY_CHUNK:(c + 1) * KEY_CHUNK].astype(BF16)


def _compress(xk, xv, pes, w1, b1, w2k, w2vt, kconst):
    b, s, w = xk.shape
    nc = s // CMP_STRIDE
    full = lambda a: pl.BlockSpec(a.shape, lambda i: (0,) * a.ndim)
    seq = pl.BlockSpec((1, s, w), lambda i: (i, 0, 0))
    return pl.pallas_call(
        _compress_kernel,
        grid=(b,),
        in_specs=[seq, seq, full(pes), full(w1), full(b1), full(w2k), full(w2vt), full(kconst)],
        out_specs=[pl.BlockSpec((1, N_KV, nc, LANES), lambda i: (i, 0, 0, 0)),
                   pl.BlockSpec((1, N_KV, nc // KEY_CHUNK, HEAD_DIM, KEY_CHUNK), lambda i: (i, 0, 0, 0, 0))],
        out_shape=[jax.ShapeDtypeStruct((b, N_KV, nc, LANES), BF16),
                   jax.ShapeDtypeStruct((b, N_KV, nc // KEY_CHUNK, HEAD_DIM, KEY_CHUNK), BF16)],
        compiler_params=pltpu.CompilerParams(dimension_semantics=("parallel",),
                                             vmem_limit_bytes=VMEM_LIMIT),
        name="compress",
    )(xk, xv, pes, w1, b1, w2k, w2vt, kconst)


def _pairs():
    return [slice(pr * 2 * Q_BLOCK, (pr + 1) * 2 * Q_BLOCK) for pr in range(GQA_R // 2)]


def _colmax(blocks):
    part = None
    for x in blocks:
        y = jnp.max(x.reshape(-1, 8, x.shape[-1]), axis=0)
        part = y if part is None else jnp.maximum(part, y)
    return jnp.max(part, axis=0, keepdims=True)


def _colsum(blocks):
    part = None
    for x in blocks:
        y = jnp.sum(x.reshape(-1, 8, x.shape[-1]), axis=0)
        part = y if part is None else part + y
    return jnp.sum(part, axis=0, keepdims=True)


def _softmax_pv(scores, vts, masks, m_ref, l_ref, acc_ref):
    for pr, ls in enumerate(_pairs()):
        sb = [x if masks[u] is None else masks[u](x, ls) for u, x in enumerate(scores[pr])]
        m_old = m_ref[0:1, ls]
        m_new = jnp.maximum(m_old, _colmax(sb))
        alpha = jnp.exp2(m_old - m_new)
        ps = [jnp.exp2(x - m_new) for x in sb]
        pv = _dot(vts[0], ps[0].astype(BF16))
        for u in range(1, len(ps)):
            pv = pv + _dot(vts[u], ps[u].astype(BF16))
        l_ref[0:1, ls] = alpha * l_ref[0:1, ls] + _colsum(ps)
        acc_ref[:, ls] = alpha * acc_ref[:, ls] + pv
        m_ref[0:1, ls] = m_new


def _nsa_kernel(qt_ref, gt_ref, kc_ref, vct_ref, ksa_ref, vst_ref, kwa_ref, vwt_ref, ovt_ref, pair_ref, tri_ref,
                o_ref, qa_ref, s0_ref, s1_ref, m_ref, l_ref, acc_ref, act_ref, *, n_sel, top_n):
    i = pl.program_id(1)
    t0 = i * Q_BLOCK
    width = GQA_R * Q_BLOCK
    n_chunks = ksa_ref.shape[2] // KEY_CHUNK
    n_cchunks = kc_ref.shape[2] // KEY_CHUNK
    n_slots = act_ref.shape[0] // N_KV
    lane = lax.broadcasted_iota(I32, (1, width), 1)
    tq = t0 + (lane & (Q_BLOCK - 1))
    sub = lax.broadcasted_iota(I32, (KEY_CHUNK, 1), 0)
    pairs = _pairs()
    heads = range(N_KV)

    def reset(g):
        m_ref[g] = jnp.full(m_ref.shape[1:], NEG, F32)
        l_ref[g] = jnp.zeros(l_ref.shape[1:], F32)
        acc_ref[g] = jnp.zeros(acc_ref.shape[1:], F32)

    sc = [[[_dot(kc_ref[0, g, u * KEY_CHUNK:(u + 1) * KEY_CHUNK, :], qt_ref[0, g, 0, :, ls])
            for u in range(n_cchunks)] for ls in pairs] for g in heads]
    o_c, imp_t = [], []
    for g in heads:
        ocg, ps = [], []
        for pr, ls in enumerate(pairs):
            sb = []
            for u in range(n_cchunks):
                c_end = (u * KEY_CHUNK + sub) * CMP_STRIDE + (CMP_LEN - 1)
                sb.append(jnp.where(c_end <= tq[:, ls], sc[g][pr][u], NEG))
            m = _colmax(sb)
            pb = [jnp.exp2(x - m) for x in sb]
            l = _colsum(pb)
            w = (tq[:, ls] >= CMP_LEN - 1).astype(F32) / l
            oc = _dot(vct_ref[0, g, 0], pb[0].astype(BF16))
            for u in range(1, n_cchunks):
                oc = oc + _dot(vct_ref[0, g, u], pb[u].astype(BF16))
            ocg.append(oc * w)
            ps.append([p[:, :Q_BLOCK] * w[:, :Q_BLOCK] + p[:, Q_BLOCK:] * w[:, Q_BLOCK:] for p in pb])
        imp2 = jnp.zeros((LANES, 2 * Q_BLOCK), F32)
        for u in range(n_cchunks):
            psu = ps[0][u] + ps[1][u]
            hi = psu.astype(BF16)
            lo = (psu - hi.astype(F32)).astype(BF16)
            imp2 = imp2 + _dot(ovt_ref[u], jnp.concatenate([hi, lo], axis=1))
        o_c.append(jnp.concatenate(ocg, axis=1))
        imp_t.append(imp2[:, :Q_BLOCK] + imp2[:, Q_BLOCK:])

    n_win = WINDOW // KEY_CHUNK
    cls, masks = [], []
    for u in range(n_win + 1):
        c = i - n_win + u
        cls.append(jnp.maximum(c, 0))
        off = jnp.where(c < 0, NEG, 0.0)
        if u == 0:
            edge = tri_ref[0] + off
            masks.append(lambda x, ls, edge=edge: x + edge)
        elif u == n_win:
            masks.append(lambda x, ls: x + tri_ref[1])
        else:
            masks.append(lambda x, ls, off=off: x + off)
    wsc = [[[_dot(kwa_ref[0, g, pl.ds(pl.multiple_of(cl * KEY_CHUNK, KEY_CHUNK), KEY_CHUNK), :],
                  qt_ref[0, g, 0, :, ls]) for cl in cls] for ls in pairs] for g in heads]
    o_w = []
    for g in heads:
        reset(g)
        _softmax_pv(wsc[g], [vwt_ref[0, g, cl] for cl in cls], masks, m_ref.at[g], l_ref.at[g], acc_ref.at[g])
        o_w.append(acc_ref[g] / l_ref[g, 0:1, :])

    j_io = lax.broadcasted_iota(I32, (LANES, Q_BLOCK), 0)
    tl = t0 + lax.broadcasted_iota(I32, (LANES, Q_BLOCK), 1)
    cur = tl >> 6
    forced = (j_io == 0) | (j_io == cur) | (j_io == cur - 1)
    prio = [jnp.where(j_io < n_sel, jnp.where(forced, FORCE, jnp.where(j_io * SEL_BLOCK <= tl, imp_t[g], -1.0)), NEG)
            for g in heads]
    for _ in range(top_n):
        for g in heads:
            m = jnp.max(prio[g], axis=0, keepdims=True)
            idx = jnp.min(jnp.where(prio[g] == m, j_io, LANES), axis=0, keepdims=True)
            prio[g] = jnp.where(j_io == idx, NEG, prio[g])
    sel_t = [jnp.where((prio[g] == NEG) & (j_io < n_sel), 1.0, 0.0) for g in heads]

    n_act = []
    for g in heads:
        qa_ref[g, 0:LANES, :] = qt_ref[0, g, 0]
        bias_t = ((sel_t[g] - 1.0) * MASK_BIAS).astype(BF16)
        for h in range(GQA_R):
            qa_ref[g, LANES:2 * LANES, h * Q_BLOCK:(h + 1) * Q_BLOCK] = bias_t
        per_block = _dot_nt(jnp.ones((8, Q_BLOCK), BF16), sel_t[g].astype(BF16))
        per_chunk = _dot(per_block.astype(BF16), pair_ref[...])
        na = jnp.int32(0)
        for c in range(n_chunks):
            act_ref[g * n_slots + na] = jnp.int32(c)
            na = na + jnp.where((per_chunk[0, c] > 0.0) & (c < i), 1, 0)
        act_ref[g * n_slots + na] = i
        for u in range(1, 2 * SEL_SLOTS):
            act_ref[g * n_slots + na + u] = jnp.int32(-1)
        n_act.append(na)

    def sel_scores(g, k, dst):
        for u in range(SEL_SLOTS):
            cl = jnp.maximum(act_ref[g * n_slots + k * SEL_SLOTS + u], 0)
            keys = ksa_ref[0, g, pl.ds(pl.multiple_of(cl * KEY_CHUNK, KEY_CHUNK), KEY_CHUNK), :]
            for ls in pairs:
                dst[g, u, :, ls] = _dot(keys, qa_ref[g, :, ls])

    def sel_step(g, k, cur_s, nxt_s, last):
        sel_scores(g, k + 1, nxt_s)
        vts, msk = [], []
        for u in range(SEL_SLOTS):
            c = act_ref[g * n_slots + k * SEL_SLOTS + u]
            vts.append(vst_ref[0, g, jnp.maximum(c, 0)])
            if last:
                kpos = jnp.where(c < 0, n_chunks * KEY_CHUNK, c * KEY_CHUNK) + sub
                msk.append(lambda x, ls, kpos=kpos: jnp.where(kpos <= tq[:, ls], x, -MASK_BIAS))
            else:
                msk.append(None)
        scores = [[cur_s[g, u, :, ls] for u in range(SEL_SLOTS)] for ls in pairs]
        _softmax_pv(scores, vts, msk, m_ref.at[g], l_ref.at[g], acc_ref.at[g])

    o_s = []
    for g in heads:
        reset(g)
        sel_scores(g, 0, s0_ref)
    for g in heads:
        n_steps = (n_act[g] + SEL_SLOTS) // SEL_SLOTS

        def sel_body(k, carry, g=g, n_steps=n_steps):
            for odd, (cur_s, nxt_s) in enumerate(((s0_ref, s1_ref), (s1_ref, s0_ref))):
                for last in (False, True):
                    @pl.when((k % 2 == odd) & ((k == n_steps - 1) == last))
                    def _():
                        sel_step(g, k, cur_s, nxt_s, last)

            return carry

        lax.fori_loop(0, n_steps, sel_body, 0)
        o_s.append(acc_ref[g] / l_ref[g, 0:1, :])

    for g in heads:
        gt = gt_ref[0, g, 0]

        def gate(br):
            return jnp.concatenate([gt[br * GQA_R + h:br * GQA_R + h + 1, :] for h in range(GQA_R)], axis=1)

        o = gate(0) * o_c[g] + gate(1) * o_s[g] + gate(2) * o_w[g]
        o_ref[0, :, g * GQA_R * HEAD_DIM:(g + 1) * GQA_R * HEAD_DIM] = jnp.concatenate(
            [o[:, h * Q_BLOCK:(h + 1) * Q_BLOCK].T for h in range(GQA_R)], axis=1).astype(BF16)


def _nsa(qt, gt, kc, vct, ksa, vst, kwa, vwt, ovt, pair, tri, n_sel):
    b, g, nt, _, width = qt.shape
    s = ksa.shape[2]
    tile = lambda a: pl.BlockSpec((1, g, 1) + a.shape[3:], lambda i, k: (i, 0, k, 0, 0))
    whole = lambda a: pl.BlockSpec((1,) + a.shape[1:], lambda i, k: (i,) + (0,) * (a.ndim - 1))
    const = lambda a: pl.BlockSpec(a.shape, lambda i, k: (0,) * a.ndim)
    kern = functools.partial(_nsa_kernel, n_sel=n_sel, top_n=min(SEL_TOPN, n_sel))
    return pl.pallas_call(
        kern,
        grid=(b, nt),
        in_specs=[tile(qt), tile(gt), whole(kc), whole(vct), whole(ksa), whole(vst), whole(kwa), whole(vwt),
                  const(ovt), const(pair), const(tri)],
        out_specs=pl.BlockSpec((1, Q_BLOCK, ATTN_WIDTH), lambda i, k: (i, k, 0)),
        out_shape=jax.ShapeDtypeStruct((b, s, ATTN_WIDTH), BF16),
        scratch_shapes=[pltpu.VMEM((g, 2 * LANES, width), BF16),
                        pltpu.VMEM((g, SEL_SLOTS, KEY_CHUNK, width), F32),
                        pltpu.VMEM((g, SEL_SLOTS, KEY_CHUNK, width), F32),
                        pltpu.VMEM((g, 8, width), F32),
                        pltpu.VMEM((g, 8, width), F32),
                        pltpu.VMEM((g, HEAD_DIM, width), F32),
                        pltpu.SMEM((g * (s // KEY_CHUNK + 2 * SEL_SLOTS),), I32)],
        compiler_params=pltpu.CompilerParams(dimension_semantics=("parallel", "arbitrary"),
                                             vmem_limit_bytes=VMEM_LIMIT),
        name="nsa",
    )(qt, gt, kc, vct, ksa, vst, kwa, vwt, ovt, pair, tri)


def _merge_kernel(x_ref, ya_ref, ga_ref, gbyb_ref, wba_ref, wout_ref, gmoe_ref, wr_ref, br_ref, ltri_ref,
                  x1_ref, hm_ref, rw_ref, ri_ref, cnt_ref, carry_ref):
    tm = x_ref.shape[0]

    @pl.when(pl.program_id(0) == 0)
    def _():
        carry_ref[...] = jnp.zeros_like(carry_ref)

    merged = ga_ref[...].astype(F32) * _dot(ya_ref[...], wba_ref[...]) + gbyb_ref[...].astype(F32)
    x1 = x_ref[...] + _dot(merged.astype(BF16), wout_ref[...])
    x1_ref[...] = x1
    hm = _rms(x1, gmoe_ref[...])
    for a in range(ROW_TILE):
        hm_ref[pl.ds(a, tm, stride=ROW_TILE), :] = hm[:, a * LANES:(a + 1) * LANES]

    hh = hm.astype(BF16)
    hl = (hm - hh.astype(F32)).astype(BF16)
    both = _dot(hh, wr_ref[...])
    logits = both[:, :LANES] + both[:, LANES:] + _dot(hl, wr_ref[:, :LANES]) + br_ref[...]
    lane = lax.broadcasted_iota(I32, (tm, LANES), 1)
    lg = jnp.where(lane < N_EXPERTS, logits, NEG)
    vals, idxs = [], []
    for _ in range(TOP_K):
        m = jnp.max(lg, axis=1, keepdims=True)
        idx = jnp.min(jnp.where(lg == m, lane, LANES), axis=1, keepdims=True)
        vals.append(m)
        idxs.append(idx)
        lg = jnp.where(lane == idx, NEG, lg)
    ex = [jnp.exp(v - vals[0]) for v in vals]
    den = ex[0] + ex[1] + ex[2] + ex[3]

    hits = [lane == idx for idx in idxs]
    multi = jnp.zeros((tm, LANES), F32)
    for h in hits:
        multi = jnp.where(h, 1.0, multi)
    carry = carry_ref[0:1, :]
    cum = _dot(ltri_ref[...], multi.astype(BF16)) + carry
    rw = jnp.zeros((tm, LANES), F32)
    ri = jnp.zeros((tm, LANES), I32)
    for k in range(TOP_K):
        rank = jnp.sum(jnp.where(hits[k], cum, 0.0), axis=1, keepdims=True).astype(I32)
        rw = jnp.where(lane == k, ex[k] / den, rw)
        ri = jnp.where(lane == k, idxs[k], jnp.where(lane == TOP_K + k, rank, ri))
    rw_ref[...] = rw
    ri_ref[...] = ri
    new_carry = carry + jnp.sum(multi, axis=0, keepdims=True)
    carry_ref[...] = jnp.broadcast_to(new_carry, carry_ref.shape)
    cnt_ref[...] = jnp.broadcast_to(new_carry, cnt_ref.shape)


def _merge(x2, ya, ga, gbyb, wba, wout, gmoe, wr, br, tm):
    n = x2.shape[0]
    full = lambda a: pl.BlockSpec(a.shape, lambda i: (0,) * a.ndim)
    rows = lambda w: pl.BlockSpec((tm, w), lambda i: (i, 0))
    ltri = jnp.asarray(np.tril(np.ones((tm, tm), np.float32), -1), BF16)
    return pl.pallas_call(
        _merge_kernel,
        grid=(n // tm,),
        in_specs=[rows(D_MODEL), rows(ATTN_WIDTH), rows(D_MODEL), rows(D_MODEL),
                  full(wba), full(wout), full(gmoe), full(wr), full(br), full(ltri)],
        out_specs=[rows(D_MODEL), pl.BlockSpec((tm * ROW_TILE, LANES), lambda i: (i, 0)), rows(LANES), rows(LANES),
                   pl.BlockSpec((8, LANES), lambda i: (0, 0))],
        out_shape=[jax.ShapeDtypeStruct((n, D_MODEL), F32),
                   jax.ShapeDtypeStruct((n * ROW_TILE, LANES), F32),
                   jax.ShapeDtypeStruct((n, LANES), F32),
                   jax.ShapeDtypeStruct((n, LANES), I32),
                   jax.ShapeDtypeStruct((8, LANES), F32)],
        scratch_shapes=[pltpu.VMEM((8, LANES), F32)],
        compiler_params=pltpu.CompilerParams(dimension_semantics=("arbitrary",),
                                             vmem_limit_bytes=VMEM_LIMIT),
        name="merge",
    )(x2, ya, ga, gbyb, wba, wout, gmoe, wr, br, ltri)


def _tile_copy(src, i, dst, d, sem):
    return pltpu.make_async_copy(src.at[pl.ds(pl.multiple_of(i * ROW_TILE, ROW_TILE), ROW_TILE)],
                                 dst.at[pl.ds(pl.multiple_of(d * ROW_TILE, ROW_TILE), ROW_TILE)], sem)


def _dispatch_kernel(seg_ref, dest_ref, hm_ref, xs_ref, zero_ref, sem, zsem, *, n_pad):
    tm = hm_ref.shape[0] // ROW_TILE

    @pl.when(pl.program_id(0) == 0)
    def _():
        zero_ref[...] = jnp.zeros_like(zero_ref)

        def seg(e, c):
            def fill(r, c2):
                _tile_copy(zero_ref, 0, xs_ref, r, zsem).start()
                return c2
            return lax.fori_loop(seg_ref[0, e], seg_ref[1, e], fill, c)

        lax.fori_loop(0, N_EXPERTS + 1, seg, 0)
        pad_rows = xs_ref.at[pl.ds(0, n_pad * ROW_TILE)]
        pltpu.make_async_copy(pad_rows, pad_rows, zsem).wait()

    def issue(r, c):
        slots = [dest_ref[r * TOP_K + k] for k in range(TOP_K)]
        for k in range(TOP_K):
            _tile_copy(hm_ref, r, xs_ref, slots[k], sem).start(priority=k % 2)
        return c

    lax.fori_loop(0, tm, issue, 0)
    for k in range(TOP_K):
        pltpu.make_async_copy(hm_ref, xs_ref.at[pl.ds(0, tm * ROW_TILE)], sem).wait()


def _dispatch(seg, dest_flat, hm, n_slots, tm):
    n = hm.shape[0] // ROW_TILE
    kern = functools.partial(_dispatch_kernel, n_pad=n_slots - n * TOP_K)
    return pl.pallas_call(
        kern,
        grid_spec=pltpu.PrefetchScalarGridSpec(
            num_scalar_prefetch=1,
            grid=(n // tm,),
            in_specs=[pl.BlockSpec((tm * TOP_K,), lambda i, sg: (i,), memory_space=pltpu.SMEM),
                      pl.BlockSpec((tm * ROW_TILE, LANES), lambda i, sg: (i, 0))],
            out_specs=pl.BlockSpec(memory_space=pl.ANY),
            scratch_shapes=[pltpu.VMEM((ROW_TILE, LANES), F32),
                            pltpu.SemaphoreType.DMA(()), pltpu.SemaphoreType.DMA(())]),
        out_shape=jax.ShapeDtypeStruct((n_slots * ROW_TILE, LANES), F32),
        compiler_params=pltpu.CompilerParams(dimension_semantics=("arbitrary",),
                                             has_side_effects=True),
        name="dispatch",
    )(seg, dest_flat, hm)


def _expert_kernel(be_ref, nu_ref, xs_ref, wgu_ref, bgu_ref, wd_ref, bd_ref, y_ref, wgu_bf, wd_bf):
    i = pl.program_id(0)

    @pl.when(i >= nu_ref[0])
    def _():
        y_ref[...] = jnp.zeros_like(y_ref)

    @pl.when((i == 0) | (be_ref[i] != be_ref[jnp.maximum(i - 1, 0)]))
    def _():
        wgu_bf[...] = wgu_ref[0].astype(BF16)
        wd_bf[...] = wd_ref[0].astype(BF16)

    @pl.when(i < nu_ref[0])
    def _():
        x = jnp.concatenate([xs_ref[pl.ds(a, MOE_BLOCK, stride=ROW_TILE), :] for a in range(ROW_TILE)], axis=1)
        gu = _dot(x.astype(BF16), wgu_bf[...]) + bgu_ref[0]
        gate = jnp.minimum(gu[:, :D_FF], SWIGLU_LIMIT)
        up = jnp.clip(gu[:, D_FF:], -SWIGLU_LIMIT, SWIGLU_LIMIT)
        act = gate * _sigmoid(SWIGLU_ALPHA * gate) * (up + 1.0)
        y = _dot(act.astype(BF16), wd_bf[...]) + bd_ref[0]
        for a in range(ROW_TILE):
            y_ref[pl.ds(a, MOE_BLOCK, stride=ROW_TILE), :] = y[:, a * LANES:(a + 1) * LANES]


def _experts(blk_expert, n_used, xs, wgu, bgu, wd, bd):
    n_blocks = xs.shape[0] // (MOE_BLOCK * ROW_TILE)
    blk = lambda i, be, nu: (jnp.minimum(i, nu[0] - 1), 0)
    exp3 = lambda i, be, nu: (be[jnp.minimum(i, nu[0] - 1)], 0, 0)
    return pl.pallas_call(
        _expert_kernel,
        grid_spec=pltpu.PrefetchScalarGridSpec(
            num_scalar_prefetch=2,
            grid=(n_blocks,),
            in_specs=[pl.BlockSpec((MOE_BLOCK * ROW_TILE, LANES), blk),
                      pl.BlockSpec((1, D_MODEL, 2 * D_FF), exp3),
                      pl.BlockSpec((1, 1, 2 * D_FF), exp3),
                      pl.BlockSpec((1, D_FF, D_MODEL), exp3),
                      pl.BlockSpec((1, 1, D_MODEL), exp3)],
            out_specs=pl.BlockSpec((MOE_BLOCK * ROW_TILE, LANES), lambda i, be, nu: (i, 0)),
            scratch_shapes=[pltpu.VMEM((D_MODEL, 2 * D_FF), BF16), pltpu.VMEM((D_FF, D_MODEL), BF16)]),
        out_shape=jax.ShapeDtypeStruct(xs.shape, F32),
        compiler_params=pltpu.CompilerParams(dimension_semantics=("arbitrary",),
                                             vmem_limit_bytes=VMEM_LIMIT),
        name="experts",
    )(blk_expert, n_used, xs, wgu, bgu, wd, bd)


def _combine_kernel(dest_ref, dest_next_ref, rw_ref, x1_ref, gfin_ref, y_ref, o_ref, ybuf0, ybuf1, sem0, sem1):
    tm = x1_ref.shape[0]
    i = pl.program_id(0)

    def gather(dst_ref, ybuf, sem):
        def issue(r, c):
            slots = [dst_ref[r * TOP_K + k] for k in range(TOP_K)]
            for k in range(TOP_K):
                _tile_copy(y_ref, slots[k], ybuf.at[k], r, sem).start(priority=k % 2)
            return c

        lax.fori_loop(0, tm, issue, 0)

    def finish(ybuf, sem):
        for k in range(TOP_K):
            pltpu.make_async_copy(y_ref.at[pl.ds(0, tm * ROW_TILE)], ybuf.at[k], sem).wait()
        rw = rw_ref[...]
        x1 = x1_ref[...]
        cols = []
        for a in range(ROW_TILE):
            acc = x1[:, a * LANES:(a + 1) * LANES]
            for k in range(TOP_K):
                acc = acc + rw[:, k:k + 1] * ybuf[k, pl.ds(a, tm, stride=ROW_TILE), :]
            cols.append(acc)
        o_ref[...] = _rms(jnp.concatenate(cols, axis=1), gfin_ref[...])

    @pl.when(i == 0)
    def _():
        gather(dest_ref, ybuf0, sem0)

    for parity, (cur, nxt) in enumerate((((ybuf0, sem0), (ybuf1, sem1)), ((ybuf1, sem1), (ybuf0, sem0)))):
        @pl.when(i % 2 == parity)
        def _():
            @pl.when(i + 1 < pl.num_programs(0))
            def _():
                gather(dest_next_ref, *nxt)

            finish(*cur)


def _combine(dest_flat, rw, x1, gfin, y, tm):
    n = x1.shape[0]
    steps = n // tm
    ybuf = pltpu.VMEM((TOP_K, tm * ROW_TILE, LANES), F32)
    return pl.pallas_call(
        _combine_kernel,
        grid=(steps,),
        in_specs=[pl.BlockSpec((tm * TOP_K,), lambda i: (i,), memory_space=pltpu.SMEM),
                  pl.BlockSpec((tm * TOP_K,), lambda i: (jnp.minimum(i + 1, steps - 1),), memory_space=pltpu.SMEM),
                  pl.BlockSpec((tm, LANES), lambda i: (i, 0)),
                  pl.BlockSpec((tm, D_MODEL), lambda i: (i, 0)),
                  pl.BlockSpec((1, D_MODEL), lambda i: (0, 0)),
                  pl.BlockSpec(memory_space=pl.ANY)],
        out_specs=pl.BlockSpec((tm, D_MODEL), lambda i: (i, 0)),
        out_shape=jax.ShapeDtypeStruct((n, D_MODEL), F32),
        scratch_shapes=[ybuf, ybuf, pltpu.SemaphoreType.DMA(()), pltpu.SemaphoreType.DMA(())],
        compiler_params=pltpu.CompilerParams(dimension_semantics=("arbitrary",),
                                             vmem_limit_bytes=VMEM_LIMIT),
        name="combine",
    )(dest_flat, dest_flat, rw, x1, gfin, y)


def _overlap_t(nc, n_cmp, n_sel):
    cs = np.arange(n_cmp)[None, :] * CMP_STRIDE
    ss = np.arange(n_sel)[:, None] * SEL_BLOCK
    ov = np.clip(np.minimum(cs + CMP_LEN, ss + SEL_BLOCK) - np.maximum(cs, ss), 0, None) / CMP_LEN
    out = np.zeros((LANES, nc), np.float32)
    out[:n_sel, :n_cmp] = ov
    return jnp.asarray(out, BF16)


def _layer(x2, b, s, g_mix, w_in, w_ck1, b_ck1, w_ck2, pe_k, w_cv1, b_cv1, w_cv2, pe_v,
           sgu_ln_g, sgu_ln_b, w_spatial, b_spatial, w_branch_a, w_branch_b, w_out,
           g_moe, w_router, b_router, w_gate_up, b_gate_up, w_down, b_down):
    n = b * s
    nc = s // CMP_STRIDE
    n_cmp = (s - CMP_LEN) // CMP_STRIDE + 1
    n_sel = s // SEL_BLOCK
    assert nc % KEY_CHUNK == 0 and n_sel <= LANES and n_cmp == nc - 1
    tm = 512
    assert s % tm == 0

    p0 = ATTN_WIDTH
    p1 = p0 + 6 * KV_WIDTH
    p2 = p1 + N_NSA_BRANCH * N_HEADS
    p3 = p2 + SGU_WIDTH
    p4 = p3 + SGU_WIDTH
    zpad = lambda a, w: jnp.pad(a, ((0, 0),) * (a.ndim - 1) + ((0, w - a.shape[-1]),))
    wq = zpad((w_in[:, :p0] * (HEAD_DIM ** -0.5 * LOG2E)).reshape(D_MODEL, N_HEADS, HEAD_DIM), LANES)
    wkv = w_in[:, p0:p1].reshape(D_MODEL, 6, N_KV, HEAD_DIM)
    wng = w_in[:, p1:p2].reshape(D_MODEL, N_KV, GQA_R, N_NSA_BRANCH).transpose(0, 1, 3, 2)
    wng = zpad(wng.reshape(D_MODEL, N_KV, N_NSA_BRANCH * GQA_R), GATE_ROWS)
    wt = jnp.concatenate([wq.reshape(D_MODEL, -1), wkv[:, 3].reshape(D_MODEL, -1),
                          wkv[:, 5].reshape(D_MODEL, -1), wng.reshape(D_MODEL, -1)], axis=1).T.astype(BF16)
    slopes = 2.0 ** (-8.0 * np.arange(1, N_HEADS + 1) / N_HEADS)
    tcol = np.zeros((T_ROWS, 1), np.float32)
    head_rows = T_Q + np.arange(N_HEADS) * LANES + HEAD_DIM
    bf16_round = lambda a: a.astype(BF16).astype(np.float32)
    for k, coef in enumerate((slopes * SEL_BLOCK * LOG2E, slopes * LOG2E)):
        hi = bf16_round(coef.astype(np.float32))
        tcol[head_rows + 2 * k, 0] = hi
        tcol[head_rows + 2 * k + 1, 0] = bf16_round(coef.astype(np.float32) - hi)
    tconst = jnp.asarray(np.broadcast_to(tcol, (T_ROWS, 2 * Q_BLOCK)))
    wk = jnp.concatenate([zpad(wkv[:, 2, g], 2 * LANES) for g in range(N_KV)]
                         + [zpad(wkv[:, 4, g], LANES) for g in range(N_KV)]
                         + [wkv[:, 0].reshape(D_MODEL, -1), wkv[:, 1].reshape(D_MODEL, -1)], axis=1).astype(BF16)
    pos = np.arange(s)
    kc_np = np.zeros((s, 2 * LANES), np.float32)
    kc_np[:, HEAD_DIM:HEAD_DIM + 2] = (pos // SEL_BLOCK)[:, None]
    kc_np[:, HEAD_DIM + 2:HEAD_DIM + 4] = (pos % SEL_BLOCK)[:, None]
    kc_np[pos, LANES + pos // SEL_BLOCK] = 1.0
    kconst = jnp.asarray(kc_np, BF16)
    wu = w_in[:, p2:p3].astype(BF16)
    wv = w_in[:, p3:p4].astype(BF16)
    wmg = w_in[:, p4:].astype(BF16)
    bsp = jnp.repeat(b_spatial.T, SGU_GROUP_DIM, axis=1)

    qt, gt, vst, vwt, ksa, kwa, kcx, vcx, ga, gbyb = _proj(
        x2, b, s, g_mix[None], wt, tconst, wk, kconst, wu, wv, wmg, sgu_ln_g[None], sgu_ln_b[None],
        w_spatial, bsp, w_branch_b.astype(BF16), tm)

    half = CMP_LEN // 2
    eye = jnp.eye(N_KV, dtype=F32)[None, None, :, None, :, None]
    bdiag = lambda w: (w.reshape(2, half, 1, HEAD_DIM, 1, CMP_HID) * eye).reshape(
        2, half, N_KV * HEAD_DIM, N_KV * CMP_HID)
    w1 = jnp.stack([bdiag(w_ck1), bdiag(w_cv1)]).astype(BF16)
    pes = jnp.stack([jnp.tile(pe_k.reshape(2, half, HEAD_DIM), (1, 1, N_KV)),
                     jnp.tile(pe_v.reshape(2, half, HEAD_DIM), (1, 1, N_KV))])
    b1 = jnp.stack([jnp.tile(b_ck1, N_KV), jnp.tile(b_cv1, N_KV)])[:, None, :]
    blk_n = np.arange(nc)
    cc_np = np.zeros((nc, LANES), np.float32)
    cc_np[:, HEAD_DIM:HEAD_DIM + 2] = (blk_n // (SEL_BLOCK // CMP_STRIDE))[:, None]
    cc_np[:, HEAD_DIM + 2:HEAD_DIM + 4] = (blk_n % (SEL_BLOCK // CMP_STRIDE) * CMP_STRIDE)[:, None]
    kc, vct = _compress(kcx, vcx, pes, w1, b1, zpad(w_ck2, LANES).astype(BF16), w_cv2.T.astype(BF16),
                        jnp.asarray(cc_np))

    ovt = _overlap_t(nc, n_cmp, n_sel).reshape(LANES, nc // KEY_CHUNK, KEY_CHUNK).transpose(1, 0, 2)
    blk = np.arange(LANES)
    pair = jnp.asarray(blk[:, None] // (KEY_CHUNK // SEL_BLOCK) == blk[None, :], BF16)
    a_io, q_io = np.meshgrid(np.arange(KEY_CHUNK), np.arange(Q_BLOCK), indexing="ij")
    tri = np.stack([np.where(a_io > q_io, 0.0, NEG), np.where(a_io <= q_io, 0.0, NEG)]).astype(np.float32)
    tri = jnp.asarray(np.tile(tri, (1, 1, 2)))
    ya = _nsa(qt, gt, kc, vct, ksa, vst, kwa, vwt, ovt, pair, tri, n_sel).reshape(n, ATTN_WIDTH)

    wr = jnp.pad(w_router, ((0, 0), (0, LANES - N_EXPERTS)))
    wrh = wr.astype(BF16)
    wr2 = jnp.concatenate([wrh, (wr - wrh.astype(F32)).astype(BF16)], axis=1)
    br = jnp.pad(b_router, (0, LANES - N_EXPERTS))[None]
    x1, hm, rw, ri, cnt = _merge(x2, ya, ga, gbyb, w_branch_a.astype(BF16), w_out.astype(BF16),
                                 g_moe[None], wr2, br, tm)

    counts = cnt[0, :N_EXPERTS].astype(I32)
    padded = (counts + MOE_BLOCK - 1) // MOE_BLOCK * MOE_BLOCK
    pad_end = jnp.cumsum(padded)
    pad_start = pad_end - padded
    dest = (pad_start[ri[:, :TOP_K]] + ri[:, TOP_K:2 * TOP_K]).reshape(-1)
    n_blocks = -(-(n * TOP_K) // MOE_BLOCK) + N_EXPERTS
    blk_start = jnp.arange(n_blocks, dtype=I32) * MOE_BLOCK
    blk_expert = jnp.minimum(jnp.sum((pad_end[None, :] <= blk_start[:, None]).astype(I32), axis=1),
                             N_EXPERTS - 1)
    n_used = (pad_end[-1:] // MOE_BLOCK).astype(I32)
    n_slots = n_blocks * MOE_BLOCK
    seg = jnp.stack([jnp.concatenate([pad_start + counts, pad_end[-1:]]),
                     jnp.concatenate([pad_end, jnp.full((1,), n_slots, I32)])]).astype(I32)

    tg = 256
    xs = _dispatch(seg, dest, hm, n_slots, tg)
    y = _experts(blk_expert, n_used, xs, w_gate_up, b_gate_up[:, None, :], w_down, b_down[:, None, :])
    return dest, rw, x1, y, tg


def kernel(x, g_mix, w_in, w_ck1, b_ck1, w_ck2, pe_k, w_cv1, b_cv1, w_cv2, pe_v, sgu_ln_g, sgu_ln_b, w_spatial, b_spatial, w_branch_a, w_branch_b, w_out, g_moe, w_router, b_router, w_gate_up, b_gate_up, w_down, b_down, g_final):
    b, s, d = x.shape
    assert g_mix.shape[0] == 1, "the final rmsnorm is fused into the single layer's combine step"
    l = 0
    dest, rw, x1, y, tg = _layer(
        x.reshape(b * s, d), b, s, g_mix[l], w_in[l], w_ck1[l], b_ck1[l], w_ck2[l], pe_k[l], w_cv1[l],
        b_cv1[l], w_cv2[l], pe_v[l], sgu_ln_g[l], sgu_ln_b[l], w_spatial[l], b_spatial[l], w_branch_a[l],
        w_branch_b[l], w_out[l], g_moe[l], w_router[l], b_router[l], w_gate_up[l], b_gate_up[l],
        w_down[l], b_down[l])
    return _combine(dest, rw, x1, g_final[None], y, tg).reshape(b, s, d)
```

```python
import functools

import numpy as np
import jax
import jax.numpy as jnp
from jax import lax
from jax.experimental import pallas as pl
from jax.experimental.pallas import tpu as pltpu

D_MODEL = 1024
N_HEADS = 8
HEAD_DIM = 64
N_KV = 2
GQA_R = N_HEADS // N_KV
ATTN_WIDTH = N_HEADS * HEAD_DIM
KV_WIDTH = N_KV * HEAD_DIM
CMP_LEN = 32
CMP_STRIDE = 16
CMP_HID = 128
SEL_BLOCK = 64
SEL_TOPN = 16
WINDOW = 512
Q_BLOCK = 128
N_NSA_BRANCH = 3
SGU_WIDTH = 512
N_GROUPS_SGU = 8
SGU_GROUP_DIM = SGU_WIDTH // N_GROUPS_SGU
CHUNK = 128
N_EXPERTS = 32
TOP_K = 4
D_FF = D_MODEL
SWIGLU_LIMIT = 7.0
SWIGLU_ALPHA = 1.702
MOE_BLOCK = 512
RMS_EPS = 1e-5
LN_EPS = 1e-5
NEG = -1e30
FORCE = 1e4

LANES = 128
ROW_TILE = D_MODEL // LANES
MASK_BIAS = 1e9
LOG2E = float(np.log2(np.e))
KEY_CHUNK = 128
SEL_SLOTS = 4
GATE_ROWS = 16
VMEM_LIMIT = 56 * 1024 * 1024

F32 = jnp.float32
BF16 = jnp.bfloat16
I32 = jnp.int32

_NT = (((1,), (1,)), ((), ()))


def _dot(a, b):
    return jnp.dot(a, b, preferred_element_type=F32)


def _dot_nt(a, b):
    return lax.dot_general(a, b, _NT, preferred_element_type=F32)


def _sigmoid(x):
    return 1.0 / (1.0 + jnp.exp(-x))


def _gelu(x):
    c = np.float32(np.sqrt(2.0 / np.pi))
    return 0.5 * x * (1.0 + jnp.tanh(c * (x + 0.044715 * (x * x * x))))


def _rms(x, g):
    return x * lax.rsqrt(jnp.mean(x * x, axis=-1, keepdims=True) + RMS_EPS) * g


T_Q = 0
T_VS = T_Q + N_HEADS * LANES
T_VW = T_VS + N_KV * HEAD_DIM
T_GATE = T_VW + N_KV * HEAD_DIM
T_ROWS = T_GATE + N_KV * GATE_ROWS
K_SEL = 0
K_WIN = K_SEL + N_KV * 2 * LANES
K_CMP = K_WIN + N_KV * LANES
K_COLS = K_CMP + 4 * HEAD_DIM


def _proj_kernel(x_ref, gmix_ref, wt_ref, tconst_ref, wk_ref, kconst_ref, wu_ref, wv_ref, wmg_ref,
                 lng_ref, lnb_ref, wsp_ref, bsp_ref, wbb_ref,
                 qt_ref, gt_ref, vst_ref, vwt_ref, ksa_ref, kwa_ref, kcx_ref, vcx_ref, ga_ref, gbyb_ref):
    tm = x_ref.shape[0]
    hb = _rms(x_ref[...], gmix_ref[...]).astype(BF16)

    for h in range(tm // (2 * Q_BLOCK)):
        t = _dot_nt(wt_ref[...], hb[h * 2 * Q_BLOCK:(h + 1) * 2 * Q_BLOCK]) + tconst_ref[...]
        for jj in range(2):
            j = 2 * h + jj
            ls = slice(jj * Q_BLOCK, (jj + 1) * Q_BLOCK)
            for g in range(N_KV):
                for r in range(GQA_R):
                    r0 = T_Q + (g * GQA_R + r) * LANES
                    qt_ref[0, g, j, :, r * Q_BLOCK:(r + 1) * Q_BLOCK] = t[r0:r0 + LANES, ls].astype(BF16)
                vst_ref[0, g, j] = t[T_VS + g * HEAD_DIM:T_VS + (g + 1) * HEAD_DIM, ls].astype(BF16)
                vwt_ref[0, g, j] = t[T_VW + g * HEAD_DIM:T_VW + (g + 1) * HEAD_DIM, ls].astype(BF16)
                gt_ref[0, g, j] = _sigmoid(t[T_GATE + g * GATE_ROWS:T_GATE + (g + 1) * GATE_ROWS, ls])

    kk = _dot(hb, wk_ref[...])
    kconst = kconst_ref[...].astype(F32)
    for g in range(N_KV):
        c0 = K_SEL + g * 2 * LANES
        ksa_ref[0, g] = (kk[:, c0:c0 + 2 * LANES] + kconst).astype(BF16)
        c0 = K_WIN + g * LANES
        kwa_ref[0, g] = (kk[:, c0:c0 + LANES] + kconst[:, :LANES]).astype(BF16)
    kcx_ref[0] = kk[:, K_CMP:K_CMP + LANES]
    vcx_ref[0] = kk[:, K_CMP + LANES:K_COLS]

    mg = _dot(hb, wmg_ref[...])
    ga_ref[...] = _sigmoid(mg[:, :D_MODEL]).astype(BF16)

    u = _gelu(_dot(hb, wu_ref[...]))
    v = _gelu(_dot(hb, wv_ref[...]))
    mu = jnp.mean(v, axis=-1, keepdims=True)
    vc = v - mu
    var = jnp.mean(vc * vc, axis=-1, keepdims=True)
    vln = (vc * lax.rsqrt(var + LN_EPS) * lng_ref[...] + lnb_ref[...]).astype(BF16)

    row = lax.broadcasted_iota(I32, (CHUNK, CHUNK), 0)
    col = lax.broadcasted_iota(I32, (CHUNK, CHUNK), 1)
    tril = row >= col
    wsp = [jnp.where(tril, wsp_ref[g], 0.0).astype(BF16) for g in range(N_GROUPS_SGU)]
    low_half = col < SGU_GROUP_DIM
    bsp = bsp_ref[...]
    chunks = []
    for c in range(tm // CHUNK):
        rs = slice(c * CHUNK, (c + 1) * CHUNK)
        parts = []
        for p in range(SGU_WIDTH // LANES):
            cs = slice(p * LANES, (p + 1) * LANES)
            vblk = vln[rs, cs]
            mixed = jnp.where(low_half, _dot(wsp[2 * p], vblk), _dot(wsp[2 * p + 1], vblk))
            parts.append(u[rs, cs] * (mixed + bsp[:, cs]))
        chunks.append(jnp.concatenate(parts, axis=1))
    yb = jnp.concatenate(chunks, axis=0).astype(BF16)
    gbyb_ref[...] = (_sigmoid(mg[:, D_MODEL:]) * _dot(yb, wbb_ref[...])).astype(BF16)


def _proj(x2, b, s, gmix, wt, tconst, wk, kconst, wu, wv, wmg, lng, lnb, wsp, bsp, wbb, tm):
    n = x2.shape[0]
    tiles = s // tm
    qtiles = tm // Q_BLOCK
    full = lambda a: pl.BlockSpec(a.shape, lambda i, j: (0,) * a.ndim)
    rows = lambda w: pl.BlockSpec((tm, w), lambda i, j: (i * tiles + j, 0))
    per_g = lambda *blk: pl.BlockSpec((1, N_KV) + blk, lambda i, j: (i, 0, j) + (0,) * (len(blk) - 1))
    ins = (x2, gmix, wt, tconst, wk, kconst, wu, wv, wmg, lng, lnb, wsp, bsp, wbb)
    in_specs = [rows(D_MODEL)] + [full(a) for a in ins[1:]]
    in_specs[5] = pl.BlockSpec((tm, 2 * LANES), lambda i, j: (j, 0))
    nt = s // Q_BLOCK
    return pl.pallas_call(
        _proj_kernel,
        grid=(b, tiles),
        in_specs=in_specs,
        out_specs=[per_g(qtiles, LANES, GQA_R * Q_BLOCK), per_g(qtiles, GATE_ROWS, Q_BLOCK),
                   per_g(qtiles, HEAD_DIM, Q_BLOCK), per_g(qtiles, HEAD_DIM, Q_BLOCK),
                   per_g(tm, 2 * LANES), per_g(tm, LANES),
                   pl.BlockSpec((1, tm, LANES), lambda i, j: (i, j, 0)),
                   pl.BlockSpec((1, tm, LANES), lambda i, j: (i, j, 0)),
                   rows(D_MODEL), rows(D_MODEL)],
        out_shape=[jax.ShapeDtypeStruct((b, N_KV, nt, LANES, GQA_R * Q_BLOCK), BF16),
                   jax.ShapeDtypeStruct((b, N_KV, nt, GATE_ROWS, Q_BLOCK), F32),
                   jax.ShapeDtypeStruct((b, N_KV, nt, HEAD_DIM, Q_BLOCK), BF16),
                   jax.ShapeDtypeStruct((b, N_KV, nt, HEAD_DIM, Q_BLOCK), BF16),
                   jax.ShapeDtypeStruct((b, N_KV, s, 2 * LANES), BF16),
                   jax.ShapeDtypeStruct((b, N_KV, s, LANES), BF16),
                   jax.ShapeDtypeStruct((b, s, LANES), F32),
                   jax.ShapeDtypeStruct((b, s, LANES), F32),
                   jax.ShapeDtypeStruct((n, D_MODEL), BF16),
                   jax.ShapeDtypeStruct((n, D_MODEL), BF16)],
        compiler_params=pltpu.CompilerParams(dimension_semantics=("parallel", "parallel"),
                                             vmem_limit_bytes=VMEM_LIMIT),
        name="proj",
    )(*ins)


def _compress_kernel(xk_ref, xv_ref, pe_ref, w1_ref, b1_ref, w2k_ref, w2vt_ref, kconst_ref, kc_ref, vct_ref):
    nc = kc_ref.shape[2]
    half = CMP_LEN // 2
    hids = []
    for which, x_ref in enumerate((xk_ref, xv_ref)):
        za = jnp.zeros((nc, N_KV * CMP_HID), F32)
        zb = jnp.zeros((nc, N_KV * CMP_HID), F32)
        for l in range(half):
            xl = x_ref[0, pl.ds(l, nc, stride=CMP_STRIDE), :]
            za = za + _dot((xl + pe_ref[which, 0, l:l + 1, :]).astype(BF16), w1_ref[which, 0, l])
            zb = zb + _dot((xl + pe_ref[which, 1, l:l + 1, :]).astype(BF16), w1_ref[which, 1, l])
        hid = _gelu(za + pltpu.roll(zb, nc - 1, 0) + b1_ref[which])
        row = lax.broadcasted_iota(I32, hid.shape, 0)
        hids.append(jnp.where(row < nc - 1, hid, 0.0).astype(BF16))
    for g in range(N_KV):
        cs = slice(g * CMP_HID, (g + 1) * CMP_HID)
        kc_ref[0, g] = (_dot(hids[0][:, cs], w2k_ref[...]) + kconst_ref[...]).astype(BF16)
        vt = _dot_nt(w2vt_ref[...], hids[1][:, cs])
        for c in range(nc // KEY_CHUNK):
            vct_ref[0, g, c] = vt[:, c * KEY_CHUNK:(c + 1) * KEY_CHUNK].astype(BF16)


def _compress(xk, xv, pes, w1, b1, w2k, w2vt, kconst):
    b, s, w = xk.shape
    nc = s // CMP_STRIDE
    full = lambda a: pl.BlockSpec(a.shape, lambda i: (0,) * a.ndim)
    seq = pl.BlockSpec((1, s, w), lambda i: (i, 0, 0))
    return pl.pallas_call(
        _compress_kernel,
        grid=(b,),
        in_specs=[seq, seq, full(pes), full(w1), full(b1), full(w2k), full(w2vt), full(kconst)],
        out_specs=[pl.BlockSpec((1, N_KV, nc, LANES), lambda i: (i, 0, 0, 0)),
                   pl.BlockSpec((1, N_KV, nc // KEY_CHUNK, HEAD_DIM, KEY_CHUNK), lambda i: (i, 0, 0, 0, 0))],
        out_shape=[jax.ShapeDtypeStruct((b, N_KV, nc, LANES), BF16),
                   jax.ShapeDtypeStruct((b, N_KV, nc // KEY_CHUNK, HEAD_DIM, KEY_CHUNK), BF16)],
        compiler_params=pltpu.CompilerParams(dimension_semantics=("parallel",),
                                             vmem_limit_bytes=VMEM_LIMIT),
        name="compress",
    )(xk, xv, pes, w1, b1, w2k, w2vt, kconst)


def _pairs():
    return [slice(pr * 2 * Q_BLOCK, (pr + 1) * 2 * Q_BLOCK) for pr in range(GQA_R // 2)]


def _colmax(blocks):
    part = None
    for x in blocks:
        y = jnp.max(x.reshape(-1, 8, x.shape[-1]), axis=0)
        part = y if part is None else jnp.maximum(part, y)
    return jnp.max(part, axis=0, keepdims=True)


def _colsum(blocks):
    part = None
    for x in blocks:
        y = jnp.sum(x.reshape(-1, 8, x.shape[-1]), axis=0)
        part = y if part is None else part + y
    return jnp.sum(part, axis=0, keepdims=True)


def _softmax_pv(scores, vts, masks, m_ref, l_ref, acc_ref):
    for pr, ls in enumerate(_pairs()):
        sb = [x if masks[u] is None else masks[u](x, ls) for u, x in enumerate(scores[pr])]
        m_old = m_ref[0:1, ls]
        m_new = jnp.maximum(m_old, _colmax(sb))
        alpha = jnp.exp2(m_old - m_new)
        ps = [jnp.exp2(x - m_new) for x in sb]
        pv = _dot(vts[0], ps[0].astype(BF16))
        for u in range(1, len(ps)):
            pv = pv + _dot(vts[u], ps[u].astype(BF16))
        l_ref[0:1, ls] = alpha * l_ref[0:1, ls] + _colsum(ps)
        acc_ref[:, ls] = alpha * acc_ref[:, ls] + pv
        m_ref[0:1, ls] = m_new


def _nsa_kernel(qt_ref, gt_ref, kc_ref, vct_ref, ksa_ref, vst_ref, kwa_ref, vwt_ref, ovt_ref, pair_ref, tri_ref,
                o_ref, qa_ref, s0_ref, s1_ref, m_ref, l_ref, acc_ref, act_ref, *, n_sel, top_n):
    i = pl.program_id(1)
    t0 = i * Q_BLOCK
    width = GQA_R * Q_BLOCK
    n_chunks = ksa_ref.shape[2] // KEY_CHUNK
    n_cchunks = kc_ref.shape[2] // KEY_CHUNK
    n_slots = act_ref.shape[0] // N_KV
    lane = lax.broadcasted_iota(I32, (1, width), 1)
    tq = t0 + (lane & (Q_BLOCK - 1))
    sub = lax.broadcasted_iota(I32, (KEY_CHUNK, 1), 0)
    pairs = _pairs()
    heads = range(N_KV)

    def reset(g):
        m_ref[g] = jnp.full(m_ref.shape[1:], NEG, F32)
        l_ref[g] = jnp.zeros(l_ref.shape[1:], F32)
        acc_ref[g] = jnp.zeros(acc_ref.shape[1:], F32)

    sc = [[[_dot(kc_ref[0, g, u * KEY_CHUNK:(u + 1) * KEY_CHUNK, :], qt_ref[0, g, 0, :, ls])
            for u in range(n_cchunks)] for ls in pairs] for g in heads]
    o_c, imp_t = [], []
    for g in heads:
        ocg, ps = [], []
        for pr, ls in enumerate(pairs):
            sb = []
            for u in range(n_cchunks):
                c_end = (u * KEY_CHUNK + sub) * CMP_STRIDE + (CMP_LEN - 1)
                sb.append(jnp.where(c_end <= tq[:, ls], sc[g][pr][u], NEG))
            m = _colmax(sb)
            pb = [jnp.exp2(x - m) for x in sb]
            l = _colsum(pb)
            w = (tq[:, ls] >= CMP_LEN - 1).astype(F32) / l
            oc = _dot(vct_ref[0, g, 0], pb[0].astype(BF16))
            for u in range(1, n_cchunks):
                oc = oc + _dot(vct_ref[0, g, u], pb[u].astype(BF16))
            ocg.append(oc * w)
            ps.append([p[:, :Q_BLOCK] * w[:, :Q_BLOCK] + p[:, Q_BLOCK:] * w[:, Q_BLOCK:] for p in pb])
        imp2 = jnp.zeros((LANES, 2 * Q_BLOCK), F32)
        for u in range(n_cchunks):
            psu = ps[0][u] + ps[1][u]
            hi = psu.astype(BF16)
            lo = (psu - hi.astype(F32)).astype(BF16)
            imp2 = imp2 + _dot(ovt_ref[u], jnp.concatenate([hi, lo], axis=1))
        o_c.append(jnp.concatenate(ocg, axis=1))
        imp_t.append(imp2[:, :Q_BLOCK] + imp2[:, Q_BLOCK:])

    n_win = WINDOW // KEY_CHUNK
    cls, masks = [], []
    for u in range(n_win + 1):
        c = i - n_win + u
        cls.append(jnp.maximum(c, 0))
        off = jnp.where(c < 0, NEG, 0.0)
        if u == 0:
            edge = tri_ref[0] + off
            masks.append(lambda x, ls, edge=edge: x + edge)
        elif u == n_win:
            masks.append(lambda x, ls: x + tri_ref[1])
        else:
            masks.append(lambda x, ls, off=off: x + off)
    wsc = [[[_dot(kwa_ref[0, g, pl.ds(pl.multiple_of(cl * KEY_CHUNK, KEY_CHUNK), KEY_CHUNK), :],
                  qt_ref[0, g, 0, :, ls]) for cl in cls] for ls in pairs] for g in heads]
    o_w = []
    for g in heads:
        reset(g)
        _softmax_pv(wsc[g], [vwt_ref[0, g, cl] for cl in cls], masks, m_ref.at[g], l_ref.at[g], acc_ref.at[g])
        o_w.append(acc_ref[g] / l_ref[g, 0:1, :])

    j_io = lax.broadcasted_iota(I32, (LANES, Q_BLOCK), 0)
    tl = t0 + lax.broadcasted_iota(I32, (LANES, Q_BLOCK), 1)
    cur = tl >> 6
    forced = (j_io == 0) | (j_io == cur) | (j_io == cur - 1)
    prio = [jnp.where(j_io < n_sel, jnp.where(forced, FORCE, jnp.where(j_io * SEL_BLOCK <= tl, imp_t[g], -1.0)), NEG)
            for g in heads]
    for _ in range(top_n):
        for g in heads:
            m = jnp.max(prio[g], axis=0, keepdims=True)
            idx = jnp.min(jnp.where(prio[g] == m, j_io, LANES), axis=0, keepdims=True)
            prio[g] = jnp.where(j_io == idx, NEG, prio[g])
    sel_t = [jnp.where((prio[g] == NEG) & (j_io < n_sel), 1.0, 0.0) for g in heads]

    n_act = []
    for u in range(act_ref.shape[0]):
        act_ref[u] = jnp.int32(-1)
    for g in heads:
        qa_ref[g, 0:LANES, :] = qt_ref[0, g, 0]
        bias_t = ((sel_t[g] - 1.0) * MASK_BIAS).astype(BF16)
        for h in range(GQA_R):
            qa_ref[g, LANES:2 * LANES, h * Q_BLOCK:(h + 1) * Q_BLOCK] = bias_t
        per_block = _dot_nt(jnp.ones((8, Q_BLOCK), BF16), sel_t[g].astype(BF16))
        per_chunk = _dot(per_block.astype(BF16), pair_ref[...])
        na = jnp.int32(0)
        for c in range(n_chunks):
            act_ref[g * n_slots + na] = jnp.int32(c)
            na = na + jnp.where((per_chunk[0, c] > 0.0) & (c < i), 1, 0)
        act_ref[g * n_slots + na] = i
        n_act.append(na)

    def sel_scores(g, k, dst):
        for u in range(SEL_SLOTS):
            cl = jnp.maximum(act_ref[g * n_slots + k * SEL_SLOTS + u], 0)
            keys = ksa_ref[0, g, pl.ds(pl.multiple_of(cl * KEY_CHUNK, KEY_CHUNK), KEY_CHUNK), :]
            for ls in pairs:
                dst[g, u, :, ls] = _dot(keys, qa_ref[g, :, ls])

    def sel_step(g, k, cur_s, nxt_s, last):
        sel_scores(g, k + 1, nxt_s)
        vts, msk = [], []
        for u in range(SEL_SLOTS):
            c = act_ref[g * n_slots + k * SEL_SLOTS + u]
            vts.append(vst_ref[0, g, jnp.maximum(c, 0)])
            if last:
                kpos = jnp.where(c < 0, n_chunks * KEY_CHUNK, c * KEY_CHUNK) + sub
                msk.append(lambda x, ls, kpos=kpos: jnp.where(kpos <= tq[:, ls], x, -MASK_BIAS))
            else:
                msk.append(None)
        scores = [[cur_s[g, u, :, ls] for u in range(SEL_SLOTS)] for ls in pairs]
        _softmax_pv(scores, vts, msk, m_ref.at[g], l_ref.at[g], acc_ref.at[g])

    for g in heads:
        reset(g)
        sel_scores(g, 0, s0_ref)
    n_steps = [(n_act[g] + SEL_SLOTS) // SEL_SLOTS for g in heads]
    bufs = ((s0_ref, s1_ref), (s1_ref, s0_ref))

    def sel_body(k, carry):
        last = [k >= n_steps[g] - 1 for g in heads]
        for odd, (cur_s, nxt_s) in enumerate(bufs):
            for l0 in (False, True):
                for l1 in (False, True):
                    @pl.when((k % 2 == odd) & (last[0] == l0) & (last[1] == l1))
                    def _():
                        sel_step(0, k, cur_s, nxt_s, l0)
                        sel_step(1, k, cur_s, nxt_s, l1)

        return carry

    lax.fori_loop(0, jnp.maximum(n_steps[0], n_steps[1]), sel_body, 0)
    o_s = [acc_ref[g] / l_ref[g, 0:1, :] for g in heads]

    for g in heads:
        gt = gt_ref[0, g, 0]

        def gate(br):
            return jnp.concatenate([gt[br * GQA_R + h:br * GQA_R + h + 1, :] for h in range(GQA_R)], axis=1)

        o = gate(0) * o_c[g] + gate(1) * o_s[g] + gate(2) * o_w[g]
        o_ref[0, :, g * GQA_R * HEAD_DIM:(g + 1) * GQA_R * HEAD_DIM] = jnp.concatenate(
            [o[:, h * Q_BLOCK:(h + 1) * Q_BLOCK].T for h in range(GQA_R)], axis=1).astype(BF16)


def _nsa(qt, gt, kc, vct, ksa, vst, kwa, vwt, ovt, pair, tri, n_sel):
    b, g, nt, _, width = qt.shape
    s = ksa.shape[2]
    tile = lambda a: pl.BlockSpec((1, g, 1) + a.shape[3:], lambda i, k: (i, 0, k, 0, 0))
    whole = lambda a: pl.BlockSpec((1,) + a.shape[1:], lambda i, k: (i,) + (0,) * (a.ndim - 1))
    const = lambda a: pl.BlockSpec(a.shape, lambda i, k: (0,) * a.ndim)
    kern = functools.partial(_nsa_kernel, n_sel=n_sel, top_n=min(SEL_TOPN, n_sel))
    return pl.pallas_call(
        kern,
        grid=(b, nt),
        in_specs=[tile(qt), tile(gt), whole(kc), whole(vct), whole(ksa), whole(vst), whole(kwa), whole(vwt),
                  const(ovt), const(pair), const(tri)],
        out_specs=pl.BlockSpec((1, Q_BLOCK, ATTN_WIDTH), lambda i, k: (i, k, 0)),
        out_shape=jax.ShapeDtypeStruct((b, s, ATTN_WIDTH), BF16),
        scratch_shapes=[pltpu.VMEM((g, 2 * LANES, width), BF16),
                        pltpu.VMEM((g, SEL_SLOTS, KEY_CHUNK, width), F32),
                        pltpu.VMEM((g, SEL_SLOTS, KEY_CHUNK, width), F32),
                        pltpu.VMEM((g, 8, width), F32),
                        pltpu.VMEM((g, 8, width), F32),
                        pltpu.VMEM((g, HEAD_DIM, width), F32),
                        pltpu.SMEM((g * (s // KEY_CHUNK + 2 * SEL_SLOTS),), I32)],
        compiler_params=pltpu.CompilerParams(dimension_semantics=("parallel", "arbitrary"),
                                             vmem_limit_bytes=VMEM_LIMIT),
        name="nsa",
    )(qt, gt, kc, vct, ksa, vst, kwa, vwt, ovt, pair, tri)


def _merge_kernel(x_ref, ya_ref, ga_ref, gbyb_ref, wba_ref, wout_ref, gmoe_ref, wr_ref, br_ref, ltri_ref,
                  x1_ref, hm_ref, rw_ref, ri_ref, cnt_ref, carry_ref):
    tm = x_ref.shape[0]

    @pl.when(pl.program_id(0) == 0)
    def _():
        carry_ref[...] = jnp.zeros_like(carry_ref)

    merged = ga_ref[...].astype(F32) * _dot(ya_ref[...], wba_ref[...]) + gbyb_ref[...].astype(F32)
    x1 = x_ref[...] + _dot(merged.astype(BF16), wout_ref[...])
    x1_ref[...] = x1
    hm = _rms(x1, gmoe_ref[...])
    for a in range(ROW_TILE):
        hm_ref[pl.ds(a, tm, stride=ROW_TILE), :] = hm[:, a * LANES:(a + 1) * LANES]

    hh = hm.astype(BF16)
    hl = (hm - hh.astype(F32)).astype(BF16)
    both = _dot(hh, wr_ref[...])
    logits = both[:, :LANES] + both[:, LANES:] + _dot(hl, wr_ref[:, :LANES]) + br_ref[...]
    lane = lax.broadcasted_iota(I32, (tm, LANES), 1)
    lg = jnp.where(lane < N_EXPERTS, logits, NEG)
    vals, idxs = [], []
    for _ in range(TOP_K):
        m = jnp.max(lg, axis=1, keepdims=True)
        idx = jnp.min(jnp.where(lg == m, lane, LANES), axis=1, keepdims=True)
        vals.append(m)
        idxs.append(idx)
        lg = jnp.where(lane == idx, NEG, lg)
    ex = [jnp.exp(v - vals[0]) for v in vals]
    den = ex[0] + ex[1] + ex[2] + ex[3]

    hits = [lane == idx for idx in idxs]
    multi = jnp.zeros((tm, LANES), F32)
    for h in hits:
        multi = jnp.where(h, 1.0, multi)
    carry = carry_ref[0:1, :]
    cum = _dot(ltri_ref[...], multi.astype(BF16)) + carry
    rw = jnp.zeros((tm, LANES), F32)
    ri = jnp.zeros((tm, LANES), I32)
    for k in range(TOP_K):
        rank = jnp.sum(jnp.where(hits[k], cum, 0.0), axis=1, keepdims=True).astype(I32)
        rw = jnp.where(lane == k, ex[k] / den, rw)
        ri = jnp.where(lane == k, idxs[k], jnp.where(lane == TOP_K + k, rank, ri))
    rw_ref[...] = rw
    ri_ref[...] = ri
    new_carry = carry + jnp.sum(multi, axis=0, keepdims=True)
    carry_ref[...] = jnp.broadcast_to(new_carry, carry_ref.shape)
    cnt_ref[...] = jnp.broadcast_to(new_carry, cnt_ref.shape)


def _merge(x2, ya, ga, gbyb, wba, wout, gmoe, wr, br, tm):
    n = x2.shape[0]
    full = lambda a: pl.BlockSpec(a.shape, lambda i: (0,) * a.ndim)
    rows = lambda w: pl.BlockSpec((tm, w), lambda i: (i, 0))
    ltri = jnp.asarray(np.tril(np.ones((tm, tm), np.float32), -1), BF16)
    return pl.pallas_call(
        _merge_kernel,
        grid=(n // tm,),
        in_specs=[rows(D_MODEL), rows(ATTN_WIDTH), rows(D_MODEL), rows(D_MODEL),
                  full(wba), full(wout), full(gmoe), full(wr), full(br), full(ltri)],
        out_specs=[rows(D_MODEL), pl.BlockSpec((tm * ROW_TILE, LANES), lambda i: (i, 0)), rows(LANES), rows(LANES),
                   pl.BlockSpec((8, LANES), lambda i: (0, 0))],
        out_shape=[jax.ShapeDtypeStruct((n, D_MODEL), F32),
                   jax.ShapeDtypeStruct((n * ROW_TILE, LANES), F32),
                   jax.ShapeDtypeStruct((n, LANES), F32),
                   jax.ShapeDtypeStruct((n, LANES), I32),
                   jax.ShapeDtypeStruct((8, LANES), F32)],
        scratch_shapes=[pltpu.VMEM((8, LANES), F32)],
        compiler_params=pltpu.CompilerParams(dimension_semantics=("arbitrary",),
                                             vmem_limit_bytes=VMEM_LIMIT),
        name="merge",
    )(x2, ya, ga, gbyb, wba, wout, gmoe, wr, br, ltri)


def _tile_copy(src, i, dst, d, sem):
    return pltpu.make_async_copy(src.at[pl.ds(pl.multiple_of(i * ROW_TILE, ROW_TILE), ROW_TILE)],
                                 dst.at[pl.ds(pl.multiple_of(d * ROW_TILE, ROW_TILE), ROW_TILE)], sem)


def _dispatch_kernel(seg_ref, dest_ref, hm_ref, xs_ref, zero_ref, sem, zsem, *, n_pad):
    tm = hm_ref.shape[0] // ROW_TILE

    @pl.when(pl.program_id(0) == 0)
    def _():
        zero_ref[...] = jnp.zeros_like(zero_ref)

        def seg(e, c):
            def fill(r, c2):
                _tile_copy(zero_ref, 0, xs_ref, r, zsem).start()
                return c2
            return lax.fori_loop(seg_ref[0, e], seg_ref[1, e], fill, c)

        lax.fori_loop(0, N_EXPERTS + 1, seg, 0)
        pad_rows = xs_ref.at[pl.ds(0, n_pad * ROW_TILE)]
        pltpu.make_async_copy(pad_rows, pad_rows, zsem).wait()

    def issue(r, c):
        slots = [dest_ref[r * TOP_K + k] for k in range(TOP_K)]
        for k in range(TOP_K):
            _tile_copy(hm_ref, r, xs_ref, slots[k], sem).start(priority=k % 2)
        return c

    lax.fori_loop(0, tm, issue, 0)
    for k in range(TOP_K):
        pltpu.make_async_copy(hm_ref, xs_ref.at[pl.ds(0, tm * ROW_TILE)], sem).wait()


def _dispatch(seg, dest_flat, hm, n_slots, tm):
    n = hm.shape[0] // ROW_TILE
    kern = functools.partial(_dispatch_kernel, n_pad=n_slots - n * TOP_K)
    return pl.pallas_call(
        kern,
        grid_spec=pltpu.PrefetchScalarGridSpec(
            num_scalar_prefetch=1,
            grid=(n // tm,),
            in_specs=[pl.BlockSpec((tm * TOP_K,), lambda i, sg: (i,), memory_space=pltpu.SMEM),
                      pl.BlockSpec((tm * ROW_TILE, LANES), lambda i, sg: (i, 0))],
            out_specs=pl.BlockSpec(memory_space=pl.ANY),
            scratch_shapes=[pltpu.VMEM((ROW_TILE, LANES), F32),
                            pltpu.SemaphoreType.DMA(()), pltpu.SemaphoreType.DMA(())]),
        out_shape=jax.ShapeDtypeStruct((n_slots * ROW_TILE, LANES), F32),
        compiler_params=pltpu.CompilerParams(dimension_semantics=("arbitrary",),
                                             has_side_effects=True),
        name="dispatch",
    )(seg, dest_flat, hm)


def _expert_kernel(be_ref, nu_ref, xs_ref, wgu_ref, bgu_ref, wd_ref, bd_ref, y_ref, wgu_bf, wd_bf):
    i = pl.program_id(0)

    @pl.when(i >= nu_ref[0])
    def _():
        y_ref[...] = jnp.zeros_like(y_ref)

    @pl.when((i == 0) | (be_ref[i] != be_ref[jnp.maximum(i - 1, 0)]))
    def _():
        wgu_bf[...] = wgu_ref[0].astype(BF16)
        wd_bf[...] = wd_ref[0].astype(BF16)

    @pl.when(i < nu_ref[0])
    def _():
        x = jnp.concatenate([xs_ref[pl.ds(a, MOE_BLOCK, stride=ROW_TILE), :] for a in range(ROW_TILE)], axis=1)
        gu = _dot(x.astype(BF16), wgu_bf[...]) + bgu_ref[0]
        gate = jnp.minimum(gu[:, :D_FF], SWIGLU_LIMIT)
        up = jnp.clip(gu[:, D_FF:], -SWIGLU_LIMIT, SWIGLU_LIMIT)
        act = gate * _sigmoid(SWIGLU_ALPHA * gate) * (up + 1.0)
        y = _dot(act.astype(BF16), wd_bf[...]) + bd_ref[0]
        for a in range(ROW_TILE):
            y_ref[pl.ds(a, MOE_BLOCK, stride=ROW_TILE), :] = y[:, a * LANES:(a + 1) * LANES]


def _experts(blk_expert, n_used, xs, wgu, bgu, wd, bd):
    n_blocks = xs.shape[0] // (MOE_BLOCK * ROW_TILE)
    blk = lambda i, be, nu: (jnp.minimum(i, nu[0] - 1), 0)
    exp3 = lambda i, be, nu: (be[jnp.minimum(i, nu[0] - 1)], 0, 0)
    return pl.pallas_call(
        _expert_kernel,
        grid_spec=pltpu.PrefetchScalarGridSpec(
            num_scalar_prefetch=2,
            grid=(n_blocks,),
            in_specs=[pl.BlockSpec((MOE_BLOCK * ROW_TILE, LANES), blk),
                      pl.BlockSpec((1, D_MODEL, 2 * D_FF), exp3),
                      pl.BlockSpec((1, 1, 2 * D_FF), exp3),
                      pl.BlockSpec((1, D_FF, D_MODEL), exp3),
                      pl.BlockSpec((1, 1, D_MODEL), exp3)],
            out_specs=pl.BlockSpec((MOE_BLOCK * ROW_TILE, LANES), lambda i, be, nu: (i, 0)),
            scratch_shapes=[pltpu.VMEM((D_MODEL, 2 * D_FF), BF16), pltpu.VMEM((D_FF, D_MODEL), BF16)]),
        out_shape=jax.ShapeDtypeStruct(xs.shape, F32),
        compiler_params=pltpu.CompilerParams(dimension_semantics=("arbitrary",),
                                             vmem_limit_bytes=VMEM_LIMIT),
        name="experts",
    )(blk_expert, n_used, xs, wgu, bgu, wd, bd)


def _combine_kernel(dest_ref, dest_next_ref, rw_ref, x1_ref, gfin_ref, y_ref, o_ref, ybuf0, ybuf1, sem0, sem1):
    tm = x1_ref.shape[0]
    i = pl.program_id(0)

    def gather(dst_ref, ybuf, sem):
        def issue(r, c):
            slots = [dst_ref[r * TOP_K + k] for k in range(TOP_K)]
            for k in range(TOP_K):
                _tile_copy(y_ref, slots[k], ybuf.at[k], r, sem).start(priority=k % 2)
            return c

        lax.fori_loop(0, tm, issue, 0)

    def finish(ybuf, sem):
        for k in range(TOP_K):
            pltpu.make_async_copy(y_ref.at[pl.ds(0, tm * ROW_TILE)], ybuf.at[k], sem).wait()
        rw = rw_ref[...]
        x1 = x1_ref[...]
        cols = []
        for a in range(ROW_TILE):
            acc = x1[:, a * LANES:(a + 1) * LANES]
            for k in range(TOP_K):
                acc = acc + rw[:, k:k + 1] * ybuf[k, pl.ds(a, tm, stride=ROW_TILE), :]
            cols.append(acc)
        o_ref[...] = _rms(jnp.concatenate(cols, axis=1), gfin_ref[...])

    @pl.when(i == 0)
    def _():
        gather(dest_ref, ybuf0, sem0)

    for parity, (cur, nxt) in enumerate((((ybuf0, sem0), (ybuf1, sem1)), ((ybuf1, sem1), (ybuf0, sem0)))):
        @pl.when(i % 2 == parity)
        def _():
            @pl.when(i + 1 < pl.num_programs(0))
            def _():
                gather(dest_next_ref, *nxt)

            finish(*cur)


def _combine(dest_flat, rw, x1, gfin, y, tm):
    n = x1.shape[0]
    steps = n // tm
    ybuf = pltpu.VMEM((TOP_K, tm * ROW_TILE, LANES), F32)
    return pl.pallas_call(
        _combine_kernel,
        grid=(steps,),
        in_specs=[pl.BlockSpec((tm * TOP_K,), lambda i: (i,), memory_space=pltpu.SMEM),
                  pl.BlockSpec((tm * TOP_K,), lambda i: (jnp.minimum(i + 1, steps - 1),), memory_space=pltpu.SMEM),
                  pl.BlockSpec((tm, LANES), lambda i: (i, 0)),
                  pl.BlockSpec((tm, D_MODEL), lambda i: (i, 0)),
                  pl.BlockSpec((1, D_MODEL), lambda i: (0, 0)),
                  pl.BlockSpec(memory_space=pl.ANY)],
        out_specs=pl.BlockSpec((tm, D_MODEL), lambda i: (i, 0)),
        out_shape=jax.ShapeDtypeStruct((n, D_MODEL), F32),
        scratch_shapes=[ybuf, ybuf, pltpu.SemaphoreType.DMA(()), pltpu.SemaphoreType.DMA(())],
        compiler_params=pltpu.CompilerParams(dimension_semantics=("arbitrary",),
                                             vmem_limit_bytes=VMEM_LIMIT),
        name="combine",
    )(dest_flat, dest_flat, rw, x1, gfin, y)


def _overlap_t(nc, n_cmp, n_sel):
    cs = np.arange(n_cmp)[None, :] * CMP_STRIDE
    ss = np.arange(n_sel)[:, None] * SEL_BLOCK
    ov = np.clip(np.minimum(cs + CMP_LEN, ss + SEL_BLOCK) - np.maximum(cs, ss), 0, None) / CMP_LEN
    out = np.zeros((LANES, nc), np.float32)
    out[:n_sel, :n_cmp] = ov
    return jnp.asarray(out, BF16)


def _layer(x2, b, s, g_mix, w_in, w_ck1, b_ck1, w_ck2, pe_k, w_cv1, b_cv1, w_cv2, pe_v,
           sgu_ln_g, sgu_ln_b, w_spatial, b_spatial, w_branch_a, w_branch_b, w_out,
           g_moe, w_router, b_router, w_gate_up, b_gate_up, w_down, b_down):
    n = b * s
    nc = s // CMP_STRIDE
    n_cmp = (s - CMP_LEN) // CMP_STRIDE + 1
    n_sel = s // SEL_BLOCK
    assert nc % KEY_CHUNK == 0 and n_sel <= LANES and n_cmp == nc - 1
    tm = 512
    assert s % tm == 0

    p0 = ATTN_WIDTH
    p1 = p0 + 6 * KV_WIDTH
    p2 = p1 + N_NSA_BRANCH * N_HEADS
    p3 = p2 + SGU_WIDTH
    p4 = p3 + SGU_WIDTH
    zpad = lambda a, w: jnp.pad(a, ((0, 0),) * (a.ndim - 1) + ((0, w - a.shape[-1]),))
    wq = zpad((w_in[:, :p0] * (HEAD_DIM ** -0.5 * LOG2E)).reshape(D_MODEL, N_HEADS, HEAD_DIM), LANES)
    wkv = w_in[:, p0:p1].reshape(D_MODEL, 6, N_KV, HEAD_DIM)
    wng = w_in[:, p1:p2].reshape(D_MODEL, N_KV, GQA_R, N_NSA_BRANCH).transpose(0, 1, 3, 2)
    wng = zpad(wng.reshape(D_MODEL, N_KV, N_NSA_BRANCH * GQA_R), GATE_ROWS)
    wt = jnp.concatenate([wq.reshape(D_MODEL, -1), wkv[:, 3].reshape(D_MODEL, -1),
                          wkv[:, 5].reshape(D_MODEL, -1), wng.reshape(D_MODEL, -1)], axis=1).T.astype(BF16)
    slopes = 2.0 ** (-8.0 * np.arange(1, N_HEADS + 1) / N_HEADS)
    tcol = np.zeros((T_ROWS, 1), np.float32)
    head_rows = T_Q + np.arange(N_HEADS) * LANES + HEAD_DIM
    bf16_round = lambda a: a.astype(BF16).astype(np.float32)
    for k, coef in enumerate((slopes * SEL_BLOCK * LOG2E, slopes * LOG2E)):
        hi = bf16_round(coef.astype(np.float32))
        tcol[head_rows + 2 * k, 0] = hi
        tcol[head_rows + 2 * k + 1, 0] = bf16_round(coef.astype(np.float32) - hi)
    tconst = jnp.asarray(np.broadcast_to(tcol, (T_ROWS, 2 * Q_BLOCK)))
    wk = jnp.concatenate([zpad(wkv[:, 2, g], 2 * LANES) for g in range(N_KV)]
                         + [zpad(wkv[:, 4, g], LANES) for g in range(N_KV)]
                         + [wkv[:, 0].reshape(D_MODEL, -1), wkv[:, 1].reshape(D_MODEL, -1)], axis=1).astype(BF16)
    pos = np.arange(s)
    kc_np = np.zeros((s, 2 * LANES), np.float32)
    kc_np[:, HEAD_DIM:HEAD_DIM + 2] = (pos // SEL_BLOCK)[:, None]
    kc_np[:, HEAD_DIM + 2:HEAD_DIM + 4] = (pos % SEL_BLOCK)[:, None]
    kc_np[pos, LANES + pos // SEL_BLOCK] = 1.0
    kconst = jnp.asarray(kc_np, BF16)
    wu = w_in[:, p2:p3].astype(BF16)
    wv = w_in[:, p3:p4].astype(BF16)
    wmg = w_in[:, p4:].astype(BF16)
    bsp = jnp.repeat(b_spatial.T, SGU_GROUP_DIM, axis=1)

    qt, gt, vst, vwt, ksa, kwa, kcx, vcx, ga, gbyb = _proj(
        x2, b, s, g_mix[None], wt, tconst, wk, kconst, wu, wv, wmg, sgu_ln_g[None], sgu_ln_b[None],
        w_spatial, bsp, w_branch_b.astype(BF16), tm)

    half = CMP_LEN // 2
    eye = jnp.eye(N_KV, dtype=F32)[None, None, :, None, :, None]
    bdiag = lambda w: (w.reshape(2, half, 1, HEAD_DIM, 1, CMP_HID) * eye).reshape(
        2, half, N_KV * HEAD_DIM, N_KV * CMP_HID)
    w1 = jnp.stack([bdiag(w_ck1), bdiag(w_cv1)]).astype(BF16)
    pes = jnp.stack([jnp.tile(pe_k.reshape(2, half, HEAD_DIM), (1, 1, N_KV)),
                     jnp.tile(pe_v.reshape(2, half, HEAD_DIM), (1, 1, N_KV))])
    b1 = jnp.stack([jnp.tile(b_ck1, N_KV), jnp.tile(b_cv1, N_KV)])[:, None, :]
    blk_n = np.arange(nc)
    cc_np = np.zeros((nc, LANES), np.float32)
    cc_np[:, HEAD_DIM:HEAD_DIM + 2] = (blk_n // (SEL_BLOCK // CMP_STRIDE))[:, None]
    cc_np[:, HEAD_DIM + 2:HEAD_DIM + 4] = (blk_n % (SEL_BLOCK // CMP_STRIDE) * CMP_STRIDE)[:, None]
    kc, vct = _compress(kcx, vcx, pes, w1, b1, zpad(w_ck2, LANES).astype(BF16), w_cv2.T.astype(BF16),
                        jnp.asarray(cc_np))

    ovt = _overlap_t(nc, n_cmp, n_sel).reshape(LANES, nc // KEY_CHUNK, KEY_CHUNK).transpose(1, 0, 2)
    blk = np.arange(LANES)
    pair = jnp.asarray(blk[:, None] // (KEY_CHUNK // SEL_BLOCK) == blk[None, :], BF16)
    a_io, q_io = np.meshgrid(np.arange(KEY_CHUNK), np.arange(Q_BLOCK), indexing="ij")
    tri = np.stack([np.where(a_io > q_io, 0.0, NEG), np.where(a_io <= q_io, 0.0, NEG)]).astype(np.float32)
    tri = jnp.asarray(np.tile(tri, (1, 1, 2)))
    ya = _nsa(qt, gt, kc, vct, ksa, vst, kwa, vwt, ovt, pair, tri, n_sel).reshape(n, ATTN_WIDTH)

    wr = jnp.pad(w_router, ((0, 0), (0, LANES - N_EXPERTS)))
    wrh = wr.astype(BF16)
    wr2 = jnp.concatenate([wrh, (wr - wrh.astype(F32)).astype(BF16)], axis=1)
    br = jnp.pad(b_router, (0, LANES - N_EXPERTS))[None]
    x1, hm, rw, ri, cnt = _merge(x2, ya, ga, gbyb, w_branch_a.astype(BF16), w_out.astype(BF16),
                                 g_moe[None], wr2, br, tm)

    counts = cnt[0, :N_EXPERTS].astype(I32)
    padded = (counts + MOE_BLOCK - 1) // MOE_BLOCK * MOE_BLOCK
    pad_end = jnp.cumsum(padded)
    pad_start = pad_end - padded
    dest = (pad_start[ri[:, :TOP_K]] + ri[:, TOP_K:2 * TOP_K]).reshape(-1)
    n_blocks = -(-(n * TOP_K) // MOE_BLOCK) + N_EXPERTS
    blk_start = jnp.arange(n_blocks, dtype=I32) * MOE_BLOCK
    blk_expert = jnp.minimum(jnp.sum((pad_end[None, :] <= blk_start[:, None]).astype(I32), axis=1),
                             N_EXPERTS - 1)
    n_used = (pad_end[-1:] // MOE_BLOCK).astype(I32)
    n_slots = n_blocks * MOE_BLOCK
    seg = jnp.stack([jnp.concatenate([pad_start + counts, pad_end[-1:]]),
                     jnp.concatenate([pad_end, jnp.full((1,), n_slots, I32)])]).astype(I32)

    tg = 256
    xs = _dispatch(seg, dest, hm, n_slots, tg)
    y = _experts(blk_expert, n_used, xs, w_gate_up, b_gate_up[:, None, :], w_down, b_down[:, None, :])
    return dest, rw, x1, y, tg


def kernel(x, g_mix, w_in, w_ck1, b_ck1, w_ck2, pe_k, w_cv1, b_cv1, w_cv2, pe_v, sgu_ln_g, sgu_ln_b, w_spatial, b_spatial, w_branch_a, w_branch_b, w_out, g_moe, w_router, b_router, w_gate_up, b_gate_up, w_down, b_down, g_final):
    b, s, d = x.shape
    assert g_mix.shape[0] == 1, "the final rmsnorm is fused into the single layer's combine step"
    l = 0
    dest, rw, x1, y, tg = _layer(
        x.reshape(b * s, d), b, s, g_mix[l], w_in[l], w_ck1[l], b_ck1[l], w_ck2[l], pe_k[l], w_cv1[l],
        b_cv1[l], w_cv2[l], pe_v[l], sgu_ln_g[l], sgu_ln_b[l], w_spatial[l], b_spatial[l], w_branch_a[l],
        w_branch_b[l], w_out[l], g_moe[l], w_router[l], b_router[l], w_gate_up[l], b_gate_up[l],
        w_down[l], b_down[l])
    return _combine(dest, rw, x1, g_final[None], y, tg).reshape(b, s, d)
```

```python
import functools

import numpy as np
import jax
import jax.numpy as jnp
from jax import lax
from jax.experimental import pallas as pl
from jax.experimental.pallas import tpu as pltpu

D_MODEL = 1024
N_HEADS = 8
HEAD_DIM = 64
N_KV = 2
GQA_R = N_HEADS // N_KV
ATTN_WIDTH = N_HEADS * HEAD_DIM
KV_WIDTH = N_KV * HEAD_DIM
CMP_LEN = 32
CMP_STRIDE = 16
CMP_HID = 128
SEL_BLOCK = 64
SEL_TOPN = 16
WINDOW = 512
Q_BLOCK = 128
N_NSA_BRANCH = 3
SGU_WIDTH = 512
N_GROUPS_SGU = 8
SGU_GROUP_DIM = SGU_WIDTH // N_GROUPS_SGU
CHUNK = 128
N_EXPERTS = 32
TOP_K = 4
D_FF = D_MODEL
SWIGLU_LIMIT = 7.0
SWIGLU_ALPHA = 1.702
MOE_BLOCK = 512
RMS_EPS = 1e-5
LN_EPS = 1e-5
NEG = -1e30
FORCE = 1e4

LANES = 128
ROW_TILE = D_MODEL // LANES
MASK_BIAS = 1e9
LOG2E = float(np.log2(np.e))
KEY_CHUNK = 128
SEL_SLOTS = 4
GATE_ROWS = 16
VMEM_LIMIT = 56 * 1024 * 1024

F32 = jnp.float32
BF16 = jnp.bfloat16
I32 = jnp.int32

_NT = (((1,), (1,)), ((), ()))


def _dot(a, b):
    return jnp.dot(a, b, preferred_element_type=F32)


def _dot_nt(a, b):
    return lax.dot_general(a, b, _NT, preferred_element_type=F32)


def _sigmoid(x):
    return 1.0 / (1.0 + jnp.exp(-x))


def _gelu(x):
    c = np.float32(np.sqrt(2.0 / np.pi))
    return 0.5 * x * (1.0 + jnp.tanh(c * (x + 0.044715 * (x * x * x))))


def _rms(x, g):
    return x * lax.rsqrt(jnp.mean(x * x, axis=-1, keepdims=True) + RMS_EPS) * g


T_Q = 0
T_VS = T_Q + N_HEADS * LANES
T_VW = T_VS + N_KV * HEAD_DIM
T_GATE = T_VW + N_KV * HEAD_DIM
T_ROWS = T_GATE + N_KV * GATE_ROWS
K_SEL = 0
K_WIN = K_SEL + N_KV * 2 * LANES
K_CMP = K_WIN + N_KV * LANES
K_COLS = K_CMP + 4 * HEAD_DIM


def _proj_kernel(x_ref, gmix_ref, wt_ref, tconst_ref, wk_ref, kconst_ref, wu_ref, wv_ref, wmg_ref,
                 lng_ref, lnb_ref, wsp_ref, bsp_ref, wbb_ref,
                 qt_ref, gt_ref, vst_ref, vwt_ref, ksa_ref, kwa_ref, kcx_ref, vcx_ref, ga_ref, gbyb_ref):
    tm = x_ref.shape[0]
    hb = _rms(x_ref[...], gmix_ref[...]).astype(BF16)

    for h in range(tm // (2 * Q_BLOCK)):
        t = _dot_nt(wt_ref[...], hb[h * 2 * Q_BLOCK:(h + 1) * 2 * Q_BLOCK]) + tconst_ref[...]
        for jj in range(2):
            j = 2 * h + jj
            ls = slice(jj * Q_BLOCK, (jj + 1) * Q_BLOCK)
            for g in range(N_KV):
                for r in range(GQA_R):
                    r0 = T_Q + (g * GQA_R + r) * LANES
                    qt_ref[0, g, j, :, r * Q_BLOCK:(r + 1) * Q_BLOCK] = t[r0:r0 + LANES, ls].astype(BF16)
                vst_ref[0, g, j] = t[T_VS + g * HEAD_DIM:T_VS + (g + 1) * HEAD_DIM, ls].astype(BF16)
                vwt_ref[0, g, j] = t[T_VW + g * HEAD_DIM:T_VW + (g + 1) * HEAD_DIM, ls].astype(BF16)
                gt_ref[0, g, j] = _sigmoid(t[T_GATE + g * GATE_ROWS:T_GATE + (g + 1) * GATE_ROWS, ls])

    kk = _dot(hb, wk_ref[...])
    kconst = kconst_ref[...].astype(F32)
    for g in range(N_KV):
        c0 = K_SEL + g * 2 * LANES
        ksa_ref[0, g] = (kk[:, c0:c0 + 2 * LANES] + kconst).astype(BF16)
        c0 = K_WIN + g * LANES
        kwa_ref[0, g] = (kk[:, c0:c0 + LANES] + kconst[:, :LANES]).astype(BF16)
    kcx_ref[0] = kk[:, K_CMP:K_CMP + LANES]
    vcx_ref[0] = kk[:, K_CMP + LANES:K_COLS]

    mg = _dot(hb, wmg_ref[...])
    ga_ref[...] = _sigmoid(mg[:, :D_MODEL]).astype(BF16)

    u = _gelu(_dot(hb, wu_ref[...]))
    v = _gelu(_dot(hb, wv_ref[...]))
    mu = jnp.mean(v, axis=-1, keepdims=True)
    vc = v - mu
    var = jnp.mean(vc * vc, axis=-1, keepdims=True)
    vln = (vc * lax.rsqrt(var + LN_EPS) * lng_ref[...] + lnb_ref[...]).astype(BF16)

    row = lax.broadcasted_iota(I32, (CHUNK, CHUNK), 0)
    col = lax.broadcasted_iota(I32, (CHUNK, CHUNK), 1)
    tril = row >= col
    wsp = [jnp.where(tril, wsp_ref[g], 0.0).astype(BF16) for g in range(N_GROUPS_SGU)]
    low_half = col < SGU_GROUP_DIM
    bsp = bsp_ref[...]
    chunks = []
    for c in range(tm // CHUNK):
        rs = slice(c * CHUNK, (c + 1) * CHUNK)
        parts = []
        for p in range(SGU_WIDTH // LANES):
            cs = slice(p * LANES, (p + 1) * LANES)
            vblk = vln[rs, cs]
            mixed = jnp.where(low_half, _dot(wsp[2 * p], vblk), _dot(wsp[2 * p + 1], vblk))
            parts.append(u[rs, cs] * (mixed + bsp[:, cs]))
        chunks.append(jnp.concatenate(parts, axis=1))
    yb = jnp.concatenate(chunks, axis=0).astype(BF16)
    gbyb_ref[...] = (_sigmoid(mg[:, D_MODEL:]) * _dot(yb, wbb_ref[...])).astype(BF16)


def _proj(x2, b, s, gmix, wt, tconst, wk, kconst, wu, wv, wmg, lng, lnb, wsp, bsp, wbb, tm):
    n = x2.shape[0]
    tiles = s // tm
    qtiles = tm // Q_BLOCK
    full = lambda a: pl.BlockSpec(a.shape, lambda i, j: (0,) * a.ndim)
    rows = lambda w: pl.BlockSpec((tm, w), lambda i, j: (i * tiles + j, 0))
    per_g = lambda *blk: pl.BlockSpec((1, N_KV) + blk, lambda i, j: (i, 0, j) + (0,) * (len(blk) - 1))
    ins = (x2, gmix, wt, tconst, wk, kconst, wu, wv, wmg, lng, lnb, wsp, bsp, wbb)
    in_specs = [rows(D_MODEL)] + [full(a) for a in ins[1:]]
    in_specs[5] = pl.BlockSpec((tm, 2 * LANES), lambda i, j: (j, 0))
    nt = s // Q_BLOCK
    return pl.pallas_call(
        _proj_kernel,
        grid=(b, tiles),
        in_specs=in_specs,
        out_specs=[per_g(qtiles, LANES, GQA_R * Q_BLOCK), per_g(qtiles, GATE_ROWS, Q_BLOCK),
                   per_g(qtiles, HEAD_DIM, Q_BLOCK), per_g(qtiles, HEAD_DIM, Q_BLOCK),
                   per_g(tm, 2 * LANES), per_g(tm, LANES),
                   pl.BlockSpec((1, tm, LANES), lambda i, j: (i, j, 0)),
                   pl.BlockSpec((1, tm, LANES), lambda i, j: (i, j, 0)),
                   rows(D_MODEL), rows(D_MODEL)],
        out_shape=[jax.ShapeDtypeStruct((b, N_KV, nt, LANES, GQA_R * Q_BLOCK), BF16),
                   jax.ShapeDtypeStruct((b, N_KV, nt, GATE_ROWS, Q_BLOCK), F32),
                   jax.ShapeDtypeStruct((b, N_KV, nt, HEAD_DIM, Q_BLOCK), BF16),
                   jax.ShapeDtypeStruct((b, N_KV, nt, HEAD_DIM, Q_BLOCK), BF16),
                   jax.ShapeDtypeStruct((b, N_KV, s, 2 * LANES), BF16),
                   jax.ShapeDtypeStruct((b, N_KV, s, LANES), BF16),
                   jax.ShapeDtypeStruct((b, s, LANES), F32),
                   jax.ShapeDtypeStruct((b, s, LANES), F32),
                   jax.ShapeDtypeStruct((n, D_MODEL), BF16),
                   jax.ShapeDtypeStruct((n, D_MODEL), BF16)],
        compiler_params=pltpu.CompilerParams(dimension_semantics=("parallel", "parallel"),
                                             vmem_limit_bytes=VMEM_LIMIT),
        name="proj",
    )(*ins)


def _compress_kernel(xk_ref, xv_ref, pe_ref, w1_ref, b1_ref, w2k_ref, w2vt_ref, kconst_ref, kc_ref, vct_ref):
    nc = kc_ref.shape[2]
    half = CMP_LEN // 2
    hids = []
    for which, x_ref in enumerate((xk_ref, xv_ref)):
        za = jnp.zeros((nc, N_KV * CMP_HID), F32)
        zb = jnp.zeros((nc, N_KV * CMP_HID), F32)
        for l in range(half):
            xl = x_ref[0, pl.ds(l, nc, stride=CMP_STRIDE), :]
            za = za + _dot((xl + pe_ref[which, 0, l:l + 1, :]).astype(BF16), w1_ref[which, 0, l])
            zb = zb + _dot((xl + pe_ref[which, 1, l:l + 1, :]).astype(BF16), w1_ref[which, 1, l])
        hid = _gelu(za + pltpu.roll(zb, nc - 1, 0) + b1_ref[which])
        row = lax.broadcasted_iota(I32, hid.shape, 0)
        hids.append(jnp.where(row < nc - 1, hid, 0.0).astype(BF16))
    for g in range(N_KV):
        cs = slice(g * CMP_HID, (g + 1) * CMP_HID)
        kc_ref[0, g] = (_dot(hids[0][:, cs], w2k_ref[...]) + kconst_ref[...]).astype(BF16)
        vt = _dot_nt(w2vt_ref[...], hids[1][:, cs])
        for c in range(nc // KEY_CHUNK):
            vct_ref[0, g, c] = vt[:, c * KEY_CHUNK:(c + 1) * KEY_CHUNK].astype(BF16)


def _compress(xk, xv, pes, w1, b1, w2k, w2vt, kconst):
    b, s, w = xk.shape
    nc = s // CMP_STRIDE
    full = lambda a: pl.BlockSpec(a.shape, lambda i: (0,) * a.ndim)
    seq = pl.BlockSpec((1, s, w), lambda i: (i, 0, 0))
    return pl.pallas_call(
        _compress_kernel,
        grid=(b,),
        in_specs=[seq, seq, full(pes), full(w1), full(b1), full(w2k), full(w2vt), full(kconst)],
        out_specs=[pl.BlockSpec((1, N_KV, nc, LANES), lambda i: (i, 0, 0, 0)),
                   pl.BlockSpec((1, N_KV, nc // KEY_CHUNK, HEAD_DIM, KEY_CHUNK), lambda i: (i, 0, 0, 0, 0))],
        out_shape=[jax.ShapeDtypeStruct((b, N_KV, nc, LANES), BF16),
                   jax.ShapeDtypeStruct((b, N_KV, nc // KEY_CHUNK, HEAD_DIM, KEY_CHUNK), BF16)],
        compiler_params=pltpu.CompilerParams(dimension_semantics=("parallel",),
                                             vmem_limit_bytes=VMEM_LIMIT),
        name="compress",
    )(xk, xv, pes, w1, b1, w2k, w2vt, kconst)


def _pairs():
    return [slice(pr * 2 * Q_BLOCK, (pr + 1) * 2 * Q_BLOCK) for pr in range(GQA_R // 2)]


def _colmax(blocks):
    part = None
    for x in blocks:
        y = jnp.max(x.reshape(-1, 8, x.shape[-1]), axis=0)
        part = y if part is None else jnp.maximum(part, y)
    return jnp.max(part, axis=0, keepdims=True)


def _colsum(blocks):
    part = None
    for x in blocks:
        y = jnp.sum(x.reshape(-1, 8, x.shape[-1]), axis=0)
        part = y if part is None else part + y
    return jnp.sum(part, axis=0, keepdims=True)


def _softmax_pv(scores, vts, masks, m_ref, l_ref, acc_ref):
    for pr, ls in enumerate(_pairs()):
        sb = [x if masks[u] is None else masks[u](x, ls) for u, x in enumerate(scores[pr])]
        m_old = m_ref[0:1, ls]
        m_new = jnp.maximum(m_old, _colmax(sb))
        alpha = jnp.exp2(m_old - m_new)
        ps = [jnp.exp2(x - m_new) for x in sb]
        pv = _dot(vts[0], ps[0].astype(BF16))
        for u in range(1, len(ps)):
            pv = pv + _dot(vts[u], ps[u].astype(BF16))
        l_ref[0:1, ls] = alpha * l_ref[0:1, ls] + _colsum(ps)
        acc_ref[:, ls] = alpha * acc_ref[:, ls] + pv
        m_ref[0:1, ls] = m_new


def _nsa_kernel(qt_ref, gt_ref, kc_ref, vct_ref, ksa_ref, vst_ref, kwa_ref, vwt_ref, ovt_ref, tri_ref,
                o_ref, qa_ref, s0_ref, s1_ref, m_ref, l_ref, acc_ref, act_ref, *, n_sel, top_n):
    i = pl.program_id(1)
    t0 = i * Q_BLOCK
    width = GQA_R * Q_BLOCK
    n_chunks = ksa_ref.shape[2] // KEY_CHUNK
    n_cchunks = kc_ref.shape[2] // KEY_CHUNK
    n_slots = act_ref.shape[0] // N_KV
    lane = lax.broadcasted_iota(I32, (1, width), 1)
    tq = t0 + (lane & (Q_BLOCK - 1))
    sub = lax.broadcasted_iota(I32, (KEY_CHUNK, 1), 0)
    pairs = _pairs()
    heads = range(N_KV)

    def reset(g):
        m_ref[g] = jnp.full(m_ref.shape[1:], NEG, F32)
        l_ref[g] = jnp.zeros(l_ref.shape[1:], F32)
        acc_ref[g] = jnp.zeros(acc_ref.shape[1:], F32)

    sc = [[[_dot(kc_ref[0, g, u * KEY_CHUNK:(u + 1) * KEY_CHUNK, :], qt_ref[0, g, 0, :, ls])
            for u in range(n_cchunks)] for ls in pairs] for g in heads]
    o_c, imp_t = [], []
    for g in heads:
        ocg, ps = [], []
        for pr, ls in enumerate(pairs):
            sb = []
            for u in range(n_cchunks):
                c_end = (u * KEY_CHUNK + sub) * CMP_STRIDE + (CMP_LEN - 1)
                sb.append(jnp.where(c_end <= tq[:, ls], sc[g][pr][u], NEG))
            m = _colmax(sb)
            pb = [jnp.exp2(x - m) for x in sb]
            l = _colsum(pb)
            w = (tq[:, ls] >= CMP_LEN - 1).astype(F32) / l
            oc = _dot(vct_ref[0, g, 0], pb[0].astype(BF16))
            for u in range(1, n_cchunks):
                oc = oc + _dot(vct_ref[0, g, u], pb[u].astype(BF16))
            ocg.append(oc * w)
            ps.append([p[:, :Q_BLOCK] * w[:, :Q_BLOCK] + p[:, Q_BLOCK:] * w[:, Q_BLOCK:] for p in pb])
        imp2 = jnp.zeros((LANES, 2 * Q_BLOCK), F32)
        for u in range(n_cchunks):
            psu = ps[0][u] + ps[1][u]
            hi = psu.astype(BF16)
            lo = (psu - hi.astype(F32)).astype(BF16)
            imp2 = imp2 + _dot(ovt_ref[u], jnp.concatenate([hi, lo], axis=1))
        o_c.append(jnp.concatenate(ocg, axis=1))
        imp_t.append(imp2[:, :Q_BLOCK] + imp2[:, Q_BLOCK:])

    n_win = WINDOW // KEY_CHUNK
    cls, masks = [], []
    for u in range(n_win + 1):
        c = i - n_win + u
        cls.append(jnp.maximum(c, 0))
        off = jnp.where(c < 0, NEG, 0.0)
        if u == 0:
            edge = tri_ref[0] + off
            masks.append(lambda x, ls, edge=edge: x + edge)
        elif u == n_win:
            masks.append(lambda x, ls: x + tri_ref[1])
        else:
            masks.append(lambda x, ls, off=off: x + off)
    wsc = [[[_dot(kwa_ref[0, g, pl.ds(pl.multiple_of(cl * KEY_CHUNK, KEY_CHUNK), KEY_CHUNK), :],
                  qt_ref[0, g, 0, :, ls]) for cl in cls] for ls in pairs] for g in heads]
    o_w = []
    for g in heads:
        reset(g)
        _softmax_pv(wsc[g], [vwt_ref[0, g, cl] for cl in cls], masks, m_ref.at[g], l_ref.at[g], acc_ref.at[g])
        o_w.append(acc_ref[g] / l_ref[g, 0:1, :])

    j_io = lax.broadcasted_iota(I32, (LANES, Q_BLOCK), 0)
    tl = t0 + lax.broadcasted_iota(I32, (LANES, Q_BLOCK), 1)
    cur = tl >> 6
    forced = (j_io == 0) | (j_io == cur) | (j_io == cur - 1)
    prio = [jnp.where(j_io < n_sel, jnp.where(forced, FORCE, jnp.where(j_io * SEL_BLOCK <= tl, imp_t[g], -1.0)), NEG)
            for g in heads]
    for _ in range(top_n):
        for g in heads:
            m = jnp.max(prio[g], axis=0, keepdims=True)
            idx = jnp.min(jnp.where(prio[g] == m, j_io, LANES), axis=0, keepdims=True)
            prio[g] = jnp.where(j_io == idx, NEG, prio[g])
    sel_t = [jnp.where((prio[g] == NEG) & (j_io < n_sel), 1.0, 0.0) for g in heads]

    n_act = []
    for u in range(act_ref.shape[0]):
        act_ref[u] = jnp.int32(-1)
    for g in heads:
        qa_ref[g, 0:LANES, :] = qt_ref[0, g, 0]
        bias_t = ((sel_t[g] - 1.0) * MASK_BIAS).astype(BF16)
        for h in range(GQA_R):
            qa_ref[g, LANES:2 * LANES, h * Q_BLOCK:(h + 1) * Q_BLOCK] = bias_t
        any_q = jnp.max(sel_t[g], axis=1, keepdims=True)
        any_pair = jnp.maximum(any_q, pltpu.roll(any_q, LANES - 1, 0))
        na = jnp.int32(0)
        for c in range(n_chunks):
            act_ref[g * n_slots + na] = jnp.int32(c)
            na = na + jnp.where((any_pair[2 * c, 0] > 0.0) & (c < i), 1, 0)
        act_ref[g * n_slots + na] = i
        n_act.append(na)

    def sel_scores(g, k, dst):
        for u in range(SEL_SLOTS):
            cl = jnp.maximum(act_ref[g * n_slots + k * SEL_SLOTS + u], 0)
            keys = ksa_ref[0, g, pl.ds(pl.multiple_of(cl * KEY_CHUNK, KEY_CHUNK), KEY_CHUNK), :]
            for ls in pairs:
                dst[g, u, :, ls] = _dot(keys, qa_ref[g, :, ls])

    def sel_step(g, k, cur_s, nxt_s, last, ahead):
        if ahead:
            sel_scores(g, k + 1, nxt_s)
        vts, msk = [], []
        for u in range(SEL_SLOTS):
            c = act_ref[g * n_slots + k * SEL_SLOTS + u]
            vts.append(vst_ref[0, g, jnp.maximum(c, 0)])
            if last:
                kpos = jnp.where(c < 0, n_chunks * KEY_CHUNK, c * KEY_CHUNK) + sub
                msk.append(lambda x, ls, kpos=kpos: jnp.where(kpos <= tq[:, ls], x, -MASK_BIAS))
            else:
                msk.append(None)
        scores = [[cur_s[g, u, :, ls] for u in range(SEL_SLOTS)] for ls in pairs]
        _softmax_pv(scores, vts, msk, m_ref.at[g], l_ref.at[g], acc_ref.at[g])

    for g in heads:
        reset(g)
        sel_scores(g, 0, s0_ref)
    n_steps = [(n_act[g] + SEL_SLOTS) // SEL_SLOTS for g in heads]
    bufs = ((s0_ref, s1_ref), (s1_ref, s0_ref))

    def sel_body(k, carry):
        last = [k >= n_steps[g] - 1 for g in heads]
        for odd, (cur_s, nxt_s) in enumerate(bufs):
            for l0 in (False, True):
                for l1 in (False, True):
                    @pl.when((k % 2 == odd) & (last[0] == l0) & (last[1] == l1))
                    def _():
                        ahead = not (l0 and l1)
                        sel_step(0, k, cur_s, nxt_s, l0, ahead)
                        sel_step(1, k, cur_s, nxt_s, l1, ahead)

        return carry

    lax.fori_loop(0, jnp.maximum(n_steps[0], n_steps[1]), sel_body, 0)
    o_s = [acc_ref[g] / l_ref[g, 0:1, :] for g in heads]

    for g in heads:
        gt = gt_ref[0, g, 0]

        def gate(br):
            return jnp.concatenate([gt[br * GQA_R + h:br * GQA_R + h + 1, :] for h in range(GQA_R)], axis=1)

        o = gate(0) * o_c[g] + gate(1) * o_s[g] + gate(2) * o_w[g]
        o_ref[0, :, g * GQA_R * HEAD_DIM:(g + 1) * GQA_R * HEAD_DIM] = jnp.concatenate(
            [o[:, h * Q_BLOCK:(h + 1) * Q_BLOCK].T for h in range(GQA_R)], axis=1).astype(BF16)


def _nsa(qt, gt, kc, vct, ksa, vst, kwa, vwt, ovt, tri, n_sel):
    b, g, nt, _, width = qt.shape
    s = ksa.shape[2]
    tile = lambda a: pl.BlockSpec((1, g, 1) + a.shape[3:], lambda i, k: (i, 0, k, 0, 0))
    whole = lambda a: pl.BlockSpec((1,) + a.shape[1:], lambda i, k: (i,) + (0,) * (a.ndim - 1))
    const = lambda a: pl.BlockSpec(a.shape, lambda i, k: (0,) * a.ndim)
    kern = functools.partial(_nsa_kernel, n_sel=n_sel, top_n=min(SEL_TOPN, n_sel))
    return pl.pallas_call(
        kern,
        grid=(b, nt),
        in_specs=[tile(qt), tile(gt), whole(kc), whole(vct), whole(ksa), whole(vst), whole(kwa), whole(vwt),
                  const(ovt), const(tri)],
        out_specs=pl.BlockSpec((1, Q_BLOCK, ATTN_WIDTH), lambda i, k: (i, k, 0)),
        out_shape=jax.ShapeDtypeStruct((b, s, ATTN_WIDTH), BF16),
        scratch_shapes=[pltpu.VMEM((g, 2 * LANES, width), BF16),
                        pltpu.VMEM((g, SEL_SLOTS, KEY_CHUNK, width), F32),
                        pltpu.VMEM((g, SEL_SLOTS, KEY_CHUNK, width), F32),
                        pltpu.VMEM((g, 8, width), F32),
                        pltpu.VMEM((g, 8, width), F32),
                        pltpu.VMEM((g, HEAD_DIM, width), F32),
                        pltpu.SMEM((g * (s // KEY_CHUNK + 2 * SEL_SLOTS),), I32)],
        compiler_params=pltpu.CompilerParams(dimension_semantics=("parallel", "arbitrary"),
                                             vmem_limit_bytes=VMEM_LIMIT),
        name="nsa",
    )(qt, gt, kc, vct, ksa, vst, kwa, vwt, ovt, tri)


def _merge_kernel(x_ref, ya_ref, ga_ref, gbyb_ref, wba_ref, wout_ref, gmoe_ref, wr_ref, br_ref, ltri_ref,
                  x1_ref, hm_ref, rw_ref, ri_ref, cnt_ref, carry_ref):
    tm = x_ref.shape[0]

    @pl.when(pl.program_id(0) == 0)
    def _():
        carry_ref[...] = jnp.zeros_like(carry_ref)

    merged = ga_ref[...].astype(F32) * _dot(ya_ref[...], wba_ref[...]) + gbyb_ref[...].astype(F32)
    x1 = x_ref[...] + _dot(merged.astype(BF16), wout_ref[...])
    x1_ref[...] = x1
    hm = _rms(x1, gmoe_ref[...])
    for a in range(ROW_TILE):
        hm_ref[pl.ds(a, tm, stride=ROW_TILE), :] = hm[:, a * LANES:(a + 1) * LANES]

    hh = hm.astype(BF16)
    hl = (hm - hh.astype(F32)).astype(BF16)
    both = _dot(hh, wr_ref[...])
    logits = both[:, :LANES] + both[:, LANES:] + _dot(hl, wr_ref[:, :LANES]) + br_ref[...]
    lane = lax.broadcasted_iota(I32, (tm, LANES), 1)
    lg = jnp.where(lane < N_EXPERTS, logits, NEG)
    vals, idxs = [], []
    for _ in range(TOP_K):
        m = jnp.max(lg, axis=1, keepdims=True)
        idx = jnp.min(jnp.where(lg == m, lane, LANES), axis=1, keepdims=True)
        vals.append(m)
        idxs.append(idx)
        lg = jnp.where(lane == idx, NEG, lg)
    ex = [jnp.exp(v - vals[0]) for v in vals]
    den = ex[0] + ex[1] + ex[2] + ex[3]

    hits = [lane == idx for idx in idxs]
    multi = jnp.zeros((tm, LANES), F32)
    for h in hits:
        multi = jnp.where(h, 1.0, multi)
    carry = carry_ref[0:1, :]
    cum = _dot(ltri_ref[...], multi.astype(BF16)) + carry
    rw = jnp.zeros((tm, LANES), F32)
    ri = jnp.zeros((tm, LANES), I32)
    for k in range(TOP_K):
        rank = jnp.sum(jnp.where(hits[k], cum, 0.0), axis=1, keepdims=True).astype(I32)
        rw = jnp.where(lane == k, ex[k] / den, rw)
        ri = jnp.where(lane == k, idxs[k], jnp.where(lane == TOP_K + k, rank, ri))
    rw_ref[...] = rw
    ri_ref[...] = ri
    new_carry = carry + jnp.sum(multi, axis=0, keepdims=True)
    carry_ref[...] = jnp.broadcast_to(new_carry, carry_ref.shape)
    cnt_ref[...] = jnp.broadcast_to(new_carry, cnt_ref.shape)


def _merge(x2, ya, ga, gbyb, wba, wout, gmoe, wr, br, tm):
    n = x2.shape[0]
    full = lambda a: pl.BlockSpec(a.shape, lambda i: (0,) * a.ndim)
    rows = lambda w: pl.BlockSpec((tm, w), lambda i: (i, 0))
    ltri = jnp.asarray(np.tril(np.ones((tm, tm), np.float32), -1), BF16)
    return pl.pallas_call(
        _merge_kernel,
        grid=(n // tm,),
        in_specs=[rows(D_MODEL), rows(ATTN_WIDTH), rows(D_MODEL), rows(D_MODEL),
                  full(wba), full(wout), full(gmoe), full(wr), full(br), full(ltri)],
        out_specs=[rows(D_MODEL), pl.BlockSpec((tm * ROW_TILE, LANES), lambda i: (i, 0)), rows(LANES), rows(LANES),
                   pl.BlockSpec((8, LANES), lambda i: (0, 0))],
        out_shape=[jax.ShapeDtypeStruct((n, D_MODEL), F32),
                   jax.ShapeDtypeStruct((n * ROW_TILE, LANES), F32),
                   jax.ShapeDtypeStruct((n, LANES), F32),
                   jax.ShapeDtypeStruct((n, LANES), I32),
                   jax.ShapeDtypeStruct((8, LANES), F32)],
        scratch_shapes=[pltpu.VMEM((8, LANES), F32)],
        compiler_params=pltpu.CompilerParams(dimension_semantics=("arbitrary",),
                                             vmem_limit_bytes=VMEM_LIMIT),
        name="merge",
    )(x2, ya, ga, gbyb, wba, wout, gmoe, wr, br, ltri)


def _tile_copy(src, i, dst, d, sem):
    return pltpu.make_async_copy(src.at[pl.ds(pl.multiple_of(i * ROW_TILE, ROW_TILE), ROW_TILE)],
                                 dst.at[pl.ds(pl.multiple_of(d * ROW_TILE, ROW_TILE), ROW_TILE)], sem)


def _dispatch_kernel(seg_ref, dest_ref, hm_ref, xs_ref, zero_ref, sem, zsem, *, n_pad):
    tm = hm_ref.shape[0] // ROW_TILE

    @pl.when(pl.program_id(0) == 0)
    def _():
        zero_ref[...] = jnp.zeros_like(zero_ref)

        def seg(e, c):
            def fill(r, c2):
                _tile_copy(zero_ref, 0, xs_ref, r, zsem).start()
                return c2
            return lax.fori_loop(seg_ref[0, e], seg_ref[1, e], fill, c)

        lax.fori_loop(0, N_EXPERTS + 1, seg, 0)
        pad_rows = xs_ref.at[pl.ds(0, n_pad * ROW_TILE)]
        pltpu.make_async_copy(pad_rows, pad_rows, zsem).wait()

    def issue(r, c):
        slots = [dest_ref[r * TOP_K + k] for k in range(TOP_K)]
        for k in range(TOP_K):
            _tile_copy(hm_ref, r, xs_ref, slots[k], sem).start(priority=k % 2)
        return c

    lax.fori_loop(0, tm, issue, 0)
    for k in range(TOP_K):
        pltpu.make_async_copy(hm_ref, xs_ref.at[pl.ds(0, tm * ROW_TILE)], sem).wait()


def _dispatch(seg, dest_flat, hm, n_slots, tm):
    n = hm.shape[0] // ROW_TILE
    kern = functools.partial(_dispatch_kernel, n_pad=n_slots - n * TOP_K)
    return pl.pallas_call(
        kern,
        grid_spec=pltpu.PrefetchScalarGridSpec(
            num_scalar_prefetch=1,
            grid=(n // tm,),
            in_specs=[pl.BlockSpec((tm * TOP_K,), lambda i, sg: (i,), memory_space=pltpu.SMEM),
                      pl.BlockSpec((tm * ROW_TILE, LANES), lambda i, sg: (i, 0))],
            out_specs=pl.BlockSpec(memory_space=pl.ANY),
            scratch_shapes=[pltpu.VMEM((ROW_TILE, LANES), F32),
                            pltpu.SemaphoreType.DMA(()), pltpu.SemaphoreType.DMA(())]),
        out_shape=jax.ShapeDtypeStruct((n_slots * ROW_TILE, LANES), F32),
        compiler_params=pltpu.CompilerParams(dimension_semantics=("arbitrary",),
                                             has_side_effects=True),
        name="dispatch",
    )(seg, dest_flat, hm)


def _expert_kernel(be_ref, nu_ref, xs_ref, wgu_ref, bgu_ref, wd_ref, bd_ref, y_ref, wgu_bf, wd_bf):
    i = pl.program_id(0)

    @pl.when(i >= nu_ref[0])
    def _():
        y_ref[...] = jnp.zeros_like(y_ref)

    @pl.when((i == 0) | (be_ref[i] != be_ref[jnp.maximum(i - 1, 0)]))
    def _():
        wgu_bf[...] = wgu_ref[0].astype(BF16)
        wd_bf[...] = wd_ref[0].astype(BF16)

    @pl.when(i < nu_ref[0])
    def _():
        x = jnp.concatenate([xs_ref[pl.ds(a, MOE_BLOCK, stride=ROW_TILE), :] for a in range(ROW_TILE)], axis=1)
        gu = _dot(x.astype(BF16), wgu_bf[...]) + bgu_ref[0]
        gate = jnp.minimum(gu[:, :D_FF], SWIGLU_LIMIT)
        up = jnp.clip(gu[:, D_FF:], -SWIGLU_LIMIT, SWIGLU_LIMIT)
        act = gate * _sigmoid(SWIGLU_ALPHA * gate) * (up + 1.0)
        y = _dot(act.astype(BF16), wd_bf[...]) + bd_ref[0]
        for a in range(ROW_TILE):
            y_ref[pl.ds(a, MOE_BLOCK, stride=ROW_TILE), :] = y[:, a * LANES:(a + 1) * LANES]


def _experts(blk_expert, n_used, xs, wgu, bgu, wd, bd):
    n_blocks = xs.shape[0] // (MOE_BLOCK * ROW_TILE)
    blk = lambda i, be, nu: (jnp.minimum(i, nu[0] - 1), 0)
    exp3 = lambda i, be, nu: (be[jnp.minimum(i, nu[0] - 1)], 0, 0)
    return pl.pallas_call(
        _expert_kernel,
        grid_spec=pltpu.PrefetchScalarGridSpec(
            num_scalar_prefetch=2,
            grid=(n_blocks,),
            in_specs=[pl.BlockSpec((MOE_BLOCK * ROW_TILE, LANES), blk),
                      pl.BlockSpec((1, D_MODEL, 2 * D_FF), exp3),
                      pl.BlockSpec((1, 1, 2 * D_FF), exp3),
                      pl.BlockSpec((1, D_FF, D_MODEL), exp3),
                      pl.BlockSpec((1, 1, D_MODEL), exp3)],
            out_specs=pl.BlockSpec((MOE_BLOCK * ROW_TILE, LANES), lambda i, be, nu: (i, 0)),
            scratch_shapes=[pltpu.VMEM((D_MODEL, 2 * D_FF), BF16), pltpu.VMEM((D_FF, D_MODEL), BF16)]),
        out_shape=jax.ShapeDtypeStruct(xs.shape, F32),
        compiler_params=pltpu.CompilerParams(dimension_semantics=("arbitrary",),
                                             vmem_limit_bytes=VMEM_LIMIT),
        name="experts",
    )(blk_expert, n_used, xs, wgu, bgu, wd, bd)


def _combine_kernel(dest_ref, dest_next_ref, rw_ref, x1_ref, gfin_ref, y_ref, o_ref, ybuf0, ybuf1, sem0, sem1):
    tm = x1_ref.shape[0]
    i = pl.program_id(0)

    def gather(dst_ref, ybuf, sem):
        def issue(r, c):
            slots = [dst_ref[r * TOP_K + k] for k in range(TOP_K)]
            for k in range(TOP_K):
                _tile_copy(y_ref, slots[k], ybuf.at[k], r, sem).start(priority=k % 2)
            return c

        lax.fori_loop(0, tm, issue, 0)

    def finish(ybuf, sem):
        for k in range(TOP_K):
            pltpu.make_async_copy(y_ref.at[pl.ds(0, tm * ROW_TILE)], ybuf.at[k], sem).wait()
        rw = rw_ref[...]
        x1 = x1_ref[...]
        cols = []
        for a in range(ROW_TILE):
            acc = x1[:, a * LANES:(a + 1) * LANES]
            for k in range(TOP_K):
                acc = acc + rw[:, k:k + 1] * ybuf[k, pl.ds(a, tm, stride=ROW_TILE), :]
            cols.append(acc)
        o_ref[...] = _rms(jnp.concatenate(cols, axis=1), gfin_ref[...])

    @pl.when(i == 0)
    def _():
        gather(dest_ref, ybuf0, sem0)

    for parity, (cur, nxt) in enumerate((((ybuf0, sem0), (ybuf1, sem1)), ((ybuf1, sem1), (ybuf0, sem0)))):
        @pl.when(i % 2 == parity)
        def _():
            @pl.when(i + 1 < pl.num_programs(0))
            def _():
                gather(dest_next_ref, *nxt)

            finish(*cur)


def _combine(dest_flat, rw, x1, gfin, y, tm):
    n = x1.shape[0]
    steps = n // tm
    ybuf = pltpu.VMEM((TOP_K, tm * ROW_TILE, LANES), F32)
    return pl.pallas_call(
        _combine_kernel,
        grid=(steps,),
        in_specs=[pl.BlockSpec((tm * TOP_K,), lambda i: (i,), memory_space=pltpu.SMEM),
                  pl.BlockSpec((tm * TOP_K,), lambda i: (jnp.minimum(i + 1, steps - 1),), memory_space=pltpu.SMEM),
                  pl.BlockSpec((tm, LANES), lambda i: (i, 0)),
                  pl.BlockSpec((tm, D_MODEL), lambda i: (i, 0)),
                  pl.BlockSpec((1, D_MODEL), lambda i: (0, 0)),
                  pl.BlockSpec(memory_space=pl.ANY)],
        out_specs=pl.BlockSpec((tm, D_MODEL), lambda i: (i, 0)),
        out_shape=jax.ShapeDtypeStruct((n, D_MODEL), F32),
        scratch_shapes=[ybuf, ybuf, pltpu.SemaphoreType.DMA(()), pltpu.SemaphoreType.DMA(())],
        compiler_params=pltpu.CompilerParams(dimension_semantics=("arbitrary",),
                                             vmem_limit_bytes=VMEM_LIMIT),
        name="combine",
    )(dest_flat, dest_flat, rw, x1, gfin, y)


def _overlap_t(nc, n_cmp, n_sel):
    cs = np.arange(n_cmp)[None, :] * CMP_STRIDE
    ss = np.arange(n_sel)[:, None] * SEL_BLOCK
    ov = np.clip(np.minimum(cs + CMP_LEN, ss + SEL_BLOCK) - np.maximum(cs, ss), 0, None) / CMP_LEN
    out = np.zeros((LANES, nc), np.float32)
    out[:n_sel, :n_cmp] = ov
    return jnp.asarray(out, BF16)


def _layer(x2, b, s, g_mix, w_in, w_ck1, b_ck1, w_ck2, pe_k, w_cv1, b_cv1, w_cv2, pe_v,
           sgu_ln_g, sgu_ln_b, w_spatial, b_spatial, w_branch_a, w_branch_b, w_out,
           g_moe, w_router, b_router, w_gate_up, b_gate_up, w_down, b_down):
    n = b * s
    nc = s // CMP_STRIDE
    n_cmp = (s - CMP_LEN) // CMP_STRIDE + 1
    n_sel = s // SEL_BLOCK
    assert nc % KEY_CHUNK == 0 and n_sel <= LANES and n_cmp == nc - 1
    tm = 512
    assert s % tm == 0

    p0 = ATTN_WIDTH
    p1 = p0 + 6 * KV_WIDTH
    p2 = p1 + N_NSA_BRANCH * N_HEADS
    p3 = p2 + SGU_WIDTH
    p4 = p3 + SGU_WIDTH
    zpad = lambda a, w: jnp.pad(a, ((0, 0),) * (a.ndim - 1) + ((0, w - a.shape[-1]),))
    wq = zpad((w_in[:, :p0] * (HEAD_DIM ** -0.5 * LOG2E)).reshape(D_MODEL, N_HEADS, HEAD_DIM), LANES)
    wkv = w_in[:, p0:p1].reshape(D_MODEL, 6, N_KV, HEAD_DIM)
    wng = w_in[:, p1:p2].reshape(D_MODEL, N_KV, GQA_R, N_NSA_BRANCH).transpose(0, 1, 3, 2)
    wng = zpad(wng.reshape(D_MODEL, N_KV, N_NSA_BRANCH * GQA_R), GATE_ROWS)
    wt = jnp.concatenate([wq.reshape(D_MODEL, -1), wkv[:, 3].reshape(D_MODEL, -1),
                          wkv[:, 5].reshape(D_MODEL, -1), wng.reshape(D_MODEL, -1)], axis=1).T.astype(BF16)
    slopes = 2.0 ** (-8.0 * np.arange(1, N_HEADS + 1) / N_HEADS)
    tcol = np.zeros((T_ROWS, 1), np.float32)
    head_rows = T_Q + np.arange(N_HEADS) * LANES + HEAD_DIM
    bf16_round = lambda a: a.astype(BF16).astype(np.float32)
    for k, coef in enumerate((slopes * SEL_BLOCK * LOG2E, slopes * LOG2E)):
        hi = bf16_round(coef.astype(np.float32))
        tcol[head_rows + 2 * k, 0] = hi
        tcol[head_rows + 2 * k + 1, 0] = bf16_round(coef.astype(np.float32) - hi)
    tconst = jnp.asarray(np.broadcast_to(tcol, (T_ROWS, 2 * Q_BLOCK)))
    wk = jnp.concatenate([zpad(wkv[:, 2, g], 2 * LANES) for g in range(N_KV)]
                         + [zpad(wkv[:, 4, g], LANES) for g in range(N_KV)]
                         + [wkv[:, 0].reshape(D_MODEL, -1), wkv[:, 1].reshape(D_MODEL, -1)], axis=1).astype(BF16)
    pos = np.arange(s)
    kc_np = np.zeros((s, 2 * LANES), np.float32)
    kc_np[:, HEAD_DIM:HEAD_DIM + 2] = (pos // SEL_BLOCK)[:, None]
    kc_np[:, HEAD_DIM + 2:HEAD_DIM + 4] = (pos % SEL_BLOCK)[:, None]
    kc_np[pos, LANES + pos // SEL_BLOCK] = 1.0
    kconst = jnp.asarray(kc_np, BF16)
    wu = w_in[:, p2:p3].astype(BF16)
    wv = w_in[:, p3:p4].astype(BF16)
    wmg = w_in[:, p4:].astype(BF16)
    bsp = jnp.repeat(b_spatial.T, SGU_GROUP_DIM, axis=1)

    qt, gt, vst, vwt, ksa, kwa, kcx, vcx, ga, gbyb = _proj(
        x2, b, s, g_mix[None], wt, tconst, wk, kconst, wu, wv, wmg, sgu_ln_g[None], sgu_ln_b[None],
        w_spatial, bsp, w_branch_b.astype(BF16), tm)

    half = CMP_LEN // 2
    eye = jnp.eye(N_KV, dtype=F32)[None, None, :, None, :, None]
    bdiag = lambda w: (w.reshape(2, half, 1, HEAD_DIM, 1, CMP_HID) * eye).reshape(
        2, half, N_KV * HEAD_DIM, N_KV * CMP_HID)
    w1 = jnp.stack([bdiag(w_ck1), bdiag(w_cv1)]).astype(BF16)
    pes = jnp.stack([jnp.tile(pe_k.reshape(2, half, HEAD_DIM), (1, 1, N_KV)),
                     jnp.tile(pe_v.reshape(2, half, HEAD_DIM), (1, 1, N_KV))])
    b1 = jnp.stack([jnp.tile(b_ck1, N_KV), jnp.tile(b_cv1, N_KV)])[:, None, :]
    blk_n = np.arange(nc)
    cc_np = np.zeros((nc, LANES), np.float32)
    cc_np[:, HEAD_DIM:HEAD_DIM + 2] = (blk_n // (SEL_BLOCK // CMP_STRIDE))[:, None]
    cc_np[:, HEAD_DIM + 2:HEAD_DIM + 4] = (blk_n % (SEL_BLOCK // CMP_STRIDE) * CMP_STRIDE)[:, None]
    kc, vct = _compress(kcx, vcx, pes, w1, b1, zpad(w_ck2, LANES).astype(BF16), w_cv2.T.astype(BF16),
                        jnp.asarray(cc_np))

    ovt = _overlap_t(nc, n_cmp, n_sel).reshape(LANES, nc // KEY_CHUNK, KEY_CHUNK).transpose(1, 0, 2)
    a_io, q_io = np.meshgrid(np.arange(KEY_CHUNK), np.arange(Q_BLOCK), indexing="ij")
    tri = np.stack([np.where(a_io > q_io, 0.0, NEG), np.where(a_io <= q_io, 0.0, NEG)]).astype(np.float32)
    tri = jnp.asarray(np.tile(tri, (1, 1, 2)))
    ya = _nsa(qt, gt, kc, vct, ksa, vst, kwa, vwt, ovt, tri, n_sel).reshape(n, ATTN_WIDTH)

    wr = jnp.pad(w_router, ((0, 0), (0, LANES - N_EXPERTS)))
    wrh = wr.astype(BF16)
    wr2 = jnp.concatenate([wrh, (wr - wrh.astype(F32)).astype(BF16)], axis=1)
    br = jnp.pad(b_router, (0, LANES - N_EXPERTS))[None]
    x1, hm, rw, ri, cnt = _merge(x2, ya, ga, gbyb, w_branch_a.astype(BF16), w_out.astype(BF16),
                                 g_moe[None], wr2, br, tm)

    counts = cnt[0, :N_EXPERTS].astype(I32)
    padded = (counts + MOE_BLOCK - 1) // MOE_BLOCK * MOE_BLOCK
    pad_end = jnp.cumsum(padded)
    pad_start = pad_end - padded
    dest = (pad_start[ri[:, :TOP_K]] + ri[:, TOP_K:2 * TOP_K]).reshape(-1)
    n_blocks = -(-(n * TOP_K) // MOE_BLOCK) + N_EXPERTS
    blk_start = jnp.arange(n_blocks, dtype=I32) * MOE_BLOCK
    blk_expert = jnp.minimum(jnp.sum((pad_end[None, :] <= blk_start[:, None]).astype(I32), axis=1),
                             N_EXPERTS - 1)
    n_used = (pad_end[-1:] // MOE_BLOCK).astype(I32)
    n_slots = n_blocks * MOE_BLOCK
    seg = jnp.stack([jnp.concatenate([pad_start + counts, pad_end[-1:]]),
                     jnp.concatenate([pad_end, jnp.full((1,), n_slots, I32)])]).astype(I32)

    tg = 256
    xs = _dispatch(seg, dest, hm, n_slots, tg)
    y = _experts(blk_expert, n_used, xs, w_gate_up, b_gate_up[:, None, :], w_down, b_down[:, None, :])
    return dest, rw, x1, y, tg


def kernel(x, g_mix, w_in, w_ck1, b_ck1, w_ck2, pe_k, w_cv1, b_cv1, w_cv2, pe_v, sgu_ln_g, sgu_ln_b, w_spatial, b_spatial, w_branch_a, w_branch_b, w_out, g_moe, w_router, b_router, w_gate_up, b_gate_up, w_down, b_down, g_final):
    b, s, d = x.shape
    assert g_mix.shape[0] == 1, "the final rmsnorm is fused into the single layer's combine step"
    l = 0
    dest, rw, x1, y, tg = _layer(
        x.reshape(b * s, d), b, s, g_mix[l], w_in[l], w_ck1[l], b_ck1[l], w_ck2[l], pe_k[l], w_cv1[l],
        b_cv1[l], w_cv2[l], pe_v[l], sgu_ln_g[l], sgu_ln_b[l], w_spatial[l], b_spatial[l], w_branch_a[l],
        w_branch_b[l], w_out[l], g_moe[l], w_router[l], b_router[l], w_gate_up[l], b_gate_up[l],
        w_down[l], b_down[l])
    return _combine(dest, rw, x1, g_final[None], y, tg).reshape(b, s, d)
```

```python
import functools

import numpy as np
import jax
import jax.numpy as jnp
from jax import lax
from jax.experimental import pallas as pl
from jax.experimental.pallas import tpu as pltpu

D_MODEL = 1024
N_HEADS = 8
HEAD_DIM = 64
N_KV = 2
GQA_R = N_HEADS // N_KV
ATTN_WIDTH = N_HEADS * HEAD_DIM
KV_WIDTH = N_KV * HEAD_DIM
CMP_LEN = 32
CMP_STRIDE = 16
CMP_HID = 128
SEL_BLOCK = 64
SEL_TOPN = 16
WINDOW = 512
Q_BLOCK = 128
N_NSA_BRANCH = 3
SGU_WIDTH = 512
N_GROUPS_SGU = 8
SGU_GROUP_DIM = SGU_WIDTH // N_GROUPS_SGU
CHUNK = 128
N_EXPERTS = 32
TOP_K = 4
D_FF = D_MODEL
SWIGLU_LIMIT = 7.0
SWIGLU_ALPHA = 1.702
MOE_BLOCK = 512
RMS_EPS = 1e-5
LN_EPS = 1e-5
NEG = -1e30
FORCE = 1e4

LANES = 128
ROW_TILE = D_MODEL // LANES
MASK_BIAS = 1e9
LOG2E = float(np.log2(np.e))
KEY_CHUNK = 128
SEL_SLOTS = 4
GATE_ROWS = 16
TOKEN_TILE = 512
ROUTE_TILE = 256
VMEM_LIMIT = 56 * 1024 * 1024

F32 = jnp.float32
BF16 = jnp.bfloat16
I32 = jnp.int32

_NT = (((1,), (1,)), ((), ()))


def _dot(a, b):
    return jnp.dot(a, b, preferred_element_type=F32)


def _dot_nt(a, b):
    return lax.dot_general(a, b, _NT, preferred_element_type=F32)


def _sigmoid(x):
    return 1.0 / (1.0 + jnp.exp(-x))


def _gelu(x):
    c = np.float32(np.sqrt(2.0 / np.pi))
    return 0.5 * x * (1.0 + jnp.tanh(c * (x + 0.044715 * (x * x * x))))


def _rms(x, g):
    return x * lax.rsqrt(jnp.mean(x * x, axis=-1, keepdims=True) + RMS_EPS) * g


T_Q = 0
T_VS = T_Q + N_HEADS * LANES
T_VW = T_VS + N_KV * HEAD_DIM
T_GATE = T_VW + N_KV * HEAD_DIM
T_ROWS = T_GATE + N_KV * GATE_ROWS
K_SEL = 0
K_WIN = K_SEL + N_KV * 2 * LANES
K_CMP = K_WIN + N_KV * LANES
K_COLS = K_CMP + 4 * HEAD_DIM


def _proj_kernel(x_ref, gmix_ref, wt_ref, tconst_ref, wk_ref, kconst_ref, wu_ref, wv_ref, wmg_ref,
                 lng_ref, lnb_ref, wsp_ref, bsp_ref, wbb_ref,
                 qt_ref, gt_ref, vst_ref, vwt_ref, ksa_ref, kwa_ref, kcx_ref, vcx_ref, ga_ref, gbyb_ref):
    tm = x_ref.shape[0]
    hb = _rms(x_ref[...], gmix_ref[...]).astype(BF16)

    for h in range(tm // (2 * Q_BLOCK)):
        t = _dot_nt(wt_ref[...], hb[h * 2 * Q_BLOCK:(h + 1) * 2 * Q_BLOCK]) + tconst_ref[...]
        for jj in range(2):
            j = 2 * h + jj
            ls = slice(jj * Q_BLOCK, (jj + 1) * Q_BLOCK)
            for g in range(N_KV):
                for r in range(GQA_R):
                    r0 = T_Q + (g * GQA_R + r) * LANES
                    qt_ref[0, g, j, :, r * Q_BLOCK:(r + 1) * Q_BLOCK] = t[r0:r0 + LANES, ls].astype(BF16)
                vst_ref[0, g, j] = t[T_VS + g * HEAD_DIM:T_VS + (g + 1) * HEAD_DIM, ls].astype(BF16)
                vwt_ref[0, g, j] = t[T_VW + g * HEAD_DIM:T_VW + (g + 1) * HEAD_DIM, ls].astype(BF16)
                gt_ref[0, g, j] = _sigmoid(t[T_GATE + g * GATE_ROWS:T_GATE + (g + 1) * GATE_ROWS, ls])

    kk = _dot(hb, wk_ref[...])
    kconst = kconst_ref[...].astype(F32)
    for g in range(N_KV):
        c0 = K_SEL + g * 2 * LANES
        ksa_ref[0, g] = (kk[:, c0:c0 + 2 * LANES] + kconst).astype(BF16)
        c0 = K_WIN + g * LANES
        kwa_ref[0, g] = (kk[:, c0:c0 + LANES] + kconst[:, :LANES]).astype(BF16)
    kcx_ref[0] = kk[:, K_CMP:K_CMP + LANES]
    vcx_ref[0] = kk[:, K_CMP + LANES:K_COLS]

    mg = _dot(hb, wmg_ref[...])
    ga_ref[...] = _sigmoid(mg[:, :D_MODEL]).astype(BF16)

    u = _gelu(_dot(hb, wu_ref[...]))
    v = _gelu(_dot(hb, wv_ref[...]))
    mu = jnp.mean(v, axis=-1, keepdims=True)
    vc = v - mu
    var = jnp.mean(vc * vc, axis=-1, keepdims=True)
    vln = (vc * lax.rsqrt(var + LN_EPS) * lng_ref[...] + lnb_ref[...]).astype(BF16)

    row = lax.broadcasted_iota(I32, (CHUNK, CHUNK), 0)
    col = lax.broadcasted_iota(I32, (CHUNK, CHUNK), 1)
    tril = row >= col
    wsp = [jnp.where(tril, wsp_ref[g], 0.0).astype(BF16) for g in range(N_GROUPS_SGU)]
    low_half = col < SGU_GROUP_DIM
    bsp = bsp_ref[...]
    chunks = []
    for c in range(tm // CHUNK):
        rs = slice(c * CHUNK, (c + 1) * CHUNK)
        parts = []
        for p in range(SGU_WIDTH // LANES):
            cs = slice(p * LANES, (p + 1) * LANES)
            vblk = vln[rs, cs]
            mixed = jnp.where(low_half, _dot(wsp[2 * p], vblk), _dot(wsp[2 * p + 1], vblk))
            parts.append(u[rs, cs] * (mixed + bsp[:, cs]))
        chunks.append(jnp.concatenate(parts, axis=1))
    yb = jnp.concatenate(chunks, axis=0).astype(BF16)
    gbyb_ref[...] = (_sigmoid(mg[:, D_MODEL:]) * _dot(yb, wbb_ref[...])).astype(BF16)


def _proj(x2, b, s, gmix, wt, tconst, wk, kconst, wu, wv, wmg, lng, lnb, wsp, bsp, wbb, tm):
    n = x2.shape[0]
    tiles = s // tm
    qtiles = tm // Q_BLOCK
    full = lambda a: pl.BlockSpec(a.shape, lambda i, j: (0,) * a.ndim)
    rows = lambda w: pl.BlockSpec((tm, w), lambda i, j: (i * tiles + j, 0))
    per_g = lambda *blk: pl.BlockSpec((1, N_KV) + blk, lambda i, j: (i, 0, j) + (0,) * (len(blk) - 1))
    ins = (x2, gmix, wt, tconst, wk, kconst, wu, wv, wmg, lng, lnb, wsp, bsp, wbb)
    in_specs = [rows(D_MODEL)] + [full(a) for a in ins[1:]]
    in_specs[5] = pl.BlockSpec((tm, 2 * LANES), lambda i, j: (j, 0))
    nt = s // Q_BLOCK
    return pl.pallas_call(
        _proj_kernel,
        grid=(b, tiles),
        in_specs=in_specs,
        out_specs=[per_g(qtiles, LANES, GQA_R * Q_BLOCK), per_g(qtiles, GATE_ROWS, Q_BLOCK),
                   per_g(qtiles, HEAD_DIM, Q_BLOCK), per_g(qtiles, HEAD_DIM, Q_BLOCK),
                   per_g(tm, 2 * LANES), per_g(tm, LANES),
                   pl.BlockSpec((1, tm, LANES), lambda i, j: (i, j, 0)),
                   pl.BlockSpec((1, tm, LANES), lambda i, j: (i, j, 0)),
                   rows(D_MODEL), rows(D_MODEL)],
        out_shape=[jax.ShapeDtypeStruct((b, N_KV, nt, LANES, GQA_R * Q_BLOCK), BF16),
                   jax.ShapeDtypeStruct((b, N_KV, nt, GATE_ROWS, Q_BLOCK), F32),
                   jax.ShapeDtypeStruct((b, N_KV, nt, HEAD_DIM, Q_BLOCK), BF16),
                   jax.ShapeDtypeStruct((b, N_KV, nt, HEAD_DIM, Q_BLOCK), BF16),
                   jax.ShapeDtypeStruct((b, N_KV, s, 2 * LANES), BF16),
                   jax.ShapeDtypeStruct((b, N_KV, s, LANES), BF16),
                   jax.ShapeDtypeStruct((b, s, LANES), F32),
                   jax.ShapeDtypeStruct((b, s, LANES), F32),
                   jax.ShapeDtypeStruct((n, D_MODEL), BF16),
                   jax.ShapeDtypeStruct((n, D_MODEL), BF16)],
        compiler_params=pltpu.CompilerParams(dimension_semantics=("parallel", "parallel"),
                                             vmem_limit_bytes=VMEM_LIMIT),
        name="proj",
    )(*ins)


def _compress_kernel(xk_ref, xv_ref, pe_ref, w1_ref, b1_ref, w2k_ref, w2vt_ref, kconst_ref, kc_ref, vct_ref):
    nc = kc_ref.shape[2]
    half = CMP_LEN // 2
    hids = []
    for which, x_ref in enumerate((xk_ref, xv_ref)):
        za = jnp.zeros((nc, N_KV * CMP_HID), F32)
        zb = jnp.zeros((nc, N_KV * CMP_HID), F32)
        for l in range(half):
            xl = x_ref[0, pl.ds(l, nc, stride=CMP_STRIDE), :]
            za = za + _dot((xl + pe_ref[which, 0, l:l + 1, :]).astype(BF16), w1_ref[which, 0, l])
            zb = zb + _dot((xl + pe_ref[which, 1, l:l + 1, :]).astype(BF16), w1_ref[which, 1, l])
        hid = _gelu(za + pltpu.roll(zb, nc - 1, 0) + b1_ref[which])
        row = lax.broadcasted_iota(I32, hid.shape, 0)
        hids.append(jnp.where(row < nc - 1, hid, 0.0).astype(BF16))
    for g in range(N_KV):
        cs = slice(g * CMP_HID, (g + 1) * CMP_HID)
        kc_ref[0, g] = (_dot(hids[0][:, cs], w2k_ref[...]) + kconst_ref[...]).astype(BF16)
        vt = _dot_nt(w2vt_ref[...], hids[1][:, cs])
        for c in range(nc // KEY_CHUNK):
            vct_ref[0, g, c] = vt[:, c * KEY_CHUNK:(c + 1) * KEY_CHUNK].astype(BF16)


def _compress(xk, xv, pes, w1, b1, w2k, w2vt, kconst):
    b, s, w = xk.shape
    nc = s // CMP_STRIDE
    full = lambda a: pl.BlockSpec(a.shape, lambda i: (0,) * a.ndim)
    seq = pl.BlockSpec((1, s, w), lambda i: (i, 0, 0))
    return pl.pallas_call(
        _compress_kernel,
        grid=(b,),
        in_specs=[seq, seq, full(pes), full(w1), full(b1), full(w2k), full(w2vt), full(kconst)],
        out_specs=[pl.BlockSpec((1, N_KV, nc, LANES), lambda i: (i, 0, 0, 0)),
                   pl.BlockSpec((1, N_KV, nc // KEY_CHUNK, HEAD_DIM, KEY_CHUNK), lambda i: (i, 0, 0, 0, 0))],
        out_shape=[jax.ShapeDtypeStruct((b, N_KV, nc, LANES), BF16),
                   jax.ShapeDtypeStruct((b, N_KV, nc // KEY_CHUNK, HEAD_DIM, KEY_CHUNK), BF16)],
        compiler_params=pltpu.CompilerParams(dimension_semantics=("parallel",),
                                             vmem_limit_bytes=VMEM_LIMIT),
        name="compress",
    )(xk, xv, pes, w1, b1, w2k, w2vt, kconst)


def _pairs():
    return [slice(pr * 2 * Q_BLOCK, (pr + 1) * 2 * Q_BLOCK) for pr in range(GQA_R // 2)]


def _colmax(blocks):
    part = None
    for x in blocks:
        y = jnp.max(x.reshape(-1, 8, x.shape[-1]), axis=0)
        part = y if part is None else jnp.maximum(part, y)
    return jnp.max(part, axis=0, keepdims=True)


def _colsum(blocks):
    part = None
    for x in blocks:
        y = jnp.sum(x.reshape(-1, 8, x.shape[-1]), axis=0)
        part = y if part is None else part + y
    return jnp.sum(part, axis=0, keepdims=True)


def _softmax_pv(scores, vts, masks, m_ref, l_ref, acc_ref):
    for pr, ls in enumerate(_pairs()):
        sb = [x if masks[u] is None else masks[u](x, ls) for u, x in enumerate(scores[pr])]
        m_old = m_ref[0:1, ls]
        m_new = jnp.maximum(m_old, _colmax(sb))
        alpha = jnp.exp2(m_old - m_new)
        ps = [jnp.exp2(x - m_new) for x in sb]
        pv = _dot(vts[0], ps[0].astype(BF16))
        for u in range(1, len(ps)):
            pv = pv + _dot(vts[u], ps[u].astype(BF16))
        l_ref[0:1, ls] = alpha * l_ref[0:1, ls] + _colsum(ps)
        acc_ref[:, ls] = alpha * acc_ref[:, ls] + pv
        m_ref[0:1, ls] = m_new


def _nsa_kernel(qt_ref, gt_ref, kc_ref, vct_ref, ksa_ref, vst_ref, kwa_ref, vwt_ref, ovt_ref, tri_ref,
                o_ref, qa_ref, s0_ref, s1_ref, m_ref, l_ref, acc_ref, act_ref, *, n_sel, top_n):
    i = pl.program_id(1)
    t0 = i * Q_BLOCK
    width = GQA_R * Q_BLOCK
    n_chunks = ksa_ref.shape[2] // KEY_CHUNK
    n_cchunks = kc_ref.shape[2] // KEY_CHUNK
    n_slots = act_ref.shape[0] // N_KV
    lane = lax.broadcasted_iota(I32, (1, width), 1)
    tq = t0 + (lane & (Q_BLOCK - 1))
    sub = lax.broadcasted_iota(I32, (KEY_CHUNK, 1), 0)
    pairs = _pairs()
    heads = range(N_KV)

    def reset(g):
        m_ref[g] = jnp.full(m_ref.shape[1:], NEG, F32)
        l_ref[g] = jnp.zeros(l_ref.shape[1:], F32)
        acc_ref[g] = jnp.zeros(acc_ref.shape[1:], F32)

    sc = [[[_dot(kc_ref[0, g, u * KEY_CHUNK:(u + 1) * KEY_CHUNK, :], qt_ref[0, g, 0, :, ls])
            for u in range(n_cchunks)] for ls in pairs] for g in heads]
    o_c, imp_t = [], []
    for g in heads:
        ocg, ps = [], []
        for pr, ls in enumerate(pairs):
            sb = []
            for u in range(n_cchunks):
                c_end = (u * KEY_CHUNK + sub) * CMP_STRIDE + (CMP_LEN - 1)
                sb.append(jnp.where(c_end <= tq[:, ls], sc[g][pr][u], NEG))
            m = _colmax(sb)
            pb = [jnp.exp2(x - m) for x in sb]
            l = _colsum(pb)
            w = (tq[:, ls] >= CMP_LEN - 1).astype(F32) / l
            oc = _dot(vct_ref[0, g, 0], pb[0].astype(BF16))
            for u in range(1, n_cchunks):
                oc = oc + _dot(vct_ref[0, g, u], pb[u].astype(BF16))
            ocg.append(oc * w)
            ps.append([p[:, :Q_BLOCK] * w[:, :Q_BLOCK] + p[:, Q_BLOCK:] * w[:, Q_BLOCK:] for p in pb])
        imp2 = jnp.zeros((LANES, 2 * Q_BLOCK), F32)
        for u in range(n_cchunks):
            psu = ps[0][u] + ps[1][u]
            hi = psu.astype(BF16)
            lo = (psu - hi.astype(F32)).astype(BF16)
            imp2 = imp2 + _dot(ovt_ref[u], jnp.concatenate([hi, lo], axis=1))
        o_c.append(jnp.concatenate(ocg, axis=1))
        imp_t.append(imp2[:, :Q_BLOCK] + imp2[:, Q_BLOCK:])

    n_win = WINDOW // KEY_CHUNK
    cls, masks = [], []
    for u in range(n_win + 1):
        c = i - n_win + u
        cls.append(jnp.maximum(c, 0))
        off = jnp.where(c < 0, NEG, 0.0)
        if u == 0:
            edge = tri_ref[0] + off
            masks.append(lambda x, ls, edge=edge: x + edge)
        elif u == n_win:
            masks.append(lambda x, ls: x + tri_ref[1])
        else:
            masks.append(lambda x, ls, off=off: x + off)
    wsc = [[[_dot(kwa_ref[0, g, pl.ds(pl.multiple_of(cl * KEY_CHUNK, KEY_CHUNK), KEY_CHUNK), :],
                  qt_ref[0, g, 0, :, ls]) for cl in cls] for ls in pairs] for g in heads]
    o_w = []
    for g in heads:
        reset(g)
        _softmax_pv(wsc[g], [vwt_ref[0, g, cl] for cl in cls], masks, m_ref.at[g], l_ref.at[g], acc_ref.at[g])
        o_w.append(acc_ref[g] / l_ref[g, 0:1, :])

    j_io = lax.broadcasted_iota(I32, (LANES, Q_BLOCK), 0)
    tl = t0 + lax.broadcasted_iota(I32, (LANES, Q_BLOCK), 1)
    cur = tl // SEL_BLOCK
    forced = (j_io == 0) | (j_io == cur) | (j_io == cur - 1)
    prio = [jnp.where(j_io < n_sel, jnp.where(forced, FORCE, jnp.where(j_io * SEL_BLOCK <= tl, imp_t[g], -1.0)), NEG)
            for g in heads]
    for _ in range(top_n):
        for g in heads:
            m = jnp.max(prio[g], axis=0, keepdims=True)
            idx = jnp.min(jnp.where(prio[g] == m, j_io, LANES), axis=0, keepdims=True)
            prio[g] = jnp.where(j_io == idx, NEG, prio[g])
    sel_t = [jnp.where((prio[g] == NEG) & (j_io < n_sel), 1.0, 0.0) for g in heads]

    n_act = []
    for u in range(act_ref.shape[0]):
        act_ref[u] = jnp.int32(-1)
    for g in heads:
        qa_ref[g, 0:LANES, :] = qt_ref[0, g, 0]
        bias_t = ((sel_t[g] - 1.0) * MASK_BIAS).astype(BF16)
        for h in range(GQA_R):
            qa_ref[g, LANES:2 * LANES, h * Q_BLOCK:(h + 1) * Q_BLOCK] = bias_t
        any_q = jnp.max(sel_t[g], axis=1, keepdims=True)
        any_pair = jnp.maximum(any_q, pltpu.roll(any_q, LANES - 1, 0))
        na = jnp.int32(0)
        for c in range(n_chunks):
            act_ref[g * n_slots + na] = jnp.int32(c)
            na = na + jnp.where((any_pair[2 * c, 0] > 0.0) & (c < i), 1, 0)
        act_ref[g * n_slots + na] = i
        n_act.append(na)

    def sel_scores(g, k, dst):
        for u in range(SEL_SLOTS):
            cl = jnp.maximum(act_ref[g * n_slots + k * SEL_SLOTS + u], 0)
            keys = ksa_ref[0, g, pl.ds(pl.multiple_of(cl * KEY_CHUNK, KEY_CHUNK), KEY_CHUNK), :]
            for ls in pairs:
                dst[g, u, :, ls] = _dot(keys, qa_ref[g, :, ls])

    def sel_step(g, k, cur_s, nxt_s, last, ahead):
        if ahead:
            sel_scores(g, k + 1, nxt_s)
        vts, msk = [], []
        for u in range(SEL_SLOTS):
            c = act_ref[g * n_slots + k * SEL_SLOTS + u]
            vts.append(vst_ref[0, g, jnp.maximum(c, 0)])
            if last:
                kpos = jnp.where(c < 0, n_chunks * KEY_CHUNK, c * KEY_CHUNK) + sub
                msk.append(lambda x, ls, kpos=kpos: jnp.where(kpos <= tq[:, ls], x, -MASK_BIAS))
            else:
                msk.append(None)
        scores = [[cur_s[g, u, :, ls] for u in range(SEL_SLOTS)] for ls in pairs]
        _softmax_pv(scores, vts, msk, m_ref.at[g], l_ref.at[g], acc_ref.at[g])

    for g in heads:
        reset(g)
        sel_scores(g, 0, s0_ref)
    n_steps = [(n_act[g] + SEL_SLOTS) // SEL_SLOTS for g in heads]
    bufs = ((s0_ref, s1_ref), (s1_ref, s0_ref))

    def sel_body(k, carry):
        last = [k >= n_steps[g] - 1 for g in heads]
        for odd, (cur_s, nxt_s) in enumerate(bufs):
            for l0 in (False, True):
                for l1 in (False, True):
                    @pl.when((k % 2 == odd) & (last[0] == l0) & (last[1] == l1))
                    def _():
                        ahead = not (l0 and l1)
                        sel_step(0, k, cur_s, nxt_s, l0, ahead)
                        sel_step(1, k, cur_s, nxt_s, l1, ahead)

        return carry

    lax.fori_loop(0, jnp.maximum(n_steps[0], n_steps[1]), sel_body, 0)
    o_s = [acc_ref[g] / l_ref[g, 0:1, :] for g in heads]

    for g in heads:
        gt = gt_ref[0, g, 0]

        def gate(br):
            return jnp.concatenate([gt[br * GQA_R + h:br * GQA_R + h + 1, :] for h in range(GQA_R)], axis=1)

        o = gate(0) * o_c[g] + gate(1) * o_s[g] + gate(2) * o_w[g]
        o_ref[0, :, g * GQA_R * HEAD_DIM:(g + 1) * GQA_R * HEAD_DIM] = jnp.concatenate(
            [o[:, h * Q_BLOCK:(h + 1) * Q_BLOCK].T for h in range(GQA_R)], axis=1).astype(BF16)


def _nsa(qt, gt, kc, vct, ksa, vst, kwa, vwt, ovt, tri, n_sel):
    b, g, nt, _, width = qt.shape
    s = ksa.shape[2]
    tile = lambda a: pl.BlockSpec((1, g, 1) + a.shape[3:], lambda i, k: (i, 0, k, 0, 0))
    whole = lambda a: pl.BlockSpec((1,) + a.shape[1:], lambda i, k: (i,) + (0,) * (a.ndim - 1))
    const = lambda a: pl.BlockSpec(a.shape, lambda i, k: (0,) * a.ndim)
    kern = functools.partial(_nsa_kernel, n_sel=n_sel, top_n=min(SEL_TOPN, n_sel))
    return pl.pallas_call(
        kern,
        grid=(b, nt),
        in_specs=[tile(qt), tile(gt), whole(kc), whole(vct), whole(ksa), whole(vst), whole(kwa), whole(vwt),
                  const(ovt), const(tri)],
        out_specs=pl.BlockSpec((1, Q_BLOCK, ATTN_WIDTH), lambda i, k: (i, k, 0)),
        out_shape=jax.ShapeDtypeStruct((b, s, ATTN_WIDTH), BF16),
        scratch_shapes=[pltpu.VMEM((g, 2 * LANES, width), BF16),
                        pltpu.VMEM((g, SEL_SLOTS, KEY_CHUNK, width), F32),
                        pltpu.VMEM((g, SEL_SLOTS, KEY_CHUNK, width), F32),
                        pltpu.VMEM((g, 8, width), F32),
                        pltpu.VMEM((g, 8, width), F32),
                        pltpu.VMEM((g, HEAD_DIM, width), F32),
                        pltpu.SMEM((g * (s // KEY_CHUNK + 2 * SEL_SLOTS),), I32)],
        compiler_params=pltpu.CompilerParams(dimension_semantics=("parallel", "arbitrary"),
                                             vmem_limit_bytes=VMEM_LIMIT),
        name="nsa",
    )(qt, gt, kc, vct, ksa, vst, kwa, vwt, ovt, tri)


def _merge_kernel(x_ref, ya_ref, ga_ref, gbyb_ref, wba_ref, wout_ref, gmoe_ref, wr_ref, br_ref, utri_ref,
                  x1_ref, hm_ref, rw_ref, ri_ref, cnt_ref, carry_ref):
    tm = x_ref.shape[0]

    @pl.when(pl.program_id(0) == 0)
    def _():
        carry_ref[...] = jnp.zeros_like(carry_ref)

    merged = ga_ref[...].astype(F32) * _dot(ya_ref[...], wba_ref[...]) + gbyb_ref[...].astype(F32)
    x1 = x_ref[...] + _dot(merged.astype(BF16), wout_ref[...])
    x1_ref[...] = x1
    hm = _rms(x1, gmoe_ref[...])
    for a in range(ROW_TILE):
        hm_ref[pl.ds(a, tm, stride=ROW_TILE), :] = hm[:, a * LANES:(a + 1) * LANES]

    hh = hm.astype(BF16)
    hl = (hm - hh.astype(F32)).astype(BF16)
    both = _dot(hh, wr_ref[...])
    logits = both[:, :LANES] + both[:, LANES:] + _dot(hl, wr_ref[:, :LANES]) + br_ref[...]
    lg = logits.T[0:N_EXPERTS, :]
    e_io = lax.broadcasted_iota(I32, (N_EXPERTS, tm), 0)
    vals, idxs = [], []
    for _ in range(TOP_K):
        m = jnp.max(lg, axis=0, keepdims=True)
        idx = jnp.min(jnp.where(lg == m, e_io, N_EXPERTS), axis=0, keepdims=True)
        vals.append(m)
        idxs.append(idx)
        lg = jnp.where(e_io == idx, NEG, lg)
    ex = [jnp.exp(v - vals[0]) for v in vals]
    den = ex[0] + ex[1] + ex[2] + ex[3]

    hits = [e_io == idx for idx in idxs]
    multi = jnp.zeros((N_EXPERTS, tm), F32)
    for h in hits:
        multi = jnp.where(h, 1.0, multi)
    carry = carry_ref[:, 0:1]
    cum = _dot(multi.astype(BF16), utri_ref[...]) + carry
    ranks = [jnp.sum(jnp.where(h, cum, 0.0), axis=0, keepdims=True).astype(I32) for h in hits]
    ri_ref[...] = jnp.concatenate(idxs + ranks, axis=0)
    wts = jnp.concatenate([e / den for e in ex] + [jnp.zeros((LANES - TOP_K, tm), F32)], axis=0)
    rw_ref[...] = wts.T
    new_carry = carry + jnp.sum(multi, axis=1, keepdims=True)
    carry_ref[...] = jnp.broadcast_to(new_carry, carry_ref.shape)
    cnt_ref[...] = jnp.broadcast_to(new_carry, cnt_ref.shape)


def _merge(x2, ya, ga, gbyb, wba, wout, gmoe, wr, br, tm):
    n = x2.shape[0]
    full = lambda a: pl.BlockSpec(a.shape, lambda i: (0,) * a.ndim)
    rows = lambda w: pl.BlockSpec((tm, w), lambda i: (i, 0))
    utri = jnp.asarray(np.triu(np.ones((tm, tm), np.float32), 1), BF16)
    return pl.pallas_call(
        _merge_kernel,
        grid=(n // tm,),
        in_specs=[rows(D_MODEL), rows(ATTN_WIDTH), rows(D_MODEL), rows(D_MODEL),
                  full(wba), full(wout), full(gmoe), full(wr), full(br), full(utri)],
        out_specs=[rows(D_MODEL), pl.BlockSpec((tm * ROW_TILE, LANES), lambda i: (i, 0)), rows(LANES),
                   pl.BlockSpec((2 * TOP_K, tm), lambda i: (0, i)),
                   pl.BlockSpec((N_EXPERTS, LANES), lambda i: (0, 0))],
        out_shape=[jax.ShapeDtypeStruct((n, D_MODEL), F32),
                   jax.ShapeDtypeStruct((n * ROW_TILE, LANES), F32),
                   jax.ShapeDtypeStruct((n, LANES), F32),
                   jax.ShapeDtypeStruct((2 * TOP_K, n), I32),
                   jax.ShapeDtypeStruct((N_EXPERTS, LANES), F32)],
        scratch_shapes=[pltpu.VMEM((N_EXPERTS, LANES), F32)],
        compiler_params=pltpu.CompilerParams(dimension_semantics=("arbitrary",),
                                             vmem_limit_bytes=VMEM_LIMIT),
        name="merge",
    )(x2, ya, ga, gbyb, wba, wout, gmoe, wr, br, utri)


def _tile_copy(src, i, dst, d, sem):
    return pltpu.make_async_copy(src.at[pl.ds(pl.multiple_of(i * ROW_TILE, ROW_TILE), ROW_TILE)],
                                 dst.at[pl.ds(pl.multiple_of(d * ROW_TILE, ROW_TILE), ROW_TILE)], sem)


def _dispatch_kernel(seg_ref, dest_ref, hm_ref, xs_ref, zero_ref, sem, zsem, *, n_pad):
    tm = hm_ref.shape[0] // ROW_TILE

    @pl.when(pl.program_id(0) == 0)
    def _():
        zero_ref[...] = jnp.zeros_like(zero_ref)

        def seg(e, c):
            def fill(r, c2):
                _tile_copy(zero_ref, 0, xs_ref, r, zsem).start()
                return c2
            return lax.fori_loop(seg_ref[0, e], seg_ref[1, e], fill, c)

        lax.fori_loop(0, N_EXPERTS + 1, seg, 0)
        pad_rows = xs_ref.at[pl.ds(0, n_pad * ROW_TILE)]
        pltpu.make_async_copy(pad_rows, pad_rows, zsem).wait()

    def issue(r, c):
        slots = [dest_ref[r * TOP_K + k] for k in range(TOP_K)]
        for k in range(TOP_K):
            _tile_copy(hm_ref, r, xs_ref, slots[k], sem).start(priority=k % 2)
        return c

    lax.fori_loop(0, tm, issue, 0)
    for k in range(TOP_K):
        pltpu.make_async_copy(hm_ref, xs_ref.at[pl.ds(0, tm * ROW_TILE)], sem).wait()


def _dispatch(seg, dest_flat, hm, n_slots, tm):
    n = hm.shape[0] // ROW_TILE
    kern = functools.partial(_dispatch_kernel, n_pad=n_slots - n * TOP_K)
    return pl.pallas_call(
        kern,
        grid_spec=pltpu.PrefetchScalarGridSpec(
            num_scalar_prefetch=1,
            grid=(n // tm,),
            in_specs=[pl.BlockSpec((tm * TOP_K,), lambda i, sg: (i,), memory_space=pltpu.SMEM),
                      pl.BlockSpec((tm * ROW_TILE, LANES), lambda i, sg: (i, 0))],
            out_specs=pl.BlockSpec(memory_space=pl.ANY),
            scratch_shapes=[pltpu.VMEM((ROW_TILE, LANES), F32),
                            pltpu.SemaphoreType.DMA(()), pltpu.SemaphoreType.DMA(())]),
        out_shape=jax.ShapeDtypeStruct((n_slots * ROW_TILE, LANES), F32),
        compiler_params=pltpu.CompilerParams(dimension_semantics=("arbitrary",),
                                             has_side_effects=True),
        name="dispatch",
    )(seg, dest_flat, hm)


def _expert_kernel(be_ref, nu_ref, xs_ref, wgu_ref, bgu_ref, wd_ref, bd_ref, y_ref, wgu_bf, wd_bf):
    i = pl.program_id(0)

    @pl.when(i >= nu_ref[0])
    def _():
        y_ref[...] = jnp.zeros_like(y_ref)

    @pl.when((i == 0) | (be_ref[i] != be_ref[jnp.maximum(i - 1, 0)]))
    def _():
        wgu_bf[...] = wgu_ref[0].astype(BF16)
        wd_bf[...] = wd_ref[0].astype(BF16)

    @pl.when(i < nu_ref[0])
    def _():
        x = jnp.concatenate([xs_ref[pl.ds(a, MOE_BLOCK, stride=ROW_TILE), :] for a in range(ROW_TILE)], axis=1)
        gu = _dot(x.astype(BF16), wgu_bf[...]) + bgu_ref[0]
        gate = jnp.minimum(gu[:, :D_FF], SWIGLU_LIMIT)
        up = jnp.clip(gu[:, D_FF:], -SWIGLU_LIMIT, SWIGLU_LIMIT)
        act = gate * _sigmoid(SWIGLU_ALPHA * gate) * (up + 1.0)
        y = _dot(act.astype(BF16), wd_bf[...]) + bd_ref[0]
        for a in range(ROW_TILE):
            y_ref[pl.ds(a, MOE_BLOCK, stride=ROW_TILE), :] = y[:, a * LANES:(a + 1) * LANES]


def _experts(blk_expert, n_used, xs, wgu, bgu, wd, bd):
    n_blocks = xs.shape[0] // (MOE_BLOCK * ROW_TILE)
    blk = lambda i, be, nu: (jnp.minimum(i, nu[0] - 1), 0)
    exp3 = lambda i, be, nu: (be[jnp.minimum(i, nu[0] - 1)], 0, 0)
    return pl.pallas_call(
        _expert_kernel,
        grid_spec=pltpu.PrefetchScalarGridSpec(
            num_scalar_prefetch=2,
            grid=(n_blocks,),
            in_specs=[pl.BlockSpec((MOE_BLOCK * ROW_TILE, LANES), blk),
                      pl.BlockSpec((1, D_MODEL, 2 * D_FF), exp3),
                      pl.BlockSpec((1, 1, 2 * D_FF), exp3),
                      pl.BlockSpec((1, D_FF, D_MODEL), exp3),
                      pl.BlockSpec((1, 1, D_MODEL), exp3)],
            out_specs=pl.BlockSpec((MOE_BLOCK * ROW_TILE, LANES), lambda i, be, nu: (i, 0)),
            scratch_shapes=[pltpu.VMEM((D_MODEL, 2 * D_FF), BF16), pltpu.VMEM((D_FF, D_MODEL), BF16)]),
        out_shape=jax.ShapeDtypeStruct(xs.shape, F32),
        compiler_params=pltpu.CompilerParams(dimension_semantics=("arbitrary",),
                                             vmem_limit_bytes=VMEM_LIMIT),
        name="experts",
    )(blk_expert, n_used, xs, wgu, bgu, wd, bd)


def _combine_kernel(dest_ref, dest_next_ref, rw_ref, x1_ref, gfin_ref, y_ref, o_ref, ybuf0, ybuf1, sem0, sem1):
    tm = x1_ref.shape[0]
    i = pl.program_id(0)

    def gather(dst_ref, ybuf, sem):
        def issue(r, c):
            slots = [dst_ref[r * TOP_K + k] for k in range(TOP_K)]
            for k in range(TOP_K):
                _tile_copy(y_ref, slots[k], ybuf.at[k], r, sem).start(priority=k % 2)
            return c

        lax.fori_loop(0, tm, issue, 0)

    def finish(ybuf, sem):
        for k in range(TOP_K):
            pltpu.make_async_copy(y_ref.at[pl.ds(0, tm * ROW_TILE)], ybuf.at[k], sem).wait()
        rw = rw_ref[...]
        x1 = x1_ref[...]
        cols = []
        for a in range(ROW_TILE):
            acc = x1[:, a * LANES:(a + 1) * LANES]
            for k in range(TOP_K):
                acc = acc + rw[:, k:k + 1] * ybuf[k, pl.ds(a, tm, stride=ROW_TILE), :]
            cols.append(acc)
        o_ref[...] = _rms(jnp.concatenate(cols, axis=1), gfin_ref[...])

    @pl.when(i == 0)
    def _():
        gather(dest_ref, ybuf0, sem0)

    for parity, (cur, nxt) in enumerate((((ybuf0, sem0), (ybuf1, sem1)), ((ybuf1, sem1), (ybuf0, sem0)))):
        @pl.when(i % 2 == parity)
        def _():
            @pl.when(i + 1 < pl.num_programs(0))
            def _():
                gather(dest_next_ref, *nxt)

            finish(*cur)


def _combine(dest_flat, rw, x1, gfin, y, tm):
    n = x1.shape[0]
    steps = n // tm
    ybuf = pltpu.VMEM((TOP_K, tm * ROW_TILE, LANES), F32)
    return pl.pallas_call(
        _combine_kernel,
        grid=(steps,),
        in_specs=[pl.BlockSpec((tm * TOP_K,), lambda i: (i,), memory_space=pltpu.SMEM),
                  pl.BlockSpec((tm * TOP_K,), lambda i: (jnp.minimum(i + 1, steps - 1),), memory_space=pltpu.SMEM),
                  pl.BlockSpec((tm, LANES), lambda i: (i, 0)),
                  pl.BlockSpec((tm, D_MODEL), lambda i: (i, 0)),
                  pl.BlockSpec((1, D_MODEL), lambda i: (0, 0)),
                  pl.BlockSpec(memory_space=pl.ANY)],
        out_specs=pl.BlockSpec((tm, D_MODEL), lambda i: (i, 0)),
        out_shape=jax.ShapeDtypeStruct((n, D_MODEL), F32),
        scratch_shapes=[ybuf, ybuf, pltpu.SemaphoreType.DMA(()), pltpu.SemaphoreType.DMA(())],
        compiler_params=pltpu.CompilerParams(dimension_semantics=("arbitrary",),
                                             vmem_limit_bytes=VMEM_LIMIT),
        name="combine",
    )(dest_flat, dest_flat, rw, x1, gfin, y)


def _overlap_t(nc, n_cmp, n_sel):
    cs = np.arange(n_cmp)[None, :] * CMP_STRIDE
    ss = np.arange(n_sel)[:, None] * SEL_BLOCK
    ov = np.clip(np.minimum(cs + CMP_LEN, ss + SEL_BLOCK) - np.maximum(cs, ss), 0, None) / CMP_LEN
    out = np.zeros((LANES, nc), np.float32)
    out[:n_sel, :n_cmp] = ov
    return jnp.asarray(out, BF16)


def _layer(x2, b, s, g_mix, w_in, w_ck1, b_ck1, w_ck2, pe_k, w_cv1, b_cv1, w_cv2, pe_v,
           sgu_ln_g, sgu_ln_b, w_spatial, b_spatial, w_branch_a, w_branch_b, w_out,
           g_moe, w_router, b_router, w_gate_up, b_gate_up, w_down, b_down):
    n = b * s
    nc = s // CMP_STRIDE
    n_cmp = (s - CMP_LEN) // CMP_STRIDE + 1
    n_sel = s // SEL_BLOCK
    assert nc % KEY_CHUNK == 0 and n_sel <= LANES and n_cmp == nc - 1
    tm = TOKEN_TILE
    assert s % tm == 0 and n % ROUTE_TILE == 0

    p0 = ATTN_WIDTH
    p1 = p0 + 6 * KV_WIDTH
    p2 = p1 + N_NSA_BRANCH * N_HEADS
    p3 = p2 + SGU_WIDTH
    p4 = p3 + SGU_WIDTH
    zpad = lambda a, w: jnp.pad(a, ((0, 0),) * (a.ndim - 1) + ((0, w - a.shape[-1]),))
    wq = zpad((w_in[:, :p0] * (HEAD_DIM ** -0.5 * LOG2E)).reshape(D_MODEL, N_HEADS, HEAD_DIM), LANES)
    wkv = w_in[:, p0:p1].reshape(D_MODEL, 6, N_KV, HEAD_DIM)
    wng = w_in[:, p1:p2].reshape(D_MODEL, N_KV, GQA_R, N_NSA_BRANCH).transpose(0, 1, 3, 2)
    wng = zpad(wng.reshape(D_MODEL, N_KV, N_NSA_BRANCH * GQA_R), GATE_ROWS)
    wt = jnp.concatenate([wq.reshape(D_MODEL, -1), wkv[:, 3].reshape(D_MODEL, -1),
                          wkv[:, 5].reshape(D_MODEL, -1), wng.reshape(D_MODEL, -1)], axis=1).T.astype(BF16)
    slopes = 2.0 ** (-8.0 * np.arange(1, N_HEADS + 1) / N_HEADS)
    tcol = np.zeros((T_ROWS, 1), np.float32)
    head_rows = T_Q + np.arange(N_HEADS) * LANES + HEAD_DIM
    bf16_round = lambda a: a.astype(BF16).astype(np.float32)
    for k, coef in enumerate((slopes * SEL_BLOCK * LOG2E, slopes * LOG2E)):
        hi = bf16_round(coef.astype(np.float32))
        tcol[head_rows + 2 * k, 0] = hi
        tcol[head_rows + 2 * k + 1, 0] = bf16_round(coef.astype(np.float32) - hi)
    tconst = jnp.asarray(np.broadcast_to(tcol, (T_ROWS, 2 * Q_BLOCK)))
    wk = jnp.concatenate([zpad(wkv[:, 2, g], 2 * LANES) for g in range(N_KV)]
                         + [zpad(wkv[:, 4, g], LANES) for g in range(N_KV)]
                         + [wkv[:, 0].reshape(D_MODEL, -1), wkv[:, 1].reshape(D_MODEL, -1)], axis=1).astype(BF16)
    pos = np.arange(s)
    kc_np = np.zeros((s, 2 * LANES), np.float32)
    kc_np[:, HEAD_DIM:HEAD_DIM + 2] = (pos // SEL_BLOCK)[:, None]
    kc_np[:, HEAD_DIM + 2:HEAD_DIM + 4] = (pos % SEL_BLOCK)[:, None]
    kc_np[pos, LANES + pos // SEL_BLOCK] = 1.0
    kconst = jnp.asarray(kc_np, BF16)
    wu = w_in[:, p2:p3].astype(BF16)
    wv = w_in[:, p3:p4].astype(BF16)
    wmg = w_in[:, p4:].astype(BF16)
    bsp = jnp.repeat(b_spatial.T, SGU_GROUP_DIM, axis=1)

    qt, gt, vst, vwt, ksa, kwa, kcx, vcx, ga, gbyb = _proj(
        x2, b, s, g_mix[None], wt, tconst, wk, kconst, wu, wv, wmg, sgu_ln_g[None], sgu_ln_b[None],
        w_spatial, bsp, w_branch_b.astype(BF16), tm)

    half = CMP_LEN // 2
    eye = jnp.eye(N_KV, dtype=F32)[None, None, :, None, :, None]
    bdiag = lambda w: (w.reshape(2, half, 1, HEAD_DIM, 1, CMP_HID) * eye).reshape(
        2, half, N_KV * HEAD_DIM, N_KV * CMP_HID)
    w1 = jnp.stack([bdiag(w_ck1), bdiag(w_cv1)]).astype(BF16)
    pes = jnp.stack([jnp.tile(pe_k.reshape(2, half, HEAD_DIM), (1, 1, N_KV)),
                     jnp.tile(pe_v.reshape(2, half, HEAD_DIM), (1, 1, N_KV))])
    b1 = jnp.stack([jnp.tile(b_ck1, N_KV), jnp.tile(b_cv1, N_KV)])[:, None, :]
    blk_n = np.arange(nc)
    cc_np = np.zeros((nc, LANES), np.float32)
    cc_np[:, HEAD_DIM:HEAD_DIM + 2] = (blk_n // (SEL_BLOCK // CMP_STRIDE))[:, None]
    cc_np[:, HEAD_DIM + 2:HEAD_DIM + 4] = (blk_n % (SEL_BLOCK // CMP_STRIDE) * CMP_STRIDE)[:, None]
    kc, vct = _compress(kcx, vcx, pes, w1, b1, zpad(w_ck2, LANES).astype(BF16), w_cv2.T.astype(BF16),
                        jnp.asarray(cc_np))

    ovt = _overlap_t(nc, n_cmp, n_sel).reshape(LANES, nc // KEY_CHUNK, KEY_CHUNK).transpose(1, 0, 2)
    a_io, q_io = np.meshgrid(np.arange(KEY_CHUNK), np.arange(Q_BLOCK), indexing="ij")
    tri = np.stack([np.where(a_io > q_io, 0.0, NEG), np.where(a_io <= q_io, 0.0, NEG)]).astype(np.float32)
    tri = jnp.asarray(np.tile(tri, (1, 1, 2)))
    ya = _nsa(qt, gt, kc, vct, ksa, vst, kwa, vwt, ovt, tri, n_sel).reshape(n, ATTN_WIDTH)

    wr = jnp.pad(w_router, ((0, 0), (0, LANES - N_EXPERTS)))
    wrh = wr.astype(BF16)
    wr2 = jnp.concatenate([wrh, (wr - wrh.astype(F32)).astype(BF16)], axis=1)
    br = jnp.pad(b_router, (0, LANES - N_EXPERTS))[None]
    x1, hm, rw, ri, cnt = _merge(x2, ya, ga, gbyb, w_branch_a.astype(BF16), w_out.astype(BF16),
                                 g_moe[None], wr2, br, tm)

    counts = cnt[:, 0].astype(I32)
    padded = (counts + MOE_BLOCK - 1) // MOE_BLOCK * MOE_BLOCK
    pad_end = jnp.cumsum(padded)
    pad_start = pad_end - padded
    dest = (pad_start[ri[:TOP_K]] + ri[TOP_K:]).T.reshape(-1)
    n_blocks = -(-(n * TOP_K) // MOE_BLOCK) + N_EXPERTS
    blk_start = jnp.arange(n_blocks, dtype=I32) * MOE_BLOCK
    blk_expert = jnp.minimum(jnp.sum((pad_end[None, :] <= blk_start[:, None]).astype(I32), axis=1),
                             N_EXPERTS - 1)
    n_used = (pad_end[-1:] // MOE_BLOCK).astype(I32)
    n_slots = n_blocks * MOE_BLOCK
    seg = jnp.stack([jnp.concatenate([pad_start + counts, pad_end[-1:]]),
                     jnp.concatenate([pad_end, jnp.full((1,), n_slots, I32)])]).astype(I32)

    xs = _dispatch(seg, dest, hm, n_slots, ROUTE_TILE)
    y = _experts(blk_expert, n_used, xs, w_gate_up, b_gate_up[:, None, :], w_down, b_down[:, None, :])
    return dest, rw, x1, y


def kernel(x, g_mix, w_in, w_ck1, b_ck1, w_ck2, pe_k, w_cv1, b_cv1, w_cv2, pe_v, sgu_ln_g, sgu_ln_b, w_spatial, b_spatial, w_branch_a, w_branch_b, w_out, g_moe, w_router, b_router, w_gate_up, b_gate_up, w_down, b_down, g_final):
    b, s, d = x.shape
    assert g_mix.shape[0] == 1, "the final rmsnorm is fused into the single layer's combine step"
    l = 0
    dest, rw, x1, y = _layer(
        x.reshape(b * s, d), b, s, g_mix[l], w_in[l], w_ck1[l], b_ck1[l], w_ck2[l], pe_k[l], w_cv1[l],
        b_cv1[l], w_cv2[l], pe_v[l], sgu_ln_g[l], sgu_ln_b[l], w_spatial[l], b_spatial[l], w_branch_a[l],
        w_branch_b[l], w_out[l], g_moe[l], w_router[l], b_router[l], w_gate_up[l], b_gate_up[l],
        w_down[l], b_down[l])
    return _combine(dest, rw, x1, g_final[None], y, ROUTE_TILE).reshape(b, s, d)
```

```python
import functools

import numpy as np
import jax
import jax.numpy as jnp
from jax import lax
from jax.experimental import pallas as pl
from jax.experimental.pallas import tpu as pltpu

D_MODEL = 1024
N_HEADS = 8
HEAD_DIM = 64
N_KV = 2
GQA_R = N_HEADS // N_KV
ATTN_WIDTH = N_HEADS * HEAD_DIM
KV_WIDTH = N_KV * HEAD_DIM
CMP_LEN = 32
CMP_STRIDE = 16
CMP_HID = 128
SEL_BLOCK = 64
SEL_TOPN = 16
WINDOW = 512
Q_BLOCK = 128
N_NSA_BRANCH = 3
SGU_WIDTH = 512
N_GROUPS_SGU = 8
SGU_GROUP_DIM = SGU_WIDTH // N_GROUPS_SGU
CHUNK = 128
N_EXPERTS = 32
TOP_K = 4
D_FF = D_MODEL
SWIGLU_LIMIT = 7.0
SWIGLU_ALPHA = 1.702
MOE_BLOCK = 512
RMS_EPS = 1e-5
LN_EPS = 1e-5
NEG = -1e30
FORCE = 1e4

LANES = 128
ROW_TILE = D_MODEL // LANES
MASK_BIAS = 1e9
LOG2E = float(np.log2(np.e))
KEY_CHUNK = 128
SEL_SLOTS = 4
GATE_ROWS = 16
TOKEN_TILE = 512
ROUTE_TILE = 256
VMEM_LIMIT = 56 * 1024 * 1024

F32 = jnp.float32
BF16 = jnp.bfloat16
I32 = jnp.int32

_NT = (((1,), (1,)), ((), ()))


def _dot(a, b):
    return jnp.dot(a, b, preferred_element_type=F32)


def _dot_nt(a, b):
    return lax.dot_general(a, b, _NT, preferred_element_type=F32)


def _sigmoid(x):
    return 1.0 / (1.0 + jnp.exp(-x))


def _gelu(x):
    c = np.float32(np.sqrt(2.0 / np.pi))
    return 0.5 * x * (1.0 + jnp.tanh(c * (x + 0.044715 * (x * x * x))))


def _rms(x, g):
    return x * lax.rsqrt(jnp.mean(x * x, axis=-1, keepdims=True) + RMS_EPS) * g


T_Q = 0
T_VS = T_Q + N_HEADS * LANES
T_VW = T_VS + N_KV * HEAD_DIM
T_GATE = T_VW + N_KV * HEAD_DIM
T_ROWS = T_GATE + N_KV * GATE_ROWS
K_SEL = 0
K_WIN = K_SEL + N_KV * 2 * LANES
K_CMP = K_WIN + N_KV * LANES
K_COLS = K_CMP + 4 * HEAD_DIM


def _proj_kernel(x_ref, gmix_ref, wt_ref, tconst_ref, wk_ref, kconst_ref, wu_ref, wv_ref, wmg_ref,
                 lng_ref, lnb_ref, wsp_ref, bsp_ref, wbb_ref,
                 qt_ref, gt_ref, vst_ref, vwt_ref, ksa_ref, kwa_ref, kcx_ref, vcx_ref, ga_ref, gbyb_ref):
    tm = x_ref.shape[0]
    hb = _rms(x_ref[...], gmix_ref[...]).astype(BF16)

    for h in range(tm // (2 * Q_BLOCK)):
        t = _dot_nt(wt_ref[...], hb[h * 2 * Q_BLOCK:(h + 1) * 2 * Q_BLOCK]) + tconst_ref[...]
        for jj in range(2):
            j = 2 * h + jj
            ls = slice(jj * Q_BLOCK, (jj + 1) * Q_BLOCK)
            for g in range(N_KV):
                for r in range(GQA_R):
                    r0 = T_Q + (g * GQA_R + r) * LANES
                    qt_ref[0, g, j, :, r * Q_BLOCK:(r + 1) * Q_BLOCK] = t[r0:r0 + LANES, ls].astype(BF16)
                vst_ref[0, g, j] = t[T_VS + g * HEAD_DIM:T_VS + (g + 1) * HEAD_DIM, ls].astype(BF16)
                vwt_ref[0, g, j] = t[T_VW + g * HEAD_DIM:T_VW + (g + 1) * HEAD_DIM, ls].astype(BF16)
                gt_ref[0, g, j] = _sigmoid(t[T_GATE + g * GATE_ROWS:T_GATE + (g + 1) * GATE_ROWS, ls])

    kk = _dot(hb, wk_ref[...])
    kconst = kconst_ref[...].astype(F32)
    for g in range(N_KV):
        c0 = K_SEL + g * 2 * LANES
        ksa_ref[0, g] = (kk[:, c0:c0 + 2 * LANES] + kconst).astype(BF16)
        c0 = K_WIN + g * LANES
        kwa_ref[0, g] = (kk[:, c0:c0 + LANES] + kconst[:, :LANES]).astype(BF16)
    kcx_ref[0] = kk[:, K_CMP:K_CMP + LANES]
    vcx_ref[0] = kk[:, K_CMP + LANES:K_COLS]

    mg = _dot(hb, wmg_ref[...])
    ga_ref[...] = _sigmoid(mg[:, :D_MODEL]).astype(BF16)

    u = _gelu(_dot(hb, wu_ref[...]))
    v = _gelu(_dot(hb, wv_ref[...]))
    mu = jnp.mean(v, axis=-1, keepdims=True)
    vc = v - mu
    var = jnp.mean(vc * vc, axis=-1, keepdims=True)
    vln = (vc * lax.rsqrt(var + LN_EPS) * lng_ref[...] + lnb_ref[...]).astype(BF16)

    row = lax.broadcasted_iota(I32, (CHUNK, CHUNK), 0)
    col = lax.broadcasted_iota(I32, (CHUNK, CHUNK), 1)
    tril = row >= col
    wsp = [jnp.where(tril, wsp_ref[g], 0.0).astype(BF16) for g in range(N_GROUPS_SGU)]
    low_half = col < SGU_GROUP_DIM
    bsp = bsp_ref[...]
    chunks = []
    for c in range(tm // CHUNK):
        rs = slice(c * CHUNK, (c + 1) * CHUNK)
        parts = []
        for p in range(SGU_WIDTH // LANES):
            cs = slice(p * LANES, (p + 1) * LANES)
            vblk = vln[rs, cs]
            mixed = jnp.where(low_half, _dot(wsp[2 * p], vblk), _dot(wsp[2 * p + 1], vblk))
            parts.append(u[rs, cs] * (mixed + bsp[:, cs]))
        chunks.append(jnp.concatenate(parts, axis=1))
    yb = jnp.concatenate(chunks, axis=0).astype(BF16)
    gbyb_ref[...] = (_sigmoid(mg[:, D_MODEL:]) * _dot(yb, wbb_ref[...])).astype(BF16)


def _proj(x2, b, s, gmix, wt, tconst, wk, kconst, wu, wv, wmg, lng, lnb, wsp, bsp, wbb, tm):
    n = x2.shape[0]
    tiles = s // tm
    qtiles = tm // Q_BLOCK
    full = lambda a: pl.BlockSpec(a.shape, lambda i, j: (0,) * a.ndim)
    rows = lambda w: pl.BlockSpec((tm, w), lambda i, j: (i * tiles + j, 0))
    per_g = lambda *blk: pl.BlockSpec((1, N_KV) + blk, lambda i, j: (i, 0, j) + (0,) * (len(blk) - 1))
    ins = (x2, gmix, wt, tconst, wk, kconst, wu, wv, wmg, lng, lnb, wsp, bsp, wbb)
    in_specs = [rows(D_MODEL)] + [full(a) for a in ins[1:]]
    in_specs[5] = pl.BlockSpec((tm, 2 * LANES), lambda i, j: (j, 0))
    nt = s // Q_BLOCK
    return pl.pallas_call(
        _proj_kernel,
        grid=(b, tiles),
        in_specs=in_specs,
        out_specs=[per_g(qtiles, LANES, GQA_R * Q_BLOCK), per_g(qtiles, GATE_ROWS, Q_BLOCK),
                   per_g(qtiles, HEAD_DIM, Q_BLOCK), per_g(qtiles, HEAD_DIM, Q_BLOCK),
                   per_g(tm, 2 * LANES), per_g(tm, LANES),
                   pl.BlockSpec((1, tm, LANES), lambda i, j: (i, j, 0)),
                   pl.BlockSpec((1, tm, LANES), lambda i, j: (i, j, 0)),
                   rows(D_MODEL), rows(D_MODEL)],
        out_shape=[jax.ShapeDtypeStruct((b, N_KV, nt, LANES, GQA_R * Q_BLOCK), BF16),
                   jax.ShapeDtypeStruct((b, N_KV, nt, GATE_ROWS, Q_BLOCK), F32),
                   jax.ShapeDtypeStruct((b, N_KV, nt, HEAD_DIM, Q_BLOCK), BF16),
                   jax.ShapeDtypeStruct((b, N_KV, nt, HEAD_DIM, Q_BLOCK), BF16),
                   jax.ShapeDtypeStruct((b, N_KV, s, 2 * LANES), BF16),
                   jax.ShapeDtypeStruct((b, N_KV, s, LANES), BF16),
                   jax.ShapeDtypeStruct((b, s, LANES), F32),
                   jax.ShapeDtypeStruct((b, s, LANES), F32),
                   jax.ShapeDtypeStruct((n, D_MODEL), BF16),
                   jax.ShapeDtypeStruct((n, D_MODEL), BF16)],
        compiler_params=pltpu.CompilerParams(dimension_semantics=("parallel", "parallel"),
                                             vmem_limit_bytes=VMEM_LIMIT),
        name="proj",
    )(*ins)


def _compress_kernel(xk_ref, xv_ref, pe_ref, w1_ref, b1_ref, w2k_ref, w2vt_ref, kconst_ref, kc_ref, vct_ref):
    nc = kc_ref.shape[2]
    half = CMP_LEN // 2
    hids = []
    for which, x_ref in enumerate((xk_ref, xv_ref)):
        za = jnp.zeros((nc, N_KV * CMP_HID), F32)
        zb = jnp.zeros((nc, N_KV * CMP_HID), F32)
        for l in range(half):
            xl = x_ref[0, pl.ds(l, nc, stride=CMP_STRIDE), :]
            za = za + _dot((xl + pe_ref[which, 0, l:l + 1, :]).astype(BF16), w1_ref[which, 0, l])
            zb = zb + _dot((xl + pe_ref[which, 1, l:l + 1, :]).astype(BF16), w1_ref[which, 1, l])
        hid = _gelu(za + pltpu.roll(zb, nc - 1, 0) + b1_ref[which])
        row = lax.broadcasted_iota(I32, hid.shape, 0)
        hids.append(jnp.where(row < nc - 1, hid, 0.0).astype(BF16))
    for g in range(N_KV):
        cs = slice(g * CMP_HID, (g + 1) * CMP_HID)
        kc_ref[0, g] = (_dot(hids[0][:, cs], w2k_ref[...]) + kconst_ref[...]).astype(BF16)
        vt = _dot_nt(w2vt_ref[...], hids[1][:, cs])
        for c in range(nc // KEY_CHUNK):
            vct_ref[0, g, c] = vt[:, c * KEY_CHUNK:(c + 1) * KEY_CHUNK].astype(BF16)


def _compress(xk, xv, pes, w1, b1, w2k, w2vt, kconst):
    b, s, w = xk.shape
    nc = s // CMP_STRIDE
    full = lambda a: pl.BlockSpec(a.shape, lambda i: (0,) * a.ndim)
    seq = pl.BlockSpec((1, s, w), lambda i: (i, 0, 0))
    return pl.pallas_call(
        _compress_kernel,
        grid=(b,),
        in_specs=[seq, seq, full(pes), full(w1), full(b1), full(w2k), full(w2vt), full(kconst)],
        out_specs=[pl.BlockSpec((1, N_KV, nc, LANES), lambda i: (i, 0, 0, 0)),
                   pl.BlockSpec((1, N_KV, nc // KEY_CHUNK, HEAD_DIM, KEY_CHUNK), lambda i: (i, 0, 0, 0, 0))],
        out_shape=[jax.ShapeDtypeStruct((b, N_KV, nc, LANES), BF16),
                   jax.ShapeDtypeStruct((b, N_KV, nc // KEY_CHUNK, HEAD_DIM, KEY_CHUNK), BF16)],
        compiler_params=pltpu.CompilerParams(dimension_semantics=("parallel",),
                                             vmem_limit_bytes=VMEM_LIMIT),
        name="compress",
    )(xk, xv, pes, w1, b1, w2k, w2vt, kconst)


def _pairs():
    return [slice(pr * 2 * Q_BLOCK, (pr + 1) * 2 * Q_BLOCK) for pr in range(GQA_R // 2)]


def _colmax(blocks):
    part = None
    for x in blocks:
        y = jnp.max(x.reshape(-1, 8, x.shape[-1]), axis=0)
        part = y if part is None else jnp.maximum(part, y)
    return jnp.max(part, axis=0, keepdims=True)


def _colsum(blocks):
    part = None
    for x in blocks:
        y = jnp.sum(x.reshape(-1, 8, x.shape[-1]), axis=0)
        part = y if part is None else part + y
    return jnp.sum(part, axis=0, keepdims=True)


def _softmax_pv(scores, vts, masks, m_ref, l_ref, acc_ref):
    for pr, ls in enumerate(_pairs()):
        sb = [x if masks[u] is None else masks[u](x, ls) for u, x in enumerate(scores[pr])]
        m_old = m_ref[0:1, ls]
        m_new = jnp.maximum(m_old, _colmax(sb))
        alpha = jnp.exp2(m_old - m_new)
        ps = [jnp.exp2(x - m_new) for x in sb]
        pv = _dot(vts[0], ps[0].astype(BF16))
        for u in range(1, len(ps)):
            pv = pv + _dot(vts[u], ps[u].astype(BF16))
        l_ref[0:1, ls] = alpha * l_ref[0:1, ls] + _colsum(ps)
        acc_ref[:, ls] = alpha * acc_ref[:, ls] + pv
        m_ref[0:1, ls] = m_new


def _nsa_kernel(qt_ref, gt_ref, kc_ref, vct_ref, ksa_ref, vst_ref, kwa_ref, vwt_ref, ovt_ref, tri_ref,
                o_ref, qa_ref, s0_ref, s1_ref, m_ref, l_ref, acc_ref, act_ref, *, n_sel, top_n):
    i = pl.program_id(1)
    t0 = i * Q_BLOCK
    width = GQA_R * Q_BLOCK
    n_chunks = ksa_ref.shape[2] // KEY_CHUNK
    n_cchunks = kc_ref.shape[2] // KEY_CHUNK
    n_slots = act_ref.shape[0] // N_KV
    lane = lax.broadcasted_iota(I32, (1, width), 1)
    tq = t0 + (lane & (Q_BLOCK - 1))
    sub = lax.broadcasted_iota(I32, (KEY_CHUNK, 1), 0)
    pairs = _pairs()
    heads = range(N_KV)

    def reset(g):
        m_ref[g] = jnp.full(m_ref.shape[1:], NEG, F32)
        l_ref[g] = jnp.zeros(l_ref.shape[1:], F32)
        acc_ref[g] = jnp.zeros(acc_ref.shape[1:], F32)

    sc = [[[_dot(kc_ref[0, g, u * KEY_CHUNK:(u + 1) * KEY_CHUNK, :], qt_ref[0, g, 0, :, ls])
            for u in range(n_cchunks)] for ls in pairs] for g in heads]
    o_c, imp_t = [], []
    for g in heads:
        ocg, ps = [], []
        for pr, ls in enumerate(pairs):
            sb = []
            for u in range(n_cchunks):
                c_end = (u * KEY_CHUNK + sub) * CMP_STRIDE + (CMP_LEN - 1)
                sb.append(jnp.where(c_end <= tq[:, ls], sc[g][pr][u], NEG))
            m = _colmax(sb)
            pb = [jnp.exp2(x - m) for x in sb]
            l = _colsum(pb)
            w = (tq[:, ls] >= CMP_LEN - 1).astype(F32) / l
            oc = _dot(vct_ref[0, g, 0], pb[0].astype(BF16))
            for u in range(1, n_cchunks):
                oc = oc + _dot(vct_ref[0, g, u], pb[u].astype(BF16))
            ocg.append(oc * w)
            ps.append([p[:, :Q_BLOCK] * w[:, :Q_BLOCK] + p[:, Q_BLOCK:] * w[:, Q_BLOCK:] for p in pb])
        imp2 = jnp.zeros((LANES, 2 * Q_BLOCK), F32)
        for u in range(n_cchunks):
            psu = ps[0][u] + ps[1][u]
            hi = psu.astype(BF16)
            lo = (psu - hi.astype(F32)).astype(BF16)
            imp2 = imp2 + _dot(ovt_ref[u], jnp.concatenate([hi, lo], axis=1))
        o_c.append(jnp.concatenate(ocg, axis=1))
        imp_t.append(imp2[:, :Q_BLOCK] + imp2[:, Q_BLOCK:])

    n_win = WINDOW // KEY_CHUNK
    cls, masks = [], []
    for u in range(n_win + 1):
        c = i - n_win + u
        cls.append(jnp.maximum(c, 0))
        off = jnp.where(c < 0, NEG, 0.0)
        if u == 0:
            edge = tri_ref[0] + off
            masks.append(lambda x, ls, edge=edge: x + edge)
        elif u == n_win:
            masks.append(lambda x, ls: x + tri_ref[1])
        else:
            masks.append(lambda x, ls, off=off: x + off)
    wsc = [[[_dot(kwa_ref[0, g, pl.ds(pl.multiple_of(cl * KEY_CHUNK, KEY_CHUNK), KEY_CHUNK), :],
                  qt_ref[0, g, 0, :, ls]) for cl in cls] for ls in pairs] for g in heads]
    o_w = []
    for g in heads:
        reset(g)
        _softmax_pv(wsc[g], [vwt_ref[0, g, cl] for cl in cls], masks, m_ref.at[g], l_ref.at[g], acc_ref.at[g])
        o_w.append(acc_ref[g] / l_ref[g, 0:1, :])

    j_io = lax.broadcasted_iota(I32, (LANES, Q_BLOCK), 0)
    tl = t0 + lax.broadcasted_iota(I32, (LANES, Q_BLOCK), 1)
    cur = tl // SEL_BLOCK
    forced = (j_io == 0) | (j_io == cur) | (j_io == cur - 1)
    prio = [jnp.where(j_io < n_sel, jnp.where(forced, FORCE, jnp.where(j_io * SEL_BLOCK <= tl, imp_t[g], -1.0)), NEG)
            for g in heads]
    for _ in range(top_n):
        for g in heads:
            m = jnp.max(prio[g], axis=0, keepdims=True)
            idx = jnp.min(jnp.where(prio[g] == m, j_io, LANES), axis=0, keepdims=True)
            prio[g] = jnp.where(j_io == idx, NEG, prio[g])
    sel_t = [jnp.where((prio[g] == NEG) & (j_io < n_sel), 1.0, 0.0) for g in heads]

    n_act = []
    for u in range(act_ref.shape[0]):
        act_ref[u] = jnp.int32(-1)
    for g in heads:
        qa_ref[g, 0:LANES, :] = qt_ref[0, g, 0]
        bias_t = ((sel_t[g] - 1.0) * MASK_BIAS).astype(BF16)
        for h in range(GQA_R):
            qa_ref[g, LANES:2 * LANES, h * Q_BLOCK:(h + 1) * Q_BLOCK] = bias_t
        any_q = jnp.max(sel_t[g], axis=1, keepdims=True)
        any_pair = jnp.maximum(any_q, pltpu.roll(any_q, LANES - 1, 0))
        na = jnp.int32(0)
        for c in range(n_chunks):
            act_ref[g * n_slots + na] = jnp.int32(c)
            na = na + jnp.where((any_pair[2 * c, 0] > 0.0) & (c < i), 1, 0)
        act_ref[g * n_slots + na] = i
        n_act.append(na)

    def sel_scores(g, k, dst):
        for u in range(SEL_SLOTS):
            cl = jnp.maximum(act_ref[g * n_slots + k * SEL_SLOTS + u], 0)
            keys = ksa_ref[0, g, pl.ds(pl.multiple_of(cl * KEY_CHUNK, KEY_CHUNK), KEY_CHUNK), :]
            for ls in pairs:
                dst[g, u, :, ls] = _dot(keys, qa_ref[g, :, ls])

    def sel_step(g, k, cur_s, nxt_s, last, ahead):
        if ahead:
            sel_scores(g, k + 1, nxt_s)
        vts, msk = [], []
        for u in range(SEL_SLOTS):
            c = act_ref[g * n_slots + k * SEL_SLOTS + u]
            vts.append(vst_ref[0, g, jnp.maximum(c, 0)])
            if last:
                kpos = jnp.where(c < 0, n_chunks * KEY_CHUNK, c * KEY_CHUNK) + sub
                msk.append(lambda x, ls, kpos=kpos: jnp.where(kpos <= tq[:, ls], x, -MASK_BIAS))
            else:
                msk.append(None)
        scores = [[cur_s[g, u, :, ls] for u in range(SEL_SLOTS)] for ls in pairs]
        _softmax_pv(scores, vts, msk, m_ref.at[g], l_ref.at[g], acc_ref.at[g])

    for g in heads:
        reset(g)
        sel_scores(g, 0, s0_ref)
    n_steps = [(n_act[g] + SEL_SLOTS) // SEL_SLOTS for g in heads]
    bufs = ((s0_ref, s1_ref), (s1_ref, s0_ref))

    def sel_body(k, carry):
        last = [k >= n_steps[g] - 1 for g in heads]
        for odd, (cur_s, nxt_s) in enumerate(bufs):
            for l0 in (False, True):
                for l1 in (False, True):
                    @pl.when((k % 2 == odd) & (last[0] == l0) & (last[1] == l1))
                    def _():
                        ahead = not (l0 and l1)
                        sel_step(0, k, cur_s, nxt_s, l0, ahead)
                        sel_step(1, k, cur_s, nxt_s, l1, ahead)

        return carry

    lax.fori_loop(0, jnp.maximum(n_steps[0], n_steps[1]), sel_body, 0)
    o_s = [acc_ref[g] / l_ref[g, 0:1, :] for g in heads]

    for g in heads:
        gt = gt_ref[0, g, 0]

        def gate(br):
            return jnp.concatenate([gt[br * GQA_R + h:br * GQA_R + h + 1, :] for h in range(GQA_R)], axis=1)

        o = gate(0) * o_c[g] + gate(1) * o_s[g] + gate(2) * o_w[g]
        o_ref[0, :, g * GQA_R * HEAD_DIM:(g + 1) * GQA_R * HEAD_DIM] = jnp.concatenate(
            [o[:, h * Q_BLOCK:(h + 1) * Q_BLOCK].T for h in range(GQA_R)], axis=1).astype(BF16)


def _nsa(qt, gt, kc, vct, ksa, vst, kwa, vwt, ovt, tri, n_sel):
    b, g, nt, _, width = qt.shape
    s = ksa.shape[2]
    tile = lambda a: pl.BlockSpec((1, g, 1) + a.shape[3:], lambda i, k: (i, 0, k, 0, 0))
    whole = lambda a: pl.BlockSpec((1,) + a.shape[1:], lambda i, k: (i,) + (0,) * (a.ndim - 1))
    const = lambda a: pl.BlockSpec(a.shape, lambda i, k: (0,) * a.ndim)
    kern = functools.partial(_nsa_kernel, n_sel=n_sel, top_n=min(SEL_TOPN, n_sel))
    return pl.pallas_call(
        kern,
        grid=(b, nt),
        in_specs=[tile(qt), tile(gt), whole(kc), whole(vct), whole(ksa), whole(vst), whole(kwa), whole(vwt),
                  const(ovt), const(tri)],
        out_specs=pl.BlockSpec((1, Q_BLOCK, ATTN_WIDTH), lambda i, k: (i, k, 0)),
        out_shape=jax.ShapeDtypeStruct((b, s, ATTN_WIDTH), BF16),
        scratch_shapes=[pltpu.VMEM((g, 2 * LANES, width), BF16),
                        pltpu.VMEM((g, SEL_SLOTS, KEY_CHUNK, width), F32),
                        pltpu.VMEM((g, SEL_SLOTS, KEY_CHUNK, width), F32),
                        pltpu.VMEM((g, 8, width), F32),
                        pltpu.VMEM((g, 8, width), F32),
                        pltpu.VMEM((g, HEAD_DIM, width), F32),
                        pltpu.SMEM((g * (s // KEY_CHUNK + 2 * SEL_SLOTS),), I32)],
        compiler_params=pltpu.CompilerParams(dimension_semantics=("parallel", "arbitrary"),
                                             vmem_limit_bytes=VMEM_LIMIT),
        name="nsa",
    )(qt, gt, kc, vct, ksa, vst, kwa, vwt, ovt, tri)


def _merge_kernel(x_ref, ya_ref, ga_ref, gbyb_ref, wba_ref, wout_ref, gmoe_ref, wr_ref, br_ref, utri_ref,
                  x1_ref, hm_ref, rw_ref, ri_ref, cnt_ref, carry_ref):
    tm = x_ref.shape[0]

    @pl.when(pl.program_id(0) == 0)
    def _():
        carry_ref[...] = jnp.zeros_like(carry_ref)

    merged = ga_ref[...].astype(F32) * _dot(ya_ref[...], wba_ref[...]) + gbyb_ref[...].astype(F32)
    x1 = x_ref[...] + _dot(merged.astype(BF16), wout_ref[...])
    x1_ref[...] = x1
    hm = _rms(x1, gmoe_ref[...])
    for a in range(ROW_TILE):
        hm_ref[pl.ds(a, tm, stride=ROW_TILE), :] = hm[:, a * LANES:(a + 1) * LANES]

    hh = hm.astype(BF16)
    hl = (hm - hh.astype(F32)).astype(BF16)
    both = _dot(hh, wr_ref[...])
    logits = both[:, :LANES] + both[:, LANES:] + _dot(hl, wr_ref[:, :LANES]) + br_ref[...]
    lg = logits.T[0:N_EXPERTS, :]
    e_io = lax.broadcasted_iota(I32, (N_EXPERTS, tm), 0)
    vals, idxs = [], []
    for _ in range(TOP_K):
        m = jnp.max(lg, axis=0, keepdims=True)
        idx = jnp.min(jnp.where(lg == m, e_io, N_EXPERTS), axis=0, keepdims=True)
        vals.append(m)
        idxs.append(idx)
        lg = jnp.where(e_io == idx, NEG, lg)
    ex = [jnp.exp(v - vals[0]) for v in vals]
    den = ex[0] + ex[1] + ex[2] + ex[3]

    hits = [e_io == idx for idx in idxs]
    multi = jnp.zeros((N_EXPERTS, tm), F32)
    for h in hits:
        multi = jnp.where(h, 1.0, multi)
    carry = carry_ref[:, 0:1]
    cum = _dot(multi.astype(BF16), utri_ref[...]) + carry
    ranks = [jnp.sum(jnp.where(h, cum, 0.0), axis=0, keepdims=True).astype(I32) for h in hits]
    ri_ref[...] = jnp.concatenate(idxs + ranks, axis=0)
    wts = jnp.concatenate([e / den for e in ex] + [jnp.zeros((LANES - TOP_K, tm), F32)], axis=0)
    rw_ref[...] = wts.T
    new_carry = carry + jnp.sum(multi, axis=1, keepdims=True)
    carry_ref[...] = jnp.broadcast_to(new_carry, carry_ref.shape)
    cnt_ref[...] = jnp.broadcast_to(new_carry, cnt_ref.shape)


def _merge(x2, ya, ga, gbyb, wba, wout, gmoe, wr, br, tm):
    n = x2.shape[0]
    full = lambda a: pl.BlockSpec(a.shape, lambda i: (0,) * a.ndim)
    rows = lambda w: pl.BlockSpec((tm, w), lambda i: (i, 0))
    utri = jnp.asarray(np.triu(np.ones((tm, tm), np.float32), 1), BF16)
    return pl.pallas_call(
        _merge_kernel,
        grid=(n // tm,),
        in_specs=[rows(D_MODEL), rows(ATTN_WIDTH), rows(D_MODEL), rows(D_MODEL),
                  full(wba), full(wout), full(gmoe), full(wr), full(br), full(utri)],
        out_specs=[rows(D_MODEL), pl.BlockSpec((tm * ROW_TILE, LANES), lambda i: (i, 0)), rows(LANES),
                   pl.BlockSpec((2 * TOP_K, tm), lambda i: (0, i)),
                   pl.BlockSpec((N_EXPERTS, LANES), lambda i: (0, 0))],
        out_shape=[jax.ShapeDtypeStruct((n, D_MODEL), F32),
                   jax.ShapeDtypeStruct((n * ROW_TILE, LANES), F32),
                   jax.ShapeDtypeStruct((n, LANES), F32),
                   jax.ShapeDtypeStruct((2 * TOP_K, n), I32),
                   jax.ShapeDtypeStruct((N_EXPERTS, LANES), F32)],
        scratch_shapes=[pltpu.VMEM((N_EXPERTS, LANES), F32)],
        compiler_params=pltpu.CompilerParams(dimension_semantics=("arbitrary",),
                                             vmem_limit_bytes=VMEM_LIMIT),
        name="merge",
    )(x2, ya, ga, gbyb, wba, wout, gmoe, wr, br, utri)


def _tile_copy(src, i, dst, d, sem):
    return pltpu.make_async_copy(src.at[pl.ds(pl.multiple_of(i * ROW_TILE, ROW_TILE), ROW_TILE)],
                                 dst.at[pl.ds(pl.multiple_of(d * ROW_TILE, ROW_TILE), ROW_TILE)], sem)


def _dispatch_kernel(seg_ref, dest_ref, hm_ref, xs_ref, zero_ref, sem, zsem, *, n_pad):
    tm = hm_ref.shape[0] // ROW_TILE

    @pl.when(pl.program_id(0) == 0)
    def _():
        zero_ref[...] = jnp.zeros_like(zero_ref)

        def seg(e, c):
            def fill(r, c2):
                _tile_copy(zero_ref, 0, xs_ref, r, zsem).start()
                return c2
            return lax.fori_loop(seg_ref[0, e], seg_ref[1, e], fill, c)

        lax.fori_loop(0, N_EXPERTS + 1, seg, 0)
        pad_rows = xs_ref.at[pl.ds(0, n_pad * ROW_TILE)]
        pltpu.make_async_copy(pad_rows, pad_rows, zsem).wait()

    def issue(r, c):
        slots = [dest_ref[r * TOP_K + k] for k in range(TOP_K)]
        for k in range(TOP_K):
            _tile_copy(hm_ref, r, xs_ref, slots[k], sem).start(priority=k % 2)
        return c

    lax.fori_loop(0, tm, issue, 0)
    for k in range(TOP_K):
        pltpu.make_async_copy(hm_ref, xs_ref.at[pl.ds(0, tm * ROW_TILE)], sem).wait()


def _dispatch(seg, dest_flat, hm, n_slots, tm):
    n = hm.shape[0] // ROW_TILE
    kern = functools.partial(_dispatch_kernel, n_pad=n_slots - n * TOP_K)
    return pl.pallas_call(
        kern,
        grid_spec=pltpu.PrefetchScalarGridSpec(
            num_scalar_prefetch=1,
            grid=(n // tm,),
            in_specs=[pl.BlockSpec((tm * TOP_K,), lambda i, sg: (i,), memory_space=pltpu.SMEM),
                      pl.BlockSpec((tm * ROW_TILE, LANES), lambda i, sg: (i, 0))],
            out_specs=pl.BlockSpec(memory_space=pl.ANY),
            scratch_shapes=[pltpu.VMEM((ROW_TILE, LANES), F32),
                            pltpu.SemaphoreType.DMA(()), pltpu.SemaphoreType.DMA(())]),
        out_shape=jax.ShapeDtypeStruct((n_slots * ROW_TILE, LANES), F32),
        compiler_params=pltpu.CompilerParams(dimension_semantics=("arbitrary",),
                                             has_side_effects=True),
        name="dispatch",
    )(seg, dest_flat, hm)


def _expert_kernel(be_ref, nu_ref, xs_ref, wgu_ref, bgu_ref, wd_ref, bd_ref, y_ref, wgu_bf, wd_bf):
    i = pl.program_id(0)

    @pl.when(i >= nu_ref[0])
    def _():
        y_ref[...] = jnp.zeros_like(y_ref)

    @pl.when((i == 0) | (be_ref[i] != be_ref[jnp.maximum(i - 1, 0)]))
    def _():
        wgu_bf[...] = wgu_ref[0].astype(BF16)
        wd_bf[...] = wd_ref[0].astype(BF16)

    @pl.when(i < nu_ref[0])
    def _():
        x = jnp.concatenate([xs_ref[pl.ds(a, MOE_BLOCK, stride=ROW_TILE), :] for a in range(ROW_TILE)], axis=1)
        gu = _dot(x.astype(BF16), wgu_bf[...]) + bgu_ref[0]
        gate = jnp.minimum(gu[:, :D_FF], SWIGLU_LIMIT)
        up = jnp.clip(gu[:, D_FF:], -SWIGLU_LIMIT, SWIGLU_LIMIT)
        act = gate * _sigmoid(SWIGLU_ALPHA * gate) * (up + 1.0)
        y = _dot(act.astype(BF16), wd_bf[...]) + bd_ref[0]
        for a in range(ROW_TILE):
            y_ref[pl.ds(a, MOE_BLOCK, stride=ROW_TILE), :] = y[:, a * LANES:(a + 1) * LANES]


def _experts(blk_expert, n_used, xs, wgu, bgu, wd, bd):
    n_blocks = xs.shape[0] // (MOE_BLOCK * ROW_TILE)
    blk = lambda i, be, nu: (jnp.minimum(i, nu[0] - 1), 0)
    exp3 = lambda i, be, nu: (be[jnp.minimum(i, nu[0] - 1)], 0, 0)
    return pl.pallas_call(
        _expert_kernel,
        grid_spec=pltpu.PrefetchScalarGridSpec(
            num_scalar_prefetch=2,
            grid=(n_blocks,),
            in_specs=[pl.BlockSpec((MOE_BLOCK * ROW_TILE, LANES), blk),
                      pl.BlockSpec((1, D_MODEL, 2 * D_FF), exp3),
                      pl.BlockSpec((1, 1, 2 * D_FF), exp3),
                      pl.BlockSpec((1, D_FF, D_MODEL), exp3),
                      pl.BlockSpec((1, 1, D_MODEL), exp3)],
            out_specs=pl.BlockSpec((MOE_BLOCK * ROW_TILE, LANES), lambda i, be, nu: (i, 0)),
            scratch_shapes=[pltpu.VMEM((D_MODEL, 2 * D_FF), BF16), pltpu.VMEM((D_FF, D_MODEL), BF16)]),
        out_shape=jax.ShapeDtypeStruct(xs.shape, F32),
        compiler_params=pltpu.CompilerParams(dimension_semantics=("arbitrary",),
                                             vmem_limit_bytes=VMEM_LIMIT),
        name="experts",
    )(blk_expert, n_used, xs, wgu, bgu, wd, bd)


def _combine_kernel(dest_ref, dest_next_ref, rw_ref, x1_ref, gfin_ref, y_ref, o_ref, ybuf0, ybuf1, sem0, sem1):
    tm = x1_ref.shape[0]
    i = pl.program_id(0)

    def gather(dst_ref, ybuf, sem):
        def issue(r, c):
            slots = [dst_ref[r * TOP_K + k] for k in range(TOP_K)]
            for k in range(TOP_K):
                _tile_copy(y_ref, slots[k], ybuf.at[k], r, sem).start(priority=k % 2)
            return c

        lax.fori_loop(0, tm, issue, 0)

    def finish(ybuf, sem):
        for k in range(TOP_K):
            pltpu.make_async_copy(y_ref.at[pl.ds(0, tm * ROW_TILE)], ybuf.at[k], sem).wait()
        rw = rw_ref[...]
        x1 = x1_ref[...]
        cols = []
        for a in range(ROW_TILE):
            acc = x1[:, a * LANES:(a + 1) * LANES]
            for k in range(TOP_K):
                acc = acc + rw[:, k:k + 1] * ybuf[k, pl.ds(a, tm, stride=ROW_TILE), :]
            cols.append(acc)
        o_ref[...] = _rms(jnp.concatenate(cols, axis=1), gfin_ref[...])

    @pl.when(i == 0)
    def _():
        gather(dest_ref, ybuf0, sem0)

    for parity, (cur, nxt) in enumerate((((ybuf0, sem0), (ybuf1, sem1)), ((ybuf1, sem1), (ybuf0, sem0)))):
        @pl.when(i % 2 == parity)
        def _():
            @pl.when(i + 1 < pl.num_programs(0))
            def _():
                gather(dest_next_ref, *nxt)

            finish(*cur)


def _combine(dest_flat, rw, x1, gfin, y, tm):
    n = x1.shape[0]
    steps = n // tm
    ybuf = pltpu.VMEM((TOP_K, tm * ROW_TILE, LANES), F32)
    return pl.pallas_call(
        _combine_kernel,
        grid=(steps,),
        in_specs=[pl.BlockSpec((tm * TOP_K,), lambda i: (i,), memory_space=pltpu.SMEM),
                  pl.BlockSpec((tm * TOP_K,), lambda i: (jnp.minimum(i + 1, steps - 1),), memory_space=pltpu.SMEM),
                  pl.BlockSpec((tm, LANES), lambda i: (i, 0)),
                  pl.BlockSpec((tm, D_MODEL), lambda i: (i, 0)),
                  pl.BlockSpec((1, D_MODEL), lambda i: (0, 0)),
                  pl.BlockSpec(memory_space=pl.ANY)],
        out_specs=pl.BlockSpec((tm, D_MODEL), lambda i: (i, 0)),
        out_shape=jax.ShapeDtypeStruct((n, D_MODEL), F32),
        scratch_shapes=[ybuf, ybuf, pltpu.SemaphoreType.DMA(()), pltpu.SemaphoreType.DMA(())],
        compiler_params=pltpu.CompilerParams(dimension_semantics=("arbitrary",),
                                             vmem_limit_bytes=VMEM_LIMIT),
        name="combine",
    )(dest_flat, dest_flat, rw, x1, gfin, y)


def _overlap_t(nc, n_cmp, n_sel):
    cs = np.arange(n_cmp)[None, :] * CMP_STRIDE
    ss = np.arange(n_sel)[:, None] * SEL_BLOCK
    ov = np.clip(np.minimum(cs + CMP_LEN, ss + SEL_BLOCK) - np.maximum(cs, ss), 0, None) / CMP_LEN
    out = np.zeros((LANES, nc), np.float32)
    out[:n_sel, :n_cmp] = ov
    return jnp.asarray(out, BF16)


def _layer(x2, b, s, g_mix, w_in, w_ck1, b_ck1, w_ck2, pe_k, w_cv1, b_cv1, w_cv2, pe_v,
           sgu_ln_g, sgu_ln_b, w_spatial, b_spatial, w_branch_a, w_branch_b, w_out,
           g_moe, w_router, b_router, w_gate_up, b_gate_up, w_down, b_down):
    n = b * s
    nc = s // CMP_STRIDE
    n_cmp = (s - CMP_LEN) // CMP_STRIDE + 1
    n_sel = s // SEL_BLOCK
    assert nc % KEY_CHUNK == 0 and n_sel <= LANES and n_cmp == nc - 1
    tm = TOKEN_TILE
    assert s % tm == 0 and n % ROUTE_TILE == 0

    p0 = ATTN_WIDTH
    p1 = p0 + 6 * KV_WIDTH
    p2 = p1 + N_NSA_BRANCH * N_HEADS
    p3 = p2 + SGU_WIDTH
    p4 = p3 + SGU_WIDTH
    zpad = lambda a, w: jnp.pad(a, ((0, 0),) * (a.ndim - 1) + ((0, w - a.shape[-1]),))
    wq = zpad((w_in[:, :p0] * (HEAD_DIM ** -0.5 * LOG2E)).reshape(D_MODEL, N_HEADS, HEAD_DIM), LANES)
    wkv = w_in[:, p0:p1].reshape(D_MODEL, 6, N_KV, HEAD_DIM)
    wng = w_in[:, p1:p2].reshape(D_MODEL, N_KV, GQA_R, N_NSA_BRANCH).transpose(0, 1, 3, 2)
    wng = zpad(wng.reshape(D_MODEL, N_KV, N_NSA_BRANCH * GQA_R), GATE_ROWS)
    wt = jnp.concatenate([wq.reshape(D_MODEL, -1), wkv[:, 3].reshape(D_MODEL, -1),
                          wkv[:, 5].reshape(D_MODEL, -1), wng.reshape(D_MODEL, -1)], axis=1).T.astype(BF16)
    slopes = 2.0 ** (-8.0 * np.arange(1, N_HEADS + 1) / N_HEADS)
    tcol = np.zeros((T_ROWS, 1), np.float32)
    head_rows = T_Q + np.arange(N_HEADS) * LANES + HEAD_DIM
    bf16_round = lambda a: a.astype(BF16).astype(np.float32)
    for k, coef in enumerate((slopes * SEL_BLOCK * LOG2E, slopes * LOG2E)):
        hi = bf16_round(coef.astype(np.float32))
        tcol[head_rows + 2 * k, 0] = hi
        tcol[head_rows + 2 * k + 1, 0] = bf16_round(coef.astype(np.float32) - hi)
    tconst = jnp.asarray(np.broadcast_to(tcol, (T_ROWS, 2 * Q_BLOCK)))
    wk = jnp.concatenate([zpad(wkv[:, 2, g], 2 * LANES) for g in range(N_KV)]
                         + [zpad(wkv[:, 4, g], LANES) for g in range(N_KV)]
                         + [wkv[:, 0].reshape(D_MODEL, -1), wkv[:, 1].reshape(D_MODEL, -1)], axis=1).astype(BF16)
    pos = np.arange(s)
    kc_np = np.zeros((s, 2 * LANES), np.float32)
    kc_np[:, HEAD_DIM:HEAD_DIM + 2] = (pos // SEL_BLOCK)[:, None]
    kc_np[:, HEAD_DIM + 2:HEAD_DIM + 4] = (pos % SEL_BLOCK)[:, None]
    kc_np[pos, LANES + pos // SEL_BLOCK] = 1.0
    kconst = jnp.asarray(kc_np, BF16)
    wu = w_in[:, p2:p3].astype(BF16)
    wv = w_in[:, p3:p4].astype(BF16)
    wmg = w_in[:, p4:].astype(BF16)
    bsp = jnp.repeat(b_spatial.T, SGU_GROUP_DIM, axis=1)

    qt, gt, vst, vwt, ksa, kwa, kcx, vcx, ga, gbyb = _proj(
        x2, b, s, g_mix[None], wt, tconst, wk, kconst, wu, wv, wmg, sgu_ln_g[None], sgu_ln_b[None],
        w_spatial, bsp, w_branch_b.astype(BF16), tm)

    half = CMP_LEN // 2
    eye = jnp.eye(N_KV, dtype=F32)[None, None, :, None, :, None]
    bdiag = lambda w: (w.reshape(2, half, 1, HEAD_DIM, 1, CMP_HID) * eye).reshape(
        2, half, N_KV * HEAD_DIM, N_KV * CMP_HID)
    w1 = jnp.stack([bdiag(w_ck1), bdiag(w_cv1)]).astype(BF16)
    pes = jnp.stack([jnp.tile(pe_k.reshape(2, half, HEAD_DIM), (1, 1, N_KV)),
                     jnp.tile(pe_v.reshape(2, half, HEAD_DIM), (1, 1, N_KV))])
    b1 = jnp.stack([jnp.tile(b_ck1, N_KV), jnp.tile(b_cv1, N_KV)])[:, None, :]
    blk_n = np.arange(nc)
    cc_np = np.zeros((nc, LANES), np.float32)
    cc_np[:, HEAD_DIM:HEAD_DIM + 2] = (blk_n // (SEL_BLOCK // CMP_STRIDE))[:, None]
    cc_np[:, HEAD_DIM + 2:HEAD_DIM + 4] = (blk_n % (SEL_BLOCK // CMP_STRIDE) * CMP_STRIDE)[:, None]
    kc, vct = _compress(kcx, vcx, pes, w1, b1, zpad(w_ck2, LANES).astype(BF16), w_cv2.T.astype(BF16),
                        jnp.asarray(cc_np))

    ovt = _overlap_t(nc, n_cmp, n_sel).reshape(LANES, nc // KEY_CHUNK, KEY_CHUNK).transpose(1, 0, 2)
    a_io, q_io = np.meshgrid(np.arange(KEY_CHUNK), np.arange(Q_BLOCK), indexing="ij")
    tri = np.stack([np.where(a_io > q_io, 0.0, NEG), np.where(a_io <= q_io, 0.0, NEG)]).astype(np.float32)
    tri = jnp.asarray(np.tile(tri, (1, 1, 2)))
    ya = _nsa(qt, gt, kc, vct, ksa, vst, kwa, vwt, ovt, tri, n_sel).reshape(n, ATTN_WIDTH)

    wr = jnp.pad(w_router, ((0, 0), (0, LANES - N_EXPERTS)))
    wrh = wr.astype(BF16)
    wr2 = jnp.concatenate([wrh, (wr - wrh.astype(F32)).astype(BF16)], axis=1)
    br = jnp.pad(b_router, (0, LANES - N_EXPERTS))[None]
    x1, hm, rw, ri, cnt = _merge(x2, ya, ga, gbyb, w_branch_a.astype(BF16), w_out.astype(BF16),
                                 g_moe[None], wr2, br, tm)

    counts = cnt[:, 0].astype(I32)
    padded = (counts + MOE_BLOCK - 1) // MOE_BLOCK * MOE_BLOCK
    pad_end = jnp.cumsum(padded)
    pad_start = pad_end - padded
    base = sum(jnp.where(ri[:TOP_K] == e, pad_start[e], 0) for e in range(N_EXPERTS))
    dest = (base + ri[TOP_K:]).T.reshape(-1)
    n_blocks = -(-(n * TOP_K) // MOE_BLOCK) + N_EXPERTS
    blk_start = jnp.arange(n_blocks, dtype=I32) * MOE_BLOCK
    blk_expert = jnp.minimum(jnp.sum((pad_end[None, :] <= blk_start[:, None]).astype(I32), axis=1),
                             N_EXPERTS - 1)
    n_used = (pad_end[-1:] // MOE_BLOCK).astype(I32)
    n_slots = n_blocks * MOE_BLOCK
    seg = jnp.stack([jnp.concatenate([pad_start + counts, pad_end[-1:]]),
                     jnp.concatenate([pad_end, jnp.full((1,), n_slots, I32)])]).astype(I32)

    xs = _dispatch(seg, dest, hm, n_slots, ROUTE_TILE)
    y = _experts(blk_expert, n_used, xs, w_gate_up, b_gate_up[:, None, :], w_down, b_down[:, None, :])
    return dest, rw, x1, y


def kernel(x, g_mix, w_in, w_ck1, b_ck1, w_ck2, pe_k, w_cv1, b_cv1, w_cv2, pe_v, sgu_ln_g, sgu_ln_b, w_spatial, b_spatial, w_branch_a, w_branch_b, w_out, g_moe, w_router, b_router, w_gate_up, b_gate_up, w_down, b_down, g_final):
    b, s, d = x.shape
    assert g_mix.shape[0] == 1, "the final rmsnorm is fused into the single layer's combine step"
    l = 0
    dest, rw, x1, y = _layer(
        x.reshape(b * s, d), b, s, g_mix[l], w_in[l], w_ck1[l], b_ck1[l], w_ck2[l], pe_k[l], w_cv1[l],
        b_cv1[l], w_cv2[l], pe_v[l], sgu_ln_g[l], sgu_ln_b[l], w_spatial[l], b_spatial[l], w_branch_a[l],
        w_branch_b[l], w_out[l], g_moe[l], w_router[l], b_router[l], w_gate_up[l], b_gate_up[l],
        w_down[l], b_down[l])
    return _combine(dest, rw, x1, g_final[None], y, ROUTE_TILE).reshape(b, s, d)
```

```python
import functools

import numpy as np
import jax
import jax.numpy as jnp
from jax import lax
from jax.experimental import pallas as pl
from jax.experimental.pallas import tpu as pltpu

D_MODEL = 1024
N_HEADS = 8
HEAD_DIM = 64
N_KV = 2
GQA_R = N_HEADS // N_KV
ATTN_WIDTH = N_HEADS * HEAD_DIM
KV_WIDTH = N_KV * HEAD_DIM
CMP_LEN = 32
CMP_STRIDE = 16
CMP_HID = 128
SEL_BLOCK = 64
SEL_TOPN = 16
WINDOW = 512
Q_BLOCK = 128
N_NSA_BRANCH = 3
SGU_WIDTH = 512
N_GROUPS_SGU = 8
SGU_GROUP_DIM = SGU_WIDTH // N_GROUPS_SGU
CHUNK = 128
N_EXPERTS = 32
TOP_K = 4
D_FF = D_MODEL
SWIGLU_LIMIT = 7.0
SWIGLU_ALPHA = 1.702
MOE_BLOCK = 512
RMS_EPS = 1e-5
LN_EPS = 1e-5
NEG = -1e30
FORCE = 1e4

LANES = 128
ROW_TILE = D_MODEL // LANES
MASK_BIAS = 1e9
LOG2E = float(np.log2(np.e))
KEY_CHUNK = 128
SEL_SLOTS = 4
GATE_ROWS = 16
TOKEN_TILE = 512
ROUTE_TILE = 512
VMEM_LIMIT = 56 * 1024 * 1024

F32 = jnp.float32
BF16 = jnp.bfloat16
I32 = jnp.int32

_NT = (((1,), (1,)), ((), ()))


def _dot(a, b):
    return jnp.dot(a, b, preferred_element_type=F32)


def _dot_nt(a, b):
    return lax.dot_general(a, b, _NT, preferred_element_type=F32)


def _sigmoid(x):
    return 1.0 / (1.0 + jnp.exp(-x))


def _gelu(x):
    c = np.float32(np.sqrt(2.0 / np.pi))
    return 0.5 * x * (1.0 + jnp.tanh(c * (x + 0.044715 * (x * x * x))))


def _rms(x, g):
    return x * lax.rsqrt(jnp.mean(x * x, axis=-1, keepdims=True) + RMS_EPS) * g


T_Q = 0
T_VS = T_Q + N_HEADS * LANES
T_VW = T_VS + N_KV * HEAD_DIM
T_GATE = T_VW + N_KV * HEAD_DIM
T_ROWS = T_GATE + N_KV * GATE_ROWS
K_SEL = 0
K_WIN = K_SEL + N_KV * 2 * LANES
K_CMP = K_WIN + N_KV * LANES
K_COLS = K_CMP + 4 * HEAD_DIM


def _proj_kernel(x_ref, gmix_ref, wt_ref, tconst_ref, wk_ref, kconst_ref, wu_ref, wv_ref, wmg_ref,
                 lng_ref, lnb_ref, wsp_ref, bsp_ref, wbb_ref,
                 qt_ref, gt_ref, vst_ref, vwt_ref, ksa_ref, kwa_ref, kcx_ref, vcx_ref, ga_ref, gbyb_ref):
    tm = x_ref.shape[0]
    hb = _rms(x_ref[...], gmix_ref[...]).astype(BF16)

    for h in range(tm // (2 * Q_BLOCK)):
        t = _dot_nt(wt_ref[...], hb[h * 2 * Q_BLOCK:(h + 1) * 2 * Q_BLOCK]) + tconst_ref[...]
        for jj in range(2):
            j = 2 * h + jj
            ls = slice(jj * Q_BLOCK, (jj + 1) * Q_BLOCK)
            for g in range(N_KV):
                for r in range(GQA_R):
                    r0 = T_Q + (g * GQA_R + r) * LANES
                    qt_ref[0, g, j, :, r * Q_BLOCK:(r + 1) * Q_BLOCK] = t[r0:r0 + LANES, ls].astype(BF16)
                vst_ref[0, g, j] = t[T_VS + g * HEAD_DIM:T_VS + (g + 1) * HEAD_DIM, ls].astype(BF16)
                vwt_ref[0, g, j] = t[T_VW + g * HEAD_DIM:T_VW + (g + 1) * HEAD_DIM, ls].astype(BF16)
                gt_ref[0, g, j] = _sigmoid(t[T_GATE + g * GATE_ROWS:T_GATE + (g + 1) * GATE_ROWS, ls])

    kk = _dot(hb, wk_ref[...])
    kconst = kconst_ref[...].astype(F32)
    for g in range(N_KV):
        c0 = K_SEL + g * 2 * LANES
        ksa_ref[0, g] = (kk[:, c0:c0 + 2 * LANES] + kconst).astype(BF16)
        c0 = K_WIN + g * LANES
        kwa_ref[0, g] = (kk[:, c0:c0 + LANES] + kconst[:, :LANES]).astype(BF16)
    kcx_ref[0] = kk[:, K_CMP:K_CMP + LANES]
    vcx_ref[0] = kk[:, K_CMP + LANES:K_COLS]

    mg = _dot(hb, wmg_ref[...])
    ga_ref[...] = _sigmoid(mg[:, :D_MODEL]).astype(BF16)

    u = _gelu(_dot(hb, wu_ref[...]))
    v = _gelu(_dot(hb, wv_ref[...]))
    mu = jnp.mean(v, axis=-1, keepdims=True)
    vc = v - mu
    var = jnp.mean(vc * vc, axis=-1, keepdims=True)
    vln = (vc * lax.rsqrt(var + LN_EPS) * lng_ref[...] + lnb_ref[...]).astype(BF16)

    row = lax.broadcasted_iota(I32, (CHUNK, CHUNK), 0)
    col = lax.broadcasted_iota(I32, (CHUNK, CHUNK), 1)
    tril = row >= col
    wsp = [jnp.where(tril, wsp_ref[g], 0.0).astype(BF16) for g in range(N_GROUPS_SGU)]
    low_half = col < SGU_GROUP_DIM
    bsp = bsp_ref[...]
    chunks = []
    for c in range(tm // CHUNK):
        rs = slice(c * CHUNK, (c + 1) * CHUNK)
        parts = []
        for p in range(SGU_WIDTH // LANES):
            cs = slice(p * LANES, (p + 1) * LANES)
            vblk = vln[rs, cs]
            mixed = jnp.where(low_half, _dot(wsp[2 * p], vblk), _dot(wsp[2 * p + 1], vblk))
            parts.append(u[rs, cs] * (mixed + bsp[:, cs]))
        chunks.append(jnp.concatenate(parts, axis=1))
    yb = jnp.concatenate(chunks, axis=0).astype(BF16)
    gbyb_ref[...] = (_sigmoid(mg[:, D_MODEL:]) * _dot(yb, wbb_ref[...])).astype(BF16)


def _proj(x2, b, s, gmix, wt, tconst, wk, kconst, wu, wv, wmg, lng, lnb, wsp, bsp, wbb, tm):
    n = x2.shape[0]
    tiles = s // tm
    qtiles = tm // Q_BLOCK
    full = lambda a: pl.BlockSpec(a.shape, lambda i, j: (0,) * a.ndim)
    rows = lambda w: pl.BlockSpec((tm, w), lambda i, j: (i * tiles + j, 0))
    per_g = lambda *blk: pl.BlockSpec((1, N_KV) + blk, lambda i, j: (i, 0, j) + (0,) * (len(blk) - 1))
    ins = (x2, gmix, wt, tconst, wk, kconst, wu, wv, wmg, lng, lnb, wsp, bsp, wbb)
    in_specs = [rows(D_MODEL)] + [full(a) for a in ins[1:]]
    in_specs[5] = pl.BlockSpec((tm, 2 * LANES), lambda i, j: (j, 0))
    nt = s // Q_BLOCK
    return pl.pallas_call(
        _proj_kernel,
        grid=(b, tiles),
        in_specs=in_specs,
        out_specs=[per_g(qtiles, LANES, GQA_R * Q_BLOCK), per_g(qtiles, GATE_ROWS, Q_BLOCK),
                   per_g(qtiles, HEAD_DIM, Q_BLOCK), per_g(qtiles, HEAD_DIM, Q_BLOCK),
                   per_g(tm, 2 * LANES), per_g(tm, LANES),
                   pl.BlockSpec((1, tm, LANES), lambda i, j: (i, j, 0)),
                   pl.BlockSpec((1, tm, LANES), lambda i, j: (i, j, 0)),
                   rows(D_MODEL), rows(D_MODEL)],
        out_shape=[jax.ShapeDtypeStruct((b, N_KV, nt, LANES, GQA_R * Q_BLOCK), BF16),
                   jax.ShapeDtypeStruct((b, N_KV, nt, GATE_ROWS, Q_BLOCK), F32),
                   jax.ShapeDtypeStruct((b, N_KV, nt, HEAD_DIM, Q_BLOCK), BF16),
                   jax.ShapeDtypeStruct((b, N_KV, nt, HEAD_DIM, Q_BLOCK), BF16),
                   jax.ShapeDtypeStruct((b, N_KV, s, 2 * LANES), BF16),
                   jax.ShapeDtypeStruct((b, N_KV, s, LANES), BF16),
                   jax.ShapeDtypeStruct((b, s, LANES), F32),
                   jax.ShapeDtypeStruct((b, s, LANES), F32),
                   jax.ShapeDtypeStruct((n, D_MODEL), BF16),
                   jax.ShapeDtypeStruct((n, D_MODEL), BF16)],
        compiler_params=pltpu.CompilerParams(dimension_semantics=("parallel", "parallel"),
                                             vmem_limit_bytes=VMEM_LIMIT),
        name="proj",
    )(*ins)


def _compress_kernel(xk_ref, xv_ref, pe_ref, w1_ref, b1_ref, w2k_ref, w2vt_ref, kconst_ref, kc_ref, vct_ref):
    nc = kc_ref.shape[2]
    half = CMP_LEN // 2
    hids = []
    for which, x_ref in enumerate((xk_ref, xv_ref)):
        za = jnp.zeros((nc, N_KV * CMP_HID), F32)
        zb = jnp.zeros((nc, N_KV * CMP_HID), F32)
        for l in range(half):
            xl = x_ref[0, pl.ds(l, nc, stride=CMP_STRIDE), :]
            za = za + _dot((xl + pe_ref[which, 0, l:l + 1, :]).astype(BF16), w1_ref[which, 0, l])
            zb = zb + _dot((xl + pe_ref[which, 1, l:l + 1, :]).astype(BF16), w1_ref[which, 1, l])
        hid = _gelu(za + pltpu.roll(zb, nc - 1, 0) + b1_ref[which])
        row = lax.broadcasted_iota(I32, hid.shape, 0)
        hids.append(jnp.where(row < nc - 1, hid, 0.0).astype(BF16))
    for g in range(N_KV):
        cs = slice(g * CMP_HID, (g + 1) * CMP_HID)
        kc_ref[0, g] = (_dot(hids[0][:, cs], w2k_ref[...]) + kconst_ref[...]).astype(BF16)
        vt = _dot_nt(w2vt_ref[...], hids[1][:, cs])
        for c in range(nc // KEY_CHUNK):
            vct_ref[0, g, c] = vt[:, c * KEY_CHUNK:(c + 1) * KEY_CHUNK].astype(BF16)


def _compress(xk, xv, pes, w1, b1, w2k, w2vt, kconst):
    b, s, w = xk.shape
    nc = s // CMP_STRIDE
    full = lambda a: pl.BlockSpec(a.shape, lambda i: (0,) * a.ndim)
    seq = pl.BlockSpec((1, s, w), lambda i: (i, 0, 0))
    return pl.pallas_call(
        _compress_kernel,
        grid=(b,),
        in_specs=[seq, seq, full(pes), full(w1), full(b1), full(w2k), full(w2vt), full(kconst)],
        out_specs=[pl.BlockSpec((1, N_KV, nc, LANES), lambda i: (i, 0, 0, 0)),
                   pl.BlockSpec((1, N_KV, nc // KEY_CHUNK, HEAD_DIM, KEY_CHUNK), lambda i: (i, 0, 0, 0, 0))],
        out_shape=[jax.ShapeDtypeStruct((b, N_KV, nc, LANES), BF16),
                   jax.ShapeDtypeStruct((b, N_KV, nc // KEY_CHUNK, HEAD_DIM, KEY_CHUNK), BF16)],
        compiler_params=pltpu.CompilerParams(dimension_semantics=("parallel",),
                                             vmem_limit_bytes=VMEM_LIMIT),
        name="compress",
    )(xk, xv, pes, w1, b1, w2k, w2vt, kconst)


def _pairs():
    return [slice(pr * 2 * Q_BLOCK, (pr + 1) * 2 * Q_BLOCK) for pr in range(GQA_R // 2)]


def _colmax(blocks):
    part = None
    for x in blocks:
        y = jnp.max(x.reshape(-1, 8, x.shape[-1]), axis=0)
        part = y if part is None else jnp.maximum(part, y)
    return jnp.max(part, axis=0, keepdims=True)


def _colsum(blocks):
    part = None
    for x in blocks:
        y = jnp.sum(x.reshape(-1, 8, x.shape[-1]), axis=0)
        part = y if part is None else part + y
    return jnp.sum(part, axis=0, keepdims=True)


def _softmax_pv(scores, vts, masks, m_ref, l_ref, acc_ref):
    for pr, ls in enumerate(_pairs()):
        sb = [x if masks[u] is None else masks[u](x, ls) for u, x in enumerate(scores[pr])]
        m_old = m_ref[0:1, ls]
        m_new = jnp.maximum(m_old, _colmax(sb))
        alpha = jnp.exp2(m_old - m_new)
        ps = [jnp.exp2(x - m_new) for x in sb]
        pv = _dot(vts[0], ps[0].astype(BF16))
        for u in range(1, len(ps)):
            pv = pv + _dot(vts[u], ps[u].astype(BF16))
        l_ref[0:1, ls] = alpha * l_ref[0:1, ls] + _colsum(ps)
        acc_ref[:, ls] = alpha * acc_ref[:, ls] + pv
        m_ref[0:1, ls] = m_new


def _nsa_kernel(qt_ref, gt_ref, kc_ref, vct_ref, ksa_ref, vst_ref, kwa_ref, vwt_ref, ovt_ref, tri_ref,
                o_ref, qa_ref, s0_ref, s1_ref, m_ref, l_ref, acc_ref, act_ref, *, n_sel, top_n):
    i = pl.program_id(1)
    t0 = i * Q_BLOCK
    width = GQA_R * Q_BLOCK
    n_chunks = ksa_ref.shape[2] // KEY_CHUNK
    n_cchunks = kc_ref.shape[2] // KEY_CHUNK
    n_slots = act_ref.shape[0] // N_KV
    lane = lax.broadcasted_iota(I32, (1, width), 1)
    tq = t0 + (lane & (Q_BLOCK - 1))
    sub = lax.broadcasted_iota(I32, (KEY_CHUNK, 1), 0)
    pairs = _pairs()
    heads = range(N_KV)

    def reset(g):
        m_ref[g] = jnp.full(m_ref.shape[1:], NEG, F32)
        l_ref[g] = jnp.zeros(l_ref.shape[1:], F32)
        acc_ref[g] = jnp.zeros(acc_ref.shape[1:], F32)

    sc = [[[_dot(kc_ref[0, g, u * KEY_CHUNK:(u + 1) * KEY_CHUNK, :], qt_ref[0, g, 0, :, ls])
            for u in range(n_cchunks)] for ls in pairs] for g in heads]
    o_c, imp_t = [], []
    for g in heads:
        ocg, ps = [], []
        for pr, ls in enumerate(pairs):
            sb = []
            for u in range(n_cchunks):
                c_end = (u * KEY_CHUNK + sub) * CMP_STRIDE + (CMP_LEN - 1)
                sb.append(jnp.where(c_end <= tq[:, ls], sc[g][pr][u], NEG))
            m = _colmax(sb)
            pb = [jnp.exp2(x - m) for x in sb]
            l = _colsum(pb)
            w = (tq[:, ls] >= CMP_LEN - 1).astype(F32) / l
            oc = _dot(vct_ref[0, g, 0], pb[0].astype(BF16))
            for u in range(1, n_cchunks):
                oc = oc + _dot(vct_ref[0, g, u], pb[u].astype(BF16))
            ocg.append(oc * w)
            ps.append([p[:, :Q_BLOCK] * w[:, :Q_BLOCK] + p[:, Q_BLOCK:] * w[:, Q_BLOCK:] for p in pb])
        imp2 = jnp.zeros((LANES, 2 * Q_BLOCK), F32)
        for u in range(n_cchunks):
            psu = ps[0][u] + ps[1][u]
            hi = psu.astype(BF16)
            lo = (psu - hi.astype(F32)).astype(BF16)
            imp2 = imp2 + _dot(ovt_ref[u], jnp.concatenate([hi, lo], axis=1))
        o_c.append(jnp.concatenate(ocg, axis=1))
        imp_t.append(imp2[:, :Q_BLOCK] + imp2[:, Q_BLOCK:])

    n_win = WINDOW // KEY_CHUNK
    cls, masks = [], []
    for u in range(n_win + 1):
        c = i - n_win + u
        cls.append(jnp.maximum(c, 0))
        off = jnp.where(c < 0, NEG, 0.0)
        if u == 0:
            edge = tri_ref[0] + off
            masks.append(lambda x, ls, edge=edge: x + edge)
        elif u == n_win:
            masks.append(lambda x, ls: x + tri_ref[1])
        else:
            masks.append(lambda x, ls, off=off: x + off)
    wsc = [[[_dot(kwa_ref[0, g, pl.ds(pl.multiple_of(cl * KEY_CHUNK, KEY_CHUNK), KEY_CHUNK), :],
                  qt_ref[0, g, 0, :, ls]) for cl in cls] for ls in pairs] for g in heads]
    o_w = []
    for g in heads:
        reset(g)
        _softmax_pv(wsc[g], [vwt_ref[0, g, cl] for cl in cls], masks, m_ref.at[g], l_ref.at[g], acc_ref.at[g])
        o_w.append(acc_ref[g] / l_ref[g, 0:1, :])

    j_io = lax.broadcasted_iota(I32, (LANES, Q_BLOCK), 0)
    tl = t0 + lax.broadcasted_iota(I32, (LANES, Q_BLOCK), 1)
    cur = tl // SEL_BLOCK
    forced = (j_io == 0) | (j_io == cur) | (j_io == cur - 1)
    prio = [jnp.where(j_io < n_sel, jnp.where(forced, FORCE, jnp.where(j_io * SEL_BLOCK <= tl, imp_t[g], -1.0)), NEG)
            for g in heads]
    for _ in range(top_n):
        for g in heads:
            m = jnp.max(prio[g], axis=0, keepdims=True)
            idx = jnp.min(jnp.where(prio[g] == m, j_io, LANES), axis=0, keepdims=True)
            prio[g] = jnp.where(j_io == idx, NEG, prio[g])
    sel_t = [jnp.where((prio[g] == NEG) & (j_io < n_sel), 1.0, 0.0) for g in heads]

    n_act = []
    for u in range(act_ref.shape[0]):
        act_ref[u] = jnp.int32(-1)
    for g in heads:
        qa_ref[g, 0:LANES, :] = qt_ref[0, g, 0]
        bias_t = ((sel_t[g] - 1.0) * MASK_BIAS).astype(BF16)
        for h in range(GQA_R):
            qa_ref[g, LANES:2 * LANES, h * Q_BLOCK:(h + 1) * Q_BLOCK] = bias_t
        any_q = jnp.max(sel_t[g], axis=1, keepdims=True)
        any_pair = jnp.maximum(any_q, pltpu.roll(any_q, LANES - 1, 0))
        na = jnp.int32(0)
        for c in range(n_chunks):
            act_ref[g * n_slots + na] = jnp.int32(c)
            na = na + jnp.where((any_pair[2 * c, 0] > 0.0) & (c < i), 1, 0)
        act_ref[g * n_slots + na] = i
        n_act.append(na)

    def sel_scores(g, k, dst):
        for u in range(SEL_SLOTS):
            cl = jnp.maximum(act_ref[g * n_slots + k * SEL_SLOTS + u], 0)
            keys = ksa_ref[0, g, pl.ds(pl.multiple_of(cl * KEY_CHUNK, KEY_CHUNK), KEY_CHUNK), :]
            for ls in pairs:
                dst[g, u, :, ls] = _dot(keys, qa_ref[g, :, ls])

    def sel_step(g, k, cur_s, nxt_s, last, ahead):
        if ahead:
            sel_scores(g, k + 1, nxt_s)
        vts, msk = [], []
        for u in range(SEL_SLOTS):
            c = act_ref[g * n_slots + k * SEL_SLOTS + u]
            vts.append(vst_ref[0, g, jnp.maximum(c, 0)])
            if last:
                kpos = jnp.where(c < 0, n_chunks * KEY_CHUNK, c * KEY_CHUNK) + sub
                msk.append(lambda x, ls, kpos=kpos: jnp.where(kpos <= tq[:, ls], x, -MASK_BIAS))
            else:
                msk.append(None)
        scores = [[cur_s[g, u, :, ls] for u in range(SEL_SLOTS)] for ls in pairs]
        _softmax_pv(scores, vts, msk, m_ref.at[g], l_ref.at[g], acc_ref.at[g])

    for g in heads:
        reset(g)
        sel_scores(g, 0, s0_ref)
    n_steps = [(n_act[g] + SEL_SLOTS) // SEL_SLOTS for g in heads]
    bufs = ((s0_ref, s1_ref), (s1_ref, s0_ref))

    def sel_body(k, carry):
        last = [k >= n_steps[g] - 1 for g in heads]
        for odd, (cur_s, nxt_s) in enumerate(bufs):
            for l0 in (False, True):
                for l1 in (False, True):
                    @pl.when((k % 2 == odd) & (last[0] == l0) & (last[1] == l1))
                    def _():
                        ahead = not (l0 and l1)
                        sel_step(0, k, cur_s, nxt_s, l0, ahead)
                        sel_step(1, k, cur_s, nxt_s, l1, ahead)

        return carry

    lax.fori_loop(0, jnp.maximum(n_steps[0], n_steps[1]), sel_body, 0)
    o_s = [acc_ref[g] / l_ref[g, 0:1, :] for g in heads]

    for g in heads:
        gt = gt_ref[0, g, 0]

        def gate(br):
            return jnp.concatenate([gt[br * GQA_R + h:br * GQA_R + h + 1, :] for h in range(GQA_R)], axis=1)

        o = gate(0) * o_c[g] + gate(1) * o_s[g] + gate(2) * o_w[g]
        o_ref[0, :, g * GQA_R * HEAD_DIM:(g + 1) * GQA_R * HEAD_DIM] = jnp.concatenate(
            [o[:, h * Q_BLOCK:(h + 1) * Q_BLOCK].T for h in range(GQA_R)], axis=1).astype(BF16)


def _nsa(qt, gt, kc, vct, ksa, vst, kwa, vwt, ovt, tri, n_sel):
    b, g, nt, _, width = qt.shape
    s = ksa.shape[2]
    tile = lambda a: pl.BlockSpec((1, g, 1) + a.shape[3:], lambda i, k: (i, 0, k, 0, 0))
    whole = lambda a: pl.BlockSpec((1,) + a.shape[1:], lambda i, k: (i,) + (0,) * (a.ndim - 1))
    const = lambda a: pl.BlockSpec(a.shape, lambda i, k: (0,) * a.ndim)
    kern = functools.partial(_nsa_kernel, n_sel=n_sel, top_n=min(SEL_TOPN, n_sel))
    return pl.pallas_call(
        kern,
        grid=(b, nt),
        in_specs=[tile(qt), tile(gt), whole(kc), whole(vct), whole(ksa), whole(vst), whole(kwa), whole(vwt),
                  const(ovt), const(tri)],
        out_specs=pl.BlockSpec((1, Q_BLOCK, ATTN_WIDTH), lambda i, k: (i, k, 0)),
        out_shape=jax.ShapeDtypeStruct((b, s, ATTN_WIDTH), BF16),
        scratch_shapes=[pltpu.VMEM((g, 2 * LANES, width), BF16),
                        pltpu.VMEM((g, SEL_SLOTS, KEY_CHUNK, width), F32),
                        pltpu.VMEM((g, SEL_SLOTS, KEY_CHUNK, width), F32),
                        pltpu.VMEM((g, 8, width), F32),
                        pltpu.VMEM((g, 8, width), F32),
                        pltpu.VMEM((g, HEAD_DIM, width), F32),
                        pltpu.SMEM((g * (s // KEY_CHUNK + 2 * SEL_SLOTS),), I32)],
        compiler_params=pltpu.CompilerParams(dimension_semantics=("parallel", "arbitrary"),
                                             vmem_limit_bytes=VMEM_LIMIT),
        name="nsa",
    )(qt, gt, kc, vct, ksa, vst, kwa, vwt, ovt, tri)


def _merge_kernel(x_ref, ya_ref, ga_ref, gbyb_ref, wba_ref, wout_ref, gmoe_ref, wr_ref, br_ref, utri_ref,
                  x1_ref, hm_ref, rw_ref, ri_ref, cnt_ref, carry_ref):
    tm = x_ref.shape[0]

    @pl.when(pl.program_id(0) == 0)
    def _():
        carry_ref[...] = jnp.zeros_like(carry_ref)

    merged = ga_ref[...].astype(F32) * _dot(ya_ref[...], wba_ref[...]) + gbyb_ref[...].astype(F32)
    x1 = x_ref[...] + _dot(merged.astype(BF16), wout_ref[...])
    x1_ref[...] = x1
    hm = _rms(x1, gmoe_ref[...])
    for a in range(ROW_TILE):
        hm_ref[pl.ds(a, tm, stride=ROW_TILE), :] = hm[:, a * LANES:(a + 1) * LANES]

    hh = hm.astype(BF16)
    hl = (hm - hh.astype(F32)).astype(BF16)
    both = _dot(hh, wr_ref[...])
    logits = both[:, :LANES] + both[:, LANES:] + _dot(hl, wr_ref[:, :LANES]) + br_ref[...]
    lg = logits.T[0:N_EXPERTS, :]
    e_io = lax.broadcasted_iota(I32, (N_EXPERTS, tm), 0)
    vals, idxs = [], []
    for _ in range(TOP_K):
        m = jnp.max(lg, axis=0, keepdims=True)
        idx = jnp.min(jnp.where(lg == m, e_io, N_EXPERTS), axis=0, keepdims=True)
        vals.append(m)
        idxs.append(idx)
        lg = jnp.where(e_io == idx, NEG, lg)
    ex = [jnp.exp(v - vals[0]) for v in vals]
    den = ex[0] + ex[1] + ex[2] + ex[3]

    hits = [e_io == idx for idx in idxs]
    multi = jnp.zeros((N_EXPERTS, tm), F32)
    for h in hits:
        multi = jnp.where(h, 1.0, multi)
    carry = carry_ref[:, 0:1]
    cum = _dot(multi.astype(BF16), utri_ref[...]) + carry
    ranks = [jnp.sum(jnp.where(h, cum, 0.0), axis=0, keepdims=True).astype(I32) for h in hits]
    ri_ref[...] = jnp.concatenate(idxs + ranks, axis=0)
    wts = jnp.concatenate([e / den for e in ex] + [jnp.zeros((LANES - TOP_K, tm), F32)], axis=0)
    rw_ref[...] = wts.T
    new_carry = carry + jnp.sum(multi, axis=1, keepdims=True)
    carry_ref[...] = jnp.broadcast_to(new_carry, carry_ref.shape)
    cnt_ref[...] = jnp.broadcast_to(new_carry, cnt_ref.shape)


def _merge(x2, ya, ga, gbyb, wba, wout, gmoe, wr, br, tm):
    n = x2.shape[0]
    full = lambda a: pl.BlockSpec(a.shape, lambda i: (0,) * a.ndim)
    rows = lambda w: pl.BlockSpec((tm, w), lambda i: (i, 0))
    utri = jnp.asarray(np.triu(np.ones((tm, tm), np.float32), 1), BF16)
    return pl.pallas_call(
        _merge_kernel,
        grid=(n // tm,),
        in_specs=[rows(D_MODEL), rows(ATTN_WIDTH), rows(D_MODEL), rows(D_MODEL),
                  full(wba), full(wout), full(gmoe), full(wr), full(br), full(utri)],
        out_specs=[rows(D_MODEL), pl.BlockSpec((tm * ROW_TILE, LANES), lambda i: (i, 0)), rows(LANES),
                   pl.BlockSpec((2 * TOP_K, tm), lambda i: (0, i)),
                   pl.BlockSpec((N_EXPERTS, LANES), lambda i: (0, 0))],
        out_shape=[jax.ShapeDtypeStruct((n, D_MODEL), F32),
                   jax.ShapeDtypeStruct((n * ROW_TILE, LANES), F32),
                   jax.ShapeDtypeStruct((n, LANES), F32),
                   jax.ShapeDtypeStruct((2 * TOP_K, n), I32),
                   jax.ShapeDtypeStruct((N_EXPERTS, LANES), F32)],
        scratch_shapes=[pltpu.VMEM((N_EXPERTS, LANES), F32)],
        compiler_params=pltpu.CompilerParams(dimension_semantics=("arbitrary",),
                                             vmem_limit_bytes=VMEM_LIMIT),
        name="merge",
    )(x2, ya, ga, gbyb, wba, wout, gmoe, wr, br, utri)


def _tile_copy(src, i, dst, d, sem):
    return pltpu.make_async_copy(src.at[pl.ds(pl.multiple_of(i * ROW_TILE, ROW_TILE), ROW_TILE)],
                                 dst.at[pl.ds(pl.multiple_of(d * ROW_TILE, ROW_TILE), ROW_TILE)], sem)


def _dispatch_kernel(seg_ref, dest_ref, hm_ref, xs_ref, zero_ref, sem, zsem, *, n_pad):
    tm = hm_ref.shape[0] // ROW_TILE

    @pl.when(pl.program_id(0) == 0)
    def _():
        zero_ref[...] = jnp.zeros_like(zero_ref)

        def seg(e, c):
            def fill(r, c2):
                _tile_copy(zero_ref, 0, xs_ref, r, zsem).start()
                return c2
            return lax.fori_loop(seg_ref[0, e], seg_ref[1, e], fill, c)

        lax.fori_loop(0, N_EXPERTS + 1, seg, 0)
        pad_rows = xs_ref.at[pl.ds(0, n_pad * ROW_TILE)]
        pltpu.make_async_copy(pad_rows, pad_rows, zsem).wait()

    def issue(r, c):
        slots = [dest_ref[r * TOP_K + k] for k in range(TOP_K)]
        for k in range(TOP_K):
            _tile_copy(hm_ref, r, xs_ref, slots[k], sem).start(priority=k % 2)
        return c

    lax.fori_loop(0, tm, issue, 0)
    for k in range(TOP_K):
        pltpu.make_async_copy(hm_ref, xs_ref.at[pl.ds(0, tm * ROW_TILE)], sem).wait()


def _dispatch(seg, dest_flat, hm, n_slots, tm):
    n = hm.shape[0] // ROW_TILE
    kern = functools.partial(_dispatch_kernel, n_pad=n_slots - n * TOP_K)
    return pl.pallas_call(
        kern,
        grid_spec=pltpu.PrefetchScalarGridSpec(
            num_scalar_prefetch=1,
            grid=(n // tm,),
            in_specs=[pl.BlockSpec((tm * TOP_K,), lambda i, sg: (i,), memory_space=pltpu.SMEM),
                      pl.BlockSpec((tm * ROW_TILE, LANES), lambda i, sg: (i, 0))],
            out_specs=pl.BlockSpec(memory_space=pl.ANY),
            scratch_shapes=[pltpu.VMEM((ROW_TILE, LANES), F32),
                            pltpu.SemaphoreType.DMA(()), pltpu.SemaphoreType.DMA(())]),
        out_shape=jax.ShapeDtypeStruct((n_slots * ROW_TILE, LANES), F32),
        compiler_params=pltpu.CompilerParams(dimension_semantics=("arbitrary",),
                                             has_side_effects=True),
        name="dispatch",
    )(seg, dest_flat, hm)


def _expert_kernel(be_ref, nu_ref, xs_ref, wgu_ref, bgu_ref, wd_ref, bd_ref, y_ref, wgu_bf, wd_bf):
    i = pl.program_id(0)

    @pl.when(i >= nu_ref[0])
    def _():
        y_ref[...] = jnp.zeros_like(y_ref)

    @pl.when((i == 0) | (be_ref[i] != be_ref[jnp.maximum(i - 1, 0)]))
    def _():
        wgu_bf[...] = wgu_ref[0].astype(BF16)
        wd_bf[...] = wd_ref[0].astype(BF16)

    @pl.when(i < nu_ref[0])
    def _():
        x = jnp.concatenate([xs_ref[pl.ds(a, MOE_BLOCK, stride=ROW_TILE), :] for a in range(ROW_TILE)], axis=1)
        gu = _dot(x.astype(BF16), wgu_bf[...]) + bgu_ref[0]
        gate = jnp.minimum(gu[:, :D_FF], SWIGLU_LIMIT)
        up = jnp.clip(gu[:, D_FF:], -SWIGLU_LIMIT, SWIGLU_LIMIT)
        act = gate * _sigmoid(SWIGLU_ALPHA * gate) * (up + 1.0)
        y = _dot(act.astype(BF16), wd_bf[...]) + bd_ref[0]
        for a in range(ROW_TILE):
            y_ref[pl.ds(a, MOE_BLOCK, stride=ROW_TILE), :] = y[:, a * LANES:(a + 1) * LANES]


def _experts(blk_expert, n_used, xs, wgu, bgu, wd, bd):
    n_blocks = xs.shape[0] // (MOE_BLOCK * ROW_TILE)
    blk = lambda i, be, nu: (jnp.minimum(i, nu[0] - 1), 0)
    exp3 = lambda i, be, nu: (be[jnp.minimum(i, nu[0] - 1)], 0, 0)
    return pl.pallas_call(
        _expert_kernel,
        grid_spec=pltpu.PrefetchScalarGridSpec(
            num_scalar_prefetch=2,
            grid=(n_blocks,),
            in_specs=[pl.BlockSpec((MOE_BLOCK * ROW_TILE, LANES), blk),
                      pl.BlockSpec((1, D_MODEL, 2 * D_FF), exp3),
                      pl.BlockSpec((1, 1, 2 * D_FF), exp3),
                      pl.BlockSpec((1, D_FF, D_MODEL), exp3),
                      pl.BlockSpec((1, 1, D_MODEL), exp3)],
            out_specs=pl.BlockSpec((MOE_BLOCK * ROW_TILE, LANES), lambda i, be, nu: (i, 0)),
            scratch_shapes=[pltpu.VMEM((D_MODEL, 2 * D_FF), BF16), pltpu.VMEM((D_FF, D_MODEL), BF16)]),
        out_shape=jax.ShapeDtypeStruct(xs.shape, F32),
        compiler_params=pltpu.CompilerParams(dimension_semantics=("arbitrary",),
                                             vmem_limit_bytes=VMEM_LIMIT),
        name="experts",
    )(blk_expert, n_used, xs, wgu, bgu, wd, bd)


def _combine_kernel(dest_ref, dest_next_ref, rw_ref, x1_ref, gfin_ref, y_ref, o_ref, ybuf0, ybuf1, sem0, sem1):
    tm = x1_ref.shape[0]
    i = pl.program_id(0)

    def gather(dst_ref, ybuf, sem):
        def issue(r, c):
            slots = [dst_ref[r * TOP_K + k] for k in range(TOP_K)]
            for k in range(TOP_K):
                _tile_copy(y_ref, slots[k], ybuf.at[k], r, sem).start(priority=k % 2)
            return c

        lax.fori_loop(0, tm, issue, 0)

    def finish(ybuf, sem):
        for k in range(TOP_K):
            pltpu.make_async_copy(y_ref.at[pl.ds(0, tm * ROW_TILE)], ybuf.at[k], sem).wait()
        rw = rw_ref[...]
        x1 = x1_ref[...]
        cols = []
        for a in range(ROW_TILE):
            acc = x1[:, a * LANES:(a + 1) * LANES]
            for k in range(TOP_K):
                acc = acc + rw[:, k:k + 1] * ybuf[k, pl.ds(a, tm, stride=ROW_TILE), :]
            cols.append(acc)
        o_ref[...] = _rms(jnp.concatenate(cols, axis=1), gfin_ref[...])

    @pl.when(i == 0)
    def _():
        gather(dest_ref, ybuf0, sem0)

    for parity, (cur, nxt) in enumerate((((ybuf0, sem0), (ybuf1, sem1)), ((ybuf1, sem1), (ybuf0, sem0)))):
        @pl.when(i % 2 == parity)
        def _():
            @pl.when(i + 1 < pl.num_programs(0))
            def _():
                gather(dest_next_ref, *nxt)

            finish(*cur)


def _combine(dest_flat, rw, x1, gfin, y, tm):
    n = x1.shape[0]
    steps = n // tm
    ybuf = pltpu.VMEM((TOP_K, tm * ROW_TILE, LANES), F32)
    return pl.pallas_call(
        _combine_kernel,
        grid=(steps,),
        in_specs=[pl.BlockSpec((tm * TOP_K,), lambda i: (i,), memory_space=pltpu.SMEM),
                  pl.BlockSpec((tm * TOP_K,), lambda i: (jnp.minimum(i + 1, steps - 1),), memory_space=pltpu.SMEM),
                  pl.BlockSpec((tm, LANES), lambda i: (i, 0)),
                  pl.BlockSpec((tm, D_MODEL), lambda i: (i, 0)),
                  pl.BlockSpec((1, D_MODEL), lambda i: (0, 0)),
                  pl.BlockSpec(memory_space=pl.ANY)],
        out_specs=pl.BlockSpec((tm, D_MODEL), lambda i: (i, 0)),
        out_shape=jax.ShapeDtypeStruct((n, D_MODEL), F32),
        scratch_shapes=[ybuf, ybuf, pltpu.SemaphoreType.DMA(()), pltpu.SemaphoreType.DMA(())],
        compiler_params=pltpu.CompilerParams(dimension_semantics=("arbitrary",),
                                             vmem_limit_bytes=VMEM_LIMIT),
        name="combine",
    )(dest_flat, dest_flat, rw, x1, gfin, y)


def _overlap_t(nc, n_cmp, n_sel):
    cs = np.arange(n_cmp)[None, :] * CMP_STRIDE
    ss = np.arange(n_sel)[:, None] * SEL_BLOCK
    ov = np.clip(np.minimum(cs + CMP_LEN, ss + SEL_BLOCK) - np.maximum(cs, ss), 0, None) / CMP_LEN
    out = np.zeros((LANES, nc), np.float32)
    out[:n_sel, :n_cmp] = ov
    return jnp.asarray(out, BF16)


def _layer(x2, b, s, g_mix, w_in, w_ck1, b_ck1, w_ck2, pe_k, w_cv1, b_cv1, w_cv2, pe_v,
           sgu_ln_g, sgu_ln_b, w_spatial, b_spatial, w_branch_a, w_branch_b, w_out,
           g_moe, w_router, b_router, w_gate_up, b_gate_up, w_down, b_down):
    n = b * s
    nc = s // CMP_STRIDE
    n_cmp = (s - CMP_LEN) // CMP_STRIDE + 1
    n_sel = s // SEL_BLOCK
    assert nc % KEY_CHUNK == 0 and n_sel <= LANES and n_cmp == nc - 1
    tm = TOKEN_TILE
    assert s % tm == 0 and n % ROUTE_TILE == 0

    p0 = ATTN_WIDTH
    p1 = p0 + 6 * KV_WIDTH
    p2 = p1 + N_NSA_BRANCH * N_HEADS
    p3 = p2 + SGU_WIDTH
    p4 = p3 + SGU_WIDTH
    zpad = lambda a, w: jnp.pad(a, ((0, 0),) * (a.ndim - 1) + ((0, w - a.shape[-1]),))
    wq = zpad((w_in[:, :p0] * (HEAD_DIM ** -0.5 * LOG2E)).reshape(D_MODEL, N_HEADS, HEAD_DIM), LANES)
    wkv = w_in[:, p0:p1].reshape(D_MODEL, 6, N_KV, HEAD_DIM)
    wng = w_in[:, p1:p2].reshape(D_MODEL, N_KV, GQA_R, N_NSA_BRANCH).transpose(0, 1, 3, 2)
    wng = zpad(wng.reshape(D_MODEL, N_KV, N_NSA_BRANCH * GQA_R), GATE_ROWS)
    wt = jnp.concatenate([wq.reshape(D_MODEL, -1), wkv[:, 3].reshape(D_MODEL, -1),
                          wkv[:, 5].reshape(D_MODEL, -1), wng.reshape(D_MODEL, -1)], axis=1).T.astype(BF16)
    slopes = 2.0 ** (-8.0 * np.arange(1, N_HEADS + 1) / N_HEADS)
    tcol = np.zeros((T_ROWS, 1), np.float32)
    head_rows = T_Q + np.arange(N_HEADS) * LANES + HEAD_DIM
    bf16_round = lambda a: a.astype(BF16).astype(np.float32)
    for k, coef in enumerate((slopes * SEL_BLOCK * LOG2E, slopes * LOG2E)):
        hi = bf16_round(coef.astype(np.float32))
        tcol[head_rows + 2 * k, 0] = hi
        tcol[head_rows + 2 * k + 1, 0] = bf16_round(coef.astype(np.float32) - hi)
    tconst = jnp.asarray(np.broadcast_to(tcol, (T_ROWS, 2 * Q_BLOCK)))
    wk = jnp.concatenate([zpad(wkv[:, 2, g], 2 * LANES) for g in range(N_KV)]
                         + [zpad(wkv[:, 4, g], LANES) for g in range(N_KV)]
                         + [wkv[:, 0].reshape(D_MODEL, -1), wkv[:, 1].reshape(D_MODEL, -1)], axis=1).astype(BF16)
    pos = np.arange(s)
    kc_np = np.zeros((s, 2 * LANES), np.float32)
    kc_np[:, HEAD_DIM:HEAD_DIM + 2] = (pos // SEL_BLOCK)[:, None]
    kc_np[:, HEAD_DIM + 2:HEAD_DIM + 4] = (pos % SEL_BLOCK)[:, None]
    kc_np[pos, LANES + pos // SEL_BLOCK] = 1.0
    kconst = jnp.asarray(kc_np, BF16)
    wu = w_in[:, p2:p3].astype(BF16)
    wv = w_in[:, p3:p4].astype(BF16)
    wmg = w_in[:, p4:].astype(BF16)
    bsp = jnp.repeat(b_spatial.T, SGU_GROUP_DIM, axis=1)

    qt, gt, vst, vwt, ksa, kwa, kcx, vcx, ga, gbyb = _proj(
        x2, b, s, g_mix[None], wt, tconst, wk, kconst, wu, wv, wmg, sgu_ln_g[None], sgu_ln_b[None],
        w_spatial, bsp, w_branch_b.astype(BF16), tm)

    half = CMP_LEN // 2
    eye = jnp.eye(N_KV, dtype=F32)[None, None, :, None, :, None]
    bdiag = lambda w: (w.reshape(2, half, 1, HEAD_DIM, 1, CMP_HID) * eye).reshape(
        2, half, N_KV * HEAD_DIM, N_KV * CMP_HID)
    w1 = jnp.stack([bdiag(w_ck1), bdiag(w_cv1)]).astype(BF16)
    pes = jnp.stack([jnp.tile(pe_k.reshape(2, half, HEAD_DIM), (1, 1, N_KV)),
                     jnp.tile(pe_v.reshape(2, half, HEAD_DIM), (1, 1, N_KV))])
    b1 = jnp.stack([jnp.tile(b_ck1, N_KV), jnp.tile(b_cv1, N_KV)])[:, None, :]
    blk_n = np.arange(nc)
    cc_np = np.zeros((nc, LANES), np.float32)
    cc_np[:, HEAD_DIM:HEAD_DIM + 2] = (blk_n // (SEL_BLOCK // CMP_STRIDE))[:, None]
    cc_np[:, HEAD_DIM + 2:HEAD_DIM + 4] = (blk_n % (SEL_BLOCK // CMP_STRIDE) * CMP_STRIDE)[:, None]
    kc, vct = _compress(kcx, vcx, pes, w1, b1, zpad(w_ck2, LANES).astype(BF16), w_cv2.T.astype(BF16),
                        jnp.asarray(cc_np))

    ovt = _overlap_t(nc, n_cmp, n_sel).reshape(LANES, nc // KEY_CHUNK, KEY_CHUNK).transpose(1, 0, 2)
    a_io, q_io = np.meshgrid(np.arange(KEY_CHUNK), np.arange(Q_BLOCK), indexing="ij")
    tri = np.stack([np.where(a_io > q_io, 0.0, NEG), np.where(a_io <= q_io, 0.0, NEG)]).astype(np.float32)
    tri = jnp.asarray(np.tile(tri, (1, 1, 2)))
    ya = _nsa(qt, gt, kc, vct, ksa, vst, kwa, vwt, ovt, tri, n_sel).reshape(n, ATTN_WIDTH)

    wr = jnp.pad(w_router, ((0, 0), (0, LANES - N_EXPERTS)))
    wrh = wr.astype(BF16)
    wr2 = jnp.concatenate([wrh, (wr - wrh.astype(F32)).astype(BF16)], axis=1)
    br = jnp.pad(b_router, (0, LANES - N_EXPERTS))[None]
    x1, hm, rw, ri, cnt = _merge(x2, ya, ga, gbyb, w_branch_a.astype(BF16), w_out.astype(BF16),
                                 g_moe[None], wr2, br, tm)

    counts = cnt[:, 0].astype(I32)
    padded = (counts + MOE_BLOCK - 1) // MOE_BLOCK * MOE_BLOCK
    pad_end = jnp.cumsum(padded)
    pad_start = pad_end - padded
    base = sum(jnp.where(ri[:TOP_K] == e, pad_start[e], 0) for e in range(N_EXPERTS))
    dest = (base + ri[TOP_K:]).T.reshape(-1)
    n_blocks = -(-(n * TOP_K) // MOE_BLOCK) + N_EXPERTS
    blk_start = jnp.arange(n_blocks, dtype=I32) * MOE_BLOCK
    blk_expert = jnp.minimum(jnp.sum((pad_end[None, :] <= blk_start[:, None]).astype(I32), axis=1),
                             N_EXPERTS - 1)
    n_used = (pad_end[-1:] // MOE_BLOCK).astype(I32)
    n_slots = n_blocks * MOE_BLOCK
    seg = jnp.stack([jnp.concatenate([pad_start + counts, pad_end[-1:]]),
                     jnp.concatenate([pad_end, jnp.full((1,), n_slots, I32)])]).astype(I32)

    xs = _dispatch(seg, dest, hm, n_slots, ROUTE_TILE)
    y = _experts(blk_expert, n_used, xs, w_gate_up, b_gate_up[:, None, :], w_down, b_down[:, None, :])
    return dest, rw, x1, y


def kernel(x, g_mix, w_in, w_ck1, b_ck1, w_ck2, pe_k, w_cv1, b_cv1, w_cv2, pe_v, sgu_ln_g, sgu_ln_b, w_spatial, b_spatial, w_branch_a, w_branch_b, w_out, g_moe, w_router, b_router, w_gate_up, b_gate_up, w_down, b_down, g_final):
    b, s, d = x.shape
    assert g_mix.shape[0] == 1, "the final rmsnorm is fused into the single layer's combine step"
    l = 0
    dest, rw, x1, y = _layer(
        x.reshape(b * s, d), b, s, g_mix[l], w_in[l], w_ck1[l], b_ck1[l], w_ck2[l], pe_k[l], w_cv1[l],
        b_cv1[l], w_cv2[l], pe_v[l], sgu_ln_g[l], sgu_ln_b[l], w_spatial[l], b_spatial[l], w_branch_a[l],
        w_branch_b[l], w_out[l], g_moe[l], w_router[l], b_router[l], w_gate_up[l], b_gate_up[l],
        w_down[l], b_down[l])
    return _combine(dest, rw, x1, g_final[None], y, ROUTE_TILE).reshape(b, s, d)
```

```python
import functools

import numpy as np
import jax
import jax.numpy as jnp
from jax import lax
from jax.experimental import pallas as pl
from jax.experimental.pallas import tpu as pltpu

D_MODEL = 1024
N_HEADS = 8
HEAD_DIM = 64
N_KV = 2
GQA_R = N_HEADS // N_KV
ATTN_WIDTH = N_HEADS * HEAD_DIM
KV_WIDTH = N_KV * HEAD_DIM
CMP_LEN = 32
CMP_STRIDE = 16
CMP_HID = 128
SEL_BLOCK = 64
SEL_TOPN = 16
WINDOW = 512
Q_BLOCK = 128
N_NSA_BRANCH = 3
SGU_WIDTH = 512
N_GROUPS_SGU = 8
SGU_GROUP_DIM = SGU_WIDTH // N_GROUPS_SGU
CHUNK = 128
N_EXPERTS = 32
TOP_K = 4
D_FF = D_MODEL
SWIGLU_LIMIT = 7.0
SWIGLU_ALPHA = 1.702
MOE_BLOCK = 512
RMS_EPS = 1e-5
LN_EPS = 1e-5
NEG = -1e30
FORCE = 1e4

LANES = 128
ROW_TILE = D_MODEL // LANES
MASK_BIAS = 1e9
LOG2E = float(np.log2(np.e))
KEY_CHUNK = 128
SEL_SLOTS = 4
GATE_ROWS = 16
TOKEN_TILE = 512
SCATTER_TILE = 2048
ROUTE_TILE = 512
VMEM_LIMIT = 56 * 1024 * 1024

F32 = jnp.float32
BF16 = jnp.bfloat16
I32 = jnp.int32

_NT = (((1,), (1,)), ((), ()))


def _dot(a, b):
    return jnp.dot(a, b, preferred_element_type=F32)


def _dot_nt(a, b):
    return lax.dot_general(a, b, _NT, preferred_element_type=F32)


def _sigmoid(x):
    return 1.0 / (1.0 + jnp.exp(-x))


def _gelu(x):
    c = np.float32(np.sqrt(2.0 / np.pi))
    return 0.5 * x * (1.0 + jnp.tanh(c * (x + 0.044715 * (x * x * x))))


def _rms(x, g):
    return x * lax.rsqrt(jnp.mean(x * x, axis=-1, keepdims=True) + RMS_EPS) * g


T_Q = 0
T_VS = T_Q + N_HEADS * LANES
T_VW = T_VS + N_KV * HEAD_DIM
T_GATE = T_VW + N_KV * HEAD_DIM
T_ROWS = T_GATE + N_KV * GATE_ROWS
K_SEL = 0
K_WIN = K_SEL + N_KV * 2 * LANES
K_CMP = K_WIN + N_KV * LANES
K_COLS = K_CMP + 4 * HEAD_DIM


def _proj_kernel(x_ref, gmix_ref, wt_ref, tconst_ref, wk_ref, kconst_ref, wu_ref, wv_ref, wmg_ref,
                 lng_ref, lnb_ref, wsp_ref, bsp_ref, wbb_ref,
                 qt_ref, gt_ref, vst_ref, vwt_ref, ksa_ref, kwa_ref, kcx_ref, vcx_ref, ga_ref, gbyb_ref):
    tm = x_ref.shape[0]
    hb = _rms(x_ref[...], gmix_ref[...]).astype(BF16)

    for h in range(tm // (2 * Q_BLOCK)):
        t = _dot_nt(wt_ref[...], hb[h * 2 * Q_BLOCK:(h + 1) * 2 * Q_BLOCK]) + tconst_ref[...]
        for jj in range(2):
            j = 2 * h + jj
            ls = slice(jj * Q_BLOCK, (jj + 1) * Q_BLOCK)
            for g in range(N_KV):
                for r in range(GQA_R):
                    r0 = T_Q + (g * GQA_R + r) * LANES
                    qt_ref[0, g, j, :, r * Q_BLOCK:(r + 1) * Q_BLOCK] = t[r0:r0 + LANES, ls].astype(BF16)
                vst_ref[0, g, j] = t[T_VS + g * HEAD_DIM:T_VS + (g + 1) * HEAD_DIM, ls].astype(BF16)
                vwt_ref[0, g, j] = t[T_VW + g * HEAD_DIM:T_VW + (g + 1) * HEAD_DIM, ls].astype(BF16)
                gt_ref[0, g, j] = _sigmoid(t[T_GATE + g * GATE_ROWS:T_GATE + (g + 1) * GATE_ROWS, ls])

    kk = _dot(hb, wk_ref[...])
    kconst = kconst_ref[...].astype(F32)
    for g in range(N_KV):
        c0 = K_SEL + g * 2 * LANES
        ksa_ref[0, g] = (kk[:, c0:c0 + 2 * LANES] + kconst).astype(BF16)
        c0 = K_WIN + g * LANES
        kwa_ref[0, g] = (kk[:, c0:c0 + LANES] + kconst[:, :LANES]).astype(BF16)
    kcx_ref[0] = kk[:, K_CMP:K_CMP + LANES]
    vcx_ref[0] = kk[:, K_CMP + LANES:K_COLS]

    mg = _dot(hb, wmg_ref[...])
    ga_ref[...] = _sigmoid(mg[:, :D_MODEL]).astype(BF16)

    u = _gelu(_dot(hb, wu_ref[...]))
    v = _gelu(_dot(hb, wv_ref[...]))
    mu = jnp.mean(v, axis=-1, keepdims=True)
    vc = v - mu
    var = jnp.mean(vc * vc, axis=-1, keepdims=True)
    vln = (vc * lax.rsqrt(var + LN_EPS) * lng_ref[...] + lnb_ref[...]).astype(BF16)

    row = lax.broadcasted_iota(I32, (CHUNK, CHUNK), 0)
    col = lax.broadcasted_iota(I32, (CHUNK, CHUNK), 1)
    tril = row >= col
    wsp = [jnp.where(tril, wsp_ref[g], 0.0).astype(BF16) for g in range(N_GROUPS_SGU)]
    low_half = col < SGU_GROUP_DIM
    bsp = bsp_ref[...]
    chunks = []
    for c in range(tm // CHUNK):
        rs = slice(c * CHUNK, (c + 1) * CHUNK)
        parts = []
        for p in range(SGU_WIDTH // LANES):
            cs = slice(p * LANES, (p + 1) * LANES)
            vblk = vln[rs, cs]
            mixed = jnp.where(low_half, _dot(wsp[2 * p], vblk), _dot(wsp[2 * p + 1], vblk))
            parts.append(u[rs, cs] * (mixed + bsp[:, cs]))
        chunks.append(jnp.concatenate(parts, axis=1))
    yb = jnp.concatenate(chunks, axis=0).astype(BF16)
    gbyb_ref[...] = (_sigmoid(mg[:, D_MODEL:]) * _dot(yb, wbb_ref[...])).astype(BF16)


def _proj(x2, b, s, gmix, wt, tconst, wk, kconst, wu, wv, wmg, lng, lnb, wsp, bsp, wbb, tm):
    n = x2.shape[0]
    tiles = s // tm
    qtiles = tm // Q_BLOCK
    full = lambda a: pl.BlockSpec(a.shape, lambda i, j: (0,) * a.ndim)
    rows = lambda w: pl.BlockSpec((tm, w), lambda i, j: (i * tiles + j, 0))
    per_g = lambda *blk: pl.BlockSpec((1, N_KV) + blk, lambda i, j: (i, 0, j) + (0,) * (len(blk) - 1))
    ins = (x2, gmix, wt, tconst, wk, kconst, wu, wv, wmg, lng, lnb, wsp, bsp, wbb)
    in_specs = [rows(D_MODEL)] + [full(a) for a in ins[1:]]
    in_specs[5] = pl.BlockSpec((tm, 2 * LANES), lambda i, j: (j, 0))
    nt = s // Q_BLOCK
    return pl.pallas_call(
        _proj_kernel,
        grid=(b, tiles),
        in_specs=in_specs,
        out_specs=[per_g(qtiles, LANES, GQA_R * Q_BLOCK), per_g(qtiles, GATE_ROWS, Q_BLOCK),
                   per_g(qtiles, HEAD_DIM, Q_BLOCK), per_g(qtiles, HEAD_DIM, Q_BLOCK),
                   per_g(tm, 2 * LANES), per_g(tm, LANES),
                   pl.BlockSpec((1, tm, LANES), lambda i, j: (i, j, 0)),
                   pl.BlockSpec((1, tm, LANES), lambda i, j: (i, j, 0)),
                   rows(D_MODEL), rows(D_MODEL)],
        out_shape=[jax.ShapeDtypeStruct((b, N_KV, nt, LANES, GQA_R * Q_BLOCK), BF16),
                   jax.ShapeDtypeStruct((b, N_KV, nt, GATE_ROWS, Q_BLOCK), F32),
                   jax.ShapeDtypeStruct((b, N_KV, nt, HEAD_DIM, Q_BLOCK), BF16),
                   jax.ShapeDtypeStruct((b, N_KV, nt, HEAD_DIM, Q_BLOCK), BF16),
                   jax.ShapeDtypeStruct((b, N_KV, s, 2 * LANES), BF16),
                   jax.ShapeDtypeStruct((b, N_KV, s, LANES), BF16),
                   jax.ShapeDtypeStruct((b, s, LANES), F32),
                   jax.ShapeDtypeStruct((b, s, LANES), F32),
                   jax.ShapeDtypeStruct((n, D_MODEL), BF16),
                   jax.ShapeDtypeStruct((n, D_MODEL), BF16)],
        compiler_params=pltpu.CompilerParams(dimension_semantics=("parallel", "parallel"),
                                             vmem_limit_bytes=VMEM_LIMIT),
        name="proj",
    )(*ins)


def _compress_kernel(xk_ref, xv_ref, pe_ref, w1_ref, b1_ref, w2k_ref, w2vt_ref, kconst_ref, kc_ref, vct_ref):
    nc = kc_ref.shape[2]
    half = CMP_LEN // 2
    hids = []
    for which, x_ref in enumerate((xk_ref, xv_ref)):
        za = jnp.zeros((nc, N_KV * CMP_HID), F32)
        zb = jnp.zeros((nc, N_KV * CMP_HID), F32)
        for l in range(half):
            xl = x_ref[0, pl.ds(l, nc, stride=CMP_STRIDE), :]
            za = za + _dot((xl + pe_ref[which, 0, l:l + 1, :]).astype(BF16), w1_ref[which, 0, l])
            zb = zb + _dot((xl + pe_ref[which, 1, l:l + 1, :]).astype(BF16), w1_ref[which, 1, l])
        hid = _gelu(za + pltpu.roll(zb, nc - 1, 0) + b1_ref[which])
        row = lax.broadcasted_iota(I32, hid.shape, 0)
        hids.append(jnp.where(row < nc - 1, hid, 0.0).astype(BF16))
    for g in range(N_KV):
        cs = slice(g * CMP_HID, (g + 1) * CMP_HID)
        kc_ref[0, g] = (_dot(hids[0][:, cs], w2k_ref[...]) + kconst_ref[...]).astype(BF16)
        vt = _dot_nt(w2vt_ref[...], hids[1][:, cs])
        for c in range(nc // KEY_CHUNK):
            vct_ref[0, g, c] = vt[:, c * KEY_CHUNK:(c + 1) * KEY_CHUNK].astype(BF16)


def _compress(xk, xv, pes, w1, b1, w2k, w2vt, kconst):
    b, s, w = xk.shape
    nc = s // CMP_STRIDE
    full = lambda a: pl.BlockSpec(a.shape, lambda i: (0,) * a.ndim)
    seq = pl.BlockSpec((1, s, w), lambda i: (i, 0, 0))
    return pl.pallas_call(
        _compress_kernel,
        grid=(b,),
        in_specs=[seq, seq, full(pes), full(w1), full(b1), full(w2k), full(w2vt), full(kconst)],
        out_specs=[pl.BlockSpec((1, N_KV, nc, LANES), lambda i: (i, 0, 0, 0)),
                   pl.BlockSpec((1, N_KV, nc // KEY_CHUNK, HEAD_DIM, KEY_CHUNK), lambda i: (i, 0, 0, 0, 0))],
        out_shape=[jax.ShapeDtypeStruct((b, N_KV, nc, LANES), BF16),
                   jax.ShapeDtypeStruct((b, N_KV, nc // KEY_CHUNK, HEAD_DIM, KEY_CHUNK), BF16)],
        compiler_params=pltpu.CompilerParams(dimension_semantics=("parallel",),
                                             vmem_limit_bytes=VMEM_LIMIT),
        name="compress",
    )(xk, xv, pes, w1, b1, w2k, w2vt, kconst)


def _pairs():
    return [slice(pr * 2 * Q_BLOCK, (pr + 1) * 2 * Q_BLOCK) for pr in range(GQA_R // 2)]


def _colmax(blocks):
    part = None
    for x in blocks:
        y = jnp.max(x.reshape(-1, 8, x.shape[-1]), axis=0)
        part = y if part is None else jnp.maximum(part, y)
    return jnp.max(part, axis=0, keepdims=True)


def _colsum(blocks):
    part = None
    for x in blocks:
        y = jnp.sum(x.reshape(-1, 8, x.shape[-1]), axis=0)
        part = y if part is None else part + y
    return jnp.sum(part, axis=0, keepdims=True)


def _softmax_pv(scores, vts, masks, m_ref, l_ref, acc_ref):
    for pr, ls in enumerate(_pairs()):
        sb = [x if masks[u] is None else masks[u](x, ls) for u, x in enumerate(scores[pr])]
        m_old = m_ref[0:1, ls]
        m_new = jnp.maximum(m_old, _colmax(sb))
        alpha = jnp.exp2(m_old - m_new)
        ps = [jnp.exp2(x - m_new) for x in sb]
        pv = _dot(vts[0], ps[0].astype(BF16))
        for u in range(1, len(ps)):
            pv = pv + _dot(vts[u], ps[u].astype(BF16))
        l_ref[0:1, ls] = alpha * l_ref[0:1, ls] + _colsum(ps)
        acc_ref[:, ls] = alpha * acc_ref[:, ls] + pv
        m_ref[0:1, ls] = m_new


def _nsa_kernel(qt_ref, gt_ref, kc_ref, vct_ref, ksa_ref, vst_ref, kwa_ref, vwt_ref, ovt_ref, tri_ref,
                o_ref, qa_ref, s0_ref, s1_ref, m_ref, l_ref, acc_ref, act_ref, *, n_sel, top_n):
    i = pl.program_id(1)
    t0 = i * Q_BLOCK
    width = GQA_R * Q_BLOCK
    n_chunks = ksa_ref.shape[2] // KEY_CHUNK
    n_cchunks = kc_ref.shape[2] // KEY_CHUNK
    n_slots = act_ref.shape[0] // N_KV
    lane = lax.broadcasted_iota(I32, (1, width), 1)
    tq = t0 + (lane & (Q_BLOCK - 1))
    sub = lax.broadcasted_iota(I32, (KEY_CHUNK, 1), 0)
    pairs = _pairs()
    heads = range(N_KV)

    def reset(g):
        m_ref[g] = jnp.full(m_ref.shape[1:], NEG, F32)
        l_ref[g] = jnp.zeros(l_ref.shape[1:], F32)
        acc_ref[g] = jnp.zeros(acc_ref.shape[1:], F32)

    sc = [[[_dot(kc_ref[0, g, u * KEY_CHUNK:(u + 1) * KEY_CHUNK, :], qt_ref[0, g, 0, :, ls])
            for u in range(n_cchunks)] for ls in pairs] for g in heads]
    o_c, imp_t = [], []
    for g in heads:
        ocg, ps = [], []
        for pr, ls in enumerate(pairs):
            sb = []
            for u in range(n_cchunks):
                c_end = (u * KEY_CHUNK + sub) * CMP_STRIDE + (CMP_LEN - 1)
                sb.append(jnp.where(c_end <= tq[:, ls], sc[g][pr][u], NEG))
            m = _colmax(sb)
            pb = [jnp.exp2(x - m) for x in sb]
            l = _colsum(pb)
            w = (tq[:, ls] >= CMP_LEN - 1).astype(F32) / l
            oc = _dot(vct_ref[0, g, 0], pb[0].astype(BF16))
            for u in range(1, n_cchunks):
                oc = oc + _dot(vct_ref[0, g, u], pb[u].astype(BF16))
            ocg.append(oc * w)
            ps.append([p[:, :Q_BLOCK] * w[:, :Q_BLOCK] + p[:, Q_BLOCK:] * w[:, Q_BLOCK:] for p in pb])
        imp2 = jnp.zeros((LANES, 2 * Q_BLOCK), F32)
        for u in range(n_cchunks):
            psu = ps[0][u] + ps[1][u]
            hi = psu.astype(BF16)
            lo = (psu - hi.astype(F32)).astype(BF16)
            imp2 = imp2 + _dot(ovt_ref[u], jnp.concatenate([hi, lo], axis=1))
        o_c.append(jnp.concatenate(ocg, axis=1))
        imp_t.append(imp2[:, :Q_BLOCK] + imp2[:, Q_BLOCK:])

    n_win = WINDOW // KEY_CHUNK
    cls, masks = [], []
    for u in range(n_win + 1):
        c = i - n_win + u
        cls.append(jnp.maximum(c, 0))
        off = jnp.where(c < 0, NEG, 0.0)
        if u == 0:
            edge = tri_ref[0] + off
            masks.append(lambda x, ls, edge=edge: x + edge)
        elif u == n_win:
            masks.append(lambda x, ls: x + tri_ref[1])
        else:
            masks.append(lambda x, ls, off=off: x + off)
    wsc = [[[_dot(kwa_ref[0, g, pl.ds(pl.multiple_of(cl * KEY_CHUNK, KEY_CHUNK), KEY_CHUNK), :],
                  qt_ref[0, g, 0, :, ls]) for cl in cls] for ls in pairs] for g in heads]
    o_w = []
    for g in heads:
        reset(g)
        _softmax_pv(wsc[g], [vwt_ref[0, g, cl] for cl in cls], masks, m_ref.at[g], l_ref.at[g], acc_ref.at[g])
        o_w.append(acc_ref[g] / l_ref[g, 0:1, :])

    j_io = lax.broadcasted_iota(I32, (LANES, Q_BLOCK), 0)
    tl = t0 + lax.broadcasted_iota(I32, (LANES, Q_BLOCK), 1)
    cur = tl // SEL_BLOCK
    forced = (j_io == 0) | (j_io == cur) | (j_io == cur - 1)
    prio = [jnp.where(j_io < n_sel, jnp.where(forced, FORCE, jnp.where(j_io * SEL_BLOCK <= tl, imp_t[g], -1.0)), NEG)
            for g in heads]
    for _ in range(top_n):
        for g in heads:
            m = jnp.max(prio[g], axis=0, keepdims=True)
            idx = jnp.min(jnp.where(prio[g] == m, j_io, LANES), axis=0, keepdims=True)
            prio[g] = jnp.where(j_io == idx, NEG, prio[g])
    sel_t = [jnp.where((prio[g] == NEG) & (j_io < n_sel), 1.0, 0.0) for g in heads]

    n_act = []
    for u in range(act_ref.shape[0]):
        act_ref[u] = jnp.int32(-1)
    for g in heads:
        qa_ref[g, 0:LANES, :] = qt_ref[0, g, 0]
        bias_t = ((sel_t[g] - 1.0) * MASK_BIAS).astype(BF16)
        for h in range(GQA_R):
            qa_ref[g, LANES:2 * LANES, h * Q_BLOCK:(h + 1) * Q_BLOCK] = bias_t
        any_q = jnp.max(sel_t[g], axis=1, keepdims=True)
        any_pair = jnp.maximum(any_q, pltpu.roll(any_q, LANES - 1, 0))
        na = jnp.int32(0)
        for c in range(n_chunks):
            act_ref[g * n_slots + na] = jnp.int32(c)
            na = na + jnp.where((any_pair[2 * c, 0] > 0.0) & (c < i), 1, 0)
        act_ref[g * n_slots + na] = i
        n_act.append(na)

    def sel_scores(g, k, dst):
        for u in range(SEL_SLOTS):
            cl = jnp.maximum(act_ref[g * n_slots + k * SEL_SLOTS + u], 0)
            keys = ksa_ref[0, g, pl.ds(pl.multiple_of(cl * KEY_CHUNK, KEY_CHUNK), KEY_CHUNK), :]
            for ls in pairs:
                dst[g, u, :, ls] = _dot(keys, qa_ref[g, :, ls])

    def sel_step(g, k, cur_s, nxt_s, last, ahead):
        if ahead:
            sel_scores(g, k + 1, nxt_s)
        vts, msk = [], []
        for u in range(SEL_SLOTS):
            c = act_ref[g * n_slots + k * SEL_SLOTS + u]
            vts.append(vst_ref[0, g, jnp.maximum(c, 0)])
            if last:
                kpos = jnp.where(c < 0, n_chunks * KEY_CHUNK, c * KEY_CHUNK) + sub
                msk.append(lambda x, ls, kpos=kpos: jnp.where(kpos <= tq[:, ls], x, -MASK_BIAS))
            else:
                msk.append(None)
        scores = [[cur_s[g, u, :, ls] for u in range(SEL_SLOTS)] for ls in pairs]
        _softmax_pv(scores, vts, msk, m_ref.at[g], l_ref.at[g], acc_ref.at[g])

    for g in heads:
        reset(g)
        sel_scores(g, 0, s0_ref)
    n_steps = [(n_act[g] + SEL_SLOTS) // SEL_SLOTS for g in heads]
    bufs = ((s0_ref, s1_ref), (s1_ref, s0_ref))

    def sel_body(k, carry):
        last = [k >= n_steps[g] - 1 for g in heads]
        for odd, (cur_s, nxt_s) in enumerate(bufs):
            for l0 in (False, True):
                for l1 in (False, True):
                    @pl.when((k % 2 == odd) & (last[0] == l0) & (last[1] == l1))
                    def _():
                        ahead = not (l0 and l1)
                        sel_step(0, k, cur_s, nxt_s, l0, ahead)
                        sel_step(1, k, cur_s, nxt_s, l1, ahead)

        return carry

    lax.fori_loop(0, jnp.maximum(n_steps[0], n_steps[1]), sel_body, 0)
    o_s = [acc_ref[g] / l_ref[g, 0:1, :] for g in heads]

    for g in heads:
        gt = gt_ref[0, g, 0]

        def gate(br):
            return jnp.concatenate([gt[br * GQA_R + h:br * GQA_R + h + 1, :] for h in range(GQA_R)], axis=1)

        o = gate(0) * o_c[g] + gate(1) * o_s[g] + gate(2) * o_w[g]
        o_ref[0, :, g * GQA_R * HEAD_DIM:(g + 1) * GQA_R * HEAD_DIM] = jnp.concatenate(
            [o[:, h * Q_BLOCK:(h + 1) * Q_BLOCK].T for h in range(GQA_R)], axis=1).astype(BF16)


def _nsa(qt, gt, kc, vct, ksa, vst, kwa, vwt, ovt, tri, n_sel):
    b, g, nt, _, width = qt.shape
    s = ksa.shape[2]
    tile = lambda a: pl.BlockSpec((1, g, 1) + a.shape[3:], lambda i, k: (i, 0, k, 0, 0))
    whole = lambda a: pl.BlockSpec((1,) + a.shape[1:], lambda i, k: (i,) + (0,) * (a.ndim - 1))
    const = lambda a: pl.BlockSpec(a.shape, lambda i, k: (0,) * a.ndim)
    kern = functools.partial(_nsa_kernel, n_sel=n_sel, top_n=min(SEL_TOPN, n_sel))
    return pl.pallas_call(
        kern,
        grid=(b, nt),
        in_specs=[tile(qt), tile(gt), whole(kc), whole(vct), whole(ksa), whole(vst), whole(kwa), whole(vwt),
                  const(ovt), const(tri)],
        out_specs=pl.BlockSpec((1, Q_BLOCK, ATTN_WIDTH), lambda i, k: (i, k, 0)),
        out_shape=jax.ShapeDtypeStruct((b, s, ATTN_WIDTH), BF16),
        scratch_shapes=[pltpu.VMEM((g, 2 * LANES, width), BF16),
                        pltpu.VMEM((g, SEL_SLOTS, KEY_CHUNK, width), F32),
                        pltpu.VMEM((g, SEL_SLOTS, KEY_CHUNK, width), F32),
                        pltpu.VMEM((g, 8, width), F32),
                        pltpu.VMEM((g, 8, width), F32),
                        pltpu.VMEM((g, HEAD_DIM, width), F32),
                        pltpu.SMEM((g * (s // KEY_CHUNK + 2 * SEL_SLOTS),), I32)],
        compiler_params=pltpu.CompilerParams(dimension_semantics=("parallel", "arbitrary"),
                                             vmem_limit_bytes=VMEM_LIMIT),
        name="nsa",
    )(qt, gt, kc, vct, ksa, vst, kwa, vwt, ovt, tri)


def _merge_kernel(x_ref, ya_ref, ga_ref, gbyb_ref, wba_ref, wout_ref, gmoe_ref, wr_ref, br_ref, utri_ref,
                  x1_ref, hm_ref, rw_ref, ri_ref, cnt_ref, carry_ref):
    tm = x_ref.shape[0]

    @pl.when(pl.program_id(0) == 0)
    def _():
        carry_ref[...] = jnp.zeros_like(carry_ref)

    merged = ga_ref[...].astype(F32) * _dot(ya_ref[...], wba_ref[...]) + gbyb_ref[...].astype(F32)
    x1 = x_ref[...] + _dot(merged.astype(BF16), wout_ref[...])
    x1_ref[...] = x1
    hm = _rms(x1, gmoe_ref[...])
    for a in range(ROW_TILE):
        hm_ref[pl.ds(a, tm, stride=ROW_TILE), :] = hm[:, a * LANES:(a + 1) * LANES]

    hh = hm.astype(BF16)
    hl = (hm - hh.astype(F32)).astype(BF16)
    both = _dot(hh, wr_ref[...])
    logits = both[:, :LANES] + both[:, LANES:] + _dot(hl, wr_ref[:, :LANES]) + br_ref[...]
    lg = logits.T[0:N_EXPERTS, :]
    e_io = lax.broadcasted_iota(I32, (N_EXPERTS, tm), 0)
    vals, idxs = [], []
    for _ in range(TOP_K):
        m = jnp.max(lg, axis=0, keepdims=True)
        idx = jnp.min(jnp.where(lg == m, e_io, N_EXPERTS), axis=0, keepdims=True)
        vals.append(m)
        idxs.append(idx)
        lg = jnp.where(e_io == idx, NEG, lg)
    ex = [jnp.exp(v - vals[0]) for v in vals]
    den = ex[0] + ex[1] + ex[2] + ex[3]

    hits = [e_io == idx for idx in idxs]
    multi = jnp.zeros((N_EXPERTS, tm), F32)
    for h in hits:
        multi = jnp.where(h, 1.0, multi)
    carry = carry_ref[:, 0:1]
    cum = _dot(multi.astype(BF16), utri_ref[...]) + carry
    ranks = [jnp.sum(jnp.where(h, cum, 0.0), axis=0, keepdims=True).astype(I32) for h in hits]
    ri_ref[...] = jnp.concatenate(idxs + ranks, axis=0)
    wts = jnp.concatenate([e / den for e in ex] + [jnp.zeros((LANES - TOP_K, tm), F32)], axis=0)
    rw_ref[...] = wts.T
    new_carry = carry + jnp.sum(multi, axis=1, keepdims=True)
    carry_ref[...] = jnp.broadcast_to(new_carry, carry_ref.shape)
    cnt_ref[...] = jnp.broadcast_to(new_carry, cnt_ref.shape)


def _merge(x2, ya, ga, gbyb, wba, wout, gmoe, wr, br, tm):
    n = x2.shape[0]
    full = lambda a: pl.BlockSpec(a.shape, lambda i: (0,) * a.ndim)
    rows = lambda w: pl.BlockSpec((tm, w), lambda i: (i, 0))
    utri = jnp.asarray(np.triu(np.ones((tm, tm), np.float32), 1), BF16)
    return pl.pallas_call(
        _merge_kernel,
        grid=(n // tm,),
        in_specs=[rows(D_MODEL), rows(ATTN_WIDTH), rows(D_MODEL), rows(D_MODEL),
                  full(wba), full(wout), full(gmoe), full(wr), full(br), full(utri)],
        out_specs=[rows(D_MODEL), pl.BlockSpec((tm * ROW_TILE, LANES), lambda i: (i, 0)), rows(LANES),
                   pl.BlockSpec((2 * TOP_K, tm), lambda i: (0, i)),
                   pl.BlockSpec((N_EXPERTS, LANES), lambda i: (0, 0))],
        out_shape=[jax.ShapeDtypeStruct((n, D_MODEL), F32),
                   jax.ShapeDtypeStruct((n * ROW_TILE, LANES), F32),
                   jax.ShapeDtypeStruct((n, LANES), F32),
                   jax.ShapeDtypeStruct((2 * TOP_K, n), I32),
                   jax.ShapeDtypeStruct((N_EXPERTS, LANES), F32)],
        scratch_shapes=[pltpu.VMEM((N_EXPERTS, LANES), F32)],
        compiler_params=pltpu.CompilerParams(dimension_semantics=("arbitrary",),
                                             vmem_limit_bytes=VMEM_LIMIT),
        name="merge",
    )(x2, ya, ga, gbyb, wba, wout, gmoe, wr, br, utri)


def _tile_copy(src, i, dst, d, sem):
    return pltpu.make_async_copy(src.at[pl.ds(pl.multiple_of(i * ROW_TILE, ROW_TILE), ROW_TILE)],
                                 dst.at[pl.ds(pl.multiple_of(d * ROW_TILE, ROW_TILE), ROW_TILE)], sem)


def _dispatch_kernel(seg_ref, dest_ref, hm_ref, xs_ref, zero_ref, sem, zsem, *, n_pad):
    tm = hm_ref.shape[0] // ROW_TILE

    @pl.when(pl.program_id(0) == 0)
    def _():
        zero_ref[...] = jnp.zeros_like(zero_ref)

        def seg(e, c):
            def fill(r, c2):
                _tile_copy(zero_ref, 0, xs_ref, r, zsem).start()
                return c2
            return lax.fori_loop(seg_ref[0, e], seg_ref[1, e], fill, c)

        lax.fori_loop(0, N_EXPERTS + 1, seg, 0)
        pad_rows = xs_ref.at[pl.ds(0, n_pad * ROW_TILE)]
        pltpu.make_async_copy(pad_rows, pad_rows, zsem).wait()

    def issue(r, c):
        slots = [dest_ref[r * TOP_K + k] for k in range(TOP_K)]
        for k in range(TOP_K):
            _tile_copy(hm_ref, r, xs_ref, slots[k], sem).start(priority=k % 2)
        return c

    lax.fori_loop(0, tm, issue, 0)
    for k in range(TOP_K):
        pltpu.make_async_copy(hm_ref, xs_ref.at[pl.ds(0, tm * ROW_TILE)], sem).wait()


def _dispatch(seg, dest_flat, hm, n_slots, tm):
    n = hm.shape[0] // ROW_TILE
    kern = functools.partial(_dispatch_kernel, n_pad=n_slots - n * TOP_K)
    return pl.pallas_call(
        kern,
        grid_spec=pltpu.PrefetchScalarGridSpec(
            num_scalar_prefetch=1,
            grid=(n // tm,),
            in_specs=[pl.BlockSpec((tm * TOP_K,), lambda i, sg: (i,), memory_space=pltpu.SMEM),
                      pl.BlockSpec((tm * ROW_TILE, LANES), lambda i, sg: (i, 0))],
            out_specs=pl.BlockSpec(memory_space=pl.ANY),
            scratch_shapes=[pltpu.VMEM((ROW_TILE, LANES), F32),
                            pltpu.SemaphoreType.DMA(()), pltpu.SemaphoreType.DMA(())]),
        out_shape=jax.ShapeDtypeStruct((n_slots * ROW_TILE, LANES), F32),
        compiler_params=pltpu.CompilerParams(dimension_semantics=("arbitrary",),
                                             has_side_effects=True),
        name="dispatch",
    )(seg, dest_flat, hm)


def _expert_kernel(be_ref, nu_ref, xs_ref, wgu_ref, bgu_ref, wd_ref, bd_ref, y_ref, wgu_bf, wd_bf):
    i = pl.program_id(0)

    @pl.when(i >= nu_ref[0])
    def _():
        y_ref[...] = jnp.zeros_like(y_ref)

    @pl.when((i == 0) | (be_ref[i] != be_ref[jnp.maximum(i - 1, 0)]))
    def _():
        wgu_bf[...] = wgu_ref[0].astype(BF16)
        wd_bf[...] = wd_ref[0].astype(BF16)

    @pl.when(i < nu_ref[0])
    def _():
        x = jnp.concatenate([xs_ref[pl.ds(a, MOE_BLOCK, stride=ROW_TILE), :] for a in range(ROW_TILE)], axis=1)
        gu = _dot(x.astype(BF16), wgu_bf[...]) + bgu_ref[0]
        gate = jnp.minimum(gu[:, :D_FF], SWIGLU_LIMIT)
        up = jnp.clip(gu[:, D_FF:], -SWIGLU_LIMIT, SWIGLU_LIMIT)
        act = gate * _sigmoid(SWIGLU_ALPHA * gate) * (up + 1.0)
        y = _dot(act.astype(BF16), wd_bf[...]) + bd_ref[0]
        for a in range(ROW_TILE):
            y_ref[pl.ds(a, MOE_BLOCK, stride=ROW_TILE), :] = y[:, a * LANES:(a + 1) * LANES]


def _experts(blk_expert, n_used, xs, wgu, bgu, wd, bd):
    n_blocks = xs.shape[0] // (MOE_BLOCK * ROW_TILE)
    blk = lambda i, be, nu: (jnp.minimum(i, nu[0] - 1), 0)
    exp3 = lambda i, be, nu: (be[jnp.minimum(i, nu[0] - 1)], 0, 0)
    return pl.pallas_call(
        _expert_kernel,
        grid_spec=pltpu.PrefetchScalarGridSpec(
            num_scalar_prefetch=2,
            grid=(n_blocks,),
            in_specs=[pl.BlockSpec((MOE_BLOCK * ROW_TILE, LANES), blk),
                      pl.BlockSpec((1, D_MODEL, 2 * D_FF), exp3),
                      pl.BlockSpec((1, 1, 2 * D_FF), exp3),
                      pl.BlockSpec((1, D_FF, D_MODEL), exp3),
                      pl.BlockSpec((1, 1, D_MODEL), exp3)],
            out_specs=pl.BlockSpec((MOE_BLOCK * ROW_TILE, LANES), lambda i, be, nu: (i, 0)),
            scratch_shapes=[pltpu.VMEM((D_MODEL, 2 * D_FF), BF16), pltpu.VMEM((D_FF, D_MODEL), BF16)]),
        out_shape=jax.ShapeDtypeStruct(xs.shape, F32),
        compiler_params=pltpu.CompilerParams(dimension_semantics=("arbitrary",),
                                             vmem_limit_bytes=VMEM_LIMIT),
        name="experts",
    )(blk_expert, n_used, xs, wgu, bgu, wd, bd)


def _combine_kernel(dest_ref, dest_next_ref, rw_ref, x1_ref, gfin_ref, y_ref, o_ref, ybuf0, ybuf1, sem0, sem1):
    tm = x1_ref.shape[0]
    i = pl.program_id(0)

    def gather(dst_ref, ybuf, sem):
        def issue(r, c):
            slots = [dst_ref[r * TOP_K + k] for k in range(TOP_K)]
            for k in range(TOP_K):
                _tile_copy(y_ref, slots[k], ybuf.at[k], r, sem).start(priority=k % 2)
            return c

        lax.fori_loop(0, tm, issue, 0)

    def finish(ybuf, sem):
        for k in range(TOP_K):
            pltpu.make_async_copy(y_ref.at[pl.ds(0, tm * ROW_TILE)], ybuf.at[k], sem).wait()
        rw = rw_ref[...]
        x1 = x1_ref[...]
        cols = []
        for a in range(ROW_TILE):
            acc = x1[:, a * LANES:(a + 1) * LANES]
            for k in range(TOP_K):
                acc = acc + rw[:, k:k + 1] * ybuf[k, pl.ds(a, tm, stride=ROW_TILE), :]
            cols.append(acc)
        o_ref[...] = _rms(jnp.concatenate(cols, axis=1), gfin_ref[...])

    @pl.when(i == 0)
    def _():
        gather(dest_ref, ybuf0, sem0)

    for parity, (cur, nxt) in enumerate((((ybuf0, sem0), (ybuf1, sem1)), ((ybuf1, sem1), (ybuf0, sem0)))):
        @pl.when(i % 2 == parity)
        def _():
            @pl.when(i + 1 < pl.num_programs(0))
            def _():
                gather(dest_next_ref, *nxt)

            finish(*cur)


def _combine(dest_flat, rw, x1, gfin, y, tm):
    n = x1.shape[0]
    steps = n // tm
    ybuf = pltpu.VMEM((TOP_K, tm * ROW_TILE, LANES), F32)
    return pl.pallas_call(
        _combine_kernel,
        grid=(steps,),
        in_specs=[pl.BlockSpec((tm * TOP_K,), lambda i: (i,), memory_space=pltpu.SMEM),
                  pl.BlockSpec((tm * TOP_K,), lambda i: (jnp.minimum(i + 1, steps - 1),), memory_space=pltpu.SMEM),
                  pl.BlockSpec((tm, LANES), lambda i: (i, 0)),
                  pl.BlockSpec((tm, D_MODEL), lambda i: (i, 0)),
                  pl.BlockSpec((1, D_MODEL), lambda i: (0, 0)),
                  pl.BlockSpec(memory_space=pl.ANY)],
        out_specs=pl.BlockSpec((tm, D_MODEL), lambda i: (i, 0)),
        out_shape=jax.ShapeDtypeStruct((n, D_MODEL), F32),
        scratch_shapes=[ybuf, ybuf, pltpu.SemaphoreType.DMA(()), pltpu.SemaphoreType.DMA(())],
        compiler_params=pltpu.CompilerParams(dimension_semantics=("arbitrary",),
                                             vmem_limit_bytes=VMEM_LIMIT),
        name="combine",
    )(dest_flat, dest_flat, rw, x1, gfin, y)


def _overlap_t(nc, n_cmp, n_sel):
    cs = np.arange(n_cmp)[None, :] * CMP_STRIDE
    ss = np.arange(n_sel)[:, None] * SEL_BLOCK
    ov = np.clip(np.minimum(cs + CMP_LEN, ss + SEL_BLOCK) - np.maximum(cs, ss), 0, None) / CMP_LEN
    out = np.zeros((LANES, nc), np.float32)
    out[:n_sel, :n_cmp] = ov
    return jnp.asarray(out, BF16)


def _layer(x2, b, s, g_mix, w_in, w_ck1, b_ck1, w_ck2, pe_k, w_cv1, b_cv1, w_cv2, pe_v,
           sgu_ln_g, sgu_ln_b, w_spatial, b_spatial, w_branch_a, w_branch_b, w_out,
           g_moe, w_router, b_router, w_gate_up, b_gate_up, w_down, b_down):
    n = b * s
    nc = s // CMP_STRIDE
    n_cmp = (s - CMP_LEN) // CMP_STRIDE + 1
    n_sel = s // SEL_BLOCK
    assert nc % KEY_CHUNK == 0 and n_sel <= LANES and n_cmp == nc - 1
    tm = TOKEN_TILE
    assert s % tm == 0 and n % ROUTE_TILE == 0 and n % SCATTER_TILE == 0

    p0 = ATTN_WIDTH
    p1 = p0 + 6 * KV_WIDTH
    p2 = p1 + N_NSA_BRANCH * N_HEADS
    p3 = p2 + SGU_WIDTH
    p4 = p3 + SGU_WIDTH
    zpad = lambda a, w: jnp.pad(a, ((0, 0),) * (a.ndim - 1) + ((0, w - a.shape[-1]),))
    wq = zpad((w_in[:, :p0] * (HEAD_DIM ** -0.5 * LOG2E)).reshape(D_MODEL, N_HEADS, HEAD_DIM), LANES)
    wkv = w_in[:, p0:p1].reshape(D_MODEL, 6, N_KV, HEAD_DIM)
    wng = w_in[:, p1:p2].reshape(D_MODEL, N_KV, GQA_R, N_NSA_BRANCH).transpose(0, 1, 3, 2)
    wng = zpad(wng.reshape(D_MODEL, N_KV, N_NSA_BRANCH * GQA_R), GATE_ROWS)
    wt = jnp.concatenate([wq.reshape(D_MODEL, -1), wkv[:, 3].reshape(D_MODEL, -1),
                          wkv[:, 5].reshape(D_MODEL, -1), wng.reshape(D_MODEL, -1)], axis=1).T.astype(BF16)
    slopes = 2.0 ** (-8.0 * np.arange(1, N_HEADS + 1) / N_HEADS)
    tcol = np.zeros((T_ROWS, 1), np.float32)
    head_rows = T_Q + np.arange(N_HEADS) * LANES + HEAD_DIM
    bf16_round = lambda a: a.astype(BF16).astype(np.float32)
    for k, coef in enumerate((slopes * SEL_BLOCK * LOG2E, slopes * LOG2E)):
        hi = bf16_round(coef.astype(np.float32))
        tcol[head_rows + 2 * k, 0] = hi
        tcol[head_rows + 2 * k + 1, 0] = bf16_round(coef.astype(np.float32) - hi)
    tconst = jnp.asarray(np.broadcast_to(tcol, (T_ROWS, 2 * Q_BLOCK)))
    wk = jnp.concatenate([zpad(wkv[:, 2, g], 2 * LANES) for g in range(N_KV)]
                         + [zpad(wkv[:, 4, g], LANES) for g in range(N_KV)]
                         + [wkv[:, 0].reshape(D_MODEL, -1), wkv[:, 1].reshape(D_MODEL, -1)], axis=1).astype(BF16)
    pos = np.arange(s)
    kc_np = np.zeros((s, 2 * LANES), np.float32)
    kc_np[:, HEAD_DIM:HEAD_DIM + 2] = (pos // SEL_BLOCK)[:, None]
    kc_np[:, HEAD_DIM + 2:HEAD_DIM + 4] = (pos % SEL_BLOCK)[:, None]
    kc_np[pos, LANES + pos // SEL_BLOCK] = 1.0
    kconst = jnp.asarray(kc_np, BF16)
    wu = w_in[:, p2:p3].astype(BF16)
    wv = w_in[:, p3:p4].astype(BF16)
    wmg = w_in[:, p4:].astype(BF16)
    bsp = jnp.repeat(b_spatial.T, SGU_GROUP_DIM, axis=1)

    qt, gt, vst, vwt, ksa, kwa, kcx, vcx, ga, gbyb = _proj(
        x2, b, s, g_mix[None], wt, tconst, wk, kconst, wu, wv, wmg, sgu_ln_g[None], sgu_ln_b[None],
        w_spatial, bsp, w_branch_b.astype(BF16), tm)

    half = CMP_LEN // 2
    eye = jnp.eye(N_KV, dtype=F32)[None, None, :, None, :, None]
    bdiag = lambda w: (w.reshape(2, half, 1, HEAD_DIM, 1, CMP_HID) * eye).reshape(
        2, half, N_KV * HEAD_DIM, N_KV * CMP_HID)
    w1 = jnp.stack([bdiag(w_ck1), bdiag(w_cv1)]).astype(BF16)
    pes = jnp.stack([jnp.tile(pe_k.reshape(2, half, HEAD_DIM), (1, 1, N_KV)),
                     jnp.tile(pe_v.reshape(2, half, HEAD_DIM), (1, 1, N_KV))])
    b1 = jnp.stack([jnp.tile(b_ck1, N_KV), jnp.tile(b_cv1, N_KV)])[:, None, :]
    blk_n = np.arange(nc)
    cc_np = np.zeros((nc, LANES), np.float32)
    cc_np[:, HEAD_DIM:HEAD_DIM + 2] = (blk_n // (SEL_BLOCK // CMP_STRIDE))[:, None]
    cc_np[:, HEAD_DIM + 2:HEAD_DIM + 4] = (blk_n % (SEL_BLOCK // CMP_STRIDE) * CMP_STRIDE)[:, None]
    kc, vct = _compress(kcx, vcx, pes, w1, b1, zpad(w_ck2, LANES).astype(BF16), w_cv2.T.astype(BF16),
                        jnp.asarray(cc_np))

    ovt = _overlap_t(nc, n_cmp, n_sel).reshape(LANES, nc // KEY_CHUNK, KEY_CHUNK).transpose(1, 0, 2)
    a_io, q_io = np.meshgrid(np.arange(KEY_CHUNK), np.arange(Q_BLOCK), indexing="ij")
    tri = np.stack([np.where(a_io > q_io, 0.0, NEG), np.where(a_io <= q_io, 0.0, NEG)]).astype(np.float32)
    tri = jnp.asarray(np.tile(tri, (1, 1, 2)))
    ya = _nsa(qt, gt, kc, vct, ksa, vst, kwa, vwt, ovt, tri, n_sel).reshape(n, ATTN_WIDTH)

    wr = jnp.pad(w_router, ((0, 0), (0, LANES - N_EXPERTS)))
    wrh = wr.astype(BF16)
    wr2 = jnp.concatenate([wrh, (wr - wrh.astype(F32)).astype(BF16)], axis=1)
    br = jnp.pad(b_router, (0, LANES - N_EXPERTS))[None]
    x1, hm, rw, ri, cnt = _merge(x2, ya, ga, gbyb, w_branch_a.astype(BF16), w_out.astype(BF16),
                                 g_moe[None], wr2, br, tm)

    counts = cnt[:, 0].astype(I32)
    padded = (counts + MOE_BLOCK - 1) // MOE_BLOCK * MOE_BLOCK
    pad_end = jnp.cumsum(padded)
    pad_start = pad_end - padded
    base = sum(jnp.where(ri[:TOP_K] == e, pad_start[e], 0) for e in range(N_EXPERTS))
    dest = (base + ri[TOP_K:]).T.reshape(-1)
    n_blocks = -(-(n * TOP_K) // MOE_BLOCK) + N_EXPERTS
    blk_start = jnp.arange(n_blocks, dtype=I32) * MOE_BLOCK
    blk_expert = jnp.minimum(jnp.sum((pad_end[None, :] <= blk_start[:, None]).astype(I32), axis=1),
                             N_EXPERTS - 1)
    n_used = (pad_end[-1:] // MOE_BLOCK).astype(I32)
    n_slots = n_blocks * MOE_BLOCK
    seg = jnp.stack([jnp.concatenate([pad_start + counts, pad_end[-1:]]),
                     jnp.concatenate([pad_end, jnp.full((1,), n_slots, I32)])]).astype(I32)

    xs = _dispatch(seg, dest, hm, n_slots, SCATTER_TILE)
    y = _experts(blk_expert, n_used, xs, w_gate_up, b_gate_up[:, None, :], w_down, b_down[:, None, :])
    return dest, rw, x1, y


def kernel(x, g_mix, w_in, w_ck1, b_ck1, w_ck2, pe_k, w_cv1, b_cv1, w_cv2, pe_v, sgu_ln_g, sgu_ln_b, w_spatial, b_spatial, w_branch_a, w_branch_b, w_out, g_moe, w_router, b_router, w_gate_up, b_gate_up, w_down, b_down, g_final):
    b, s, d = x.shape
    assert g_mix.shape[0] == 1, "the final rmsnorm is fused into the single layer's combine step"
    l = 0
    dest, rw, x1, y = _layer(
        x.reshape(b * s, d), b, s, g_mix[l], w_in[l], w_ck1[l], b_ck1[l], w_ck2[l], pe_k[l], w_cv1[l],
        b_cv1[l], w_cv2[l], pe_v[l], sgu_ln_g[l], sgu_ln_b[l], w_spatial[l], b_spatial[l], w_branch_a[l],
        w_branch_b[l], w_out[l], g_moe[l], w_router[l], b_router[l], w_gate_up[l], b_gate_up[l],
        w_down[l], b_down[l])
    return _combine(dest, rw, x1, g_final[None], y, ROUTE_TILE).reshape(b, s, d)
```

```python
import functools

import numpy as np
import jax
import jax.numpy as jnp
from jax import lax
from jax.experimental import pallas as pl
from jax.experimental.pallas import tpu as pltpu

D_MODEL = 1024
N_HEADS = 8
HEAD_DIM = 64
N_KV = 2
GQA_R = N_HEADS // N_KV
ATTN_WIDTH = N_HEADS * HEAD_DIM
KV_WIDTH = N_KV * HEAD_DIM
CMP_LEN = 32
CMP_STRIDE = 16
CMP_HID = 128
SEL_BLOCK = 64
SEL_TOPN = 16
WINDOW = 512
Q_BLOCK = 128
N_NSA_BRANCH = 3
SGU_WIDTH = 512
N_GROUPS_SGU = 8
SGU_GROUP_DIM = SGU_WIDTH // N_GROUPS_SGU
CHUNK = 128
N_EXPERTS = 32
TOP_K = 4
D_FF = D_MODEL
SWIGLU_LIMIT = 7.0
SWIGLU_ALPHA = 1.702
MOE_BLOCK = 512
RMS_EPS = 1e-5
LN_EPS = 1e-5
NEG = -1e30
FORCE = 1e4

LANES = 128
ROW_TILE = D_MODEL // LANES
MASK_BIAS = 1e9
LOG2E = float(np.log2(np.e))
KEY_CHUNK = 128
SEL_SLOTS = 4
GATE_ROWS = 16
TOKEN_TILE = 512
SCATTER_TILE = 2048
ROUTE_TILE = 512
VMEM_LIMIT = 56 * 1024 * 1024

F32 = jnp.float32
BF16 = jnp.bfloat16
I32 = jnp.int32

_NT = (((1,), (1,)), ((), ()))


def _dot(a, b):
    return jnp.dot(a, b, preferred_element_type=F32)


def _dot_nt(a, b):
    return lax.dot_general(a, b, _NT, preferred_element_type=F32)


def _sigmoid(x):
    return 1.0 / (1.0 + jnp.exp(-x))


def _gelu(x):
    c = np.float32(np.sqrt(2.0 / np.pi))
    return 0.5 * x * (1.0 + jnp.tanh(c * (x + 0.044715 * (x * x * x))))


def _rms(x, g):
    return x * lax.rsqrt(jnp.mean(x * x, axis=-1, keepdims=True) + RMS_EPS) * g


T_Q = 0
T_VS = T_Q + N_HEADS * LANES
T_VW = T_VS + N_KV * HEAD_DIM
T_GATE = T_VW + N_KV * HEAD_DIM
T_ROWS = T_GATE + N_KV * GATE_ROWS
K_SEL = 0
K_WIN = K_SEL + N_KV * 2 * LANES
K_CMP = K_WIN + N_KV * LANES
K_COLS = K_CMP + 4 * HEAD_DIM


def _proj_kernel(x_ref, gmix_ref, wt_ref, tconst_ref, wk_ref, kconst_ref, wu_ref, wv_ref, wmg_ref,
                 lng_ref, lnb_ref, wsp_ref, bsp_ref, wbb_ref,
                 qt_ref, gt_ref, vst_ref, vwt_ref, ksa_ref, kwa_ref, kcx_ref, vcx_ref, ga_ref, gbyb_ref):
    tm = x_ref.shape[0]
    hb = _rms(x_ref[...], gmix_ref[...]).astype(BF16)

    for h in range(tm // (2 * Q_BLOCK)):
        t = _dot_nt(wt_ref[...], hb[h * 2 * Q_BLOCK:(h + 1) * 2 * Q_BLOCK]) + tconst_ref[...]
        for jj in range(2):
            j = 2 * h + jj
            ls = slice(jj * Q_BLOCK, (jj + 1) * Q_BLOCK)
            for g in range(N_KV):
                for r in range(GQA_R):
                    r0 = T_Q + (g * GQA_R + r) * LANES
                    qt_ref[0, g, j, :, r * Q_BLOCK:(r + 1) * Q_BLOCK] = t[r0:r0 + LANES, ls].astype(BF16)
                vst_ref[0, g, j] = t[T_VS + g * HEAD_DIM:T_VS + (g + 1) * HEAD_DIM, ls].astype(BF16)
                vwt_ref[0, g, j] = t[T_VW + g * HEAD_DIM:T_VW + (g + 1) * HEAD_DIM, ls].astype(BF16)
                gt_ref[0, g, j] = _sigmoid(t[T_GATE + g * GATE_ROWS:T_GATE + (g + 1) * GATE_ROWS, ls])

    kk = _dot(hb, wk_ref[...])
    kconst = kconst_ref[...].astype(F32)
    for g in range(N_KV):
        c0 = K_SEL + g * 2 * LANES
        ksa_ref[0, g] = (kk[:, c0:c0 + 2 * LANES] + kconst).astype(BF16)
        c0 = K_WIN + g * LANES
        kwa_ref[0, g] = (kk[:, c0:c0 + LANES] + kconst[:, :LANES]).astype(BF16)
    kcx_ref[0] = kk[:, K_CMP:K_CMP + LANES]
    vcx_ref[0] = kk[:, K_CMP + LANES:K_COLS]

    mg = _dot(hb, wmg_ref[...])
    ga_ref[...] = _sigmoid(mg[:, :D_MODEL]).astype(BF16)

    u = _gelu(_dot(hb, wu_ref[...]))
    v = _gelu(_dot(hb, wv_ref[...]))
    mu = jnp.mean(v, axis=-1, keepdims=True)
    vc = v - mu
    var = jnp.mean(vc * vc, axis=-1, keepdims=True)
    vln = (vc * lax.rsqrt(var + LN_EPS) * lng_ref[...] + lnb_ref[...]).astype(BF16)

    row = lax.broadcasted_iota(I32, (CHUNK, CHUNK), 0)
    col = lax.broadcasted_iota(I32, (CHUNK, CHUNK), 1)
    tril = row >= col
    wsp = [jnp.where(tril, wsp_ref[g], 0.0).astype(BF16) for g in range(N_GROUPS_SGU)]
    low_half = col < SGU_GROUP_DIM
    bsp = bsp_ref[...]
    chunks = []
    for c in range(tm // CHUNK):
        rs = slice(c * CHUNK, (c + 1) * CHUNK)
        parts = []
        for p in range(SGU_WIDTH // LANES):
            cs = slice(p * LANES, (p + 1) * LANES)
            vblk = vln[rs, cs]
            mixed = jnp.where(low_half, _dot(wsp[2 * p], vblk), _dot(wsp[2 * p + 1], vblk))
            parts.append(u[rs, cs] * (mixed + bsp[:, cs]))
        chunks.append(jnp.concatenate(parts, axis=1))
    yb = jnp.concatenate(chunks, axis=0).astype(BF16)
    gbyb_ref[...] = (_sigmoid(mg[:, D_MODEL:]) * _dot(yb, wbb_ref[...])).astype(BF16)


def _proj(x2, b, s, gmix, wt, tconst, wk, kconst, wu, wv, wmg, lng, lnb, wsp, bsp, wbb, tm):
    n = x2.shape[0]
    tiles = s // tm
    qtiles = tm // Q_BLOCK
    full = lambda a: pl.BlockSpec(a.shape, lambda i, j: (0,) * a.ndim)
    rows = lambda w: pl.BlockSpec((tm, w), lambda i, j: (i * tiles + j, 0))
    per_g = lambda *blk: pl.BlockSpec((1, N_KV) + blk, lambda i, j: (i, 0, j) + (0,) * (len(blk) - 1))
    ins = (x2, gmix, wt, tconst, wk, kconst, wu, wv, wmg, lng, lnb, wsp, bsp, wbb)
    in_specs = [rows(D_MODEL)] + [full(a) for a in ins[1:]]
    in_specs[5] = pl.BlockSpec((tm, 2 * LANES), lambda i, j: (j, 0))
    nt = s // Q_BLOCK
    return pl.pallas_call(
        _proj_kernel,
        grid=(b, tiles),
        in_specs=in_specs,
        out_specs=[per_g(qtiles, LANES, GQA_R * Q_BLOCK), per_g(qtiles, GATE_ROWS, Q_BLOCK),
                   per_g(qtiles, HEAD_DIM, Q_BLOCK), per_g(qtiles, HEAD_DIM, Q_BLOCK),
                   per_g(tm, 2 * LANES), per_g(tm, LANES),
                   pl.BlockSpec((1, tm, LANES), lambda i, j: (i, j, 0)),
                   pl.BlockSpec((1, tm, LANES), lambda i, j: (i, j, 0)),
                   rows(D_MODEL), rows(D_MODEL)],
        out_shape=[jax.ShapeDtypeStruct((b, N_KV, nt, LANES, GQA_R * Q_BLOCK), BF16),
                   jax.ShapeDtypeStruct((b, N_KV, nt, GATE_ROWS, Q_BLOCK), F32),
                   jax.ShapeDtypeStruct((b, N_KV, nt, HEAD_DIM, Q_BLOCK), BF16),
                   jax.ShapeDtypeStruct((b, N_KV, nt, HEAD_DIM, Q_BLOCK), BF16),
                   jax.ShapeDtypeStruct((b, N_KV, s, 2 * LANES), BF16),
                   jax.ShapeDtypeStruct((b, N_KV, s, LANES), BF16),
                   jax.ShapeDtypeStruct((b, s, LANES), F32),
                   jax.ShapeDtypeStruct((b, s, LANES), F32),
                   jax.ShapeDtypeStruct((n, D_MODEL), BF16),
                   jax.ShapeDtypeStruct((n, D_MODEL), BF16)],
        compiler_params=pltpu.CompilerParams(dimension_semantics=("parallel", "parallel"),
                                             vmem_limit_bytes=VMEM_LIMIT),
        name="proj",
    )(*ins)


def _compress_kernel(xk_ref, xv_ref, pe_ref, w1_ref, b1_ref, w2k_ref, w2vt_ref, kconst_ref, kc_ref, vct_ref):
    nc = kc_ref.shape[2]
    half = CMP_LEN // 2
    hids = []
    for which, x_ref in enumerate((xk_ref, xv_ref)):
        za = jnp.zeros((nc, N_KV * CMP_HID), F32)
        zb = jnp.zeros((nc, N_KV * CMP_HID), F32)
        for l in range(half):
            xl = x_ref[0, pl.ds(l, nc, stride=CMP_STRIDE), :]
            za = za + _dot((xl + pe_ref[which, 0, l:l + 1, :]).astype(BF16), w1_ref[which, 0, l])
            zb = zb + _dot((xl + pe_ref[which, 1, l:l + 1, :]).astype(BF16), w1_ref[which, 1, l])
        hid = _gelu(za + pltpu.roll(zb, nc - 1, 0) + b1_ref[which])
        row = lax.broadcasted_iota(I32, hid.shape, 0)
        hids.append(jnp.where(row < nc - 1, hid, 0.0).astype(BF16))
    for g in range(N_KV):
        cs = slice(g * CMP_HID, (g + 1) * CMP_HID)
        kc_ref[0, g] = (_dot(hids[0][:, cs], w2k_ref[...]) + kconst_ref[...]).astype(BF16)
        vt = _dot_nt(w2vt_ref[...], hids[1][:, cs])
        for c in range(nc // KEY_CHUNK):
            vct_ref[0, g, c] = vt[:, c * KEY_CHUNK:(c + 1) * KEY_CHUNK].astype(BF16)


def _compress(xk, xv, pes, w1, b1, w2k, w2vt, kconst):
    b, s, w = xk.shape
    nc = s // CMP_STRIDE
    full = lambda a: pl.BlockSpec(a.shape, lambda i: (0,) * a.ndim)
    seq = pl.BlockSpec((1, s, w), lambda i: (i, 0, 0))
    return pl.pallas_call(
        _compress_kernel,
        grid=(b,),
        in_specs=[seq, seq, full(pes), full(w1), full(b1), full(w2k), full(w2vt), full(kconst)],
        out_specs=[pl.BlockSpec((1, N_KV, nc, LANES), lambda i: (i, 0, 0, 0)),
                   pl.BlockSpec((1, N_KV, nc // KEY_CHUNK, HEAD_DIM, KEY_CHUNK), lambda i: (i, 0, 0, 0, 0))],
        out_shape=[jax.ShapeDtypeStruct((b, N_KV, nc, LANES), BF16),
                   jax.ShapeDtypeStruct((b, N_KV, nc // KEY_CHUNK, HEAD_DIM, KEY_CHUNK), BF16)],
        compiler_params=pltpu.CompilerParams(dimension_semantics=("parallel",),
                                             vmem_limit_bytes=VMEM_LIMIT),
        name="compress",
    )(xk, xv, pes, w1, b1, w2k, w2vt, kconst)


def _pairs():
    return [slice(pr * 2 * Q_BLOCK, (pr + 1) * 2 * Q_BLOCK) for pr in range(GQA_R // 2)]


def _colmax(blocks):
    part = None
    for x in blocks:
        y = jnp.max(x.reshape(-1, 8, x.shape[-1]), axis=0)
        part = y if part is None else jnp.maximum(part, y)
    return jnp.max(part, axis=0, keepdims=True)


def _colsum(blocks):
    part = None
    for x in blocks:
        y = jnp.sum(x.reshape(-1, 8, x.shape[-1]), axis=0)
        part = y if part is None else part + y
    return jnp.sum(part, axis=0, keepdims=True)


def _softmax_pv(scores, vts, masks, m_ref, l_ref, acc_ref):
    for pr, ls in enumerate(_pairs()):
        sb = [x if masks[u] is None else masks[u](x, ls) for u, x in enumerate(scores[pr])]
        m_old = m_ref[0:1, ls]
        m_new = jnp.maximum(m_old, _colmax(sb))
        alpha = jnp.exp2(m_old - m_new)
        ps = [jnp.exp2(x - m_new) for x in sb]
        pv = _dot(vts[0], ps[0].astype(BF16))
        for u in range(1, len(ps)):
            pv = pv + _dot(vts[u], ps[u].astype(BF16))
        l_ref[0:1, ls] = alpha * l_ref[0:1, ls] + _colsum(ps)
        acc_ref[:, ls] = alpha * acc_ref[:, ls] + pv
        m_ref[0:1, ls] = m_new


def _nsa_kernel(qt_ref, gt_ref, kc_ref, vct_ref, ksa_ref, vst_ref, kwa_ref, vwt_ref, ovt_ref, tri_ref,
                o_ref, qa_ref, s0_ref, s1_ref, m_ref, l_ref, acc_ref, act_ref, *, n_sel, top_n):
    i = pl.program_id(1)
    t0 = i * Q_BLOCK
    width = GQA_R * Q_BLOCK
    n_chunks = ksa_ref.shape[2] // KEY_CHUNK
    n_cchunks = kc_ref.shape[2] // KEY_CHUNK
    n_slots = act_ref.shape[0] // N_KV
    lane = lax.broadcasted_iota(I32, (1, width), 1)
    tq = t0 + (lane & (Q_BLOCK - 1))
    sub = lax.broadcasted_iota(I32, (KEY_CHUNK, 1), 0)
    pairs = _pairs()
    heads = range(N_KV)

    def reset(g):
        m_ref[g] = jnp.full(m_ref.shape[1:], NEG, F32)
        l_ref[g] = jnp.zeros(l_ref.shape[1:], F32)
        acc_ref[g] = jnp.zeros(acc_ref.shape[1:], F32)

    sc = [[[_dot(kc_ref[0, g, u * KEY_CHUNK:(u + 1) * KEY_CHUNK, :], qt_ref[0, g, 0, :, ls])
            for u in range(n_cchunks)] for ls in pairs] for g in heads]
    o_c, imp_t = [], []
    for g in heads:
        ocg, ps = [], []
        for pr, ls in enumerate(pairs):
            sb = []
            for u in range(n_cchunks):
                c_end = (u * KEY_CHUNK + sub) * CMP_STRIDE + (CMP_LEN - 1)
                sb.append(jnp.where(c_end <= tq[:, ls], sc[g][pr][u], NEG))
            m = _colmax(sb)
            pb = [jnp.exp2(x - m) for x in sb]
            l = _colsum(pb)
            w = (tq[:, ls] >= CMP_LEN - 1).astype(F32) / l
            oc = _dot(vct_ref[0, g, 0], pb[0].astype(BF16))
            for u in range(1, n_cchunks):
                oc = oc + _dot(vct_ref[0, g, u], pb[u].astype(BF16))
            ocg.append(oc * w)
            ps.append([p[:, :Q_BLOCK] * w[:, :Q_BLOCK] + p[:, Q_BLOCK:] * w[:, Q_BLOCK:] for p in pb])
        imp2 = jnp.zeros((LANES, 2 * Q_BLOCK), F32)
        for u in range(n_cchunks):
            psu = ps[0][u] + ps[1][u]
            hi = psu.astype(BF16)
            lo = (psu - hi.astype(F32)).astype(BF16)
            imp2 = imp2 + _dot(ovt_ref[u], jnp.concatenate([hi, lo], axis=1))
        o_c.append(jnp.concatenate(ocg, axis=1))
        imp_t.append(imp2[:, :Q_BLOCK] + imp2[:, Q_BLOCK:])

    n_win = WINDOW // KEY_CHUNK
    cls, masks = [], []
    for u in range(n_win + 1):
        c = i - n_win + u
        cls.append(jnp.maximum(c, 0))
        off = jnp.where(c < 0, NEG, 0.0)
        if u == 0:
            edge = tri_ref[0] + off
            masks.append(lambda x, ls, edge=edge: x + edge)
        elif u == n_win:
            masks.append(lambda x, ls: x + tri_ref[1])
        else:
            masks.append(lambda x, ls, off=off: x + off)
    wsc = [[[_dot(kwa_ref[0, g, pl.ds(pl.multiple_of(cl * KEY_CHUNK, KEY_CHUNK), KEY_CHUNK), :],
                  qt_ref[0, g, 0, :, ls]) for cl in cls] for ls in pairs] for g in heads]
    o_w = []
    for g in heads:
        reset(g)
        _softmax_pv(wsc[g], [vwt_ref[0, g, cl] for cl in cls], masks, m_ref.at[g], l_ref.at[g], acc_ref.at[g])
        o_w.append(acc_ref[g] / l_ref[g, 0:1, :])

    j_io = lax.broadcasted_iota(I32, (LANES, Q_BLOCK), 0)
    tl = t0 + lax.broadcasted_iota(I32, (LANES, Q_BLOCK), 1)
    cur = tl // SEL_BLOCK
    forced = (j_io == 0) | (j_io == cur) | (j_io == cur - 1)
    prio = [jnp.where(j_io < n_sel, jnp.where(forced, FORCE, jnp.where(j_io * SEL_BLOCK <= tl, imp_t[g], -1.0)), NEG)
            for g in heads]
    for _ in range(top_n):
        for g in heads:
            m = jnp.max(prio[g], axis=0, keepdims=True)
            idx = jnp.min(jnp.where(prio[g] == m, j_io, LANES), axis=0, keepdims=True)
            prio[g] = jnp.where(j_io == idx, NEG, prio[g])
    sel_t = [jnp.where((prio[g] == NEG) & (j_io < n_sel), 1.0, 0.0) for g in heads]

    n_act = []
    for u in range(act_ref.shape[0]):
        act_ref[u] = jnp.int32(-1)
    for g in heads:
        qa_ref[g, 0:LANES, :] = qt_ref[0, g, 0]
        bias_t = ((sel_t[g] - 1.0) * MASK_BIAS).astype(BF16)
        for h in range(GQA_R):
            qa_ref[g, LANES:2 * LANES, h * Q_BLOCK:(h + 1) * Q_BLOCK] = bias_t
        any_q = jnp.max(sel_t[g], axis=1, keepdims=True)
        any_pair = jnp.maximum(any_q, pltpu.roll(any_q, LANES - 1, 0))
        na = jnp.int32(0)
        for c in range(n_chunks):
            act_ref[g * n_slots + na] = jnp.int32(c)
            na = na + jnp.where((any_pair[2 * c, 0] > 0.0) & (c < i), 1, 0)
        act_ref[g * n_slots + na] = i
        n_act.append(na)

    def sel_scores(g, k, dst):
        for u in range(SEL_SLOTS):
            cl = jnp.maximum(act_ref[g * n_slots + k * SEL_SLOTS + u], 0)
            keys = ksa_ref[0, g, pl.ds(pl.multiple_of(cl * KEY_CHUNK, KEY_CHUNK), KEY_CHUNK), :]
            for ls in pairs:
                dst[g, u, :, ls] = _dot(keys, qa_ref[g, :, ls])

    def sel_step(g, k, cur_s, nxt_s, last, ahead):
        if ahead:
            sel_scores(g, k + 1, nxt_s)
        vts, msk = [], []
        for u in range(SEL_SLOTS):
            c = act_ref[g * n_slots + k * SEL_SLOTS + u]
            vts.append(vst_ref[0, g, jnp.maximum(c, 0)])
            if last:
                kpos = jnp.where(c < 0, n_chunks * KEY_CHUNK, c * KEY_CHUNK) + sub
                msk.append(lambda x, ls, kpos=kpos: jnp.where(kpos <= tq[:, ls], x, -MASK_BIAS))
            else:
                msk.append(None)
        scores = [[cur_s[g, u, :, ls] for u in range(SEL_SLOTS)] for ls in pairs]
        _softmax_pv(scores, vts, msk, m_ref.at[g], l_ref.at[g], acc_ref.at[g])

    for g in heads:
        reset(g)
        sel_scores(g, 0, s0_ref)
    n_steps = [(n_act[g] + SEL_SLOTS) // SEL_SLOTS for g in heads]
    bufs = ((s0_ref, s1_ref), (s1_ref, s0_ref))

    def sel_body(k, carry):
        last = [k >= n_steps[g] - 1 for g in heads]
        for odd, (cur_s, nxt_s) in enumerate(bufs):
            for l0 in (False, True):
                for l1 in (False, True):
                    @pl.when((k % 2 == odd) & (last[0] == l0) & (last[1] == l1))
                    def _():
                        ahead = not (l0 and l1)
                        sel_step(0, k, cur_s, nxt_s, l0, ahead)
                        sel_step(1, k, cur_s, nxt_s, l1, ahead)

        return carry

    lax.fori_loop(0, jnp.maximum(n_steps[0], n_steps[1]), sel_body, 0)
    o_s = [acc_ref[g] / l_ref[g, 0:1, :] for g in heads]

    for g in heads:
        gt = gt_ref[0, g, 0]

        def gate(br):
            return jnp.concatenate([gt[br * GQA_R + h:br * GQA_R + h + 1, :] for h in range(GQA_R)], axis=1)

        o = gate(0) * o_c[g] + gate(1) * o_s[g] + gate(2) * o_w[g]
        o_ref[0, :, g * GQA_R * HEAD_DIM:(g + 1) * GQA_R * HEAD_DIM] = jnp.concatenate(
            [o[:, h * Q_BLOCK:(h + 1) * Q_BLOCK].T for h in range(GQA_R)], axis=1).astype(BF16)


def _nsa(qt, gt, kc, vct, ksa, vst, kwa, vwt, ovt, tri, n_sel):
    b, g, nt, _, width = qt.shape
    s = ksa.shape[2]
    tile = lambda a: pl.BlockSpec((1, g, 1) + a.shape[3:], lambda i, k: (i, 0, k, 0, 0))
    whole = lambda a: pl.BlockSpec((1,) + a.shape[1:], lambda i, k: (i,) + (0,) * (a.ndim - 1))
    const = lambda a: pl.BlockSpec(a.shape, lambda i, k: (0,) * a.ndim)
    kern = functools.partial(_nsa_kernel, n_sel=n_sel, top_n=min(SEL_TOPN, n_sel))
    return pl.pallas_call(
        kern,
        grid=(b, nt),
        in_specs=[tile(qt), tile(gt), whole(kc), whole(vct), whole(ksa), whole(vst), whole(kwa), whole(vwt),
                  const(ovt), const(tri)],
        out_specs=pl.BlockSpec((1, Q_BLOCK, ATTN_WIDTH), lambda i, k: (i, k, 0)),
        out_shape=jax.ShapeDtypeStruct((b, s, ATTN_WIDTH), BF16),
        scratch_shapes=[pltpu.VMEM((g, 2 * LANES, width), BF16),
                        pltpu.VMEM((g, SEL_SLOTS, KEY_CHUNK, width), F32),
                        pltpu.VMEM((g, SEL_SLOTS, KEY_CHUNK, width), F32),
                        pltpu.VMEM((g, 8, width), F32),
                        pltpu.VMEM((g, 8, width), F32),
                        pltpu.VMEM((g, HEAD_DIM, width), F32),
                        pltpu.SMEM((g * (s // KEY_CHUNK + 2 * SEL_SLOTS),), I32)],
        compiler_params=pltpu.CompilerParams(dimension_semantics=("parallel", "arbitrary"),
                                             vmem_limit_bytes=VMEM_LIMIT),
        name="nsa",
    )(qt, gt, kc, vct, ksa, vst, kwa, vwt, ovt, tri)


def _merge_kernel(x_ref, ya_ref, ga_ref, gbyb_ref, wba_ref, wout_ref, gmoe_ref, wr_ref, br_ref, utri_ref,
                  x1_ref, hm_ref, rw_ref, ri_ref, cnt_ref, carry_ref):
    tm = x_ref.shape[0]

    @pl.when(pl.program_id(0) == 0)
    def _():
        carry_ref[...] = jnp.zeros_like(carry_ref)

    merged = ga_ref[...].astype(F32) * _dot(ya_ref[...], wba_ref[...]) + gbyb_ref[...].astype(F32)
    x1 = x_ref[...] + _dot(merged.astype(BF16), wout_ref[...])
    x1_ref[...] = x1
    hm = _rms(x1, gmoe_ref[...])
    for a in range(ROW_TILE):
        hm_ref[pl.ds(a, tm, stride=ROW_TILE), :] = hm[:, a * LANES:(a + 1) * LANES]

    hh = hm.astype(BF16)
    hl = (hm - hh.astype(F32)).astype(BF16)
    both = _dot(hh, wr_ref[...])
    logits = both[:, :LANES] + both[:, LANES:] + _dot(hl, wr_ref[:, :LANES]) + br_ref[...]
    lg = logits.T[0:N_EXPERTS, :]
    e_io = lax.broadcasted_iota(I32, (N_EXPERTS, tm), 0)
    vals, idxs = [], []
    for _ in range(TOP_K):
        m = jnp.max(lg, axis=0, keepdims=True)
        idx = jnp.min(jnp.where(lg == m, e_io, N_EXPERTS), axis=0, keepdims=True)
        vals.append(m)
        idxs.append(idx)
        lg = jnp.where(e_io == idx, NEG, lg)
    ex = [jnp.exp(v - vals[0]) for v in vals]
    den = ex[0] + ex[1] + ex[2] + ex[3]

    hits = [e_io == idx for idx in idxs]
    multi = jnp.zeros((N_EXPERTS, tm), F32)
    for h in hits:
        multi = jnp.where(h, 1.0, multi)
    carry = carry_ref[:, 0:1]
    cum = _dot(multi.astype(BF16), utri_ref[...]) + carry
    ranks = [jnp.sum(jnp.where(h, cum, 0.0), axis=0, keepdims=True).astype(I32) for h in hits]
    ri_ref[...] = jnp.concatenate(idxs + ranks, axis=0)
    wts = jnp.concatenate([e / den for e in ex] + [jnp.zeros((LANES - TOP_K, tm), F32)], axis=0)
    rw_ref[...] = wts.T
    new_carry = carry + jnp.sum(multi, axis=1, keepdims=True)
    carry_ref[...] = jnp.broadcast_to(new_carry, carry_ref.shape)
    cnt_ref[...] = jnp.broadcast_to(new_carry, cnt_ref.shape)


def _merge(x2, ya, ga, gbyb, wba, wout, gmoe, wr, br, tm):
    n = x2.shape[0]
    full = lambda a: pl.BlockSpec(a.shape, lambda i: (0,) * a.ndim)
    rows = lambda w: pl.BlockSpec((tm, w), lambda i: (i, 0))
    utri = jnp.asarray(np.triu(np.ones((tm, tm), np.float32), 1), BF16)
    return pl.pallas_call(
        _merge_kernel,
        grid=(n // tm,),
        in_specs=[rows(D_MODEL), rows(ATTN_WIDTH), rows(D_MODEL), rows(D_MODEL),
                  full(wba), full(wout), full(gmoe), full(wr), full(br), full(utri)],
        out_specs=[rows(D_MODEL), pl.BlockSpec((tm * ROW_TILE, LANES), lambda i: (i, 0)), rows(LANES),
                   pl.BlockSpec((2 * TOP_K, tm), lambda i: (0, i)),
                   pl.BlockSpec((N_EXPERTS, LANES), lambda i: (0, 0))],
        out_shape=[jax.ShapeDtypeStruct((n, D_MODEL), F32),
                   jax.ShapeDtypeStruct((n * ROW_TILE, LANES), F32),
                   jax.ShapeDtypeStruct((n, LANES), F32),
                   jax.ShapeDtypeStruct((2 * TOP_K, n), I32),
                   jax.ShapeDtypeStruct((N_EXPERTS, LANES), F32)],
        scratch_shapes=[pltpu.VMEM((N_EXPERTS, LANES), F32)],
        compiler_params=pltpu.CompilerParams(dimension_semantics=("arbitrary",),
                                             vmem_limit_bytes=VMEM_LIMIT),
        name="merge",
    )(x2, ya, ga, gbyb, wba, wout, gmoe, wr, br, utri)


def _tile_copy(src, i, dst, d, sem):
    return pltpu.make_async_copy(src.at[pl.ds(pl.multiple_of(i * ROW_TILE, ROW_TILE), ROW_TILE)],
                                 dst.at[pl.ds(pl.multiple_of(d * ROW_TILE, ROW_TILE), ROW_TILE)], sem)


def _dispatch_kernel(seg_ref, dest_ref, hm_ref, xs_ref, zero_ref, sem, zsem, *, n_pad):
    tm = hm_ref.shape[0] // ROW_TILE

    @pl.when(pl.program_id(0) == 0)
    def _():
        zero_ref[...] = jnp.zeros_like(zero_ref)

        def seg(e, c):
            def fill(r, c2):
                _tile_copy(zero_ref, 0, xs_ref, r, zsem).start()
                return c2
            return lax.fori_loop(seg_ref[0, e], seg_ref[1, e], fill, c)

        lax.fori_loop(0, N_EXPERTS + 1, seg, 0)
        pad_rows = xs_ref.at[pl.ds(0, n_pad * ROW_TILE)]
        pltpu.make_async_copy(pad_rows, pad_rows, zsem).wait()

    def issue(r, c):
        slots = [dest_ref[r * TOP_K + k] for k in range(TOP_K)]
        for k in range(TOP_K):
            _tile_copy(hm_ref, r, xs_ref, slots[k], sem).start(priority=k % 2)
        return c

    lax.fori_loop(0, tm, issue, 0, unroll=4)
    for k in range(TOP_K):
        pltpu.make_async_copy(hm_ref, xs_ref.at[pl.ds(0, tm * ROW_TILE)], sem).wait()


def _dispatch(seg, dest_flat, hm, n_slots, tm):
    n = hm.shape[0] // ROW_TILE
    kern = functools.partial(_dispatch_kernel, n_pad=n_slots - n * TOP_K)
    return pl.pallas_call(
        kern,
        grid_spec=pltpu.PrefetchScalarGridSpec(
            num_scalar_prefetch=1,
            grid=(n // tm,),
            in_specs=[pl.BlockSpec((tm * TOP_K,), lambda i, sg: (i,), memory_space=pltpu.SMEM),
                      pl.BlockSpec((tm * ROW_TILE, LANES), lambda i, sg: (i, 0))],
            out_specs=pl.BlockSpec(memory_space=pl.ANY),
            scratch_shapes=[pltpu.VMEM((ROW_TILE, LANES), F32),
                            pltpu.SemaphoreType.DMA(()), pltpu.SemaphoreType.DMA(())]),
        out_shape=jax.ShapeDtypeStruct((n_slots * ROW_TILE, LANES), F32),
        compiler_params=pltpu.CompilerParams(dimension_semantics=("arbitrary",),
                                             has_side_effects=True),
        name="dispatch",
    )(seg, dest_flat, hm)


def _expert_kernel(be_ref, nu_ref, xs_ref, wgu_ref, bgu_ref, wd_ref, bd_ref, y_ref, wgu_bf, wd_bf):
    i = pl.program_id(0)

    @pl.when(i >= nu_ref[0])
    def _():
        y_ref[...] = jnp.zeros_like(y_ref)

    @pl.when((i == 0) | (be_ref[i] != be_ref[jnp.maximum(i - 1, 0)]))
    def _():
        wgu_bf[...] = wgu_ref[0].astype(BF16)
        wd_bf[...] = wd_ref[0].astype(BF16)

    @pl.when(i < nu_ref[0])
    def _():
        x = jnp.concatenate([xs_ref[pl.ds(a, MOE_BLOCK, stride=ROW_TILE), :] for a in range(ROW_TILE)], axis=1)
        gu = _dot(x.astype(BF16), wgu_bf[...]) + bgu_ref[0]
        gate = jnp.minimum(gu[:, :D_FF], SWIGLU_LIMIT)
        up = jnp.clip(gu[:, D_FF:], -SWIGLU_LIMIT, SWIGLU_LIMIT)
        act = gate * _sigmoid(SWIGLU_ALPHA * gate) * (up + 1.0)
        y = _dot(act.astype(BF16), wd_bf[...]) + bd_ref[0]
        for a in range(ROW_TILE):
            y_ref[pl.ds(a, MOE_BLOCK, stride=ROW_TILE), :] = y[:, a * LANES:(a + 1) * LANES]


def _experts(blk_expert, n_used, xs, wgu, bgu, wd, bd):
    n_blocks = xs.shape[0] // (MOE_BLOCK * ROW_TILE)
    blk = lambda i, be, nu: (jnp.minimum(i, nu[0] - 1), 0)
    exp3 = lambda i, be, nu: (be[jnp.minimum(i, nu[0] - 1)], 0, 0)
    return pl.pallas_call(
        _expert_kernel,
        grid_spec=pltpu.PrefetchScalarGridSpec(
            num_scalar_prefetch=2,
            grid=(n_blocks,),
            in_specs=[pl.BlockSpec((MOE_BLOCK * ROW_TILE, LANES), blk),
                      pl.BlockSpec((1, D_MODEL, 2 * D_FF), exp3),
                      pl.BlockSpec((1, 1, 2 * D_FF), exp3),
                      pl.BlockSpec((1, D_FF, D_MODEL), exp3),
                      pl.BlockSpec((1, 1, D_MODEL), exp3)],
            out_specs=pl.BlockSpec((MOE_BLOCK * ROW_TILE, LANES), lambda i, be, nu: (i, 0)),
            scratch_shapes=[pltpu.VMEM((D_MODEL, 2 * D_FF), BF16), pltpu.VMEM((D_FF, D_MODEL), BF16)]),
        out_shape=jax.ShapeDtypeStruct(xs.shape, F32),
        compiler_params=pltpu.CompilerParams(dimension_semantics=("arbitrary",),
                                             vmem_limit_bytes=VMEM_LIMIT),
        name="experts",
    )(blk_expert, n_used, xs, wgu, bgu, wd, bd)


def _combine_kernel(dest_ref, dest_next_ref, rw_ref, x1_ref, gfin_ref, y_ref, o_ref, ybuf0, ybuf1, sem0, sem1):
    tm = x1_ref.shape[0]
    i = pl.program_id(0)

    def gather(dst_ref, ybuf, sem):
        def issue(r, c):
            slots = [dst_ref[r * TOP_K + k] for k in range(TOP_K)]
            for k in range(TOP_K):
                _tile_copy(y_ref, slots[k], ybuf.at[k], r, sem).start(priority=k % 2)
            return c

        lax.fori_loop(0, tm, issue, 0, unroll=4)

    def finish(ybuf, sem):
        for k in range(TOP_K):
            pltpu.make_async_copy(y_ref.at[pl.ds(0, tm * ROW_TILE)], ybuf.at[k], sem).wait()
        rw = rw_ref[...]
        x1 = x1_ref[...]
        cols = []
        for a in range(ROW_TILE):
            acc = x1[:, a * LANES:(a + 1) * LANES]
            for k in range(TOP_K):
                acc = acc + rw[:, k:k + 1] * ybuf[k, pl.ds(a, tm, stride=ROW_TILE), :]
            cols.append(acc)
        o_ref[...] = _rms(jnp.concatenate(cols, axis=1), gfin_ref[...])

    @pl.when(i == 0)
    def _():
        gather(dest_ref, ybuf0, sem0)

    for parity, (cur, nxt) in enumerate((((ybuf0, sem0), (ybuf1, sem1)), ((ybuf1, sem1), (ybuf0, sem0)))):
        @pl.when(i % 2 == parity)
        def _():
            @pl.when(i + 1 < pl.num_programs(0))
            def _():
                gather(dest_next_ref, *nxt)

            finish(*cur)


def _combine(dest_flat, rw, x1, gfin, y, tm):
    n = x1.shape[0]
    steps = n // tm
    ybuf = pltpu.VMEM((TOP_K, tm * ROW_TILE, LANES), F32)
    return pl.pallas_call(
        _combine_kernel,
        grid=(steps,),
        in_specs=[pl.BlockSpec((tm * TOP_K,), lambda i: (i,), memory_space=pltpu.SMEM),
                  pl.BlockSpec((tm * TOP_K,), lambda i: (jnp.minimum(i + 1, steps - 1),), memory_space=pltpu.SMEM),
                  pl.BlockSpec((tm, LANES), lambda i: (i, 0)),
                  pl.BlockSpec((tm, D_MODEL), lambda i: (i, 0)),
                  pl.BlockSpec((1, D_MODEL), lambda i: (0, 0)),
                  pl.BlockSpec(memory_space=pl.ANY)],
        out_specs=pl.BlockSpec((tm, D_MODEL), lambda i: (i, 0)),
        out_shape=jax.ShapeDtypeStruct((n, D_MODEL), F32),
        scratch_shapes=[ybuf, ybuf, pltpu.SemaphoreType.DMA(()), pltpu.SemaphoreType.DMA(())],
        compiler_params=pltpu.CompilerParams(dimension_semantics=("arbitrary",),
                                             vmem_limit_bytes=VMEM_LIMIT),
        name="combine",
    )(dest_flat, dest_flat, rw, x1, gfin, y)


def _overlap_t(nc, n_cmp, n_sel):
    cs = np.arange(n_cmp)[None, :] * CMP_STRIDE
    ss = np.arange(n_sel)[:, None] * SEL_BLOCK
    ov = np.clip(np.minimum(cs + CMP_LEN, ss + SEL_BLOCK) - np.maximum(cs, ss), 0, None) / CMP_LEN
    out = np.zeros((LANES, nc), np.float32)
    out[:n_sel, :n_cmp] = ov
    return jnp.asarray(out, BF16)


def _layer(x2, b, s, g_mix, w_in, w_ck1, b_ck1, w_ck2, pe_k, w_cv1, b_cv1, w_cv2, pe_v,
           sgu_ln_g, sgu_ln_b, w_spatial, b_spatial, w_branch_a, w_branch_b, w_out,
           g_moe, w_router, b_router, w_gate_up, b_gate_up, w_down, b_down):
    n = b * s
    nc = s // CMP_STRIDE
    n_cmp = (s - CMP_LEN) // CMP_STRIDE + 1
    n_sel = s // SEL_BLOCK
    assert nc % KEY_CHUNK == 0 and n_sel <= LANES and n_cmp == nc - 1
    tm = TOKEN_TILE
    assert s % tm == 0 and n % ROUTE_TILE == 0 and n % SCATTER_TILE == 0

    p0 = ATTN_WIDTH
    p1 = p0 + 6 * KV_WIDTH
    p2 = p1 + N_NSA_BRANCH * N_HEADS
    p3 = p2 + SGU_WIDTH
    p4 = p3 + SGU_WIDTH
    zpad = lambda a, w: jnp.pad(a, ((0, 0),) * (a.ndim - 1) + ((0, w - a.shape[-1]),))
    wq = zpad((w_in[:, :p0] * (HEAD_DIM ** -0.5 * LOG2E)).reshape(D_MODEL, N_HEADS, HEAD_DIM), LANES)
    wkv = w_in[:, p0:p1].reshape(D_MODEL, 6, N_KV, HEAD_DIM)
    wng = w_in[:, p1:p2].reshape(D_MODEL, N_KV, GQA_R, N_NSA_BRANCH).transpose(0, 1, 3, 2)
    wng = zpad(wng.reshape(D_MODEL, N_KV, N_NSA_BRANCH * GQA_R), GATE_ROWS)
    wt = jnp.concatenate([wq.reshape(D_MODEL, -1), wkv[:, 3].reshape(D_MODEL, -1),
                          wkv[:, 5].reshape(D_MODEL, -1), wng.reshape(D_MODEL, -1)], axis=1).T.astype(BF16)
    slopes = 2.0 ** (-8.0 * np.arange(1, N_HEADS + 1) / N_HEADS)
    tcol = np.zeros((T_ROWS, 1), np.float32)
    head_rows = T_Q + np.arange(N_HEADS) * LANES + HEAD_DIM
    bf16_round = lambda a: a.astype(BF16).astype(np.float32)
    for k, coef in enumerate((slopes * SEL_BLOCK * LOG2E, slopes * LOG2E)):
        hi = bf16_round(coef.astype(np.float32))
        tcol[head_rows + 2 * k, 0] = hi
        tcol[head_rows + 2 * k + 1, 0] = bf16_round(coef.astype(np.float32) - hi)
    tconst = jnp.asarray(np.broadcast_to(tcol, (T_ROWS, 2 * Q_BLOCK)))
    wk = jnp.concatenate([zpad(wkv[:, 2, g], 2 * LANES) for g in range(N_KV)]
                         + [zpad(wkv[:, 4, g], LANES) for g in range(N_KV)]
                         + [wkv[:, 0].reshape(D_MODEL, -1), wkv[:, 1].reshape(D_MODEL, -1)], axis=1).astype(BF16)
    pos = np.arange(s)
    kc_np = np.zeros((s, 2 * LANES), np.float32)
    kc_np[:, HEAD_DIM:HEAD_DIM + 2] = (pos // SEL_BLOCK)[:, None]
    kc_np[:, HEAD_DIM + 2:HEAD_DIM + 4] = (pos % SEL_BLOCK)[:, None]
    kc_np[pos, LANES + pos // SEL_BLOCK] = 1.0
    kconst = jnp.asarray(kc_np, BF16)
    wu = w_in[:, p2:p3].astype(BF16)
    wv = w_in[:, p3:p4].astype(BF16)
    wmg = w_in[:, p4:].astype(BF16)
    bsp = jnp.repeat(b_spatial.T, SGU_GROUP_DIM, axis=1)

    qt, gt, vst, vwt, ksa, kwa, kcx, vcx, ga, gbyb = _proj(
        x2, b, s, g_mix[None], wt, tconst, wk, kconst, wu, wv, wmg, sgu_ln_g[None], sgu_ln_b[None],
        w_spatial, bsp, w_branch_b.astype(BF16), tm)

    half = CMP_LEN // 2
    eye = jnp.eye(N_KV, dtype=F32)[None, None, :, None, :, None]
    bdiag = lambda w: (w.reshape(2, half, 1, HEAD_DIM, 1, CMP_HID) * eye).reshape(
        2, half, N_KV * HEAD_DIM, N_KV * CMP_HID)
    w1 = jnp.stack([bdiag(w_ck1), bdiag(w_cv1)]).astype(BF16)
    pes = jnp.stack([jnp.tile(pe_k.reshape(2, half, HEAD_DIM), (1, 1, N_KV)),
                     jnp.tile(pe_v.reshape(2, half, HEAD_DIM), (1, 1, N_KV))])
    b1 = jnp.stack([jnp.tile(b_ck1, N_KV), jnp.tile(b_cv1, N_KV)])[:, None, :]
    blk_n = np.arange(nc)
    cc_np = np.zeros((nc, LANES), np.float32)
    cc_np[:, HEAD_DIM:HEAD_DIM + 2] = (blk_n // (SEL_BLOCK // CMP_STRIDE))[:, None]
    cc_np[:, HEAD_DIM + 2:HEAD_DIM + 4] = (blk_n % (SEL_BLOCK // CMP_STRIDE) * CMP_STRIDE)[:, None]
    kc, vct = _compress(kcx, vcx, pes, w1, b1, zpad(w_ck2, LANES).astype(BF16), w_cv2.T.astype(BF16),
                        jnp.asarray(cc_np))

    ovt = _overlap_t(nc, n_cmp, n_sel).reshape(LANES, nc // KEY_CHUNK, KEY_CHUNK).transpose(1, 0, 2)
    a_io, q_io = np.meshgrid(np.arange(KEY_CHUNK), np.arange(Q_BLOCK), indexing="ij")
    tri = np.stack([np.where(a_io > q_io, 0.0, NEG), np.where(a_io <= q_io, 0.0, NEG)]).astype(np.float32)
    tri = jnp.asarray(np.tile(tri, (1, 1, 2)))
    ya = _nsa(qt, gt, kc, vct, ksa, vst, kwa, vwt, ovt, tri, n_sel).reshape(n, ATTN_WIDTH)

    wr = jnp.pad(w_router, ((0, 0), (0, LANES - N_EXPERTS)))
    wrh = wr.astype(BF16)
    wr2 = jnp.concatenate([wrh, (wr - wrh.astype(F32)).astype(BF16)], axis=1)
    br = jnp.pad(b_router, (0, LANES - N_EXPERTS))[None]
    x1, hm, rw, ri, cnt = _merge(x2, ya, ga, gbyb, w_branch_a.astype(BF16), w_out.astype(BF16),
                                 g_moe[None], wr2, br, tm)

    counts = cnt[:, 0].astype(I32)
    padded = (counts + MOE_BLOCK - 1) // MOE_BLOCK * MOE_BLOCK
    pad_end = jnp.cumsum(padded)
    pad_start = pad_end - padded
    base = sum(jnp.where(ri[:TOP_K] == e, pad_start[e], 0) for e in range(N_EXPERTS))
    dest = (base + ri[TOP_K:]).T.reshape(-1)
    n_blocks = -(-(n * TOP_K) // MOE_BLOCK) + N_EXPERTS
    blk_start = jnp.arange(n_blocks, dtype=I32) * MOE_BLOCK
    blk_expert = jnp.minimum(jnp.sum((pad_end[None, :] <= blk_start[:, None]).astype(I32), axis=1),
                             N_EXPERTS - 1)
    n_used = (pad_end[-1:] // MOE_BLOCK).astype(I32)
    n_slots = n_blocks * MOE_BLOCK
    seg = jnp.stack([jnp.concatenate([pad_start + counts, pad_end[-1:]]),
                     jnp.concatenate([pad_end, jnp.full((1,), n_slots, I32)])]).astype(I32)

    xs = _dispatch(seg, dest, hm, n_slots, SCATTER_TILE)
    y = _experts(blk_expert, n_used, xs, w_gate_up, b_gate_up[:, None, :], w_down, b_down[:, None, :])
    return dest, rw, x1, y


def kernel(x, g_mix, w_in, w_ck1, b_ck1, w_ck2, pe_k, w_cv1, b_cv1, w_cv2, pe_v, sgu_ln_g, sgu_ln_b, w_spatial, b_spatial, w_branch_a, w_branch_b, w_out, g_moe, w_router, b_router, w_gate_up, b_gate_up, w_down, b_down, g_final):
    b, s, d = x.shape
    assert g_mix.shape[0] == 1, "the final rmsnorm is fused into the single layer's combine step"
    l = 0
    dest, rw, x1, y = _layer(
        x.reshape(b * s, d), b, s, g_mix[l], w_in[l], w_ck1[l], b_ck1[l], w_ck2[l], pe_k[l], w_cv1[l],
        b_cv1[l], w_cv2[l], pe_v[l], sgu_ln_g[l], sgu_ln_b[l], w_spatial[l], b_spatial[l], w_branch_a[l],
        w_branch_b[l], w_out[l], g_moe[l], w_router[l], b_router[l], w_gate_up[l], b_gate_up[l],
        w_down[l], b_down[l])
    return _combine(dest, rw, x1, g_final[None], y, ROUTE_TILE).reshape(b, s, d)
```

```python
import functools

import numpy as np
import jax
import jax.numpy as jnp
from jax import lax
from jax.experimental import pallas as pl
from jax.experimental.pallas import tpu as pltpu

D_MODEL = 1024
N_HEADS = 8
HEAD_DIM = 64
N_KV = 2
GQA_R = N_HEADS // N_KV
ATTN_WIDTH = N_HEADS * HEAD_DIM
KV_WIDTH = N_KV * HEAD_DIM
CMP_LEN = 32
CMP_STRIDE = 16
CMP_HID = 128
SEL_BLOCK = 64
SEL_TOPN = 16
WINDOW = 512
Q_BLOCK = 128
N_NSA_BRANCH = 3
SGU_WIDTH = 512
N_GROUPS_SGU = 8
SGU_GROUP_DIM = SGU_WIDTH // N_GROUPS_SGU
CHUNK = 128
N_EXPERTS = 32
TOP_K = 4
D_FF = D_MODEL
SWIGLU_LIMIT = 7.0
SWIGLU_ALPHA = 1.702
MOE_BLOCK = 512
RMS_EPS = 1e-5
LN_EPS = 1e-5
NEG = -1e30
FORCE = 1e4

LANES = 128
ROW_TILE = D_MODEL // LANES
MASK_BIAS = 1e9
LOG2E = float(np.log2(np.e))
KEY_CHUNK = 128
SEL_SLOTS = 4
GATE_ROWS = 16
TOKEN_TILE = 512
SCATTER_TILE = 2048
ROUTE_TILE = 512
VMEM_LIMIT = 56 * 1024 * 1024

F32 = jnp.float32
BF16 = jnp.bfloat16
I32 = jnp.int32

_NT = (((1,), (1,)), ((), ()))


def _dot(a, b):
    return jnp.dot(a, b, preferred_element_type=F32)


def _dot_nt(a, b):
    return lax.dot_general(a, b, _NT, preferred_element_type=F32)


def _sigmoid(x):
    return 1.0 / (1.0 + jnp.exp(-x))


def _gelu(x):
    c = np.float32(np.sqrt(2.0 / np.pi))
    return 0.5 * x * (1.0 + jnp.tanh(c * (x + 0.044715 * (x * x * x))))


def _rms(x, g):
    return x * lax.rsqrt(jnp.mean(x * x, axis=-1, keepdims=True) + RMS_EPS) * g


T_Q = 0
T_VS = T_Q + N_HEADS * LANES
T_VW = T_VS + N_KV * HEAD_DIM
T_GATE = T_VW + N_KV * HEAD_DIM
T_ROWS = T_GATE + N_KV * GATE_ROWS
K_SEL = 0
K_WIN = K_SEL + N_KV * 2 * LANES
K_CMP = K_WIN + N_KV * LANES
K_COLS = K_CMP + 4 * HEAD_DIM


def _proj_kernel(x_ref, gmix_ref, wt_ref, tconst_ref, wk_ref, kconst_ref, wu_ref, wv_ref, wmg_ref,
                 lng_ref, lnb_ref, wsp_ref, bsp_ref, wbb_ref,
                 qt_ref, gt_ref, vst_ref, vwt_ref, ksa_ref, kwa_ref, kcx_ref, vcx_ref, ga_ref, gbyb_ref):
    tm = x_ref.shape[0]
    hb = _rms(x_ref[...], gmix_ref[...]).astype(BF16)

    for h in range(tm // (2 * Q_BLOCK)):
        t = _dot_nt(wt_ref[...], hb[h * 2 * Q_BLOCK:(h + 1) * 2 * Q_BLOCK]) + tconst_ref[...]
        for jj in range(2):
            j = 2 * h + jj
            ls = slice(jj * Q_BLOCK, (jj + 1) * Q_BLOCK)
            for g in range(N_KV):
                for r in range(GQA_R):
                    r0 = T_Q + (g * GQA_R + r) * LANES
                    qt_ref[0, g, j, :, r * Q_BLOCK:(r + 1) * Q_BLOCK] = t[r0:r0 + LANES, ls].astype(BF16)
                vst_ref[0, g, j] = t[T_VS + g * HEAD_DIM:T_VS + (g + 1) * HEAD_DIM, ls].astype(BF16)
                vwt_ref[0, g, j] = t[T_VW + g * HEAD_DIM:T_VW + (g + 1) * HEAD_DIM, ls].astype(BF16)
                gt_ref[0, g, j] = _sigmoid(t[T_GATE + g * GATE_ROWS:T_GATE + (g + 1) * GATE_ROWS, ls])

    kk = _dot(hb, wk_ref[...])
    kconst = kconst_ref[...].astype(F32)
    for g in range(N_KV):
        c0 = K_SEL + g * 2 * LANES
        ksa_ref[0, g] = (kk[:, c0:c0 + 2 * LANES] + kconst).astype(BF16)
        c0 = K_WIN + g * LANES
        kwa_ref[0, g] = (kk[:, c0:c0 + LANES] + kconst[:, :LANES]).astype(BF16)
    kcx_ref[0] = kk[:, K_CMP:K_CMP + LANES]
    vcx_ref[0] = kk[:, K_CMP + LANES:K_COLS]

    mg = _dot(hb, wmg_ref[...])
    ga_ref[...] = _sigmoid(mg[:, :D_MODEL]).astype(BF16)

    u = _gelu(_dot(hb, wu_ref[...]))
    v = _gelu(_dot(hb, wv_ref[...]))
    mu = jnp.mean(v, axis=-1, keepdims=True)
    vc = v - mu
    var = jnp.mean(vc * vc, axis=-1, keepdims=True)
    vln = (vc * lax.rsqrt(var + LN_EPS) * lng_ref[...] + lnb_ref[...]).astype(BF16)

    row = lax.broadcasted_iota(I32, (CHUNK, CHUNK), 0)
    col = lax.broadcasted_iota(I32, (CHUNK, CHUNK), 1)
    tril = row >= col
    wsp = [jnp.where(tril, wsp_ref[g], 0.0).astype(BF16) for g in range(N_GROUPS_SGU)]
    low_half = col < SGU_GROUP_DIM
    bsp = bsp_ref[...]
    chunks = []
    for c in range(tm // CHUNK):
        rs = slice(c * CHUNK, (c + 1) * CHUNK)
        parts = []
        for p in range(SGU_WIDTH // LANES):
            cs = slice(p * LANES, (p + 1) * LANES)
            vblk = vln[rs, cs]
            mixed = jnp.where(low_half, _dot(wsp[2 * p], vblk), _dot(wsp[2 * p + 1], vblk))
            parts.append(u[rs, cs] * (mixed + bsp[:, cs]))
        chunks.append(jnp.concatenate(parts, axis=1))
    yb = jnp.concatenate(chunks, axis=0).astype(BF16)
    gbyb_ref[...] = (_sigmoid(mg[:, D_MODEL:]) * _dot(yb, wbb_ref[...])).astype(BF16)


def _proj(x2, b, s, gmix, wt, tconst, wk, kconst, wu, wv, wmg, lng, lnb, wsp, bsp, wbb, tm):
    n = x2.shape[0]
    tiles = s // tm
    qtiles = tm // Q_BLOCK
    full = lambda a: pl.BlockSpec(a.shape, lambda i, j: (0,) * a.ndim)
    rows = lambda w: pl.BlockSpec((tm, w), lambda i, j: (i * tiles + j, 0))
    per_g = lambda *blk: pl.BlockSpec((1, N_KV) + blk, lambda i, j: (i, 0, j) + (0,) * (len(blk) - 1))
    ins = (x2, gmix, wt, tconst, wk, kconst, wu, wv, wmg, lng, lnb, wsp, bsp, wbb)
    in_specs = [rows(D_MODEL)] + [full(a) for a in ins[1:]]
    in_specs[5] = pl.BlockSpec((tm, 2 * LANES), lambda i, j: (j, 0))
    nt = s // Q_BLOCK
    return pl.pallas_call(
        _proj_kernel,
        grid=(b, tiles),
        in_specs=in_specs,
        out_specs=[per_g(qtiles, LANES, GQA_R * Q_BLOCK), per_g(qtiles, GATE_ROWS, Q_BLOCK),
                   per_g(qtiles, HEAD_DIM, Q_BLOCK), per_g(qtiles, HEAD_DIM, Q_BLOCK),
                   per_g(tm, 2 * LANES), per_g(tm, LANES),
                   pl.BlockSpec((1, tm, LANES), lambda i, j: (i, j, 0)),
                   pl.BlockSpec((1, tm, LANES), lambda i, j: (i, j, 0)),
                   rows(D_MODEL), rows(D_MODEL)],
        out_shape=[jax.ShapeDtypeStruct((b, N_KV, nt, LANES, GQA_R * Q_BLOCK), BF16),
                   jax.ShapeDtypeStruct((b, N_KV, nt, GATE_ROWS, Q_BLOCK), F32),
                   jax.ShapeDtypeStruct((b, N_KV, nt, HEAD_DIM, Q_BLOCK), BF16),
                   jax.ShapeDtypeStruct((b, N_KV, nt, HEAD_DIM, Q_BLOCK), BF16),
                   jax.ShapeDtypeStruct((b, N_KV, s, 2 * LANES), BF16),
                   jax.ShapeDtypeStruct((b, N_KV, s, LANES), BF16),
                   jax.ShapeDtypeStruct((b, s, LANES), F32),
                   jax.ShapeDtypeStruct((b, s, LANES), F32),
                   jax.ShapeDtypeStruct((n, D_MODEL), BF16),
                   jax.ShapeDtypeStruct((n, D_MODEL), BF16)],
        compiler_params=pltpu.CompilerParams(dimension_semantics=("parallel", "parallel"),
                                             vmem_limit_bytes=VMEM_LIMIT),
        name="proj",
    )(*ins)


def _compress_kernel(xk_ref, xv_ref, pe_ref, w1_ref, b1_ref, w2k_ref, w2vt_ref, kconst_ref, kc_ref, vct_ref):
    nc = kc_ref.shape[2]
    half = CMP_LEN // 2
    hids = []
    for which, x_ref in enumerate((xk_ref, xv_ref)):
        za = jnp.zeros((nc, N_KV * CMP_HID), F32)
        zb = jnp.zeros((nc, N_KV * CMP_HID), F32)
        for l in range(half):
            xl = x_ref[0, pl.ds(l, nc, stride=CMP_STRIDE), :]
            za = za + _dot((xl + pe_ref[which, 0, l:l + 1, :]).astype(BF16), w1_ref[which, 0, l])
            zb = zb + _dot((xl + pe_ref[which, 1, l:l + 1, :]).astype(BF16), w1_ref[which, 1, l])
        hid = _gelu(za + pltpu.roll(zb, nc - 1, 0) + b1_ref[which])
        row = lax.broadcasted_iota(I32, hid.shape, 0)
        hids.append(jnp.where(row < nc - 1, hid, 0.0).astype(BF16))
    for g in range(N_KV):
        cs = slice(g * CMP_HID, (g + 1) * CMP_HID)
        kc_ref[0, g] = (_dot(hids[0][:, cs], w2k_ref[...]) + kconst_ref[...]).astype(BF16)
        vt = _dot_nt(w2vt_ref[...], hids[1][:, cs])
        for c in range(nc // KEY_CHUNK):
            vct_ref[0, g, c] = vt[:, c * KEY_CHUNK:(c + 1) * KEY_CHUNK].astype(BF16)


def _compress(xk, xv, pes, w1, b1, w2k, w2vt, kconst):
    b, s, w = xk.shape
    nc = s // CMP_STRIDE
    full = lambda a: pl.BlockSpec(a.shape, lambda i: (0,) * a.ndim)
    seq = pl.BlockSpec((1, s, w), lambda i: (i, 0, 0))
    return pl.pallas_call(
        _compress_kernel,
        grid=(b,),
        in_specs=[seq, seq, full(pes), full(w1), full(b1), full(w2k), full(w2vt), full(kconst)],
        out_specs=[pl.BlockSpec((1, N_KV, nc, LANES), lambda i: (i, 0, 0, 0)),
                   pl.BlockSpec((1, N_KV, nc // KEY_CHUNK, HEAD_DIM, KEY_CHUNK), lambda i: (i, 0, 0, 0, 0))],
        out_shape=[jax.ShapeDtypeStruct((b, N_KV, nc, LANES), BF16),
                   jax.ShapeDtypeStruct((b, N_KV, nc // KEY_CHUNK, HEAD_DIM, KEY_CHUNK), BF16)],
        compiler_params=pltpu.CompilerParams(dimension_semantics=("parallel",),
                                             vmem_limit_bytes=VMEM_LIMIT),
        name="compress",
    )(xk, xv, pes, w1, b1, w2k, w2vt, kconst)


def _pairs():
    return [slice(pr * 2 * Q_BLOCK, (pr + 1) * 2 * Q_BLOCK) for pr in range(GQA_R // 2)]


def _colmax(blocks):
    part = None
    for x in blocks:
        y = jnp.max(x.reshape(-1, 8, x.shape[-1]), axis=0)
        part = y if part is None else jnp.maximum(part, y)
    return jnp.max(part, axis=0, keepdims=True)


def _colsum(blocks):
    part = None
    for x in blocks:
        y = jnp.sum(x.reshape(-1, 8, x.shape[-1]), axis=0)
        part = y if part is None else part + y
    return jnp.sum(part, axis=0, keepdims=True)


def _softmax_pv(scores, vts, masks, m_ref, l_ref, acc_ref):
    for pr, ls in enumerate(_pairs()):
        sb = [x if masks[u] is None else masks[u](x, ls) for u, x in enumerate(scores[pr])]
        m_old = m_ref[0:1, ls]
        m_new = jnp.maximum(m_old, _colmax(sb))
        alpha = jnp.exp2(m_old - m_new)
        ps = [jnp.exp2(x - m_new) for x in sb]
        pv = _dot(vts[0], ps[0].astype(BF16))
        for u in range(1, len(ps)):
            pv = pv + _dot(vts[u], ps[u].astype(BF16))
        l_ref[0:1, ls] = alpha * l_ref[0:1, ls] + _colsum(ps)
        acc_ref[:, ls] = alpha * acc_ref[:, ls] + pv
        m_ref[0:1, ls] = m_new


def _nsa_kernel(qt_ref, gt_ref, kc_ref, vct_ref, ksa_ref, vst_ref, kwa_ref, vwt_ref, ovt_ref, tri_ref,
                o_ref, qa_ref, s0_ref, s1_ref, m_ref, l_ref, acc_ref, act_ref, *, n_sel, top_n):
    i = pl.program_id(1)
    t0 = i * Q_BLOCK
    width = GQA_R * Q_BLOCK
    n_chunks = ksa_ref.shape[2] // KEY_CHUNK
    n_cchunks = kc_ref.shape[2] // KEY_CHUNK
    n_slots = act_ref.shape[0] // N_KV
    lane = lax.broadcasted_iota(I32, (1, width), 1)
    tq = t0 + (lane & (Q_BLOCK - 1))
    sub = lax.broadcasted_iota(I32, (KEY_CHUNK, 1), 0)
    pairs = _pairs()
    heads = range(N_KV)

    def reset(g):
        m_ref[g] = jnp.full(m_ref.shape[1:], NEG, F32)
        l_ref[g] = jnp.zeros(l_ref.shape[1:], F32)
        acc_ref[g] = jnp.zeros(acc_ref.shape[1:], F32)

    sc = [[[_dot(kc_ref[0, g, u * KEY_CHUNK:(u + 1) * KEY_CHUNK, :], qt_ref[0, g, 0, :, ls])
            for u in range(n_cchunks)] for ls in pairs] for g in heads]
    o_c, imp_t = [], []
    for g in heads:
        ocg, ps = [], []
        for pr, ls in enumerate(pairs):
            sb = []
            for u in range(n_cchunks):
                c_end = (u * KEY_CHUNK + sub) * CMP_STRIDE + (CMP_LEN - 1)
                sb.append(jnp.where(c_end <= tq[:, ls], sc[g][pr][u], NEG))
            m = _colmax(sb)
            pb = [jnp.exp2(x - m) for x in sb]
            l = _colsum(pb)
            w = (tq[:, ls] >= CMP_LEN - 1).astype(F32) / l
            oc = _dot(vct_ref[0, g, 0], pb[0].astype(BF16))
            for u in range(1, n_cchunks):
                oc = oc + _dot(vct_ref[0, g, u], pb[u].astype(BF16))
            ocg.append(oc * w)
            ps.append([p[:, :Q_BLOCK] * w[:, :Q_BLOCK] + p[:, Q_BLOCK:] * w[:, Q_BLOCK:] for p in pb])
        imp2 = jnp.zeros((LANES, 2 * Q_BLOCK), F32)
        for u in range(n_cchunks):
            psu = ps[0][u] + ps[1][u]
            hi = psu.astype(BF16)
            lo = (psu - hi.astype(F32)).astype(BF16)
            imp2 = imp2 + _dot(ovt_ref[u], jnp.concatenate([hi, lo], axis=1))
        o_c.append(jnp.concatenate(ocg, axis=1))
        imp_t.append(imp2[:, :Q_BLOCK] + imp2[:, Q_BLOCK:])

    n_win = WINDOW // KEY_CHUNK
    cls, masks = [], []
    for u in range(n_win + 1):
        c = i - n_win + u
        cls.append(jnp.maximum(c, 0))
        off = jnp.where(c < 0, NEG, 0.0)
        if u == 0:
            edge = tri_ref[0] + off
            masks.append(lambda x, ls, edge=edge: x + edge)
        elif u == n_win:
            masks.append(lambda x, ls: x + tri_ref[1])
        else:
            masks.append(lambda x, ls, off=off: x + off)
    wsc = [[[_dot(kwa_ref[0, g, pl.ds(pl.multiple_of(cl * KEY_CHUNK, KEY_CHUNK), KEY_CHUNK), :],
                  qt_ref[0, g, 0, :, ls]) for cl in cls] for ls in pairs] for g in heads]
    o_w = []
    for g in heads:
        reset(g)
        _softmax_pv(wsc[g], [vwt_ref[0, g, cl] for cl in cls], masks, m_ref.at[g], l_ref.at[g], acc_ref.at[g])
        o_w.append(acc_ref[g] / l_ref[g, 0:1, :])

    j_io = lax.broadcasted_iota(I32, (LANES, Q_BLOCK), 0)
    tl = t0 + lax.broadcasted_iota(I32, (LANES, Q_BLOCK), 1)
    cur = tl // SEL_BLOCK
    forced = (j_io == 0) | (j_io == cur) | (j_io == cur - 1)
    prio = [jnp.where(j_io < n_sel, jnp.where(forced, FORCE, jnp.where(j_io * SEL_BLOCK <= tl, imp_t[g], -1.0)), NEG)
            for g in heads]
    for _ in range(top_n):
        for g in heads:
            m = jnp.max(prio[g], axis=0, keepdims=True)
            idx = jnp.min(jnp.where(prio[g] == m, j_io, LANES), axis=0, keepdims=True)
            prio[g] = jnp.where(j_io == idx, NEG, prio[g])
    sel_t = [jnp.where((prio[g] == NEG) & (j_io < n_sel), 1.0, 0.0) for g in heads]

    n_act = []
    for u in range(act_ref.shape[0]):
        act_ref[u] = jnp.int32(-1)
    for g in heads:
        qa_ref[g, 0:LANES, :] = qt_ref[0, g, 0]
        bias_t = ((sel_t[g] - 1.0) * MASK_BIAS).astype(BF16)
        for h in range(GQA_R):
            qa_ref[g, LANES:2 * LANES, h * Q_BLOCK:(h + 1) * Q_BLOCK] = bias_t
        any_q = jnp.max(sel_t[g], axis=1, keepdims=True)
        any_pair = jnp.maximum(any_q, pltpu.roll(any_q, LANES - 1, 0))
        na = jnp.int32(0)
        for c in range(n_chunks):
            act_ref[g * n_slots + na] = jnp.int32(c)
            na = na + jnp.where((any_pair[2 * c, 0] > 0.0) & (c < i), 1, 0)
        act_ref[g * n_slots + na] = i
        n_act.append(na)

    def sel_scores(g, k, dst):
        for u in range(SEL_SLOTS):
            cl = jnp.maximum(act_ref[g * n_slots + k * SEL_SLOTS + u], 0)
            keys = ksa_ref[0, g, pl.ds(pl.multiple_of(cl * KEY_CHUNK, KEY_CHUNK), KEY_CHUNK), :]
            for ls in pairs:
                dst[g, u, :, ls] = _dot(keys, qa_ref[g, :, ls])

    def sel_step(g, k, cur_s, nxt_s, last, ahead):
        if ahead:
            sel_scores(g, k + 1, nxt_s)
        vts, msk = [], []
        for u in range(SEL_SLOTS):
            c = act_ref[g * n_slots + k * SEL_SLOTS + u]
            vts.append(vst_ref[0, g, jnp.maximum(c, 0)])
            if last:
                kpos = jnp.where(c < 0, n_chunks * KEY_CHUNK, c * KEY_CHUNK) + sub
                msk.append(lambda x, ls, kpos=kpos: jnp.where(kpos <= tq[:, ls], x, -MASK_BIAS))
            else:
                msk.append(None)
        scores = [[cur_s[g, u, :, ls] for u in range(SEL_SLOTS)] for ls in pairs]
        _softmax_pv(scores, vts, msk, m_ref.at[g], l_ref.at[g], acc_ref.at[g])

    for g in heads:
        reset(g)
        sel_scores(g, 0, s0_ref)
    n_steps = [(n_act[g] + SEL_SLOTS) // SEL_SLOTS for g in heads]
    bufs = ((s0_ref, s1_ref), (s1_ref, s0_ref))

    RUN, LAST, DONE = 0, 1, 2

    def sel_body(k, carry):
        state = [jnp.where(k < n_steps[g] - 1, RUN, jnp.where(k == n_steps[g] - 1, LAST, DONE)) for g in heads]
        for odd, (cur_s, nxt_s) in enumerate(bufs):
            for st0 in (RUN, LAST, DONE):
                for st1 in (RUN, LAST, DONE):
                    if st0 == DONE and st1 == DONE:
                        continue

                    @pl.when((k % 2 == odd) & (state[0] == st0) & (state[1] == st1))
                    def _():
                        for g, st in enumerate((st0, st1)):
                            if st != DONE:
                                sel_step(g, k, cur_s, nxt_s, st == LAST, st == RUN)

        return carry

    lax.fori_loop(0, jnp.maximum(n_steps[0], n_steps[1]), sel_body, 0)
    o_s = [acc_ref[g] / l_ref[g, 0:1, :] for g in heads]

    for g in heads:
        gt = gt_ref[0, g, 0]

        def gate(br):
            return jnp.concatenate([gt[br * GQA_R + h:br * GQA_R + h + 1, :] for h in range(GQA_R)], axis=1)

        o = gate(0) * o_c[g] + gate(1) * o_s[g] + gate(2) * o_w[g]
        o_ref[0, :, g * GQA_R * HEAD_DIM:(g + 1) * GQA_R * HEAD_DIM] = jnp.concatenate(
            [o[:, h * Q_BLOCK:(h + 1) * Q_BLOCK].T for h in range(GQA_R)], axis=1).astype(BF16)


def _nsa(qt, gt, kc, vct, ksa, vst, kwa, vwt, ovt, tri, n_sel):
    b, g, nt, _, width = qt.shape
    s = ksa.shape[2]
    tile = lambda a: pl.BlockSpec((1, g, 1) + a.shape[3:], lambda i, k: (i, 0, k, 0, 0))
    whole = lambda a: pl.BlockSpec((1,) + a.shape[1:], lambda i, k: (i,) + (0,) * (a.ndim - 1))
    const = lambda a: pl.BlockSpec(a.shape, lambda i, k: (0,) * a.ndim)
    kern = functools.partial(_nsa_kernel, n_sel=n_sel, top_n=min(SEL_TOPN, n_sel))
    return pl.pallas_call(
        kern,
        grid=(b, nt),
        in_specs=[tile(qt), tile(gt), whole(kc), whole(vct), whole(ksa), whole(vst), whole(kwa), whole(vwt),
                  const(ovt), const(tri)],
        out_specs=pl.BlockSpec((1, Q_BLOCK, ATTN_WIDTH), lambda i, k: (i, k, 0)),
        out_shape=jax.ShapeDtypeStruct((b, s, ATTN_WIDTH), BF16),
        scratch_shapes=[pltpu.VMEM((g, 2 * LANES, width), BF16),
                        pltpu.VMEM((g, SEL_SLOTS, KEY_CHUNK, width), F32),
                        pltpu.VMEM((g, SEL_SLOTS, KEY_CHUNK, width), F32),
                        pltpu.VMEM((g, 8, width), F32),
                        pltpu.VMEM((g, 8, width), F32),
                        pltpu.VMEM((g, HEAD_DIM, width), F32),
                        pltpu.SMEM((g * (s // KEY_CHUNK + 2 * SEL_SLOTS),), I32)],
        compiler_params=pltpu.CompilerParams(dimension_semantics=("parallel", "arbitrary"),
                                             vmem_limit_bytes=VMEM_LIMIT),
        name="nsa",
    )(qt, gt, kc, vct, ksa, vst, kwa, vwt, ovt, tri)


def _merge_kernel(x_ref, ya_ref, ga_ref, gbyb_ref, wba_ref, wout_ref, gmoe_ref, wr_ref, br_ref, utri_ref,
                  x1_ref, hm_ref, rw_ref, ri_ref, cnt_ref, carry_ref):
    tm = x_ref.shape[0]

    @pl.when(pl.program_id(0) == 0)
    def _():
        carry_ref[...] = jnp.zeros_like(carry_ref)

    merged = ga_ref[...].astype(F32) * _dot(ya_ref[...], wba_ref[...]) + gbyb_ref[...].astype(F32)
    x1 = x_ref[...] + _dot(merged.astype(BF16), wout_ref[...])
    x1_ref[...] = x1
    hm = _rms(x1, gmoe_ref[...])
    for a in range(ROW_TILE):
        hm_ref[pl.ds(a, tm, stride=ROW_TILE), :] = hm[:, a * LANES:(a + 1) * LANES]

    hh = hm.astype(BF16)
    hl = (hm - hh.astype(F32)).astype(BF16)
    both = _dot(hh, wr_ref[...])
    logits = both[:, :LANES] + both[:, LANES:] + _dot(hl, wr_ref[:, :LANES]) + br_ref[...]
    lg = logits.T[0:N_EXPERTS, :]
    e_io = lax.broadcasted_iota(I32, (N_EXPERTS, tm), 0)
    vals, idxs = [], []
    for _ in range(TOP_K):
        m = jnp.max(lg, axis=0, keepdims=True)
        idx = jnp.min(jnp.where(lg == m, e_io, N_EXPERTS), axis=0, keepdims=True)
        vals.append(m)
        idxs.append(idx)
        lg = jnp.where(e_io == idx, NEG, lg)
    ex = [jnp.exp(v - vals[0]) for v in vals]
    den = ex[0] + ex[1] + ex[2] + ex[3]

    hits = [e_io == idx for idx in idxs]
    multi = jnp.zeros((N_EXPERTS, tm), F32)
    for h in hits:
        multi = jnp.where(h, 1.0, multi)
    carry = carry_ref[:, 0:1]
    cum = _dot(multi.astype(BF16), utri_ref[...]) + carry
    ranks = [jnp.sum(jnp.where(h, cum, 0.0), axis=0, keepdims=True).astype(I32) for h in hits]
    ri_ref[...] = jnp.concatenate(idxs + ranks, axis=0)
    wts = jnp.concatenate([e / den for e in ex] + [jnp.zeros((LANES - TOP_K, tm), F32)], axis=0)
    rw_ref[...] = wts.T
    new_carry = carry + jnp.sum(multi, axis=1, keepdims=True)
    carry_ref[...] = jnp.broadcast_to(new_carry, carry_ref.shape)
    cnt_ref[...] = jnp.broadcast_to(new_carry, cnt_ref.shape)


def _merge(x2, ya, ga, gbyb, wba, wout, gmoe, wr, br, tm):
    n = x2.shape[0]
    full = lambda a: pl.BlockSpec(a.shape, lambda i: (0,) * a.ndim)
    rows = lambda w: pl.BlockSpec((tm, w), lambda i: (i, 0))
    utri = jnp.asarray(np.triu(np.ones((tm, tm), np.float32), 1), BF16)
    return pl.pallas_call(
        _merge_kernel,
        grid=(n // tm,),
        in_specs=[rows(D_MODEL), rows(ATTN_WIDTH), rows(D_MODEL), rows(D_MODEL),
                  full(wba), full(wout), full(gmoe), full(wr), full(br), full(utri)],
        out_specs=[rows(D_MODEL), pl.BlockSpec((tm * ROW_TILE, LANES), lambda i: (i, 0)), rows(LANES),
                   pl.BlockSpec((2 * TOP_K, tm), lambda i: (0, i)),
                   pl.BlockSpec((N_EXPERTS, LANES), lambda i: (0, 0))],
        out_shape=[jax.ShapeDtypeStruct((n, D_MODEL), F32),
                   jax.ShapeDtypeStruct((n * ROW_TILE, LANES), F32),
                   jax.ShapeDtypeStruct((n, LANES), F32),
                   jax.ShapeDtypeStruct((2 * TOP_K, n), I32),
                   jax.ShapeDtypeStruct((N_EXPERTS, LANES), F32)],
        scratch_shapes=[pltpu.VMEM((N_EXPERTS, LANES), F32)],
        compiler_params=pltpu.CompilerParams(dimension_semantics=("arbitrary",),
                                             vmem_limit_bytes=VMEM_LIMIT),
        name="merge",
    )(x2, ya, ga, gbyb, wba, wout, gmoe, wr, br, utri)


def _tile_copy(src, i, dst, d, sem):
    return pltpu.make_async_copy(src.at[pl.ds(pl.multiple_of(i * ROW_TILE, ROW_TILE), ROW_TILE)],
                                 dst.at[pl.ds(pl.multiple_of(d * ROW_TILE, ROW_TILE), ROW_TILE)], sem)


def _dispatch_kernel(seg_ref, dest_ref, hm_ref, xs_ref, zero_ref, sem, zsem, *, n_pad):
    tm = hm_ref.shape[0] // ROW_TILE

    @pl.when(pl.program_id(0) == 0)
    def _():
        zero_ref[...] = jnp.zeros_like(zero_ref)

        def seg(e, c):
            def fill(r, c2):
                _tile_copy(zero_ref, 0, xs_ref, r, zsem).start()
                return c2
            return lax.fori_loop(seg_ref[0, e], seg_ref[1, e], fill, c)

        lax.fori_loop(0, N_EXPERTS + 1, seg, 0)
        pad_rows = xs_ref.at[pl.ds(0, n_pad * ROW_TILE)]
        pltpu.make_async_copy(pad_rows, pad_rows, zsem).wait()

    def issue(r, c):
        slots = [dest_ref[r * TOP_K + k] for k in range(TOP_K)]
        for k in range(TOP_K):
            _tile_copy(hm_ref, r, xs_ref, slots[k], sem).start(priority=k % 2)
        return c

    lax.fori_loop(0, tm, issue, 0, unroll=4)
    for k in range(TOP_K):
        pltpu.make_async_copy(hm_ref, xs_ref.at[pl.ds(0, tm * ROW_TILE)], sem).wait()


def _dispatch(seg, dest_flat, hm, n_slots, tm):
    n = hm.shape[0] // ROW_TILE
    kern = functools.partial(_dispatch_kernel, n_pad=n_slots - n * TOP_K)
    return pl.pallas_call(
        kern,
        grid_spec=pltpu.PrefetchScalarGridSpec(
            num_scalar_prefetch=1,
            grid=(n // tm,),
            in_specs=[pl.BlockSpec((tm * TOP_K,), lambda i, sg: (i,), memory_space=pltpu.SMEM),
                      pl.BlockSpec((tm * ROW_TILE, LANES), lambda i, sg: (i, 0))],
            out_specs=pl.BlockSpec(memory_space=pl.ANY),
            scratch_shapes=[pltpu.VMEM((ROW_TILE, LANES), F32),
                            pltpu.SemaphoreType.DMA(()), pltpu.SemaphoreType.DMA(())]),
        out_shape=jax.ShapeDtypeStruct((n_slots * ROW_TILE, LANES), F32),
        compiler_params=pltpu.CompilerParams(dimension_semantics=("arbitrary",),
                                             has_side_effects=True),
        name="dispatch",
    )(seg, dest_flat, hm)


def _expert_kernel(be_ref, nu_ref, xs_ref, wgu_ref, bgu_ref, wd_ref, bd_ref, y_ref, wgu_bf, wd_bf):
    i = pl.program_id(0)

    @pl.when(i >= nu_ref[0])
    def _():
        y_ref[...] = jnp.zeros_like(y_ref)

    @pl.when((i == 0) | (be_ref[i] != be_ref[jnp.maximum(i - 1, 0)]))
    def _():
        wgu_bf[...] = wgu_ref[0].astype(BF16)
        wd_bf[...] = wd_ref[0].astype(BF16)

    @pl.when(i < nu_ref[0])
    def _():
        x = jnp.concatenate([xs_ref[pl.ds(a, MOE_BLOCK, stride=ROW_TILE), :] for a in range(ROW_TILE)], axis=1)
        gu = _dot(x.astype(BF16), wgu_bf[...]) + bgu_ref[0]
        gate = jnp.minimum(gu[:, :D_FF], SWIGLU_LIMIT)
        up = jnp.clip(gu[:, D_FF:], -SWIGLU_LIMIT, SWIGLU_LIMIT)
        act = gate * _sigmoid(SWIGLU_ALPHA * gate) * (up + 1.0)
        y = _dot(act.astype(BF16), wd_bf[...]) + bd_ref[0]
        for a in range(ROW_TILE):
            y_ref[pl.ds(a, MOE_BLOCK, stride=ROW_TILE), :] = y[:, a * LANES:(a + 1) * LANES]


def _experts(blk_expert, n_used, xs, wgu, bgu, wd, bd):
    n_blocks = xs.shape[0] // (MOE_BLOCK * ROW_TILE)
    blk = lambda i, be, nu: (jnp.minimum(i, nu[0] - 1), 0)
    exp3 = lambda i, be, nu: (be[jnp.minimum(i, nu[0] - 1)], 0, 0)
    return pl.pallas_call(
        _expert_kernel,
        grid_spec=pltpu.PrefetchScalarGridSpec(
            num_scalar_prefetch=2,
            grid=(n_blocks,),
            in_specs=[pl.BlockSpec((MOE_BLOCK * ROW_TILE, LANES), blk),
                      pl.BlockSpec((1, D_MODEL, 2 * D_FF), exp3),
                      pl.BlockSpec((1, 1, 2 * D_FF), exp3),
                      pl.BlockSpec((1, D_FF, D_MODEL), exp3),
                      pl.BlockSpec((1, 1, D_MODEL), exp3)],
            out_specs=pl.BlockSpec((MOE_BLOCK * ROW_TILE, LANES), lambda i, be, nu: (i, 0)),
            scratch_shapes=[pltpu.VMEM((D_MODEL, 2 * D_FF), BF16), pltpu.VMEM((D_FF, D_MODEL), BF16)]),
        out_shape=jax.ShapeDtypeStruct(xs.shape, F32),
        compiler_params=pltpu.CompilerParams(dimension_semantics=("arbitrary",),
                                             vmem_limit_bytes=VMEM_LIMIT),
        name="experts",
    )(blk_expert, n_used, xs, wgu, bgu, wd, bd)


def _combine_kernel(dest_ref, dest_next_ref, rw_ref, x1_ref, gfin_ref, y_ref, o_ref, ybuf0, ybuf1, sem0, sem1):
    tm = x1_ref.shape[0]
    i = pl.program_id(0)

    def gather(dst_ref, ybuf, sem):
        def issue(r, c):
            slots = [dst_ref[r * TOP_K + k] for k in range(TOP_K)]
            for k in range(TOP_K):
                _tile_copy(y_ref, slots[k], ybuf.at[k], r, sem).start(priority=k % 2)
            return c

        lax.fori_loop(0, tm, issue, 0, unroll=4)

    def finish(ybuf, sem):
        for k in range(TOP_K):
            pltpu.make_async_copy(y_ref.at[pl.ds(0, tm * ROW_TILE)], ybuf.at[k], sem).wait()
        rw = rw_ref[...]
        x1 = x1_ref[...]
        cols = []
        for a in range(ROW_TILE):
            acc = x1[:, a * LANES:(a + 1) * LANES]
            for k in range(TOP_K):
                acc = acc + rw[:, k:k + 1] * ybuf[k, pl.ds(a, tm, stride=ROW_TILE), :]
            cols.append(acc)
        o_ref[...] = _rms(jnp.concatenate(cols, axis=1), gfin_ref[...])

    @pl.when(i == 0)
    def _():
        gather(dest_ref, ybuf0, sem0)

    for parity, (cur, nxt) in enumerate((((ybuf0, sem0), (ybuf1, sem1)), ((ybuf1, sem1), (ybuf0, sem0)))):
        @pl.when(i % 2 == parity)
        def _():
            @pl.when(i + 1 < pl.num_programs(0))
            def _():
                gather(dest_next_ref, *nxt)

            finish(*cur)


def _combine(dest_flat, rw, x1, gfin, y, tm):
    n = x1.shape[0]
    steps = n // tm
    ybuf = pltpu.VMEM((TOP_K, tm * ROW_TILE, LANES), F32)
    return pl.pallas_call(
        _combine_kernel,
        grid=(steps,),
        in_specs=[pl.BlockSpec((tm * TOP_K,), lambda i: (i,), memory_space=pltpu.SMEM),
                  pl.BlockSpec((tm * TOP_K,), lambda i: (jnp.minimum(i + 1, steps - 1),), memory_space=pltpu.SMEM),
                  pl.BlockSpec((tm, LANES), lambda i: (i, 0)),
                  pl.BlockSpec((tm, D_MODEL), lambda i: (i, 0)),
                  pl.BlockSpec((1, D_MODEL), lambda i: (0, 0)),
                  pl.BlockSpec(memory_space=pl.ANY)],
        out_specs=pl.BlockSpec((tm, D_MODEL), lambda i: (i, 0)),
        out_shape=jax.ShapeDtypeStruct((n, D_MODEL), F32),
        scratch_shapes=[ybuf, ybuf, pltpu.SemaphoreType.DMA(()), pltpu.SemaphoreType.DMA(())],
        compiler_params=pltpu.CompilerParams(dimension_semantics=("arbitrary",),
                                             vmem_limit_bytes=VMEM_LIMIT),
        name="combine",
    )(dest_flat, dest_flat, rw, x1, gfin, y)


def _overlap_t(nc, n_cmp, n_sel):
    cs = np.arange(n_cmp)[None, :] * CMP_STRIDE
    ss = np.arange(n_sel)[:, None] * SEL_BLOCK
    ov = np.clip(np.minimum(cs + CMP_LEN, ss + SEL_BLOCK) - np.maximum(cs, ss), 0, None) / CMP_LEN
    out = np.zeros((LANES, nc), np.float32)
    out[:n_sel, :n_cmp] = ov
    return jnp.asarray(out, BF16)


def _layer(x2, b, s, g_mix, w_in, w_ck1, b_ck1, w_ck2, pe_k, w_cv1, b_cv1, w_cv2, pe_v,
           sgu_ln_g, sgu_ln_b, w_spatial, b_spatial, w_branch_a, w_branch_b, w_out,
           g_moe, w_router, b_router, w_gate_up, b_gate_up, w_down, b_down):
    n = b * s
    nc = s // CMP_STRIDE
    n_cmp = (s - CMP_LEN) // CMP_STRIDE + 1
    n_sel = s // SEL_BLOCK
    assert nc % KEY_CHUNK == 0 and n_sel <= LANES and n_cmp == nc - 1
    tm = TOKEN_TILE
    assert s % tm == 0 and n % ROUTE_TILE == 0 and n % SCATTER_TILE == 0

    p0 = ATTN_WIDTH
    p1 = p0 + 6 * KV_WIDTH
    p2 = p1 + N_NSA_BRANCH * N_HEADS
    p3 = p2 + SGU_WIDTH
    p4 = p3 + SGU_WIDTH
    zpad = lambda a, w: jnp.pad(a, ((0, 0),) * (a.ndim - 1) + ((0, w - a.shape[-1]),))
    wq = zpad((w_in[:, :p0] * (HEAD_DIM ** -0.5 * LOG2E)).reshape(D_MODEL, N_HEADS, HEAD_DIM), LANES)
    wkv = w_in[:, p0:p1].reshape(D_MODEL, 6, N_KV, HEAD_DIM)
    wng = w_in[:, p1:p2].reshape(D_MODEL, N_KV, GQA_R, N_NSA_BRANCH).transpose(0, 1, 3, 2)
    wng = zpad(wng.reshape(D_MODEL, N_KV, N_NSA_BRANCH * GQA_R), GATE_ROWS)
    wt = jnp.concatenate([wq.reshape(D_MODEL, -1), wkv[:, 3].reshape(D_MODEL, -1),
                          wkv[:, 5].reshape(D_MODEL, -1), wng.reshape(D_MODEL, -1)], axis=1).T.astype(BF16)
    slopes = 2.0 ** (-8.0 * np.arange(1, N_HEADS + 1) / N_HEADS)
    tcol = np.zeros((T_ROWS, 1), np.float32)
    head_rows = T_Q + np.arange(N_HEADS) * LANES + HEAD_DIM
    bf16_round = lambda a: a.astype(BF16).astype(np.float32)
    for k, coef in enumerate((slopes * SEL_BLOCK * LOG2E, slopes * LOG2E)):
        hi = bf16_round(coef.astype(np.float32))
        tcol[head_rows + 2 * k, 0] = hi
        tcol[head_rows + 2 * k + 1, 0] = bf16_round(coef.astype(np.float32) - hi)
    tconst = jnp.asarray(np.broadcast_to(tcol, (T_ROWS, 2 * Q_BLOCK)))
    wk = jnp.concatenate([zpad(wkv[:, 2, g], 2 * LANES) for g in range(N_KV)]
                         + [zpad(wkv[:, 4, g], LANES) for g in range(N_KV)]
                         + [wkv[:, 0].reshape(D_MODEL, -1), wkv[:, 1].reshape(D_MODEL, -1)], axis=1).astype(BF16)
    pos = np.arange(s)
    kc_np = np.zeros((s, 2 * LANES), np.float32)
    kc_np[:, HEAD_DIM:HEAD_DIM + 2] = (pos // SEL_BLOCK)[:, None]
    kc_np[:, HEAD_DIM + 2:HEAD_DIM + 4] = (pos % SEL_BLOCK)[:, None]
    kc_np[pos, LANES + pos // SEL_BLOCK] = 1.0
    kconst = jnp.asarray(kc_np, BF16)
    wu = w_in[:, p2:p3].astype(BF16)
    wv = w_in[:, p3:p4].astype(BF16)
    wmg = w_in[:, p4:].astype(BF16)
    bsp = jnp.repeat(b_spatial.T, SGU_GROUP_DIM, axis=1)

    qt, gt, vst, vwt, ksa, kwa, kcx, vcx, ga, gbyb = _proj(
        x2, b, s, g_mix[None], wt, tconst, wk, kconst, wu, wv, wmg, sgu_ln_g[None], sgu_ln_b[None],
        w_spatial, bsp, w_branch_b.astype(BF16), tm)

    half = CMP_LEN // 2
    eye = jnp.eye(N_KV, dtype=F32)[None, None, :, None, :, None]
    bdiag = lambda w: (w.reshape(2, half, 1, HEAD_DIM, 1, CMP_HID) * eye).reshape(
        2, half, N_KV * HEAD_DIM, N_KV * CMP_HID)
    w1 = jnp.stack([bdiag(w_ck1), bdiag(w_cv1)]).astype(BF16)
    pes = jnp.stack([jnp.tile(pe_k.reshape(2, half, HEAD_DIM), (1, 1, N_KV)),
                     jnp.tile(pe_v.reshape(2, half, HEAD_DIM), (1, 1, N_KV))])
    b1 = jnp.stack([jnp.tile(b_ck1, N_KV), jnp.tile(b_cv1, N_KV)])[:, None, :]
    blk_n = np.arange(nc)
    cc_np = np.zeros((nc, LANES), np.float32)
    cc_np[:, HEAD_DIM:HEAD_DIM + 2] = (blk_n // (SEL_BLOCK // CMP_STRIDE))[:, None]
    cc_np[:, HEAD_DIM + 2:HEAD_DIM + 4] = (blk_n % (SEL_BLOCK // CMP_STRIDE) * CMP_STRIDE)[:, None]
    kc, vct = _compress(kcx, vcx, pes, w1, b1, zpad(w_ck2, LANES).astype(BF16), w_cv2.T.astype(BF16),
                        jnp.asarray(cc_np))

    ovt = _overlap_t(nc, n_cmp, n_sel).reshape(LANES, nc // KEY_CHUNK, KEY_CHUNK).transpose(1, 0, 2)
    a_io, q_io = np.meshgrid(np.arange(KEY_CHUNK), np.arange(Q_BLOCK), indexing="ij")
    tri = np.stack([np.where(a_io > q_io, 0.0, NEG), np.where(a_io <= q_io, 0.0, NEG)]).astype(np.float32)
    tri = jnp.asarray(np.tile(tri, (1, 1, 2)))
    ya = _nsa(qt, gt, kc, vct, ksa, vst, kwa, vwt, ovt, tri, n_sel).reshape(n, ATTN_WIDTH)

    wr = jnp.pad(w_router, ((0, 0), (0, LANES - N_EXPERTS)))
    wrh = wr.astype(BF16)
    wr2 = jnp.concatenate([wrh, (wr - wrh.astype(F32)).astype(BF16)], axis=1)
    br = jnp.pad(b_router, (0, LANES - N_EXPERTS))[None]
    x1, hm, rw, ri, cnt = _merge(x2, ya, ga, gbyb, w_branch_a.astype(BF16), w_out.astype(BF16),
                                 g_moe[None], wr2, br, tm)

    counts = cnt[:, 0].astype(I32)
    padded = (counts + MOE_BLOCK - 1) // MOE_BLOCK * MOE_BLOCK
    pad_end = jnp.cumsum(padded)
    pad_start = pad_end - padded
    base = sum(jnp.where(ri[:TOP_K] == e, pad_start[e], 0) for e in range(N_EXPERTS))
    dest = (base + ri[TOP_K:]).T.reshape(-1)
    n_blocks = -(-(n * TOP_K) // MOE_BLOCK) + N_EXPERTS
    blk_start = jnp.arange(n_blocks, dtype=I32) * MOE_BLOCK
    blk_expert = jnp.minimum(jnp.sum((pad_end[None, :] <= blk_start[:, None]).astype(I32), axis=1),
                             N_EXPERTS - 1)
    n_used = (pad_end[-1:] // MOE_BLOCK).astype(I32)
    n_slots = n_blocks * MOE_BLOCK
    seg = jnp.stack([jnp.concatenate([pad_start + counts, pad_end[-1:]]),
                     jnp.concatenate([pad_end, jnp.full((1,), n_slots, I32)])]).astype(I32)

    xs = _dispatch(seg, dest, hm, n_slots, SCATTER_TILE)
    y = _experts(blk_expert, n_used, xs, w_gate_up, b_gate_up[:, None, :], w_down, b_down[:, None, :])
    return dest, rw, x1, y


def kernel(x, g_mix, w_in, w_ck1, b_ck1, w_ck2, pe_k, w_cv1, b_cv1, w_cv2, pe_v, sgu_ln_g, sgu_ln_b, w_spatial, b_spatial, w_branch_a, w_branch_b, w_out, g_moe, w_router, b_router, w_gate_up, b_gate_up, w_down, b_down, g_final):
    b, s, d = x.shape
    assert g_mix.shape[0] == 1, "the final rmsnorm is fused into the single layer's combine step"
    l = 0
    dest, rw, x1, y = _layer(
        x.reshape(b * s, d), b, s, g_mix[l], w_in[l], w_ck1[l], b_ck1[l], w_ck2[l], pe_k[l], w_cv1[l],
        b_cv1[l], w_cv2[l], pe_v[l], sgu_ln_g[l], sgu_ln_b[l], w_spatial[l], b_spatial[l], w_branch_a[l],
        w_branch_b[l], w_out[l], g_moe[l], w_router[l], b_router[l], w_gate_up[l], b_gate_up[l],
        w_down[l], b_down[l])
    return _combine(dest, rw, x1, g_final[None], y, ROUTE_TILE).reshape(b, s, d)
```

```python
import functools

import numpy as np
import jax
import jax.numpy as jnp
from jax import lax
from jax.experimental import pallas as pl
from jax.experimental.pallas import tpu as pltpu

D_MODEL = 1024
N_HEADS = 8
HEAD_DIM = 64
N_KV = 2
GQA_R = N_HEADS // N_KV
ATTN_WIDTH = N_HEADS * HEAD_DIM
KV_WIDTH = N_KV * HEAD_DIM
CMP_LEN = 32
CMP_STRIDE = 16
CMP_HID = 128
SEL_BLOCK = 64
SEL_TOPN = 16
WINDOW = 512
Q_BLOCK = 128
N_NSA_BRANCH = 3
SGU_WIDTH = 512
N_GROUPS_SGU = 8
SGU_GROUP_DIM = SGU_WIDTH // N_GROUPS_SGU
CHUNK = 128
N_EXPERTS = 32
TOP_K = 4
D_FF = D_MODEL
SWIGLU_LIMIT = 7.0
SWIGLU_ALPHA = 1.702
MOE_BLOCK = 512
RMS_EPS = 1e-5
LN_EPS = 1e-5
NEG = -1e30
FORCE = 1e4

LANES = 128
ROW_TILE = D_MODEL // LANES
MASK_BIAS = 1e9
LOG2E = float(np.log2(np.e))
KEY_CHUNK = 128
SEL_SLOTS = 4
GATE_ROWS = 16
TOKEN_TILE = 512
SCATTER_TILE = 2048
ROUTE_TILE = 512
VMEM_LIMIT = 56 * 1024 * 1024

F32 = jnp.float32
BF16 = jnp.bfloat16
I32 = jnp.int32

_NT = (((1,), (1,)), ((), ()))


def _dot(a, b):
    return jnp.dot(a, b, preferred_element_type=F32)


def _dot_nt(a, b):
    return lax.dot_general(a, b, _NT, preferred_element_type=F32)


def _sigmoid(x):
    return 1.0 / (1.0 + jnp.exp(-x))


def _gelu(x):
    c = np.float32(np.sqrt(2.0 / np.pi))
    return 0.5 * x * (1.0 + jnp.tanh(c * (x + 0.044715 * (x * x * x))))


def _rms(x, g):
    return x * lax.rsqrt(jnp.mean(x * x, axis=-1, keepdims=True) + RMS_EPS) * g


T_Q = 0
T_VS = T_Q + N_HEADS * LANES
T_VW = T_VS + N_KV * HEAD_DIM
T_GATE = T_VW + N_KV * HEAD_DIM
T_ROWS = T_GATE + N_KV * GATE_ROWS
K_SEL = 0
K_WIN = K_SEL + N_KV * HEAD_DIM
K_CMP = K_WIN + N_KV * HEAD_DIM
K_COLS = K_CMP + 4 * HEAD_DIM


def _proj_kernel(x_ref, gmix_ref, wt_ref, tconst_ref, wk_ref, kconst_ref, wu_ref, wv_ref, wmg_ref,
                 lng_ref, lnb_ref, wsp_ref, bsp_ref, wbb_ref,
                 qt_ref, gt_ref, vst_ref, vwt_ref, ksa_ref, kwa_ref, kcx_ref, vcx_ref, ga_ref, gbyb_ref):
    tm = x_ref.shape[0]
    hb = _rms(x_ref[...], gmix_ref[...]).astype(BF16)

    for h in range(tm // (2 * Q_BLOCK)):
        t = _dot_nt(wt_ref[...], hb[h * 2 * Q_BLOCK:(h + 1) * 2 * Q_BLOCK]) + tconst_ref[...]
        for jj in range(2):
            j = 2 * h + jj
            ls = slice(jj * Q_BLOCK, (jj + 1) * Q_BLOCK)
            for g in range(N_KV):
                for r in range(GQA_R):
                    r0 = T_Q + (g * GQA_R + r) * LANES
                    qt_ref[0, g, j, :, r * Q_BLOCK:(r + 1) * Q_BLOCK] = t[r0:r0 + LANES, ls].astype(BF16)
                vst_ref[0, g, j] = t[T_VS + g * HEAD_DIM:T_VS + (g + 1) * HEAD_DIM, ls].astype(BF16)
                vwt_ref[0, g, j] = t[T_VW + g * HEAD_DIM:T_VW + (g + 1) * HEAD_DIM, ls].astype(BF16)
                gt_ref[0, g, j] = _sigmoid(t[T_GATE + g * GATE_ROWS:T_GATE + (g + 1) * GATE_ROWS, ls])

    kk = _dot(hb, wk_ref[...])
    kconst = kconst_ref[...].astype(F32)
    feat = lax.broadcasted_iota(I32, (tm, LANES), 1) < HEAD_DIM
    for g in range(N_KV):
        ks = kk[:, K_SEL + g * HEAD_DIM:K_SEL + (g + 1) * HEAD_DIM]
        kw = kk[:, K_WIN + g * HEAD_DIM:K_WIN + (g + 1) * HEAD_DIM]
        ks = jnp.where(feat, jnp.concatenate([ks, ks], axis=1), kconst[:, :LANES])
        ksa_ref[0, g] = jnp.concatenate([ks, kconst[:, LANES:]], axis=1).astype(BF16)
        kwa_ref[0, g] = jnp.where(feat, jnp.concatenate([kw, kw], axis=1), kconst[:, :LANES]).astype(BF16)
    kcx_ref[0] = kk[:, K_CMP:K_CMP + LANES]
    vcx_ref[0] = kk[:, K_CMP + LANES:K_COLS]

    mg = _dot(hb, wmg_ref[...])
    ga_ref[...] = _sigmoid(mg[:, :D_MODEL]).astype(BF16)

    u = _gelu(_dot(hb, wu_ref[...]))
    v = _gelu(_dot(hb, wv_ref[...]))
    mu = jnp.mean(v, axis=-1, keepdims=True)
    vc = v - mu
    var = jnp.mean(vc * vc, axis=-1, keepdims=True)
    vln = (vc * lax.rsqrt(var + LN_EPS) * lng_ref[...] + lnb_ref[...]).astype(BF16)

    row = lax.broadcasted_iota(I32, (CHUNK, CHUNK), 0)
    col = lax.broadcasted_iota(I32, (CHUNK, CHUNK), 1)
    tril = row >= col
    wsp = [jnp.where(tril, wsp_ref[g], 0.0).astype(BF16) for g in range(N_GROUPS_SGU)]
    low_half = col < SGU_GROUP_DIM
    bsp = bsp_ref[...]
    chunks = []
    for c in range(tm // CHUNK):
        rs = slice(c * CHUNK, (c + 1) * CHUNK)
        parts = []
        for p in range(SGU_WIDTH // LANES):
            cs = slice(p * LANES, (p + 1) * LANES)
            vblk = vln[rs, cs]
            mixed = jnp.where(low_half, _dot(wsp[2 * p], vblk), _dot(wsp[2 * p + 1], vblk))
            parts.append(u[rs, cs] * (mixed + bsp[:, cs]))
        chunks.append(jnp.concatenate(parts, axis=1))
    yb = jnp.concatenate(chunks, axis=0).astype(BF16)
    gbyb_ref[...] = (_sigmoid(mg[:, D_MODEL:]) * _dot(yb, wbb_ref[...])).astype(BF16)


def _proj(x2, b, s, gmix, wt, tconst, wk, kconst, wu, wv, wmg, lng, lnb, wsp, bsp, wbb, tm):
    n = x2.shape[0]
    tiles = s // tm
    qtiles = tm // Q_BLOCK
    full = lambda a: pl.BlockSpec(a.shape, lambda i, j: (0,) * a.ndim)
    rows = lambda w: pl.BlockSpec((tm, w), lambda i, j: (i * tiles + j, 0))
    per_g = lambda *blk: pl.BlockSpec((1, N_KV) + blk, lambda i, j: (i, 0, j) + (0,) * (len(blk) - 1))
    ins = (x2, gmix, wt, tconst, wk, kconst, wu, wv, wmg, lng, lnb, wsp, bsp, wbb)
    in_specs = [rows(D_MODEL)] + [full(a) for a in ins[1:]]
    in_specs[5] = pl.BlockSpec((tm, 2 * LANES), lambda i, j: (j, 0))
    nt = s // Q_BLOCK
    return pl.pallas_call(
        _proj_kernel,
        grid=(b, tiles),
        in_specs=in_specs,
        out_specs=[per_g(qtiles, LANES, GQA_R * Q_BLOCK), per_g(qtiles, GATE_ROWS, Q_BLOCK),
                   per_g(qtiles, HEAD_DIM, Q_BLOCK), per_g(qtiles, HEAD_DIM, Q_BLOCK),
                   per_g(tm, 2 * LANES), per_g(tm, LANES),
                   pl.BlockSpec((1, tm, LANES), lambda i, j: (i, j, 0)),
                   pl.BlockSpec((1, tm, LANES), lambda i, j: (i, j, 0)),
                   rows(D_MODEL), rows(D_MODEL)],
        out_shape=[jax.ShapeDtypeStruct((b, N_KV, nt, LANES, GQA_R * Q_BLOCK), BF16),
                   jax.ShapeDtypeStruct((b, N_KV, nt, GATE_ROWS, Q_BLOCK), F32),
                   jax.ShapeDtypeStruct((b, N_KV, nt, HEAD_DIM, Q_BLOCK), BF16),
                   jax.ShapeDtypeStruct((b, N_KV, nt, HEAD_DIM, Q_BLOCK), BF16),
                   jax.ShapeDtypeStruct((b, N_KV, s, 2 * LANES), BF16),
                   jax.ShapeDtypeStruct((b, N_KV, s, LANES), BF16),
                   jax.ShapeDtypeStruct((b, s, LANES), F32),
                   jax.ShapeDtypeStruct((b, s, LANES), F32),
                   jax.ShapeDtypeStruct((n, D_MODEL), BF16),
                   jax.ShapeDtypeStruct((n, D_MODEL), BF16)],
        compiler_params=pltpu.CompilerParams(dimension_semantics=("parallel", "parallel"),
                                             vmem_limit_bytes=VMEM_LIMIT),
        name="proj",
    )(*ins)


def _compress_kernel(xk_ref, xv_ref, pe_ref, w1_ref, b1_ref, w2k_ref, w2vt_ref, kconst_ref, kc_ref, vct_ref):
    nc = kc_ref.shape[2]
    half = CMP_LEN // 2
    hids = []
    for which, x_ref in enumerate((xk_ref, xv_ref)):
        za = jnp.zeros((nc, N_KV * CMP_HID), F32)
        zb = jnp.zeros((nc, N_KV * CMP_HID), F32)
        for l in range(half):
            xl = x_ref[0, pl.ds(l, nc, stride=CMP_STRIDE), :]
            za = za + _dot((xl + pe_ref[which, 0, l:l + 1, :]).astype(BF16), w1_ref[which, 0, l])
            zb = zb + _dot((xl + pe_ref[which, 1, l:l + 1, :]).astype(BF16), w1_ref[which, 1, l])
        hid = _gelu(za + pltpu.roll(zb, nc - 1, 0) + b1_ref[which])
        row = lax.broadcasted_iota(I32, hid.shape, 0)
        hids.append(jnp.where(row < nc - 1, hid, 0.0).astype(BF16))
    for g in range(N_KV):
        cs = slice(g * CMP_HID, (g + 1) * CMP_HID)
        kc_ref[0, g] = (_dot(hids[0][:, cs], w2k_ref[...]) + kconst_ref[...]).astype(BF16)
        vt = _dot_nt(w2vt_ref[...], hids[1][:, cs])
        for c in range(nc // KEY_CHUNK):
            vct_ref[0, g, c] = vt[:, c * KEY_CHUNK:(c + 1) * KEY_CHUNK].astype(BF16)


def _compress(xk, xv, pes, w1, b1, w2k, w2vt, kconst):
    b, s, w = xk.shape
    nc = s // CMP_STRIDE
    full = lambda a: pl.BlockSpec(a.shape, lambda i: (0,) * a.ndim)
    seq = pl.BlockSpec((1, s, w), lambda i: (i, 0, 0))
    return pl.pallas_call(
        _compress_kernel,
        grid=(b,),
        in_specs=[seq, seq, full(pes), full(w1), full(b1), full(w2k), full(w2vt), full(kconst)],
        out_specs=[pl.BlockSpec((1, N_KV, nc, LANES), lambda i: (i, 0, 0, 0)),
                   pl.BlockSpec((1, N_KV, nc // KEY_CHUNK, HEAD_DIM, KEY_CHUNK), lambda i: (i, 0, 0, 0, 0))],
        out_shape=[jax.ShapeDtypeStruct((b, N_KV, nc, LANES), BF16),
                   jax.ShapeDtypeStruct((b, N_KV, nc // KEY_CHUNK, HEAD_DIM, KEY_CHUNK), BF16)],
        compiler_params=pltpu.CompilerParams(dimension_semantics=("parallel",),
                                             vmem_limit_bytes=VMEM_LIMIT),
        name="compress",
    )(xk, xv, pes, w1, b1, w2k, w2vt, kconst)


def _pairs():
    return [slice(pr * 2 * Q_BLOCK, (pr + 1) * 2 * Q_BLOCK) for pr in range(GQA_R // 2)]


def _colmax(blocks):
    part = None
    for x in blocks:
        y = jnp.max(x.reshape(-1, 8, x.shape[-1]), axis=0)
        part = y if part is None else jnp.maximum(part, y)
    return jnp.max(part, axis=0, keepdims=True)


def _colsum(blocks):
    part = None
    for x in blocks:
        y = jnp.sum(x.reshape(-1, 8, x.shape[-1]), axis=0)
        part = y if part is None else part + y
    return jnp.sum(part, axis=0, keepdims=True)


def _softmax_pv(scores, vts, masks, m_ref, l_ref, acc_ref):
    for pr, ls in enumerate(_pairs()):
        sb = [x if masks[u] is None else masks[u](x, ls) for u, x in enumerate(scores[pr])]
        m_old = m_ref[0:1, ls]
        m_new = jnp.maximum(m_old, _colmax(sb))
        alpha = jnp.exp2(m_old - m_new)
        ps = [jnp.exp2(x - m_new) for x in sb]
        pv = _dot(vts[0], ps[0].astype(BF16))
        for u in range(1, len(ps)):
            pv = pv + _dot(vts[u], ps[u].astype(BF16))
        l_ref[0:1, ls] = alpha * l_ref[0:1, ls] + _colsum(ps)
        acc_ref[:, ls] = alpha * acc_ref[:, ls] + pv
        m_ref[0:1, ls] = m_new


def _nsa_kernel(qt_ref, gt_ref, kc_ref, vct_ref, ksa_ref, vst_ref, kwa_ref, vwt_ref, ovt_ref, tri_ref,
                o_ref, qa_ref, s0_ref, s1_ref, m_ref, l_ref, acc_ref, act_ref, *, n_sel, top_n):
    i = pl.program_id(1)
    t0 = i * Q_BLOCK
    width = GQA_R * Q_BLOCK
    n_chunks = ksa_ref.shape[2] // KEY_CHUNK
    n_cchunks = kc_ref.shape[2] // KEY_CHUNK
    n_slots = act_ref.shape[0] // N_KV
    lane = lax.broadcasted_iota(I32, (1, width), 1)
    tq = t0 + (lane & (Q_BLOCK - 1))
    sub = lax.broadcasted_iota(I32, (KEY_CHUNK, 1), 0)
    pairs = _pairs()
    heads = range(N_KV)

    def reset(g):
        m_ref[g] = jnp.full(m_ref.shape[1:], NEG, F32)
        l_ref[g] = jnp.zeros(l_ref.shape[1:], F32)
        acc_ref[g] = jnp.zeros(acc_ref.shape[1:], F32)

    sc = [[[_dot(kc_ref[0, g, u * KEY_CHUNK:(u + 1) * KEY_CHUNK, :], qt_ref[0, g, 0, :, ls])
            for u in range(n_cchunks)] for ls in pairs] for g in heads]
    o_c, imp_t = [], []
    for g in heads:
        ocg, ps = [], []
        for pr, ls in enumerate(pairs):
            sb = []
            for u in range(n_cchunks):
                c_end = (u * KEY_CHUNK + sub) * CMP_STRIDE + (CMP_LEN - 1)
                sb.append(jnp.where(c_end <= tq[:, ls], sc[g][pr][u], NEG))
            m = _colmax(sb)
            pb = [jnp.exp2(x - m) for x in sb]
            l = _colsum(pb)
            w = (tq[:, ls] >= CMP_LEN - 1).astype(F32) / l
            oc = _dot(vct_ref[0, g, 0], pb[0].astype(BF16))
            for u in range(1, n_cchunks):
                oc = oc + _dot(vct_ref[0, g, u], pb[u].astype(BF16))
            ocg.append(oc * w)
            ps.append([p[:, :Q_BLOCK] * w[:, :Q_BLOCK] + p[:, Q_BLOCK:] * w[:, Q_BLOCK:] for p in pb])
        imp2 = jnp.zeros((LANES, 2 * Q_BLOCK), F32)
        for u in range(n_cchunks):
            psu = ps[0][u] + ps[1][u]
            hi = psu.astype(BF16)
            lo = (psu - hi.astype(F32)).astype(BF16)
            imp2 = imp2 + _dot(ovt_ref[u], jnp.concatenate([hi, lo], axis=1))
        o_c.append(jnp.concatenate(ocg, axis=1))
        imp_t.append(imp2[:, :Q_BLOCK] + imp2[:, Q_BLOCK:])

    n_win = WINDOW // KEY_CHUNK
    cls, masks = [], []
    for u in range(n_win + 1):
        c = i - n_win + u
        cls.append(jnp.maximum(c, 0))
        off = jnp.where(c < 0, NEG, 0.0)
        if u == 0:
            edge = tri_ref[0] + off
            masks.append(lambda x, ls, edge=edge: x + edge)
        elif u == n_win:
            masks.append(lambda x, ls: x + tri_ref[1])
        else:
            masks.append(lambda x, ls, off=off: x + off)
    wsc = [[[_dot(kwa_ref[0, g, pl.ds(pl.multiple_of(cl * KEY_CHUNK, KEY_CHUNK), KEY_CHUNK), :],
                  qt_ref[0, g, 0, :, ls]) for cl in cls] for ls in pairs] for g in heads]
    o_w = []
    for g in heads:
        reset(g)
        _softmax_pv(wsc[g], [vwt_ref[0, g, cl] for cl in cls], masks, m_ref.at[g], l_ref.at[g], acc_ref.at[g])
        o_w.append(acc_ref[g] / l_ref[g, 0:1, :])

    j_io = lax.broadcasted_iota(I32, (LANES, Q_BLOCK), 0)
    tl = t0 + lax.broadcasted_iota(I32, (LANES, Q_BLOCK), 1)
    cur = tl // SEL_BLOCK
    forced = (j_io == 0) | (j_io == cur) | (j_io == cur - 1)
    prio = [jnp.where(j_io < n_sel, jnp.where(forced, FORCE, jnp.where(j_io * SEL_BLOCK <= tl, imp_t[g], -1.0)), NEG)
            for g in heads]
    for _ in range(top_n):
        for g in heads:
            m = jnp.max(prio[g], axis=0, keepdims=True)
            idx = jnp.min(jnp.where(prio[g] == m, j_io, LANES), axis=0, keepdims=True)
            prio[g] = jnp.where(j_io == idx, NEG, prio[g])
    sel_t = [jnp.where((prio[g] == NEG) & (j_io < n_sel), 1.0, 0.0) for g in heads]

    n_act = []
    for u in range(act_ref.shape[0]):
        act_ref[u] = jnp.int32(-1)
    for g in heads:
        qa_ref[g, 0:LANES, :] = qt_ref[0, g, 0]
        bias_t = ((sel_t[g] - 1.0) * MASK_BIAS).astype(BF16)
        for h in range(GQA_R):
            qa_ref[g, LANES:2 * LANES, h * Q_BLOCK:(h + 1) * Q_BLOCK] = bias_t
        any_q = jnp.max(sel_t[g], axis=1, keepdims=True)
        any_pair = jnp.maximum(any_q, pltpu.roll(any_q, LANES - 1, 0))
        na = jnp.int32(0)
        for c in range(n_chunks):
            act_ref[g * n_slots + na] = jnp.int32(c)
            na = na + jnp.where((any_pair[2 * c, 0] > 0.0) & (c < i), 1, 0)
        act_ref[g * n_slots + na] = i
        n_act.append(na)

    def sel_scores(g, k, dst):
        for u in range(SEL_SLOTS):
            cl = jnp.maximum(act_ref[g * n_slots + k * SEL_SLOTS + u], 0)
            keys = ksa_ref[0, g, pl.ds(pl.multiple_of(cl * KEY_CHUNK, KEY_CHUNK), KEY_CHUNK), :]
            for ls in pairs:
                dst[g, u, :, ls] = _dot(keys, qa_ref[g, :, ls])

    def sel_step(g, k, cur_s, nxt_s, last, ahead):
        if ahead:
            sel_scores(g, k + 1, nxt_s)
        vts, msk = [], []
        for u in range(SEL_SLOTS):
            c = act_ref[g * n_slots + k * SEL_SLOTS + u]
            vts.append(vst_ref[0, g, jnp.maximum(c, 0)])
            if last:
                kpos = jnp.where(c < 0, n_chunks * KEY_CHUNK, c * KEY_CHUNK) + sub
                msk.append(lambda x, ls, kpos=kpos: jnp.where(kpos <= tq[:, ls], x, -MASK_BIAS))
            else:
                msk.append(None)
        scores = [[cur_s[g, u, :, ls] for u in range(SEL_SLOTS)] for ls in pairs]
        _softmax_pv(scores, vts, msk, m_ref.at[g], l_ref.at[g], acc_ref.at[g])

    for g in heads:
        reset(g)
        sel_scores(g, 0, s0_ref)
    n_steps = [(n_act[g] + SEL_SLOTS) // SEL_SLOTS for g in heads]
    bufs = ((s0_ref, s1_ref), (s1_ref, s0_ref))

    def sel_body(k, carry):
        last = [k >= n_steps[g] - 1 for g in heads]
        for odd, (cur_s, nxt_s) in enumerate(bufs):
            for l0 in (False, True):
                for l1 in (False, True):
                    @pl.when((k % 2 == odd) & (last[0] == l0) & (last[1] == l1))
                    def _():
                        ahead = not (l0 and l1)
                        sel_step(0, k, cur_s, nxt_s, l0, ahead)
                        sel_step(1, k, cur_s, nxt_s, l1, ahead)

        return carry

    lax.fori_loop(0, jnp.maximum(n_steps[0], n_steps[1]), sel_body, 0)
    o_s = [acc_ref[g] / l_ref[g, 0:1, :] for g in heads]

    for g in heads:
        gt = gt_ref[0, g, 0]

        def gate(br):
            return jnp.concatenate([gt[br * GQA_R + h:br * GQA_R + h + 1, :] for h in range(GQA_R)], axis=1)

        o = gate(0) * o_c[g] + gate(1) * o_s[g] + gate(2) * o_w[g]
        o_ref[0, :, g * GQA_R * HEAD_DIM:(g + 1) * GQA_R * HEAD_DIM] = jnp.concatenate(
            [o[:, h * Q_BLOCK:(h + 1) * Q_BLOCK].T for h in range(GQA_R)], axis=1).astype(BF16)


def _nsa(qt, gt, kc, vct, ksa, vst, kwa, vwt, ovt, tri, n_sel):
    b, g, nt, _, width = qt.shape
    s = ksa.shape[2]
    tile = lambda a: pl.BlockSpec((1, g, 1) + a.shape[3:], lambda i, k: (i, 0, k, 0, 0))
    whole = lambda a: pl.BlockSpec((1,) + a.shape[1:], lambda i, k: (i,) + (0,) * (a.ndim - 1))
    const = lambda a: pl.BlockSpec(a.shape, lambda i, k: (0,) * a.ndim)
    kern = functools.partial(_nsa_kernel, n_sel=n_sel, top_n=min(SEL_TOPN, n_sel))
    return pl.pallas_call(
        kern,
        grid=(b, nt),
        in_specs=[tile(qt), tile(gt), whole(kc), whole(vct), whole(ksa), whole(vst), whole(kwa), whole(vwt),
                  const(ovt), const(tri)],
        out_specs=pl.BlockSpec((1, Q_BLOCK, ATTN_WIDTH), lambda i, k: (i, k, 0)),
        out_shape=jax.ShapeDtypeStruct((b, s, ATTN_WIDTH), BF16),
        scratch_shapes=[pltpu.VMEM((g, 2 * LANES, width), BF16),
                        pltpu.VMEM((g, SEL_SLOTS, KEY_CHUNK, width), F32),
                        pltpu.VMEM((g, SEL_SLOTS, KEY_CHUNK, width), F32),
                        pltpu.VMEM((g, 8, width), F32),
                        pltpu.VMEM((g, 8, width), F32),
                        pltpu.VMEM((g, HEAD_DIM, width), F32),
                        pltpu.SMEM((g * (s // KEY_CHUNK + 2 * SEL_SLOTS),), I32)],
        compiler_params=pltpu.CompilerParams(dimension_semantics=("parallel", "arbitrary"),
                                             vmem_limit_bytes=VMEM_LIMIT),
        name="nsa",
    )(qt, gt, kc, vct, ksa, vst, kwa, vwt, ovt, tri)


def _merge_kernel(x_ref, ya_ref, ga_ref, gbyb_ref, wba_ref, wout_ref, gmoe_ref, wr_ref, br_ref, utri_ref,
                  x1_ref, hm_ref, rw_ref, ri_ref, cnt_ref, carry_ref):
    tm = x_ref.shape[0]

    @pl.when(pl.program_id(0) == 0)
    def _():
        carry_ref[...] = jnp.zeros_like(carry_ref)

    merged = ga_ref[...].astype(F32) * _dot(ya_ref[...], wba_ref[...]) + gbyb_ref[...].astype(F32)
    x1 = x_ref[...] + _dot(merged.astype(BF16), wout_ref[...])
    x1_ref[...] = x1
    hm = _rms(x1, gmoe_ref[...])
    for a in range(ROW_TILE):
        hm_ref[pl.ds(a, tm, stride=ROW_TILE), :] = hm[:, a * LANES:(a + 1) * LANES]

    hh = hm.astype(BF16)
    hl = (hm - hh.astype(F32)).astype(BF16)
    both = _dot(hh, wr_ref[...])
    logits = both[:, :LANES] + both[:, LANES:] + _dot(hl, wr_ref[:, :LANES]) + br_ref[...]
    lg = logits.T[0:N_EXPERTS, :]
    e_io = lax.broadcasted_iota(I32, (N_EXPERTS, tm), 0)
    vals, idxs = [], []
    for _ in range(TOP_K):
        m = jnp.max(lg, axis=0, keepdims=True)
        idx = jnp.min(jnp.where(lg == m, e_io, N_EXPERTS), axis=0, keepdims=True)
        vals.append(m)
        idxs.append(idx)
        lg = jnp.where(e_io == idx, NEG, lg)
    ex = [jnp.exp(v - vals[0]) for v in vals]
    den = ex[0] + ex[1] + ex[2] + ex[3]

    hits = [e_io == idx for idx in idxs]
    multi = jnp.zeros((N_EXPERTS, tm), F32)
    for h in hits:
        multi = jnp.where(h, 1.0, multi)
    carry = carry_ref[:, 0:1]
    cum = _dot(multi.astype(BF16), utri_ref[...]) + carry
    ranks = [jnp.sum(jnp.where(h, cum, 0.0), axis=0, keepdims=True).astype(I32) for h in hits]
    ri_ref[...] = jnp.concatenate(idxs + ranks, axis=0)
    wts = jnp.concatenate([e / den for e in ex] + [jnp.zeros((LANES - TOP_K, tm), F32)], axis=0)
    rw_ref[...] = wts.T
    new_carry = carry + jnp.sum(multi, axis=1, keepdims=True)
    carry_ref[...] = jnp.broadcast_to(new_carry, carry_ref.shape)
    cnt_ref[...] = jnp.broadcast_to(new_carry, cnt_ref.shape)


def _merge(x2, ya, ga, gbyb, wba, wout, gmoe, wr, br, tm):
    n = x2.shape[0]
    full = lambda a: pl.BlockSpec(a.shape, lambda i: (0,) * a.ndim)
    rows = lambda w: pl.BlockSpec((tm, w), lambda i: (i, 0))
    utri = jnp.asarray(np.triu(np.ones((tm, tm), np.float32), 1), BF16)
    return pl.pallas_call(
        _merge_kernel,
        grid=(n // tm,),
        in_specs=[rows(D_MODEL), rows(ATTN_WIDTH), rows(D_MODEL), rows(D_MODEL),
                  full(wba), full(wout), full(gmoe), full(wr), full(br), full(utri)],
        out_specs=[rows(D_MODEL), pl.BlockSpec((tm * ROW_TILE, LANES), lambda i: (i, 0)), rows(LANES),
                   pl.BlockSpec((2 * TOP_K, tm), lambda i: (0, i)),
                   pl.BlockSpec((N_EXPERTS, LANES), lambda i: (0, 0))],
        out_shape=[jax.ShapeDtypeStruct((n, D_MODEL), F32),
                   jax.ShapeDtypeStruct((n * ROW_TILE, LANES), F32),
                   jax.ShapeDtypeStruct((n, LANES), F32),
                   jax.ShapeDtypeStruct((2 * TOP_K, n), I32),
                   jax.ShapeDtypeStruct((N_EXPERTS, LANES), F32)],
        scratch_shapes=[pltpu.VMEM((N_EXPERTS, LANES), F32)],
        compiler_params=pltpu.CompilerParams(dimension_semantics=("arbitrary",),
                                             vmem_limit_bytes=VMEM_LIMIT),
        name="merge",
    )(x2, ya, ga, gbyb, wba, wout, gmoe, wr, br, utri)


def _tile_copy(src, i, dst, d, sem):
    return pltpu.make_async_copy(src.at[pl.ds(pl.multiple_of(i * ROW_TILE, ROW_TILE), ROW_TILE)],
                                 dst.at[pl.ds(pl.multiple_of(d * ROW_TILE, ROW_TILE), ROW_TILE)], sem)


def _dispatch_kernel(seg_ref, dest_ref, hm_ref, xs_ref, zero_ref, sem, zsem, *, n_pad):
    tm = hm_ref.shape[0] // ROW_TILE

    @pl.when(pl.program_id(0) == 0)
    def _():
        zero_ref[...] = jnp.zeros_like(zero_ref)

        def seg(e, c):
            def fill(r, c2):
                _tile_copy(zero_ref, 0, xs_ref, r, zsem).start()
                return c2
            return lax.fori_loop(seg_ref[0, e], seg_ref[1, e], fill, c)

        lax.fori_loop(0, N_EXPERTS + 1, seg, 0)
        pad_rows = xs_ref.at[pl.ds(0, n_pad * ROW_TILE)]
        pltpu.make_async_copy(pad_rows, pad_rows, zsem).wait()

    def issue(r, c):
        slots = [dest_ref[r * TOP_K + k] for k in range(TOP_K)]
        for k in range(TOP_K):
            _tile_copy(hm_ref, r, xs_ref, slots[k], sem).start(priority=k % 2)
        return c

    lax.fori_loop(0, tm, issue, 0, unroll=4)
    for k in range(TOP_K):
        pltpu.make_async_copy(hm_ref, xs_ref.at[pl.ds(0, tm * ROW_TILE)], sem).wait()


def _dispatch(seg, dest_flat, hm, n_slots, tm):
    n = hm.shape[0] // ROW_TILE
    kern = functools.partial(_dispatch_kernel, n_pad=n_slots - n * TOP_K)
    return pl.pallas_call(
        kern,
        grid_spec=pltpu.PrefetchScalarGridSpec(
            num_scalar_prefetch=1,
            grid=(n // tm,),
            in_specs=[pl.BlockSpec((tm * TOP_K,), lambda i, sg: (i,), memory_space=pltpu.SMEM),
                      pl.BlockSpec((tm * ROW_TILE, LANES), lambda i, sg: (i, 0))],
            out_specs=pl.BlockSpec(memory_space=pl.ANY),
            scratch_shapes=[pltpu.VMEM((ROW_TILE, LANES), F32),
                            pltpu.SemaphoreType.DMA(()), pltpu.SemaphoreType.DMA(())]),
        out_shape=jax.ShapeDtypeStruct((n_slots * ROW_TILE, LANES), F32),
        compiler_params=pltpu.CompilerParams(dimension_semantics=("arbitrary",),
                                             has_side_effects=True),
        name="dispatch",
    )(seg, dest_flat, hm)


def _expert_kernel(be_ref, nu_ref, xs_ref, wgu_ref, bgu_ref, wd_ref, bd_ref, y_ref, wgu_bf, wd_bf):
    i = pl.program_id(0)

    @pl.when(i >= nu_ref[0])
    def _():
        y_ref[...] = jnp.zeros_like(y_ref)

    @pl.when((i == 0) | (be_ref[i] != be_ref[jnp.maximum(i - 1, 0)]))
    def _():
        wgu_bf[...] = wgu_ref[0].astype(BF16)
        wd_bf[...] = wd_ref[0].astype(BF16)

    @pl.when(i < nu_ref[0])
    def _():
        x = jnp.concatenate([xs_ref[pl.ds(a, MOE_BLOCK, stride=ROW_TILE), :] for a in range(ROW_TILE)], axis=1)
        gu = _dot(x.astype(BF16), wgu_bf[...]) + bgu_ref[0]
        gate = jnp.minimum(gu[:, :D_FF], SWIGLU_LIMIT)
        up = jnp.clip(gu[:, D_FF:], -SWIGLU_LIMIT, SWIGLU_LIMIT)
        act = gate * _sigmoid(SWIGLU_ALPHA * gate) * (up + 1.0)
        y = _dot(act.astype(BF16), wd_bf[...]) + bd_ref[0]
        for a in range(ROW_TILE):
            y_ref[pl.ds(a, MOE_BLOCK, stride=ROW_TILE), :] = y[:, a * LANES:(a + 1) * LANES]


def _experts(blk_expert, n_used, xs, wgu, bgu, wd, bd):
    n_blocks = xs.shape[0] // (MOE_BLOCK * ROW_TILE)
    blk = lambda i, be, nu: (jnp.minimum(i, nu[0] - 1), 0)
    exp3 = lambda i, be, nu: (be[jnp.minimum(i, nu[0] - 1)], 0, 0)
    return pl.pallas_call(
        _expert_kernel,
        grid_spec=pltpu.PrefetchScalarGridSpec(
            num_scalar_prefetch=2,
            grid=(n_blocks,),
            in_specs=[pl.BlockSpec((MOE_BLOCK * ROW_TILE, LANES), blk),
                      pl.BlockSpec((1, D_MODEL, 2 * D_FF), exp3),
                      pl.BlockSpec((1, 1, 2 * D_FF), exp3),
                      pl.BlockSpec((1, D_FF, D_MODEL), exp3),
                      pl.BlockSpec((1, 1, D_MODEL), exp3)],
            out_specs=pl.BlockSpec((MOE_BLOCK * ROW_TILE, LANES), lambda i, be, nu: (i, 0)),
            scratch_shapes=[pltpu.VMEM((D_MODEL, 2 * D_FF), BF16), pltpu.VMEM((D_FF, D_MODEL), BF16)]),
        out_shape=jax.ShapeDtypeStruct(xs.shape, F32),
        compiler_params=pltpu.CompilerParams(dimension_semantics=("arbitrary",),
                                             vmem_limit_bytes=VMEM_LIMIT),
        name="experts",
    )(blk_expert, n_used, xs, wgu, bgu, wd, bd)


def _combine_kernel(dest_ref, dest_next_ref, rw_ref, x1_ref, gfin_ref, y_ref, o_ref, ybuf0, ybuf1, sem0, sem1):
    tm = x1_ref.shape[0]
    i = pl.program_id(0)

    def gather(dst_ref, ybuf, sem):
        def issue(r, c):
            slots = [dst_ref[r * TOP_K + k] for k in range(TOP_K)]
            for k in range(TOP_K):
                _tile_copy(y_ref, slots[k], ybuf.at[k], r, sem).start(priority=k % 2)
            return c

        lax.fori_loop(0, tm, issue, 0, unroll=4)

    def finish(ybuf, sem):
        for k in range(TOP_K):
            pltpu.make_async_copy(y_ref.at[pl.ds(0, tm * ROW_TILE)], ybuf.at[k], sem).wait()
        rw = rw_ref[...]
        x1 = x1_ref[...]
        cols = []
        for a in range(ROW_TILE):
            acc = x1[:, a * LANES:(a + 1) * LANES]
            for k in range(TOP_K):
                acc = acc + rw[:, k:k + 1] * ybuf[k, pl.ds(a, tm, stride=ROW_TILE), :]
            cols.append(acc)
        o_ref[...] = _rms(jnp.concatenate(cols, axis=1), gfin_ref[...])

    @pl.when(i == 0)
    def _():
        gather(dest_ref, ybuf0, sem0)

    for parity, (cur, nxt) in enumerate((((ybuf0, sem0), (ybuf1, sem1)), ((ybuf1, sem1), (ybuf0, sem0)))):
        @pl.when(i % 2 == parity)
        def _():
            @pl.when(i + 1 < pl.num_programs(0))
            def _():
                gather(dest_next_ref, *nxt)

            finish(*cur)


def _combine(dest_flat, rw, x1, gfin, y, tm):
    n = x1.shape[0]
    steps = n // tm
    ybuf = pltpu.VMEM((TOP_K, tm * ROW_TILE, LANES), F32)
    return pl.pallas_call(
        _combine_kernel,
        grid=(steps,),
        in_specs=[pl.BlockSpec((tm * TOP_K,), lambda i: (i,), memory_space=pltpu.SMEM),
                  pl.BlockSpec((tm * TOP_K,), lambda i: (jnp.minimum(i + 1, steps - 1),), memory_space=pltpu.SMEM),
                  pl.BlockSpec((tm, LANES), lambda i: (i, 0)),
                  pl.BlockSpec((tm, D_MODEL), lambda i: (i, 0)),
                  pl.BlockSpec((1, D_MODEL), lambda i: (0, 0)),
                  pl.BlockSpec(memory_space=pl.ANY)],
        out_specs=pl.BlockSpec((tm, D_MODEL), lambda i: (i, 0)),
        out_shape=jax.ShapeDtypeStruct((n, D_MODEL), F32),
        scratch_shapes=[ybuf, ybuf, pltpu.SemaphoreType.DMA(()), pltpu.SemaphoreType.DMA(())],
        compiler_params=pltpu.CompilerParams(dimension_semantics=("arbitrary",),
                                             vmem_limit_bytes=VMEM_LIMIT),
        name="combine",
    )(dest_flat, dest_flat, rw, x1, gfin, y)


def _overlap_t(nc, n_cmp, n_sel):
    cs = np.arange(n_cmp)[None, :] * CMP_STRIDE
    ss = np.arange(n_sel)[:, None] * SEL_BLOCK
    ov = np.clip(np.minimum(cs + CMP_LEN, ss + SEL_BLOCK) - np.maximum(cs, ss), 0, None) / CMP_LEN
    out = np.zeros((LANES, nc), np.float32)
    out[:n_sel, :n_cmp] = ov
    return jnp.asarray(out, BF16)


def _layer(x2, b, s, g_mix, w_in, w_ck1, b_ck1, w_ck2, pe_k, w_cv1, b_cv1, w_cv2, pe_v,
           sgu_ln_g, sgu_ln_b, w_spatial, b_spatial, w_branch_a, w_branch_b, w_out,
           g_moe, w_router, b_router, w_gate_up, b_gate_up, w_down, b_down):
    n = b * s
    nc = s // CMP_STRIDE
    n_cmp = (s - CMP_LEN) // CMP_STRIDE + 1
    n_sel = s // SEL_BLOCK
    assert nc % KEY_CHUNK == 0 and n_sel <= LANES and n_cmp == nc - 1
    tm = TOKEN_TILE
    assert s % tm == 0 and n % ROUTE_TILE == 0 and n % SCATTER_TILE == 0

    p0 = ATTN_WIDTH
    p1 = p0 + 6 * KV_WIDTH
    p2 = p1 + N_NSA_BRANCH * N_HEADS
    p3 = p2 + SGU_WIDTH
    p4 = p3 + SGU_WIDTH
    zpad = lambda a, w: jnp.pad(a, ((0, 0),) * (a.ndim - 1) + ((0, w - a.shape[-1]),))
    wq = zpad((w_in[:, :p0] * (HEAD_DIM ** -0.5 * LOG2E)).reshape(D_MODEL, N_HEADS, HEAD_DIM), LANES)
    wkv = w_in[:, p0:p1].reshape(D_MODEL, 6, N_KV, HEAD_DIM)
    wng = w_in[:, p1:p2].reshape(D_MODEL, N_KV, GQA_R, N_NSA_BRANCH).transpose(0, 1, 3, 2)
    wng = zpad(wng.reshape(D_MODEL, N_KV, N_NSA_BRANCH * GQA_R), GATE_ROWS)
    wt = jnp.concatenate([wq.reshape(D_MODEL, -1), wkv[:, 3].reshape(D_MODEL, -1),
                          wkv[:, 5].reshape(D_MODEL, -1), wng.reshape(D_MODEL, -1)], axis=1).T.astype(BF16)
    slopes = 2.0 ** (-8.0 * np.arange(1, N_HEADS + 1) / N_HEADS)
    tcol = np.zeros((T_ROWS, 1), np.float32)
    head_rows = T_Q + np.arange(N_HEADS) * LANES + HEAD_DIM
    bf16_round = lambda a: a.astype(BF16).astype(np.float32)
    for k, coef in enumerate((slopes * SEL_BLOCK * LOG2E, slopes * LOG2E)):
        hi = bf16_round(coef.astype(np.float32))
        tcol[head_rows + 2 * k, 0] = hi
        tcol[head_rows + 2 * k + 1, 0] = bf16_round(coef.astype(np.float32) - hi)
    tconst = jnp.asarray(np.broadcast_to(tcol, (T_ROWS, 2 * Q_BLOCK)))
    wk = jnp.concatenate([wkv[:, c].reshape(D_MODEL, -1) for c in (2, 4, 0, 1)], axis=1).astype(BF16)
    pos = np.arange(s)
    kc_np = np.zeros((s, 2 * LANES), np.float32)
    kc_np[:, HEAD_DIM:HEAD_DIM + 2] = (pos // SEL_BLOCK)[:, None]
    kc_np[:, HEAD_DIM + 2:HEAD_DIM + 4] = (pos % SEL_BLOCK)[:, None]
    kc_np[pos, LANES + pos // SEL_BLOCK] = 1.0
    kconst = jnp.asarray(kc_np, BF16)
    wu = w_in[:, p2:p3].astype(BF16)
    wv = w_in[:, p3:p4].astype(BF16)
    wmg = w_in[:, p4:].astype(BF16)
    bsp = jnp.repeat(b_spatial.T, SGU_GROUP_DIM, axis=1)

    qt, gt, vst, vwt, ksa, kwa, kcx, vcx, ga, gbyb = _proj(
        x2, b, s, g_mix[None], wt, tconst, wk, kconst, wu, wv, wmg, sgu_ln_g[None], sgu_ln_b[None],
        w_spatial, bsp, w_branch_b.astype(BF16), tm)

    half = CMP_LEN // 2
    eye = jnp.eye(N_KV, dtype=F32)[None, None, :, None, :, None]
    bdiag = lambda w: (w.reshape(2, half, 1, HEAD_DIM, 1, CMP_HID) * eye).reshape(
        2, half, N_KV * HEAD_DIM, N_KV * CMP_HID)
    w1 = jnp.stack([bdiag(w_ck1), bdiag(w_cv1)]).astype(BF16)
    pes = jnp.stack([jnp.tile(pe_k.reshape(2, half, HEAD_DIM), (1, 1, N_KV)),
                     jnp.tile(pe_v.reshape(2, half, HEAD_DIM), (1, 1, N_KV))])
    b1 = jnp.stack([jnp.tile(b_ck1, N_KV), jnp.tile(b_cv1, N_KV)])[:, None, :]
    blk_n = np.arange(nc)
    cc_np = np.zeros((nc, LANES), np.float32)
    cc_np[:, HEAD_DIM:HEAD_DIM + 2] = (blk_n // (SEL_BLOCK // CMP_STRIDE))[:, None]
    cc_np[:, HEAD_DIM + 2:HEAD_DIM + 4] = (blk_n % (SEL_BLOCK // CMP_STRIDE) * CMP_STRIDE)[:, None]
    kc, vct = _compress(kcx, vcx, pes, w1, b1, zpad(w_ck2, LANES).astype(BF16), w_cv2.T.astype(BF16),
                        jnp.asarray(cc_np))

    ovt = _overlap_t(nc, n_cmp, n_sel).reshape(LANES, nc // KEY_CHUNK, KEY_CHUNK).transpose(1, 0, 2)
    a_io, q_io = np.meshgrid(np.arange(KEY_CHUNK), np.arange(Q_BLOCK), indexing="ij")
    tri = np.stack([np.where(a_io > q_io, 0.0, NEG), np.where(a_io <= q_io, 0.0, NEG)]).astype(np.float32)
    tri = jnp.asarray(np.tile(tri, (1, 1, 2)))
    ya = _nsa(qt, gt, kc, vct, ksa, vst, kwa, vwt, ovt, tri, n_sel).reshape(n, ATTN_WIDTH)

    wr = jnp.pad(w_router, ((0, 0), (0, LANES - N_EXPERTS)))
    wrh = wr.astype(BF16)
    wr2 = jnp.concatenate([wrh, (wr - wrh.astype(F32)).astype(BF16)], axis=1)
    br = jnp.pad(b_router, (0, LANES - N_EXPERTS))[None]
    x1, hm, rw, ri, cnt = _merge(x2, ya, ga, gbyb, w_branch_a.astype(BF16), w_out.astype(BF16),
                                 g_moe[None], wr2, br, tm)

    counts = cnt[:, 0].astype(I32)
    padded = (counts + MOE_BLOCK - 1) // MOE_BLOCK * MOE_BLOCK
    pad_end = jnp.cumsum(padded)
    pad_start = pad_end - padded
    base = sum(jnp.where(ri[:TOP_K] == e, pad_start[e], 0) for e in range(N_EXPERTS))
    dest = (base + ri[TOP_K:]).T.reshape(-1)
    n_blocks = -(-(n * TOP_K) // MOE_BLOCK) + N_EXPERTS
    blk_start = jnp.arange(n_blocks, dtype=I32) * MOE_BLOCK
    blk_expert = jnp.minimum(jnp.sum((pad_end[None, :] <= blk_start[:, None]).astype(I32), axis=1),
                             N_EXPERTS - 1)
    n_used = (pad_end[-1:] // MOE_BLOCK).astype(I32)
    n_slots = n_blocks * MOE_BLOCK
    seg = jnp.stack([jnp.concatenate([pad_start + counts, pad_end[-1:]]),
                     jnp.concatenate([pad_end, jnp.full((1,), n_slots, I32)])]).astype(I32)

    xs = _dispatch(seg, dest, hm, n_slots, SCATTER_TILE)
    y = _experts(blk_expert, n_used, xs, w_gate_up, b_gate_up[:, None, :], w_down, b_down[:, None, :])
    return dest, rw, x1, y


def kernel(x, g_mix, w_in, w_ck1, b_ck1, w_ck2, pe_k, w_cv1, b_cv1, w_cv2, pe_v, sgu_ln_g, sgu_ln_b, w_spatial, b_spatial, w_branch_a, w_branch_b, w_out, g_moe, w_router, b_router, w_gate_up, b_gate_up, w_down, b_down, g_final):
    b, s, d = x.shape
    assert g_mix.shape[0] == 1, "the final rmsnorm is fused into the single layer's combine step"
    l = 0
    dest, rw, x1, y = _layer(
        x.reshape(b * s, d), b, s, g_mix[l], w_in[l], w_ck1[l], b_ck1[l], w_ck2[l], pe_k[l], w_cv1[l],
        b_cv1[l], w_cv2[l], pe_v[l], sgu_ln_g[l], sgu_ln_b[l], w_spatial[l], b_spatial[l], w_branch_a[l],
        w_branch_b[l], w_out[l], g_moe[l], w_router[l], b_router[l], w_gate_up[l], b_gate_up[l],
        w_down[l], b_down[l])
    return _combine(dest, rw, x1, g_final[None], y, ROUTE_TILE).reshape(b, s, d)
```

```python
import functools

import numpy as np
import jax
import jax.numpy as jnp
from jax import lax
from jax.experimental import pallas as pl
from jax.experimental.pallas import tpu as pltpu

D_MODEL = 1024
N_HEADS = 8
HEAD_DIM = 64
N_KV = 2
GQA_R = N_HEADS // N_KV
ATTN_WIDTH = N_HEADS * HEAD_DIM
KV_WIDTH = N_KV * HEAD_DIM
CMP_LEN = 32
CMP_STRIDE = 16
CMP_HID = 128
SEL_BLOCK = 64
SEL_TOPN = 16
WINDOW = 512
Q_BLOCK = 128
N_NSA_BRANCH = 3
SGU_WIDTH = 512
N_GROUPS_SGU = 8
SGU_GROUP_DIM = SGU_WIDTH // N_GROUPS_SGU
CHUNK = 128
N_EXPERTS = 32
TOP_K = 4
D_FF = D_MODEL
SWIGLU_LIMIT = 7.0
SWIGLU_ALPHA = 1.702
MOE_BLOCK = 512
RMS_EPS = 1e-5
LN_EPS = 1e-5
NEG = -1e30
FORCE = 1e4

LANES = 128
ROW_TILE = D_MODEL // LANES
MASK_BIAS = 1e9
LOG2E = float(np.log2(np.e))
KEY_CHUNK = 128
SEL_SLOTS = 4
GATE_ROWS = 16
TOKEN_TILE = 512
SCATTER_TILE = 2048
ROUTE_TILE = 512
VMEM_LIMIT = 56 * 1024 * 1024

F32 = jnp.float32
BF16 = jnp.bfloat16
I32 = jnp.int32

_NT = (((1,), (1,)), ((), ()))


def _dot(a, b):
    return jnp.dot(a, b, preferred_element_type=F32)


def _dot_nt(a, b):
    return lax.dot_general(a, b, _NT, preferred_element_type=F32)


def _sigmoid(x):
    return 1.0 / (1.0 + jnp.exp(-x))


def _gelu(x):
    c = np.float32(np.sqrt(2.0 / np.pi))
    return 0.5 * x * (1.0 + jnp.tanh(c * (x + 0.044715 * (x * x * x))))


def _rms(x, g):
    return x * lax.rsqrt(jnp.mean(x * x, axis=-1, keepdims=True) + RMS_EPS) * g


T_Q = 0
T_VS = T_Q + N_HEADS * HEAD_DIM
T_VW = T_VS + N_KV * HEAD_DIM
T_GATE = T_VW + N_KV * HEAD_DIM
T_ROWS = T_GATE + N_KV * GATE_ROWS
K_SEL = 0
K_WIN = K_SEL + N_KV * HEAD_DIM
K_CMP = K_WIN + N_KV * HEAD_DIM
K_COLS = K_CMP + 4 * HEAD_DIM


def _proj_kernel(x_ref, gmix_ref, wt_ref, qconst_ref, wk_ref, kconst_ref, wu_ref, wv_ref, wmg_ref,
                 lng_ref, lnb_ref, wsp_ref, bsp_ref, wbb_ref,
                 qt_ref, gt_ref, vst_ref, vwt_ref, ksa_ref, kwa_ref, kcx_ref, vcx_ref, ga_ref, gbyb_ref):
    tm = x_ref.shape[0]
    hb = _rms(x_ref[...], gmix_ref[...]).astype(BF16)

    for h in range(tm // (2 * Q_BLOCK)):
        t = _dot_nt(wt_ref[...], hb[h * 2 * Q_BLOCK:(h + 1) * 2 * Q_BLOCK])
        for jj in range(2):
            j = 2 * h + jj
            ls = slice(jj * Q_BLOCK, (jj + 1) * Q_BLOCK)
            for g in range(N_KV):
                for r in range(GQA_R):
                    hd = g * GQA_R + r
                    qs = slice(r * Q_BLOCK, (r + 1) * Q_BLOCK)
                    qt_ref[0, g, j, 0:HEAD_DIM, qs] = t[T_Q + hd * HEAD_DIM:T_Q + (hd + 1) * HEAD_DIM, ls].astype(BF16)
                    qt_ref[0, g, j, HEAD_DIM:LANES, qs] = qconst_ref[hd]
                vst_ref[0, g, j] = t[T_VS + g * HEAD_DIM:T_VS + (g + 1) * HEAD_DIM, ls].astype(BF16)
                vwt_ref[0, g, j] = t[T_VW + g * HEAD_DIM:T_VW + (g + 1) * HEAD_DIM, ls].astype(BF16)
                gt_ref[0, g, j] = _sigmoid(t[T_GATE + g * GATE_ROWS:T_GATE + (g + 1) * GATE_ROWS, ls])

    kk = _dot(hb, wk_ref[...])
    kconst = kconst_ref[...].astype(F32)
    feat = lax.broadcasted_iota(I32, (tm, LANES), 1) < HEAD_DIM
    for g in range(N_KV):
        ks = kk[:, K_SEL + g * HEAD_DIM:K_SEL + (g + 1) * HEAD_DIM]
        kw = kk[:, K_WIN + g * HEAD_DIM:K_WIN + (g + 1) * HEAD_DIM]
        ks = jnp.where(feat, jnp.concatenate([ks, ks], axis=1), kconst[:, :LANES])
        ksa_ref[0, g] = jnp.concatenate([ks, kconst[:, LANES:]], axis=1).astype(BF16)
        kwa_ref[0, g] = jnp.where(feat, jnp.concatenate([kw, kw], axis=1), kconst[:, :LANES]).astype(BF16)
    kcx_ref[0] = kk[:, K_CMP:K_CMP + LANES]
    vcx_ref[0] = kk[:, K_CMP + LANES:K_COLS]

    mg = _dot(hb, wmg_ref[...])
    ga_ref[...] = _sigmoid(mg[:, :D_MODEL]).astype(BF16)

    u = _gelu(_dot(hb, wu_ref[...]))
    v = _gelu(_dot(hb, wv_ref[...]))
    mu = jnp.mean(v, axis=-1, keepdims=True)
    vc = v - mu
    var = jnp.mean(vc * vc, axis=-1, keepdims=True)
    vln = (vc * lax.rsqrt(var + LN_EPS) * lng_ref[...] + lnb_ref[...]).astype(BF16)

    row = lax.broadcasted_iota(I32, (CHUNK, CHUNK), 0)
    col = lax.broadcasted_iota(I32, (CHUNK, CHUNK), 1)
    tril = row >= col
    wsp = [jnp.where(tril, wsp_ref[g], 0.0).astype(BF16) for g in range(N_GROUPS_SGU)]
    low_half = col < SGU_GROUP_DIM
    bsp = bsp_ref[...]
    chunks = []
    for c in range(tm // CHUNK):
        rs = slice(c * CHUNK, (c + 1) * CHUNK)
        parts = []
        for p in range(SGU_WIDTH // LANES):
            cs = slice(p * LANES, (p + 1) * LANES)
            vblk = vln[rs, cs]
            mixed = jnp.where(low_half, _dot(wsp[2 * p], vblk), _dot(wsp[2 * p + 1], vblk))
            parts.append(u[rs, cs] * (mixed + bsp[:, cs]))
        chunks.append(jnp.concatenate(parts, axis=1))
    yb = jnp.concatenate(chunks, axis=0).astype(BF16)
    gbyb_ref[...] = (_sigmoid(mg[:, D_MODEL:]) * _dot(yb, wbb_ref[...])).astype(BF16)


def _proj(x2, b, s, gmix, wt, qconst, wk, kconst, wu, wv, wmg, lng, lnb, wsp, bsp, wbb, tm):
    n = x2.shape[0]
    tiles = s // tm
    qtiles = tm // Q_BLOCK
    full = lambda a: pl.BlockSpec(a.shape, lambda i, j: (0,) * a.ndim)
    rows = lambda w: pl.BlockSpec((tm, w), lambda i, j: (i * tiles + j, 0))
    per_g = lambda *blk: pl.BlockSpec((1, N_KV) + blk, lambda i, j: (i, 0, j) + (0,) * (len(blk) - 1))
    ins = (x2, gmix, wt, qconst, wk, kconst, wu, wv, wmg, lng, lnb, wsp, bsp, wbb)
    in_specs = [rows(D_MODEL)] + [full(a) for a in ins[1:]]
    in_specs[5] = pl.BlockSpec((tm, 2 * LANES), lambda i, j: (j, 0))
    nt = s // Q_BLOCK
    return pl.pallas_call(
        _proj_kernel,
        grid=(b, tiles),
        in_specs=in_specs,
        out_specs=[per_g(qtiles, LANES, GQA_R * Q_BLOCK), per_g(qtiles, GATE_ROWS, Q_BLOCK),
                   per_g(qtiles, HEAD_DIM, Q_BLOCK), per_g(qtiles, HEAD_DIM, Q_BLOCK),
                   per_g(tm, 2 * LANES), per_g(tm, LANES),
                   pl.BlockSpec((1, tm, LANES), lambda i, j: (i, j, 0)),
                   pl.BlockSpec((1, tm, LANES), lambda i, j: (i, j, 0)),
                   rows(D_MODEL), rows(D_MODEL)],
        out_shape=[jax.ShapeDtypeStruct((b, N_KV, nt, LANES, GQA_R * Q_BLOCK), BF16),
                   jax.ShapeDtypeStruct((b, N_KV, nt, GATE_ROWS, Q_BLOCK), F32),
                   jax.ShapeDtypeStruct((b, N_KV, nt, HEAD_DIM, Q_BLOCK), BF16),
                   jax.ShapeDtypeStruct((b, N_KV, nt, HEAD_DIM, Q_BLOCK), BF16),
                   jax.ShapeDtypeStruct((b, N_KV, s, 2 * LANES), BF16),
                   jax.ShapeDtypeStruct((b, N_KV, s, LANES), BF16),
                   jax.ShapeDtypeStruct((b, s, LANES), F32),
                   jax.ShapeDtypeStruct((b, s, LANES), F32),
                   jax.ShapeDtypeStruct((n, D_MODEL), BF16),
                   jax.ShapeDtypeStruct((n, D_MODEL), BF16)],
        compiler_params=pltpu.CompilerParams(dimension_semantics=("parallel", "parallel"),
                                             vmem_limit_bytes=VMEM_LIMIT),
        name="proj",
    )(*ins)


def _compress_kernel(xk_ref, xv_ref, pe_ref, w1_ref, b1_ref, w2k_ref, w2vt_ref, kconst_ref, kc_ref, vct_ref):
    nc = kc_ref.shape[2]
    half = CMP_LEN // 2
    hids = []
    for which, x_ref in enumerate((xk_ref, xv_ref)):
        za = jnp.zeros((nc, N_KV * CMP_HID), F32)
        zb = jnp.zeros((nc, N_KV * CMP_HID), F32)
        for l in range(half):
            xl = x_ref[0, pl.ds(l, nc, stride=CMP_STRIDE), :]
            za = za + _dot((xl + pe_ref[which, 0, l:l + 1, :]).astype(BF16), w1_ref[which, 0, l])
            zb = zb + _dot((xl + pe_ref[which, 1, l:l + 1, :]).astype(BF16), w1_ref[which, 1, l])
        hid = _gelu(za + pltpu.roll(zb, nc - 1, 0) + b1_ref[which])
        row = lax.broadcasted_iota(I32, hid.shape, 0)
        hids.append(jnp.where(row < nc - 1, hid, 0.0).astype(BF16))
    for g in range(N_KV):
        cs = slice(g * CMP_HID, (g + 1) * CMP_HID)
        kc_ref[0, g] = (_dot(hids[0][:, cs], w2k_ref[...]) + kconst_ref[...]).astype(BF16)
        vt = _dot_nt(w2vt_ref[...], hids[1][:, cs])
        for c in range(nc // KEY_CHUNK):
            vct_ref[0, g, c] = vt[:, c * KEY_CHUNK:(c + 1) * KEY_CHUNK].astype(BF16)


def _compress(xk, xv, pes, w1, b1, w2k, w2vt, kconst):
    b, s, w = xk.shape
    nc = s // CMP_STRIDE
    full = lambda a: pl.BlockSpec(a.shape, lambda i: (0,) * a.ndim)
    seq = pl.BlockSpec((1, s, w), lambda i: (i, 0, 0))
    return pl.pallas_call(
        _compress_kernel,
        grid=(b,),
        in_specs=[seq, seq, full(pes), full(w1), full(b1), full(w2k), full(w2vt), full(kconst)],
        out_specs=[pl.BlockSpec((1, N_KV, nc, LANES), lambda i: (i, 0, 0, 0)),
                   pl.BlockSpec((1, N_KV, nc // KEY_CHUNK, HEAD_DIM, KEY_CHUNK), lambda i: (i, 0, 0, 0, 0))],
        out_shape=[jax.ShapeDtypeStruct((b, N_KV, nc, LANES), BF16),
                   jax.ShapeDtypeStruct((b, N_KV, nc // KEY_CHUNK, HEAD_DIM, KEY_CHUNK), BF16)],
        compiler_params=pltpu.CompilerParams(dimension_semantics=("parallel",),
                                             vmem_limit_bytes=VMEM_LIMIT),
        name="compress",
    )(xk, xv, pes, w1, b1, w2k, w2vt, kconst)


def _pairs():
    return [slice(pr * 2 * Q_BLOCK, (pr + 1) * 2 * Q_BLOCK) for pr in range(GQA_R // 2)]


def _colmax(blocks):
    part = None
    for x in blocks:
        y = jnp.max(x.reshape(-1, 8, x.shape[-1]), axis=0)
        part = y if part is None else jnp.maximum(part, y)
    return jnp.max(part, axis=0, keepdims=True)


def _colsum(blocks):
    part = None
    for x in blocks:
        y = jnp.sum(x.reshape(-1, 8, x.shape[-1]), axis=0)
        part = y if part is None else part + y
    return jnp.sum(part, axis=0, keepdims=True)


def _softmax_pv(scores, vts, masks, m_ref, l_ref, acc_ref):
    for pr, ls in enumerate(_pairs()):
        sb = [x if masks[u] is None else masks[u](x, ls) for u, x in enumerate(scores[pr])]
        m_old = m_ref[0:1, ls]
        m_new = jnp.maximum(m_old, _colmax(sb))
        alpha = jnp.exp2(m_old - m_new)
        ps = [jnp.exp2(x - m_new) for x in sb]
        pv = _dot(vts[0], ps[0].astype(BF16))
        for u in range(1, len(ps)):
            pv = pv + _dot(vts[u], ps[u].astype(BF16))
        l_ref[0:1, ls] = alpha * l_ref[0:1, ls] + _colsum(ps)
        acc_ref[:, ls] = alpha * acc_ref[:, ls] + pv
        m_ref[0:1, ls] = m_new


def _nsa_kernel(qt_ref, gt_ref, kc_ref, vct_ref, ksa_ref, vst_ref, kwa_ref, vwt_ref, ovt_ref, tri_ref,
                o_ref, qa_ref, s0_ref, s1_ref, m_ref, l_ref, acc_ref, act_ref, *, n_sel, top_n):
    i = pl.program_id(1)
    t0 = i * Q_BLOCK
    width = GQA_R * Q_BLOCK
    n_chunks = ksa_ref.shape[2] // KEY_CHUNK
    n_cchunks = kc_ref.shape[2] // KEY_CHUNK
    n_slots = act_ref.shape[0] // N_KV
    lane = lax.broadcasted_iota(I32, (1, width), 1)
    tq = t0 + (lane & (Q_BLOCK - 1))
    sub = lax.broadcasted_iota(I32, (KEY_CHUNK, 1), 0)
    pairs = _pairs()
    heads = range(N_KV)

    def reset(g):
        m_ref[g] = jnp.full(m_ref.shape[1:], NEG, F32)
        l_ref[g] = jnp.zeros(l_ref.shape[1:], F32)
        acc_ref[g] = jnp.zeros(acc_ref.shape[1:], F32)

    sc = [[[_dot(kc_ref[0, g, u * KEY_CHUNK:(u + 1) * KEY_CHUNK, :], qt_ref[0, g, 0, :, ls])
            for u in range(n_cchunks)] for ls in pairs] for g in heads]
    o_c, imp_t = [], []
    for g in heads:
        ocg, ps = [], []
        for pr, ls in enumerate(pairs):
            sb = []
            for u in range(n_cchunks):
                c_end = (u * KEY_CHUNK + sub) * CMP_STRIDE + (CMP_LEN - 1)
                sb.append(jnp.where(c_end <= tq[:, ls], sc[g][pr][u], NEG))
            m = _colmax(sb)
            pb = [jnp.exp2(x - m) for x in sb]
            l = _colsum(pb)
            w = (tq[:, ls] >= CMP_LEN - 1).astype(F32) / l
            oc = _dot(vct_ref[0, g, 0], pb[0].astype(BF16))
            for u in range(1, n_cchunks):
                oc = oc + _dot(vct_ref[0, g, u], pb[u].astype(BF16))
            ocg.append(oc * w)
            ps.append([p[:, :Q_BLOCK] * w[:, :Q_BLOCK] + p[:, Q_BLOCK:] * w[:, Q_BLOCK:] for p in pb])
        imp2 = jnp.zeros((LANES, 2 * Q_BLOCK), F32)
        for u in range(n_cchunks):
            psu = ps[0][u] + ps[1][u]
            hi = psu.astype(BF16)
            lo = (psu - hi.astype(F32)).astype(BF16)
            imp2 = imp2 + _dot(ovt_ref[u], jnp.concatenate([hi, lo], axis=1))
        o_c.append(jnp.concatenate(ocg, axis=1))
        imp_t.append(imp2[:, :Q_BLOCK] + imp2[:, Q_BLOCK:])

    n_win = WINDOW // KEY_CHUNK
    cls, masks = [], []
    for u in range(n_win + 1):
        c = i - n_win + u
        cls.append(jnp.maximum(c, 0))
        off = jnp.where(c < 0, NEG, 0.0)
        if u == 0:
            edge = tri_ref[0] + off
            masks.append(lambda x, ls, edge=edge: x + edge)
        elif u == n_win:
            masks.append(lambda x, ls: x + tri_ref[1])
        else:
            masks.append(lambda x, ls, off=off: x + off)
    wsc = [[[_dot(kwa_ref[0, g, pl.ds(pl.multiple_of(cl * KEY_CHUNK, KEY_CHUNK), KEY_CHUNK), :],
                  qt_ref[0, g, 0, :, ls]) for cl in cls] for ls in pairs] for g in heads]
    o_w = []
    for g in heads:
        reset(g)
        _softmax_pv(wsc[g], [vwt_ref[0, g, cl] for cl in cls], masks, m_ref.at[g], l_ref.at[g], acc_ref.at[g])
        o_w.append(acc_ref[g] / l_ref[g, 0:1, :])

    j_io = lax.broadcasted_iota(I32, (LANES, Q_BLOCK), 0)
    tl = t0 + lax.broadcasted_iota(I32, (LANES, Q_BLOCK), 1)
    cur = tl // SEL_BLOCK
    forced = (j_io == 0) | (j_io == cur) | (j_io == cur - 1)
    prio = [jnp.where(j_io < n_sel, jnp.where(forced, FORCE, jnp.where(j_io * SEL_BLOCK <= tl, imp_t[g], -1.0)), NEG)
            for g in heads]
    for _ in range(top_n):
        for g in heads:
            m = jnp.max(prio[g], axis=0, keepdims=True)
            idx = jnp.min(jnp.where(prio[g] == m, j_io, LANES), axis=0, keepdims=True)
            prio[g] = jnp.where(j_io == idx, NEG, prio[g])
    sel_t = [jnp.where((prio[g] == NEG) & (j_io < n_sel), 1.0, 0.0) for g in heads]

    n_act = []
    for u in range(act_ref.shape[0]):
        act_ref[u] = jnp.int32(-1)
    for g in heads:
        qa_ref[g, 0:LANES, :] = qt_ref[0, g, 0]
        bias_t = ((sel_t[g] - 1.0) * MASK_BIAS).astype(BF16)
        for h in range(GQA_R):
            qa_ref[g, LANES:2 * LANES, h * Q_BLOCK:(h + 1) * Q_BLOCK] = bias_t
        any_q = jnp.max(sel_t[g], axis=1, keepdims=True)
        any_pair = jnp.maximum(any_q, pltpu.roll(any_q, LANES - 1, 0))
        na = jnp.int32(0)
        for c in range(n_chunks):
            act_ref[g * n_slots + na] = jnp.int32(c)
            na = na + jnp.where((any_pair[2 * c, 0] > 0.0) & (c < i), 1, 0)
        act_ref[g * n_slots + na] = i
        n_act.append(na)

    def sel_scores(g, k, dst):
        for u in range(SEL_SLOTS):
            cl = jnp.maximum(act_ref[g * n_slots + k * SEL_SLOTS + u], 0)
            keys = ksa_ref[0, g, pl.ds(pl.multiple_of(cl * KEY_CHUNK, KEY_CHUNK), KEY_CHUNK), :]
            for ls in pairs:
                dst[g, u, :, ls] = _dot(keys, qa_ref[g, :, ls])

    def sel_step(g, k, cur_s, nxt_s, last, ahead):
        if ahead:
            sel_scores(g, k + 1, nxt_s)
        vts, msk = [], []
        for u in range(SEL_SLOTS):
            c = act_ref[g * n_slots + k * SEL_SLOTS + u]
            vts.append(vst_ref[0, g, jnp.maximum(c, 0)])
            if last:
                kpos = jnp.where(c < 0, n_chunks * KEY_CHUNK, c * KEY_CHUNK) + sub
                msk.append(lambda x, ls, kpos=kpos: jnp.where(kpos <= tq[:, ls], x, -MASK_BIAS))
            else:
                msk.append(None)
        scores = [[cur_s[g, u, :, ls] for u in range(SEL_SLOTS)] for ls in pairs]
        _softmax_pv(scores, vts, msk, m_ref.at[g], l_ref.at[g], acc_ref.at[g])

    for g in heads:
        reset(g)
        sel_scores(g, 0, s0_ref)
    n_steps = [(n_act[g] + SEL_SLOTS) // SEL_SLOTS for g in heads]
    bufs = ((s0_ref, s1_ref), (s1_ref, s0_ref))

    def sel_body(k, carry):
        last = [k >= n_steps[g] - 1 for g in heads]
        for odd, (cur_s, nxt_s) in enumerate(bufs):
            for l0 in (False, True):
                for l1 in (False, True):
                    @pl.when((k % 2 == odd) & (last[0] == l0) & (last[1] == l1))
                    def _():
                        ahead = not (l0 and l1)
                        sel_step(0, k, cur_s, nxt_s, l0, ahead)
                        sel_step(1, k, cur_s, nxt_s, l1, ahead)

        return carry

    lax.fori_loop(0, jnp.maximum(n_steps[0], n_steps[1]), sel_body, 0)
    o_s = [acc_ref[g] / l_ref[g, 0:1, :] for g in heads]

    for g in heads:
        gt = gt_ref[0, g, 0]

        def gate(br):
            return jnp.concatenate([gt[br * GQA_R + h:br * GQA_R + h + 1, :] for h in range(GQA_R)], axis=1)

        o = gate(0) * o_c[g] + gate(1) * o_s[g] + gate(2) * o_w[g]
        o_ref[0, :, g * GQA_R * HEAD_DIM:(g + 1) * GQA_R * HEAD_DIM] = jnp.concatenate(
            [o[:, h * Q_BLOCK:(h + 1) * Q_BLOCK].T for h in range(GQA_R)], axis=1).astype(BF16)


def _nsa(qt, gt, kc, vct, ksa, vst, kwa, vwt, ovt, tri, n_sel):
    b, g, nt, _, width = qt.shape
    s = ksa.shape[2]
    tile = lambda a: pl.BlockSpec((1, g, 1) + a.shape[3:], lambda i, k: (i, 0, k, 0, 0))
    whole = lambda a: pl.BlockSpec((1,) + a.shape[1:], lambda i, k: (i,) + (0,) * (a.ndim - 1))
    const = lambda a: pl.BlockSpec(a.shape, lambda i, k: (0,) * a.ndim)
    kern = functools.partial(_nsa_kernel, n_sel=n_sel, top_n=min(SEL_TOPN, n_sel))
    return pl.pallas_call(
        kern,
        grid=(b, nt),
        in_specs=[tile(qt), tile(gt), whole(kc), whole(vct), whole(ksa), whole(vst), whole(kwa), whole(vwt),
                  const(ovt), const(tri)],
        out_specs=pl.BlockSpec((1, Q_BLOCK, ATTN_WIDTH), lambda i, k: (i, k, 0)),
        out_shape=jax.ShapeDtypeStruct((b, s, ATTN_WIDTH), BF16),
        scratch_shapes=[pltpu.VMEM((g, 2 * LANES, width), BF16),
                        pltpu.VMEM((g, SEL_SLOTS, KEY_CHUNK, width), F32),
                        pltpu.VMEM((g, SEL_SLOTS, KEY_CHUNK, width), F32),
                        pltpu.VMEM((g, 8, width), F32),
                        pltpu.VMEM((g, 8, width), F32),
                        pltpu.VMEM((g, HEAD_DIM, width), F32),
                        pltpu.SMEM((g * (s // KEY_CHUNK + 2 * SEL_SLOTS),), I32)],
        compiler_params=pltpu.CompilerParams(dimension_semantics=("parallel", "arbitrary"),
                                             vmem_limit_bytes=VMEM_LIMIT),
        name="nsa",
    )(qt, gt, kc, vct, ksa, vst, kwa, vwt, ovt, tri)


def _merge_kernel(x_ref, ya_ref, ga_ref, gbyb_ref, wba_ref, wout_ref, gmoe_ref, wr_ref, br_ref, utri_ref,
                  x1_ref, hm_ref, rw_ref, ri_ref, cnt_ref, carry_ref):
    tm = x_ref.shape[0]

    @pl.when(pl.program_id(0) == 0)
    def _():
        carry_ref[...] = jnp.zeros_like(carry_ref)

    merged = ga_ref[...].astype(F32) * _dot(ya_ref[...], wba_ref[...]) + gbyb_ref[...].astype(F32)
    x1 = x_ref[...] + _dot(merged.astype(BF16), wout_ref[...])
    x1_ref[...] = x1
    hm = _rms(x1, gmoe_ref[...])
    for a in range(ROW_TILE):
        hm_ref[pl.ds(a, tm, stride=ROW_TILE), :] = hm[:, a * LANES:(a + 1) * LANES]

    hh = hm.astype(BF16)
    hl = (hm - hh.astype(F32)).astype(BF16)
    both = _dot(hh, wr_ref[...])
    logits = both[:, :LANES] + both[:, LANES:] + _dot(hl, wr_ref[:, :LANES]) + br_ref[...]
    lg = logits.T[0:N_EXPERTS, :]
    e_io = lax.broadcasted_iota(I32, (N_EXPERTS, tm), 0)
    vals, idxs = [], []
    for _ in range(TOP_K):
        m = jnp.max(lg, axis=0, keepdims=True)
        idx = jnp.min(jnp.where(lg == m, e_io, N_EXPERTS), axis=0, keepdims=True)
        vals.append(m)
        idxs.append(idx)
        lg = jnp.where(e_io == idx, NEG, lg)
    ex = [jnp.exp(v - vals[0]) for v in vals]
    den = ex[0] + ex[1] + ex[2] + ex[3]

    hits = [e_io == idx for idx in idxs]
    multi = jnp.zeros((N_EXPERTS, tm), F32)
    for h in hits:
        multi = jnp.where(h, 1.0, multi)
    carry = carry_ref[:, 0:1]
    cum = _dot(multi.astype(BF16), utri_ref[...]) + carry
    ranks = [jnp.sum(jnp.where(h, cum, 0.0), axis=0, keepdims=True).astype(I32) for h in hits]
    ri_ref[...] = jnp.concatenate(idxs + ranks, axis=0)
    wts = jnp.concatenate([e / den for e in ex] + [jnp.zeros((LANES - TOP_K, tm), F32)], axis=0)
    rw_ref[...] = wts.T
    new_carry = carry + jnp.sum(multi, axis=1, keepdims=True)
    carry_ref[...] = jnp.broadcast_to(new_carry, carry_ref.shape)
    cnt_ref[...] = jnp.broadcast_to(new_carry, cnt_ref.shape)


def _merge(x2, ya, ga, gbyb, wba, wout, gmoe, wr, br, tm):
    n = x2.shape[0]
    full = lambda a: pl.BlockSpec(a.shape, lambda i: (0,) * a.ndim)
    rows = lambda w: pl.BlockSpec((tm, w), lambda i: (i, 0))
    utri = jnp.asarray(np.triu(np.ones((tm, tm), np.float32), 1), BF16)
    return pl.pallas_call(
        _merge_kernel,
        grid=(n // tm,),
        in_specs=[rows(D_MODEL), rows(ATTN_WIDTH), rows(D_MODEL), rows(D_MODEL),
                  full(wba), full(wout), full(gmoe), full(wr), full(br), full(utri)],
        out_specs=[rows(D_MODEL), pl.BlockSpec((tm * ROW_TILE, LANES), lambda i: (i, 0)), rows(LANES),
                   pl.BlockSpec((2 * TOP_K, tm), lambda i: (0, i)),
                   pl.BlockSpec((N_EXPERTS, LANES), lambda i: (0, 0))],
        out_shape=[jax.ShapeDtypeStruct((n, D_MODEL), F32),
                   jax.ShapeDtypeStruct((n * ROW_TILE, LANES), F32),
                   jax.ShapeDtypeStruct((n, LANES), F32),
                   jax.ShapeDtypeStruct((2 * TOP_K, n), I32),
                   jax.ShapeDtypeStruct((N_EXPERTS, LANES), F32)],
        scratch_shapes=[pltpu.VMEM((N_EXPERTS, LANES), F32)],
        compiler_params=pltpu.CompilerParams(dimension_semantics=("arbitrary",),
                                             vmem_limit_bytes=VMEM_LIMIT),
        name="merge",
    )(x2, ya, ga, gbyb, wba, wout, gmoe, wr, br, utri)


def _tile_copy(src, i, dst, d, sem):
    return pltpu.make_async_copy(src.at[pl.ds(pl.multiple_of(i * ROW_TILE, ROW_TILE), ROW_TILE)],
                                 dst.at[pl.ds(pl.multiple_of(d * ROW_TILE, ROW_TILE), ROW_TILE)], sem)


def _dispatch_kernel(seg_ref, dest_ref, hm_ref, xs_ref, zero_ref, sem, zsem, *, n_pad):
    tm = hm_ref.shape[0] // ROW_TILE

    @pl.when(pl.program_id(0) == 0)
    def _():
        zero_ref[...] = jnp.zeros_like(zero_ref)

        def seg(e, c):
            def fill(r, c2):
                _tile_copy(zero_ref, 0, xs_ref, r, zsem).start()
                return c2
            return lax.fori_loop(seg_ref[0, e], seg_ref[1, e], fill, c)

        lax.fori_loop(0, N_EXPERTS + 1, seg, 0)
        pad_rows = xs_ref.at[pl.ds(0, n_pad * ROW_TILE)]
        pltpu.make_async_copy(pad_rows, pad_rows, zsem).wait()

    def issue(r, c):
        slots = [dest_ref[r * TOP_K + k] for k in range(TOP_K)]
        for k in range(TOP_K):
            _tile_copy(hm_ref, r, xs_ref, slots[k], sem).start(priority=k % 2)
        return c

    lax.fori_loop(0, tm, issue, 0, unroll=4)
    for k in range(TOP_K):
        pltpu.make_async_copy(hm_ref, xs_ref.at[pl.ds(0, tm * ROW_TILE)], sem).wait()


def _dispatch(seg, dest_flat, hm, n_slots, tm):
    n = hm.shape[0] // ROW_TILE
    kern = functools.partial(_dispatch_kernel, n_pad=n_slots - n * TOP_K)
    return pl.pallas_call(
        kern,
        grid_spec=pltpu.PrefetchScalarGridSpec(
            num_scalar_prefetch=1,
            grid=(n // tm,),
            in_specs=[pl.BlockSpec((tm * TOP_K,), lambda i, sg: (i,), memory_space=pltpu.SMEM),
                      pl.BlockSpec((tm * ROW_TILE, LANES), lambda i, sg: (i, 0))],
            out_specs=pl.BlockSpec(memory_space=pl.ANY),
            scratch_shapes=[pltpu.VMEM((ROW_TILE, LANES), F32),
                            pltpu.SemaphoreType.DMA(()), pltpu.SemaphoreType.DMA(())]),
        out_shape=jax.ShapeDtypeStruct((n_slots * ROW_TILE, LANES), F32),
        compiler_params=pltpu.CompilerParams(dimension_semantics=("arbitrary",),
                                             has_side_effects=True),
        name="dispatch",
    )(seg, dest_flat, hm)


def _expert_kernel(be_ref, nu_ref, xs_ref, wgu_ref, bgu_ref, wd_ref, bd_ref, y_ref, wgu_bf, wd_bf):
    i = pl.program_id(0)

    @pl.when(i >= nu_ref[0])
    def _():
        y_ref[...] = jnp.zeros_like(y_ref)

    @pl.when((i == 0) | (be_ref[i] != be_ref[jnp.maximum(i - 1, 0)]))
    def _():
        wgu_bf[...] = wgu_ref[0].astype(BF16)
        wd_bf[...] = wd_ref[0].astype(BF16)

    @pl.when(i < nu_ref[0])
    def _():
        x = jnp.concatenate([xs_ref[pl.ds(a, MOE_BLOCK, stride=ROW_TILE), :] for a in range(ROW_TILE)], axis=1)
        gu = _dot(x.astype(BF16), wgu_bf[...]) + bgu_ref[0]
        gate = jnp.minimum(gu[:, :D_FF], SWIGLU_LIMIT)
        up = jnp.clip(gu[:, D_FF:], -SWIGLU_LIMIT, SWIGLU_LIMIT)
        act = gate * _sigmoid(SWIGLU_ALPHA * gate) * (up + 1.0)
        y = _dot(act.astype(BF16), wd_bf[...]) + bd_ref[0]
        for a in range(ROW_TILE):
            y_ref[pl.ds(a, MOE_BLOCK, stride=ROW_TILE), :] = y[:, a * LANES:(a + 1) * LANES]


def _experts(blk_expert, n_used, xs, wgu, bgu, wd, bd):
    n_blocks = xs.shape[0] // (MOE_BLOCK * ROW_TILE)
    blk = lambda i, be, nu: (jnp.minimum(i, nu[0] - 1), 0)
    exp3 = lambda i, be, nu: (be[jnp.minimum(i, nu[0] - 1)], 0, 0)
    return pl.pallas_call(
        _expert_kernel,
        grid_spec=pltpu.PrefetchScalarGridSpec(
            num_scalar_prefetch=2,
            grid=(n_blocks,),
            in_specs=[pl.BlockSpec((MOE_BLOCK * ROW_TILE, LANES), blk),
                      pl.BlockSpec((1, D_MODEL, 2 * D_FF), exp3),
                      pl.BlockSpec((1, 1, 2 * D_FF), exp3),
                      pl.BlockSpec((1, D_FF, D_MODEL), exp3),
                      pl.BlockSpec((1, 1, D_MODEL), exp3)],
            out_specs=pl.BlockSpec((MOE_BLOCK * ROW_TILE, LANES), lambda i, be, nu: (i, 0)),
            scratch_shapes=[pltpu.VMEM((D_MODEL, 2 * D_FF), BF16), pltpu.VMEM((D_FF, D_MODEL), BF16)]),
        out_shape=jax.ShapeDtypeStruct(xs.shape, F32),
        compiler_params=pltpu.CompilerParams(dimension_semantics=("arbitrary",),
                                             vmem_limit_bytes=VMEM_LIMIT),
        name="experts",
    )(blk_expert, n_used, xs, wgu, bgu, wd, bd)


def _combine_kernel(dest_ref, dest_next_ref, rw_ref, x1_ref, gfin_ref, y_ref, o_ref, ybuf0, ybuf1, sem0, sem1):
    tm = x1_ref.shape[0]
    i = pl.program_id(0)

    def gather(dst_ref, ybuf, sem):
        def issue(r, c):
            slots = [dst_ref[r * TOP_K + k] for k in range(TOP_K)]
            for k in range(TOP_K):
                _tile_copy(y_ref, slots[k], ybuf.at[k], r, sem).start(priority=k % 2)
            return c

        lax.fori_loop(0, tm, issue, 0, unroll=4)

    def finish(ybuf, sem):
        for k in range(TOP_K):
            pltpu.make_async_copy(y_ref.at[pl.ds(0, tm * ROW_TILE)], ybuf.at[k], sem).wait()
        rw = rw_ref[...]
        x1 = x1_ref[...]
        cols = []
        for a in range(ROW_TILE):
            acc = x1[:, a * LANES:(a + 1) * LANES]
            for k in range(TOP_K):
                acc = acc + rw[:, k:k + 1] * ybuf[k, pl.ds(a, tm, stride=ROW_TILE), :]
            cols.append(acc)
        o_ref[...] = _rms(jnp.concatenate(cols, axis=1), gfin_ref[...])

    @pl.when(i == 0)
    def _():
        gather(dest_ref, ybuf0, sem0)

    for parity, (cur, nxt) in enumerate((((ybuf0, sem0), (ybuf1, sem1)), ((ybuf1, sem1), (ybuf0, sem0)))):
        @pl.when(i % 2 == parity)
        def _():
            @pl.when(i + 1 < pl.num_programs(0))
            def _():
                gather(dest_next_ref, *nxt)

            finish(*cur)


def _combine(dest_flat, rw, x1, gfin, y, tm):
    n = x1.shape[0]
    steps = n // tm
    ybuf = pltpu.VMEM((TOP_K, tm * ROW_TILE, LANES), F32)
    return pl.pallas_call(
        _combine_kernel,
        grid=(steps,),
        in_specs=[pl.BlockSpec((tm * TOP_K,), lambda i: (i,), memory_space=pltpu.SMEM),
                  pl.BlockSpec((tm * TOP_K,), lambda i: (jnp.minimum(i + 1, steps - 1),), memory_space=pltpu.SMEM),
                  pl.BlockSpec((tm, LANES), lambda i: (i, 0)),
                  pl.BlockSpec((tm, D_MODEL), lambda i: (i, 0)),
                  pl.BlockSpec((1, D_MODEL), lambda i: (0, 0)),
                  pl.BlockSpec(memory_space=pl.ANY)],
        out_specs=pl.BlockSpec((tm, D_MODEL), lambda i: (i, 0)),
        out_shape=jax.ShapeDtypeStruct((n, D_MODEL), F32),
        scratch_shapes=[ybuf, ybuf, pltpu.SemaphoreType.DMA(()), pltpu.SemaphoreType.DMA(())],
        compiler_params=pltpu.CompilerParams(dimension_semantics=("arbitrary",),
                                             vmem_limit_bytes=VMEM_LIMIT),
        name="combine",
    )(dest_flat, dest_flat, rw, x1, gfin, y)


def _overlap_t(nc, n_cmp, n_sel):
    cs = np.arange(n_cmp)[None, :] * CMP_STRIDE
    ss = np.arange(n_sel)[:, None] * SEL_BLOCK
    ov = np.clip(np.minimum(cs + CMP_LEN, ss + SEL_BLOCK) - np.maximum(cs, ss), 0, None) / CMP_LEN
    out = np.zeros((LANES, nc), np.float32)
    out[:n_sel, :n_cmp] = ov
    return jnp.asarray(out, BF16)


def _layer(x2, b, s, g_mix, w_in, w_ck1, b_ck1, w_ck2, pe_k, w_cv1, b_cv1, w_cv2, pe_v,
           sgu_ln_g, sgu_ln_b, w_spatial, b_spatial, w_branch_a, w_branch_b, w_out,
           g_moe, w_router, b_router, w_gate_up, b_gate_up, w_down, b_down):
    n = b * s
    nc = s // CMP_STRIDE
    n_cmp = (s - CMP_LEN) // CMP_STRIDE + 1
    n_sel = s // SEL_BLOCK
    assert nc % KEY_CHUNK == 0 and n_sel <= LANES and n_cmp == nc - 1
    tm = TOKEN_TILE
    assert s % tm == 0 and n % ROUTE_TILE == 0 and n % SCATTER_TILE == 0

    p0 = ATTN_WIDTH
    p1 = p0 + 6 * KV_WIDTH
    p2 = p1 + N_NSA_BRANCH * N_HEADS
    p3 = p2 + SGU_WIDTH
    p4 = p3 + SGU_WIDTH
    zpad = lambda a, w: jnp.pad(a, ((0, 0),) * (a.ndim - 1) + ((0, w - a.shape[-1]),))
    wq = w_in[:, :p0] * (HEAD_DIM ** -0.5 * LOG2E)
    wkv = w_in[:, p0:p1].reshape(D_MODEL, 6, N_KV, HEAD_DIM)
    wng = w_in[:, p1:p2].reshape(D_MODEL, N_KV, GQA_R, N_NSA_BRANCH).transpose(0, 1, 3, 2)
    wng = zpad(wng.reshape(D_MODEL, N_KV, N_NSA_BRANCH * GQA_R), GATE_ROWS)
    wt = jnp.concatenate([wq, wkv[:, 3].reshape(D_MODEL, -1),
                          wkv[:, 5].reshape(D_MODEL, -1), wng.reshape(D_MODEL, -1)], axis=1).T.astype(BF16)
    slopes = 2.0 ** (-8.0 * np.arange(1, N_HEADS + 1) / N_HEADS)
    qrows = np.zeros((N_HEADS, LANES - HEAD_DIM, 1), np.float32)
    bf16_round = lambda a: a.astype(BF16).astype(np.float32)
    for k, coef in enumerate((slopes * SEL_BLOCK * LOG2E, slopes * LOG2E)):
        hi = bf16_round(coef.astype(np.float32))
        qrows[:, 2 * k, 0] = hi
        qrows[:, 2 * k + 1, 0] = bf16_round(coef.astype(np.float32) - hi)
    qconst = jnp.asarray(np.broadcast_to(qrows, (N_HEADS, LANES - HEAD_DIM, Q_BLOCK)), BF16)
    wk = jnp.concatenate([wkv[:, c].reshape(D_MODEL, -1) for c in (2, 4, 0, 1)], axis=1).astype(BF16)
    pos = np.arange(s)
    kc_np = np.zeros((s, 2 * LANES), np.float32)
    kc_np[:, HEAD_DIM:HEAD_DIM + 2] = (pos // SEL_BLOCK)[:, None]
    kc_np[:, HEAD_DIM + 2:HEAD_DIM + 4] = (pos % SEL_BLOCK)[:, None]
    kc_np[pos, LANES + pos // SEL_BLOCK] = 1.0
    kconst = jnp.asarray(kc_np, BF16)
    wu = w_in[:, p2:p3].astype(BF16)
    wv = w_in[:, p3:p4].astype(BF16)
    wmg = w_in[:, p4:].astype(BF16)
    bsp = jnp.repeat(b_spatial.T, SGU_GROUP_DIM, axis=1)

    qt, gt, vst, vwt, ksa, kwa, kcx, vcx, ga, gbyb = _proj(
        x2, b, s, g_mix[None], wt, qconst, wk, kconst, wu, wv, wmg, sgu_ln_g[None], sgu_ln_b[None],
        w_spatial, bsp, w_branch_b.astype(BF16), tm)

    half = CMP_LEN // 2
    eye = jnp.eye(N_KV, dtype=F32)[None, None, :, None, :, None]
    bdiag = lambda w: (w.reshape(2, half, 1, HEAD_DIM, 1, CMP_HID) * eye).reshape(
        2, half, N_KV * HEAD_DIM, N_KV * CMP_HID)
    w1 = jnp.stack([bdiag(w_ck1), bdiag(w_cv1)]).astype(BF16)
    pes = jnp.stack([jnp.tile(pe_k.reshape(2, half, HEAD_DIM), (1, 1, N_KV)),
                     jnp.tile(pe_v.reshape(2, half, HEAD_DIM), (1, 1, N_KV))])
    b1 = jnp.stack([jnp.tile(b_ck1, N_KV), jnp.tile(b_cv1, N_KV)])[:, None, :]
    blk_n = np.arange(nc)
    cc_np = np.zeros((nc, LANES), np.float32)
    cc_np[:, HEAD_DIM:HEAD_DIM + 2] = (blk_n // (SEL_BLOCK // CMP_STRIDE))[:, None]
    cc_np[:, HEAD_DIM + 2:HEAD_DIM + 4] = (blk_n % (SEL_BLOCK // CMP_STRIDE) * CMP_STRIDE)[:, None]
    kc, vct = _compress(kcx, vcx, pes, w1, b1, zpad(w_ck2, LANES).astype(BF16), w_cv2.T.astype(BF16),
                        jnp.asarray(cc_np))

    ovt = _overlap_t(nc, n_cmp, n_sel).reshape(LANES, nc // KEY_CHUNK, KEY_CHUNK).transpose(1, 0, 2)
    a_io, q_io = np.meshgrid(np.arange(KEY_CHUNK), np.arange(Q_BLOCK), indexing="ij")
    tri = np.stack([np.where(a_io > q_io, 0.0, NEG), np.where(a_io <= q_io, 0.0, NEG)]).astype(np.float32)
    tri = jnp.asarray(np.tile(tri, (1, 1, 2)))
    ya = _nsa(qt, gt, kc, vct, ksa, vst, kwa, vwt, ovt, tri, n_sel).reshape(n, ATTN_WIDTH)

    wr = jnp.pad(w_router, ((0, 0), (0, LANES - N_EXPERTS)))
    wrh = wr.astype(BF16)
    wr2 = jnp.concatenate([wrh, (wr - wrh.astype(F32)).astype(BF16)], axis=1)
    br = jnp.pad(b_router, (0, LANES - N_EXPERTS))[None]
    x1, hm, rw, ri, cnt = _merge(x2, ya, ga, gbyb, w_branch_a.astype(BF16), w_out.astype(BF16),
                                 g_moe[None], wr2, br, tm)

    counts = cnt[:, 0].astype(I32)
    padded = (counts + MOE_BLOCK - 1) // MOE_BLOCK * MOE_BLOCK
    pad_end = jnp.cumsum(padded)
    pad_start = pad_end - padded
    base = sum(jnp.where(ri[:TOP_K] == e, pad_start[e], 0) for e in range(N_EXPERTS))
    dest = (base + ri[TOP_K:]).T.reshape(-1)
    n_blocks = -(-(n * TOP_K) // MOE_BLOCK) + N_EXPERTS
    blk_start = jnp.arange(n_blocks, dtype=I32) * MOE_BLOCK
    blk_expert = jnp.minimum(jnp.sum((pad_end[None, :] <= blk_start[:, None]).astype(I32), axis=1),
                             N_EXPERTS - 1)
    n_used = (pad_end[-1:] // MOE_BLOCK).astype(I32)
    n_slots = n_blocks * MOE_BLOCK
    seg = jnp.stack([jnp.concatenate([pad_start + counts, pad_end[-1:]]),
                     jnp.concatenate([pad_end, jnp.full((1,), n_slots, I32)])]).astype(I32)

    xs = _dispatch(seg, dest, hm, n_slots, SCATTER_TILE)
    y = _experts(blk_expert, n_used, xs, w_gate_up, b_gate_up[:, None, :], w_down, b_down[:, None, :])
    return dest, rw, x1, y


def kernel(x, g_mix, w_in, w_ck1, b_ck1, w_ck2, pe_k, w_cv1, b_cv1, w_cv2, pe_v, sgu_ln_g, sgu_ln_b, w_spatial, b_spatial, w_branch_a, w_branch_b, w_out, g_moe, w_router, b_router, w_gate_up, b_gate_up, w_down, b_down, g_final):
    b, s, d = x.shape
    assert g_mix.shape[0] == 1, "the final rmsnorm is fused into the single layer's combine step"
    l = 0
    dest, rw, x1, y = _layer(
        x.reshape(b * s, d), b, s, g_mix[l], w_in[l], w_ck1[l], b_ck1[l], w_ck2[l], pe_k[l], w_cv1[l],
        b_cv1[l], w_cv2[l], pe_v[l], sgu_ln_g[l], sgu_ln_b[l], w_spatial[l], b_spatial[l], w_branch_a[l],
        w_branch_b[l], w_out[l], g_moe[l], w_router[l], b_router[l], w_gate_up[l], b_gate_up[l],
        w_down[l], b_down[l])
    return _combine(dest, rw, x1, g_final[None], y, ROUTE_TILE).reshape(b, s, d)
```
